```python
import jax, jax.numpy as jnp
from jax import lax
import numpy as np

D_MODEL = 2048
BATCH = 8
SEQ = 4096
DEPTH = 1

N_ATTN_HEADS = 8
ATTN_HEAD_DIM = 128
ATTN_WIDTH = N_ATTN_HEADS * ATTN_HEAD_DIM
Q_BLOCK = 128
LRU_WIDTH = D_MODEL // 2
LRU_BLOCKS = 8
LRU_BLOCK = LRU_WIDTH // LRU_BLOCKS
CONV_WIDTH = 4
LRU_C = 8.0
D_FF = ((8 * D_MODEL // 3 + 255) // 256) * 256
N_MOD = 9
EPS = 1e-6

IN_WIDTHS = (ATTN_WIDTH, ATTN_WIDTH, ATTN_WIDTH, LRU_WIDTH, LRU_WIDTH, D_MODEL, D_MODEL)
IN_TOTAL = sum(IN_WIDTHS)
IN_SPLITS = tuple(int(s) for s in np.cumsum(IN_WIDTHS)[:-1])

kernel_name = "hybrid_stickbreak_rglru_macaron_adaln"


def rms_norm(x, g):
    xf = x.astype(jnp.float32)
    y = xf * lax.rsqrt(jnp.mean(xf * xf, axis=-1, keepdims=True) + EPS)
    return (y * g.astype(jnp.float32)).astype(x.dtype)


def modulate(x, shift, scale):
    return x * (1.0 + scale[:, None, :]) + shift[:, None, :]


def swiglu(x, w_in, w_out):
    gate, up = jnp.split(x @ w_in, 2, axis=-1)
    return (jax.nn.silu(gate) * up) @ w_out


def causal_depthwise_conv(x, w, b):
    y = lax.conv_general_dilated(
        x, w[:, None, :].astype(x.dtype), window_strides=(1,),
        padding=[(CONV_WIDTH - 1, 0)],
        dimension_numbers=("NWC", "WIO", "NWC"),
        feature_group_count=x.shape[-1])
    return y + b


def stick_breaking_attention(q, k, v):
    S = q.shape[2]
    scale = ATTN_HEAD_DIM ** -0.5
    outs = []
    for blk in range(S // Q_BLOCK):
        q0 = blk * Q_BLOCK
        n_keys = q0 + Q_BLOCK
        qb = q[:, :, q0:n_keys]
        kb = k[:, :, :n_keys]
        vb = v[:, :, :n_keys]
        z = jnp.einsum("bhqd,bhkd->bhqk", qb, kb).astype(jnp.float32) * scale
        t_idx = q0 + jnp.arange(Q_BLOCK)[:, None]
        s_idx = jnp.arange(n_keys)[None, :]
        before = s_idx < t_idx
        log_keep = jnp.where(before, jax.nn.log_sigmoid(-z), 0.0)
        rev = lax.cumsum(log_keep, axis=3, reverse=True)
        between = jnp.concatenate([rev[..., 1:], jnp.zeros_like(rev[..., :1])], axis=-1)
        w = jnp.where(before, jnp.exp(jax.nn.log_sigmoid(z) + between), 0.0)
        outs.append(jnp.einsum("bhqk,bhkd->bhqd", w.astype(vb.dtype), vb))
    return jnp.concatenate(outs, axis=2)


def rg_lru(x, w_r, b_r, w_i, b_i, lam):
    B, S, C = x.shape
    xb = x.reshape(B, S, LRU_BLOCKS, LRU_BLOCK)
    r = jax.nn.sigmoid(jnp.einsum("bsnc,ncd->bsnd", xb, w_r).reshape(B, S, C) + b_r)
    i = jax.nn.sigmoid(jnp.einsum("bsnc,ncd->bsnd", xb, w_i).reshape(B, S, C) + b_i)
    log_a = -LRU_C * r.astype(jnp.float32) * jax.nn.softplus(-lam.astype(jnp.float32))
    a = jnp.exp(log_a)
    u = jnp.sqrt(-jnp.expm1(2.0 * log_a)) * (i * x).astype(jnp.float32)

    def combine(left, right):
        a_l, b_l = left
        a_r, b_r2 = right
        return a_l * a_r, a_r * b_l + b_r2

    _, h = lax.associative_scan(combine, (a, u), axis=1)
    return h.astype(x.dtype)


def hybrid_mixer(h, w_in, conv_w, conv_b, w_r, b_r, w_i, b_i, lam, w_br_attn, w_br_lru, w_out):
    B, S, _ = h.shape
    q, k, v, xr, gr, g_attn, g_lru = jnp.split(h @ w_in, IN_SPLITS, axis=-1)

    def heads(t):
        return t.reshape(B, S, N_ATTN_HEADS, ATTN_HEAD_DIM).transpose(0, 2, 1, 3)

    y_attn = stick_breaking_attention(heads(q), heads(k), heads(v))
    y_attn = y_attn.transpose(0, 2, 1, 3).reshape(B, S, ATTN_WIDTH)

    xr = causal_depthwise_conv(xr, conv_w, conv_b)
    y_lru = rg_lru(xr, w_r, b_r, w_i, b_i, lam) * jax.nn.gelu(gr)

    merged = (jax.nn.sigmoid(g_attn) * (y_attn @ w_br_attn)
              + jax.nn.sigmoid(g_lru) * (y_lru @ w_br_lru))
    return merged @ w_out


def _fwd_setup_inputs(seed: int = 0) -> dict:
    key = jax.random.key(seed)
    ks = jax.random.split(key, 24)
    f32 = jnp.float32

    def nrm(k, shape, std):
        return jax.random.normal(k, shape, f32) * std

    u = jax.random.uniform(ks[13], (DEPTH, LRU_WIDTH), f32, 0.9, 0.999)
    a0 = u ** (1.0 / LRU_C)
    lam = jnp.log(a0) - jnp.log1p(-a0)

    return {
        "x": nrm(ks[0], (BATCH, SEQ, D_MODEL), 1.0),
        "c": nrm(ks[1], (BATCH, D_MODEL), 1.0),
        "w_ada": nrm(ks[2], (DEPTH, D_MODEL, N_MOD * D_MODEL), 0.5 * D_MODEL ** -0.5),
        "b_ada": nrm(ks[3], (DEPTH, N_MOD * D_MODEL), 0.01),
        "norm_ffn1": 1.0 + nrm(ks[4], (DEPTH, D_MODEL), 0.01),
        "w_ffn1_in": nrm(ks[5], (DEPTH, D_MODEL, 2 * D_FF), D_MODEL ** -0.5),
        "w_ffn1_out": nrm(ks[6], (DEPTH, D_FF, D_MODEL), D_FF ** -0.5),
        "norm_mix": 1.0 + nrm(ks[7], (DEPTH, D_MODEL), 0.01),
        "w_in": nrm(ks[8], (DEPTH, D_MODEL, IN_TOTAL), D_MODEL ** -0.5),
        "conv_w": nrm(ks[9], (DEPTH, CONV_WIDTH, LRU_WIDTH), CONV_WIDTH ** -0.5),
        "conv_b": nrm(ks[10], (DEPTH, LRU_WIDTH), 0.01),
        "w_rg_gate": nrm(ks[11], (DEPTH, LRU_BLOCKS, LRU_BLOCK, LRU_BLOCK), LRU_BLOCK ** -0.5),
        "b_rg_gate": nrm(ks[12], (DEPTH, LRU_WIDTH), 0.01),
        "w_in_gate": nrm(ks[14], (DEPTH, LRU_BLOCKS, LRU_BLOCK, LRU_BLOCK), LRU_BLOCK ** -0.5),
        "b_in_gate": nrm(ks[15], (DEPTH, LRU_WIDTH), 0.01),
        "lru_lambda": lam,
        "w_branch_attn": nrm(ks[16], (DEPTH, ATTN_WIDTH, D_MODEL), ATTN_WIDTH ** -0.5),
        "w_branch_lru": nrm(ks[17], (DEPTH, LRU_WIDTH, D_MODEL), LRU_WIDTH ** -0.5),
        "w_out": nrm(ks[18], (DEPTH, D_MODEL, D_MODEL), D_MODEL ** -0.5),
        "norm_ffn2": 1.0 + nrm(ks[19], (DEPTH, D_MODEL), 0.01),
        "w_ffn2_in": nrm(ks[20], (DEPTH, D_MODEL, 2 * D_FF), D_MODEL ** -0.5),
        "w_ffn2_out": nrm(ks[21], (DEPTH, D_FF, D_MODEL), D_FF ** -0.5),
        "norm_final": 1.0 + nrm(ks[22], (D_MODEL,), 0.01),
    }


def _fwd_reference(x, c, w_ada, b_ada, norm_ffn1, w_ffn1_in, w_ffn1_out, norm_mix, w_in,
              conv_w, conv_b, w_rg_gate, b_rg_gate, w_in_gate, b_in_gate, lru_lambda,
              w_branch_attn, w_branch_lru, w_out, norm_ffn2, w_ffn2_in, w_ffn2_out,
              norm_final):
    h = x
    c_act = jax.nn.silu(c)
    for l in range(DEPTH):
        mod = c_act @ w_ada[l] + b_ada[l]
        sh1, sc1, g1, sh2, sc2, g2, sh3, sc3, g3 = jnp.split(mod, N_MOD, axis=-1)

        y = modulate(rms_norm(h, norm_ffn1[l]), sh1, sc1)
        h = h + 0.5 * g1[:, None, :] * swiglu(y, w_ffn1_in[l], w_ffn1_out[l])

        y = modulate(rms_norm(h, norm_mix[l]), sh2, sc2)
        h = h + g2[:, None, :] * hybrid_mixer(
            y, w_in[l], conv_w[l], conv_b[l], w_rg_gate[l], b_rg_gate[l],
            w_in_gate[l], b_in_gate[l], lru_lambda[l],
            w_branch_attn[l], w_branch_lru[l], w_out[l])

        y = modulate(rms_norm(h, norm_ffn2[l]), sh3, sc3)
        h = h + 0.5 * g3[:, None, :] * swiglu(y, w_ffn2_in[l], w_ffn2_out[l])
    return rms_norm(h, norm_final)


import jax as _jax
import jax.numpy as _jnp

TWIN_FORMAT = 'train_step'
FWD_PARAMS = ['x', 'c', 'w_ada', 'b_ada', 'norm_ffn1', 'w_ffn1_in', 'w_ffn1_out', 'norm_mix', 'w_in', 'conv_w', 'conv_b', 'w_rg_gate', 'b_rg_gate', 'w_in_gate', 'b_in_gate', 'lru_lambda', 'w_branch_attn', 'w_branch_lru', 'w_out', 'norm_ffn2', 'w_ffn2_in', 'w_ffn2_out', 'norm_final']
TWIN_WEIGHTS = ['w_ada', 'b_ada', 'norm_ffn1', 'w_ffn1_in', 'w_ffn1_out', 'norm_mix', 'w_in', 'conv_w', 'conv_b', 'w_rg_gate', 'b_rg_gate', 'w_in_gate', 'b_in_gate', 'lru_lambda', 'w_branch_attn', 'w_branch_lru', 'w_out', 'norm_ffn2', 'w_ffn2_in', 'w_ffn2_out', 'norm_final']
TWIN_DIFF_INPUT = 'x'
TWIN_INPUTS = ['x', 'c', 'w_ada', 'b_ada', 'norm_ffn1', 'w_ffn1_in', 'w_ffn1_out', 'norm_mix', 'w_in', 'conv_w', 'conv_b', 'w_rg_gate', 'b_rg_gate', 'w_in_gate', 'b_in_gate', 'lru_lambda', 'w_branch_attn', 'w_branch_lru', 'w_out', 'norm_ffn2', 'w_ffn2_in', 'w_ffn2_out', 'norm_final', 'loss_target', 'm_w_ada', 'm_b_ada', 'm_norm_ffn1', 'm_w_ffn1_in', 'm_w_ffn1_out', 'm_norm_mix', 'm_w_in', 'm_conv_w', 'm_conv_b', 'm_w_rg_gate', 'm_b_rg_gate', 'm_w_in_gate', 'm_b_in_gate', 'm_lru_lambda', 'm_w_branch_attn', 'm_w_branch_lru', 'm_w_out', 'm_norm_ffn2', 'm_w_ffn2_in', 'm_w_ffn2_out', 'm_norm_final', 'v_w_ada', 'v_b_ada', 'v_norm_ffn1', 'v_w_ffn1_in', 'v_w_ffn1_out', 'v_norm_mix', 'v_w_in', 'v_conv_w', 'v_conv_b', 'v_w_rg_gate', 'v_b_rg_gate', 'v_w_in_gate', 'v_b_in_gate', 'v_lru_lambda', 'v_w_branch_attn', 'v_w_branch_lru', 'v_w_out', 'v_norm_ffn2', 'v_w_ffn2_in', 'v_w_ffn2_out', 'v_norm_final']
TWIN_OUTPUTS = ['loss', 'grad_x', 'grad_w_ada', 'grad_b_ada', 'grad_norm_ffn1', 'grad_w_ffn1_in', 'grad_w_ffn1_out', 'grad_norm_mix', 'grad_w_in', 'grad_conv_w', 'grad_conv_b', 'grad_w_rg_gate', 'grad_b_rg_gate', 'grad_w_in_gate', 'grad_b_in_gate', 'grad_lru_lambda', 'grad_w_branch_attn', 'grad_w_branch_lru', 'grad_w_out', 'grad_norm_ffn2', 'grad_w_ffn2_in', 'grad_w_ffn2_out', 'grad_norm_final', 'delta_w_ada', 'delta_b_ada', 'delta_norm_ffn1', 'delta_w_ffn1_in', 'delta_w_ffn1_out', 'delta_norm_mix', 'delta_w_in', 'delta_conv_w', 'delta_conv_b', 'delta_w_rg_gate', 'delta_b_rg_gate', 'delta_w_in_gate', 'delta_b_in_gate', 'delta_lru_lambda', 'delta_w_branch_attn', 'delta_w_branch_lru', 'delta_w_out', 'delta_norm_ffn2', 'delta_w_ffn2_in', 'delta_w_ffn2_out', 'delta_norm_final', 'new_m_w_ada', 'new_m_b_ada', 'new_m_norm_ffn1', 'new_m_w_ffn1_in', 'new_m_w_ffn1_out', 'new_m_norm_mix', 'new_m_w_in', 'new_m_conv_w', 'new_m_conv_b', 'new_m_w_rg_gate', 'new_m_b_rg_gate', 'new_m_w_in_gate', 'new_m_b_in_gate', 'new_m_lru_lambda', 'new_m_w_branch_attn', 'new_m_w_branch_lru', 'new_m_w_out', 'new_m_norm_ffn2', 'new_m_w_ffn2_in', 'new_m_w_ffn2_out', 'new_m_norm_final', 'new_v_w_ada', 'new_v_b_ada', 'new_v_norm_ffn1', 'new_v_w_ffn1_in', 'new_v_w_ffn1_out', 'new_v_norm_mix', 'new_v_w_in', 'new_v_conv_w', 'new_v_conv_b', 'new_v_w_rg_gate', 'new_v_b_rg_gate', 'new_v_w_in_gate', 'new_v_b_in_gate', 'new_v_lru_lambda', 'new_v_w_branch_attn', 'new_v_w_branch_lru', 'new_v_w_out', 'new_v_norm_ffn2', 'new_v_w_ffn2_in', 'new_v_w_ffn2_out', 'new_v_norm_final']
TWIN_LEAF_KINDS = {'loss': 'loss', 'grad_x': 'grad_x', 'grad_w_ada': 'grad_w', 'grad_b_ada': 'grad_w', 'grad_norm_ffn1': 'grad_w', 'grad_w_ffn1_in': 'grad_w', 'grad_w_ffn1_out': 'grad_w', 'grad_norm_mix': 'grad_w', 'grad_w_in': 'grad_w', 'grad_conv_w': 'grad_w', 'grad_conv_b': 'grad_w', 'grad_w_rg_gate': 'grad_w', 'grad_b_rg_gate': 'grad_w', 'grad_w_in_gate': 'grad_w', 'grad_b_in_gate': 'grad_w', 'grad_lru_lambda': 'grad_w', 'grad_w_branch_attn': 'grad_w', 'grad_w_branch_lru': 'grad_w', 'grad_w_out': 'grad_w', 'grad_norm_ffn2': 'grad_w', 'grad_w_ffn2_in': 'grad_w', 'grad_w_ffn2_out': 'grad_w', 'grad_norm_final': 'grad_w', 'delta_w_ada': 'delta_w', 'delta_b_ada': 'delta_w', 'delta_norm_ffn1': 'delta_w', 'delta_w_ffn1_in': 'delta_w', 'delta_w_ffn1_out': 'delta_w', 'delta_norm_mix': 'delta_w', 'delta_w_in': 'delta_w', 'delta_conv_w': 'delta_w', 'delta_conv_b': 'delta_w', 'delta_w_rg_gate': 'delta_w', 'delta_b_rg_gate': 'delta_w', 'delta_w_in_gate': 'delta_w', 'delta_b_in_gate': 'delta_w', 'delta_lru_lambda': 'delta_w', 'delta_w_branch_attn': 'delta_w', 'delta_w_branch_lru': 'delta_w', 'delta_w_out': 'delta_w', 'delta_norm_ffn2': 'delta_w', 'delta_w_ffn2_in': 'delta_w', 'delta_w_ffn2_out': 'delta_w', 'delta_norm_final': 'delta_w', 'new_m_w_ada': 'new_m', 'new_m_b_ada': 'new_m', 'new_m_norm_ffn1': 'new_m', 'new_m_w_ffn1_in': 'new_m', 'new_m_w_ffn1_out': 'new_m', 'new_m_norm_mix': 'new_m', 'new_m_w_in': 'new_m', 'new_m_conv_w': 'new_m', 'new_m_conv_b': 'new_m', 'new_m_w_rg_gate': 'new_m', 'new_m_b_rg_gate': 'new_m', 'new_m_w_in_gate': 'new_m', 'new_m_b_in_gate': 'new_m', 'new_m_lru_lambda': 'new_m', 'new_m_w_branch_attn': 'new_m', 'new_m_w_branch_lru': 'new_m', 'new_m_w_out': 'new_m', 'new_m_norm_ffn2': 'new_m', 'new_m_w_ffn2_in': 'new_m', 'new_m_w_ffn2_out': 'new_m', 'new_m_norm_final': 'new_m', 'new_v_w_ada': 'new_v', 'new_v_b_ada': 'new_v', 'new_v_norm_ffn1': 'new_v', 'new_v_w_ffn1_in': 'new_v', 'new_v_w_ffn1_out': 'new_v', 'new_v_norm_mix': 'new_v', 'new_v_w_in': 'new_v', 'new_v_conv_w': 'new_v', 'new_v_conv_b': 'new_v', 'new_v_w_rg_gate': 'new_v', 'new_v_b_rg_gate': 'new_v', 'new_v_w_in_gate': 'new_v', 'new_v_b_in_gate': 'new_v', 'new_v_lru_lambda': 'new_v', 'new_v_w_branch_attn': 'new_v', 'new_v_w_branch_lru': 'new_v', 'new_v_w_out': 'new_v', 'new_v_norm_ffn2': 'new_v', 'new_v_w_ffn2_in': 'new_v', 'new_v_w_ffn2_out': 'new_v', 'new_v_norm_final': 'new_v'}


def _forward(args):
    return _fwd_reference(*[args[k] for k in FWD_PARAMS])


def _output_shape():
    def fwd():
        inp = _fwd_setup_inputs(0)
        return _fwd_reference(*[inp[k] for k in FWD_PARAMS])
    out = _jax.eval_shape(fwd)
    return out.shape, out.dtype

N_MICROBATCH = 1
ADAM_LR = 0.001
ADAM_B1 = 0.9
ADAM_B2 = 0.999
ADAM_EPS = 1e-08
ADAM_WD = 0.01
ADAM_STEP = 10
PER_EXAMPLE_BATCH_AXIS = {'x': 0, 'c': 0, 'loss_target': 0}
SHARED_INPUTS = []
_WEIGHT_DTYPES = {'w_ada': _jnp.float32, 'b_ada': _jnp.float32, 'norm_ffn1': _jnp.float32, 'w_ffn1_in': _jnp.float32, 'w_ffn1_out': _jnp.float32, 'norm_mix': _jnp.float32, 'w_in': _jnp.float32, 'conv_w': _jnp.float32, 'conv_b': _jnp.float32, 'w_rg_gate': _jnp.float32, 'b_rg_gate': _jnp.float32, 'w_in_gate': _jnp.float32, 'b_in_gate': _jnp.float32, 'lru_lambda': _jnp.float32, 'w_branch_attn': _jnp.float32, 'w_branch_lru': _jnp.float32, 'w_out': _jnp.float32, 'norm_ffn2': _jnp.float32, 'w_ffn2_in': _jnp.float32, 'w_ffn2_out': _jnp.float32, 'norm_final': _jnp.float32}
MOMENT_SCALE = {'w_ada': 2.367916e-02, 'b_ada': 3.851964e-02, 'norm_ffn1': 1.326643e-02, 'w_ffn1_in': 5.893227e-03, 'w_ffn1_out': 9.612248e-03, 'norm_mix': 2.511492e-02, 'w_in': 1.600650e-02, 'conv_w': 3.020117e-02, 'conv_b': 9.875492e-02, 'w_rg_gate': 3.663385e-03, 'b_rg_gate': 5.394292e-03, 'w_in_gate': 6.843297e-03, 'b_in_gate': 1.152145e-02, 'lru_lambda': 1.377425e-02, 'w_branch_attn': 1.062575e-02, 'w_branch_lru': 2.162241e-02, 'w_out': 2.317972e-02, 'norm_ffn2': 1.334240e-02, 'w_ffn2_in': 5.668976e-03, 'w_ffn2_out': 9.257358e-03, 'norm_final': 1.600180e+01}


def _to_microbatches(a, axis):
    t = _jnp.moveaxis(a, axis, 0)
    t = t.reshape((N_MICROBATCH, t.shape[0] // N_MICROBATCH) + t.shape[1:])
    return _jnp.moveaxis(t, 1, axis + 1)


def setup_inputs(seed: int = 0) -> dict:
    inp = _fwd_setup_inputs(seed)
    key = _jax.random.fold_in(_jax.random.key(seed), 7919)
    shape, _ = _output_shape()
    out = dict(inp)
    out["loss_target"] = _jax.random.normal(_jax.random.fold_in(key, 0), shape, _jnp.float32)
    for i, name in enumerate(TWIN_WEIGHTS):
        w = inp[name].astype(_jnp.float32)
        if MOMENT_SCALE is None:
            s = _jnp.sqrt(_jnp.mean(_jnp.square(w)) + 1e-30)
        else:
            s = MOMENT_SCALE[name]
        km, kv = _jax.random.split(_jax.random.fold_in(key, i + 1))
        out[name] = w
        out["m_" + name] = s * _jax.random.normal(km, w.shape, _jnp.float32)
        out["v_" + name] = (s * s) * _jax.random.uniform(kv, w.shape, _jnp.float32, 0.5, 1.5)
    if N_MICROBATCH > 1:
        for name, axis in PER_EXAMPLE_BATCH_AXIS.items():
            out[name] = _to_microbatches(out[name], axis)
    return {'x': out['x'], 'c': out['c'], 'w_ada': out['w_ada'], 'b_ada': out['b_ada'], 'norm_ffn1': out['norm_ffn1'], 'w_ffn1_in': out['w_ffn1_in'], 'w_ffn1_out': out['w_ffn1_out'], 'norm_mix': out['norm_mix'], 'w_in': out['w_in'], 'conv_w': out['conv_w'], 'conv_b': out['conv_b'], 'w_rg_gate': out['w_rg_gate'], 'b_rg_gate': out['b_rg_gate'], 'w_in_gate': out['w_in_gate'], 'b_in_gate': out['b_in_gate'], 'lru_lambda': out['lru_lambda'], 'w_branch_attn': out['w_branch_attn'], 'w_branch_lru': out['w_branch_lru'], 'w_out': out['w_out'], 'norm_ffn2': out['norm_ffn2'], 'w_ffn2_in': out['w_ffn2_in'], 'w_ffn2_out': out['w_ffn2_out'], 'norm_final': out['norm_final'], 'loss_target': out['loss_target'], 'm_w_ada': out['m_w_ada'], 'm_b_ada': out['m_b_ada'], 'm_norm_ffn1': out['m_norm_ffn1'], 'm_w_ffn1_in': out['m_w_ffn1_in'], 'm_w_ffn1_out': out['m_w_ffn1_out'], 'm_norm_mix': out['m_norm_mix'], 'm_w_in': out['m_w_in'], 'm_conv_w': out['m_conv_w'], 'm_conv_b': out['m_conv_b'], 'm_w_rg_gate': out['m_w_rg_gate'], 'm_b_rg_gate': out['m_b_rg_gate'], 'm_w_in_gate': out['m_w_in_gate'], 'm_b_in_gate': out['m_b_in_gate'], 'm_lru_lambda': out['m_lru_lambda'], 'm_w_branch_attn': out['m_w_branch_attn'], 'm_w_branch_lru': out['m_w_branch_lru'], 'm_w_out': out['m_w_out'], 'm_norm_ffn2': out['m_norm_ffn2'], 'm_w_ffn2_in': out['m_w_ffn2_in'], 'm_w_ffn2_out': out['m_w_ffn2_out'], 'm_norm_final': out['m_norm_final'], 'v_w_ada': out['v_w_ada'], 'v_b_ada': out['v_b_ada'], 'v_norm_ffn1': out['v_norm_ffn1'], 'v_w_ffn1_in': out['v_w_ffn1_in'], 'v_w_ffn1_out': out['v_w_ffn1_out'], 'v_norm_mix': out['v_norm_mix'], 'v_w_in': out['v_w_in'], 'v_conv_w': out['v_conv_w'], 'v_conv_b': out['v_conv_b'], 'v_w_rg_gate': out['v_w_rg_gate'], 'v_b_rg_gate': out['v_b_rg_gate'], 'v_w_in_gate': out['v_w_in_gate'], 'v_b_in_gate': out['v_b_in_gate'], 'v_lru_lambda': out['v_lru_lambda'], 'v_w_branch_attn': out['v_w_branch_attn'], 'v_w_branch_lru': out['v_w_branch_lru'], 'v_w_out': out['v_w_out'], 'v_norm_ffn2': out['v_norm_ffn2'], 'v_w_ffn2_in': out['v_w_ffn2_in'], 'v_w_ffn2_out': out['v_w_ffn2_out'], 'v_norm_final': out['v_norm_final']}


def _loss(weights, diff, rest, loss_target):
    with _jax.named_scope("forward"):
        args = {**rest, TWIN_DIFF_INPUT: diff, **{k: w.astype(_WEIGHT_DTYPES[k]) for k, w in weights.items()}}
        y = _forward(args)
    with _jax.named_scope("loss_head"):
        err = _jnp.square(y.astype(_jnp.float32) - loss_target)
        return 0.5 * _jnp.sum(_jnp.mean(err, axis=-1)) if err.ndim else 0.5 * err


def _adamw(w, g, m, v):
    m = ADAM_B1 * m + (1.0 - ADAM_B1) * g
    v = ADAM_B2 * v + (1.0 - ADAM_B2) * _jnp.square(g)
    m_hat = m / (1.0 - ADAM_B1 ** ADAM_STEP)
    v_hat = v / (1.0 - ADAM_B2 ** ADAM_STEP)
    delta = -ADAM_LR * (m_hat / (_jnp.sqrt(v_hat) + ADAM_EPS) + ADAM_WD * w)
    return delta, m, v


def reference(x, c, w_ada, b_ada, norm_ffn1, w_ffn1_in, w_ffn1_out, norm_mix, w_in, conv_w, conv_b, w_rg_gate, b_rg_gate, w_in_gate, b_in_gate, lru_lambda, w_branch_attn, w_branch_lru, w_out, norm_ffn2, w_ffn2_in, w_ffn2_out, norm_final, loss_target, m_w_ada, m_b_ada, m_norm_ffn1, m_w_ffn1_in, m_w_ffn1_out, m_norm_mix, m_w_in, m_conv_w, m_conv_b, m_w_rg_gate, m_b_rg_gate, m_w_in_gate, m_b_in_gate, m_lru_lambda, m_w_branch_attn, m_w_branch_lru, m_w_out, m_norm_ffn2, m_w_ffn2_in, m_w_ffn2_out, m_norm_final, v_w_ada, v_b_ada, v_norm_ffn1, v_w_ffn1_in, v_w_ffn1_out, v_norm_mix, v_w_in, v_conv_w, v_conv_b, v_w_rg_gate, v_b_rg_gate, v_w_in_gate, v_b_in_gate, v_lru_lambda, v_w_branch_attn, v_w_branch_lru, v_w_out, v_norm_ffn2, v_w_ffn2_in, v_w_ffn2_out, v_norm_final):
    given = dict(x=x, c=c, w_ada=w_ada, b_ada=b_ada, norm_ffn1=norm_ffn1, w_ffn1_in=w_ffn1_in, w_ffn1_out=w_ffn1_out, norm_mix=norm_mix, w_in=w_in, conv_w=conv_w, conv_b=conv_b, w_rg_gate=w_rg_gate, b_rg_gate=b_rg_gate, w_in_gate=w_in_gate, b_in_gate=b_in_gate, lru_lambda=lru_lambda, w_branch_attn=w_branch_attn, w_branch_lru=w_branch_lru, w_out=w_out, norm_ffn2=norm_ffn2, w_ffn2_in=w_ffn2_in, w_ffn2_out=w_ffn2_out, norm_final=norm_final, loss_target=loss_target, m_w_ada=m_w_ada, m_b_ada=m_b_ada, m_norm_ffn1=m_norm_ffn1, m_w_ffn1_in=m_w_ffn1_in, m_w_ffn1_out=m_w_ffn1_out, m_norm_mix=m_norm_mix, m_w_in=m_w_in, m_conv_w=m_conv_w, m_conv_b=m_conv_b, m_w_rg_gate=m_w_rg_gate, m_b_rg_gate=m_b_rg_gate, m_w_in_gate=m_w_in_gate, m_b_in_gate=m_b_in_gate, m_lru_lambda=m_lru_lambda, m_w_branch_attn=m_w_branch_attn, m_w_branch_lru=m_w_branch_lru, m_w_out=m_w_out, m_norm_ffn2=m_norm_ffn2, m_w_ffn2_in=m_w_ffn2_in, m_w_ffn2_out=m_w_ffn2_out, m_norm_final=m_norm_final, v_w_ada=v_w_ada, v_b_ada=v_b_ada, v_norm_ffn1=v_norm_ffn1, v_w_ffn1_in=v_w_ffn1_in, v_w_ffn1_out=v_w_ffn1_out, v_norm_mix=v_norm_mix, v_w_in=v_w_in, v_conv_w=v_conv_w, v_conv_b=v_conv_b, v_w_rg_gate=v_w_rg_gate, v_b_rg_gate=v_b_rg_gate, v_w_in_gate=v_w_in_gate, v_b_in_gate=v_b_in_gate, v_lru_lambda=v_lru_lambda, v_w_branch_attn=v_w_branch_attn, v_w_branch_lru=v_w_branch_lru, v_w_out=v_w_out, v_norm_ffn2=v_norm_ffn2, v_w_ffn2_in=v_w_ffn2_in, v_w_ffn2_out=v_w_ffn2_out, v_norm_final=v_norm_final)
    weights = {n: given[n] for n in TWIN_WEIGHTS}
    shared = {n: given[n] for n in SHARED_INPUTS}
    per_example = {n: given[n] for n in ['x', 'c']}
    grad_fn = _jax.value_and_grad(_loss, argnums=(0, 1))

    def one_microbatch(ex, loss_target):
        ex = dict(ex)
        diff = ex.pop(TWIN_DIFF_INPUT)
        return grad_fn(weights, diff, {**shared, **ex}, loss_target)

    if N_MICROBATCH == 1:
        loss, (grad_w, grad_x) = one_microbatch(per_example, given["loss_target"])
    else:
        def body(carry, xs):
            loss_sum, grad_sum = carry
            l_k, (gw_k, gx_k) = one_microbatch(xs[0], xs[1])
            with _jax.named_scope("update"):
                return (loss_sum + l_k, _jax.tree.map(_jnp.add, grad_sum, gw_k)), gx_k

        init = (_jnp.zeros((), _jnp.float32), _jax.tree.map(_jnp.zeros_like, weights))
        (loss, grad_w), grad_x = _jax.lax.scan(body, init, (per_example, given["loss_target"]))
    with _jax.named_scope("update"):
        delta_w, new_m, new_v = {}, {}, {}
        for n in TWIN_WEIGHTS:
            delta_w[n], new_m[n], new_v[n] = _adamw(weights[n], grad_w[n], given["m_" + n], given["v_" + n])
    return (loss, grad_x, *[grad_w[n] for n in TWIN_WEIGHTS], *[delta_w[n] for n in TWIN_WEIGHTS],
            *[new_m[n] for n in TWIN_WEIGHTS], *[new_v[n] for n in TWIN_WEIGHTS])
```

```python
import functools
import math

import jax
import jax.numpy as jnp
from jax import lax
from jax.experimental import pallas as pl
from jax.experimental.pallas import tpu as pltpu

F32 = jnp.float32
BF16 = jnp.bfloat16
N_DEV = 8
HEAD_DIM = 128
CONV_WIDTH = 4
CONV_HALO = 8
LRU_C = 8.0
EPS = 1e-6
ADAM_LR, ADAM_B1, ADAM_B2, ADAM_EPS, ADAM_WD, ADAM_STEP = 0.001, 0.9, 0.999, 1e-08, 0.01, 10
LANE = 128
VMEM_LIMIT = 56 * 1024 * 1024
MESH = pl.DeviceIdType.MESH

NT = (((1,), (1,)), ((), ()))
NN = (((1,), (0,)), ((), ()))
TN = (((0,), (0,)), ((), ()))


def _tile(dim, target, align=LANE):
    t = (min(target, dim) // align) * align
    while t >= align:
        if dim % t == 0:
            return t
        t -= align
    return dim


def _params(sem):
    return pltpu.CompilerParams(dimension_semantics=sem, vmem_limit_bytes=VMEM_LIMIT)


def _sigmoid(x):
    return 1.0 / (1.0 + jnp.exp(-x))


def _softplus(x):
    return jnp.maximum(x, 0.0) + jnp.log(1.0 + jnp.exp(-jnp.abs(x)))


def _log1p(z):
    w = 1.0 + z
    return jnp.where(w == 1.0, z, jnp.log(w) * z / jnp.where(w == 1.0, 1.0, w - 1.0))


def _expm1(x):
    poly = x * (1.0 + x * (0.5 + x * (1.0 / 6 + x * (1.0 / 24 + x * (1.0 / 120 + x * (1.0 / 720))))))
    return jnp.where(jnp.abs(x) < 0.25, poly, jnp.exp(x) - 1.0)


_GELU_C = math.sqrt(2.0 / math.pi)


def _gelu_and_grad(x):
    inner = _GELU_C * (x + 0.044715 * x * x * x)
    th = jnp.tanh(inner)
    val = 0.5 * x * (1.0 + th)
    grad = 0.5 * (1.0 + th) + 0.5 * x * (1.0 - th * th) * _GELU_C * (1.0 + 3 * 0.044715 * x * x)
    return val, grad


def _dot_split(x, u):
    hi = x.astype(BF16)
    lo = (x - hi.astype(F32)).astype(BF16)
    return jnp.dot(hi, u, preferred_element_type=F32) + jnp.dot(lo, u, preferred_element_type=F32)


def _mesh_position():
    x, y, c = lax.axis_index("x"), lax.axis_index("y"), lax.axis_index("c")
    return x, y, c, 4 * x + 2 * y + c


def _peers(x, y, c):
    out = []
    for mask in range(1, N_DEV):
        px = 1 - x if mask & 4 else x
        py = 1 - y if mask & 2 else y
        pc = 1 - c if mask & 1 else c
        out.append((mask, (px, py, pc), 4 * px + 2 * py + pc))
    return out


def _exchange(name, arrs, scatter):
    n = len(arrs)

    def body(*refs):
        ins, outs = refs[:n], refs[n:2 * n]
        send_sems, recv_sems, local_sems = refs[2 * n:]
        x, y, c, me = _mesh_position()
        peers = _peers(x, y, c)
        waits = []
        for a in range(n):
            mine = ins[a].at[me] if scatter else ins[a]
            local = pltpu.make_async_copy(mine, outs[a].at[me], local_sems.at[a])
            local.start()
            waits.append(local.wait)
            for mask, dev, idx in peers:
                k = a * (N_DEV - 1) + mask - 1
                src = ins[a].at[idx] if scatter else ins[a]
                send = pltpu.make_async_remote_copy(src_ref=src, dst_ref=outs[a].at[me], send_sem=send_sems.at[k],
                                                    recv_sem=recv_sems.at[k], device_id=dev, device_id_type=MESH)
                send.start()
                arrival = pltpu.make_async_remote_copy(src_ref=src, dst_ref=outs[a].at[idx], send_sem=send_sems.at[k],
                                                       recv_sem=recv_sems.at[k], device_id=dev, device_id_type=MESH)
                waits.append(send.wait_send)
                waits.append(arrival.wait_recv)
        for w in waits:
            w()

    any_spec = pl.BlockSpec(memory_space=pl.ANY)
    out_shape = [jax.ShapeDtypeStruct(a.shape if scatter else (N_DEV,) + a.shape, a.dtype) for a in arrs]
    return pl.pallas_call(
        body, name=name, out_shape=out_shape, in_specs=[any_spec] * n, out_specs=[any_spec] * n,
        scratch_shapes=[pltpu.SemaphoreType.DMA((n * (N_DEV - 1),)), pltpu.SemaphoreType.DMA((n * (N_DEV - 1),)),
                        pltpu.SemaphoreType.DMA((n,))],
    )(*arrs)


def _bspec(shape, tr, tc, rc):
    per = shape[-1] // tc
    if len(shape) == 3:
        return pl.BlockSpec((None, tr, tc), lambda i, j, k: (rc(i, j, k)[1] // per, rc(i, j, k)[0], rc(i, j, k)[1] % per))
    return pl.BlockSpec((shape[0], None, tr, tc),
                        lambda i, j, k: (0, rc(i, j, k)[1] // per, rc(i, j, k)[0], rc(i, j, k)[1] % per))


def _ij(i, j, k):
    return i, j


def _row_spec(tn, col_tile_offset=0):
    return pl.BlockSpec((1, tn), lambda i, j, k: (0, j + col_tile_offset))


def _matmul(name, a, b, mode, tm, tn, tk, outs, epilogue=None, extras=()):
    groups = b.shape[0] if b.ndim == 4 else 1
    if mode == "nn":
        m, k_dim, n = a.shape[1], a.shape[0] * a.shape[2], b.shape[-3] * b.shape[-1]
        a_spec = _bspec(a.shape, tm, tk, lambda i, j, k: (i, k))
        b_spec = _bspec(b.shape, tk, tn, lambda i, j, k: (k, j))
        dims = NN
    elif mode == "nt":
        m, k_dim, n = a.shape[1], a.shape[0] * a.shape[2], b.shape[-2]
        a_spec = _bspec(a.shape, tm, tk, lambda i, j, k: (i, k))
        b_spec = _bspec(b.shape, tn, tk, lambda i, j, k: (j, k))
        dims = NT
    else:
        m, k_dim, n = a.shape[0] * a.shape[2], a.shape[1], b.shape[-3] * b.shape[-1]
        a_spec = _bspec(a.shape, tk, tm, lambda i, j, k: (k, i))
        b_spec = _bspec(b.shape, tk, tn, lambda i, j, k: (k, j))
        dims = TN
    assert m % tm == 0 and n % tn == 0 and k_dim % tk == 0, (name, m, n, k_dim, tm, tn, tk)
    nk = k_dim // tk
    n_extra, n_out = len(extras), len(outs)

    def body(*refs):
        a_ref, b_ref = refs[0], refs[1]
        extra_refs = refs[2:2 + n_extra]
        out_refs = refs[2 + n_extra:2 + n_extra + n_out]
        acc_ref = refs[-1]
        k = pl.program_id(2)

        @pl.when(k == 0)
        def _():
            acc_ref[...] = jnp.zeros_like(acc_ref)

        a_tile = a_ref[...].astype(BF16)
        for g in range(groups):
            b_tile = (b_ref[g] if b.ndim == 4 else b_ref[...]).astype(BF16)
            acc_ref[g] += lax.dot_general(a_tile, b_tile, dims, preferred_element_type=F32)

        @pl.when(k == nk - 1)
        def _():
            if epilogue is None:
                out_refs[0][...] = acc_ref[0].astype(out_refs[0].dtype)
            else:
                epilogue(acc_ref, extra_refs, out_refs)

    return pl.pallas_call(
        body, name=name, grid=(m // tm, n // tn, nk),
        in_specs=[a_spec, b_spec] + [s for _, s in extras],
        out_specs=[s for _, _, s in outs],
        out_shape=[jax.ShapeDtypeStruct(shape, dtype) for shape, dtype, _ in outs],
        scratch_shapes=[pltpu.VMEM((groups, tm, tn), F32)],
        compiler_params=_params(("parallel", "parallel", "arbitrary")),
    )(a, b, *[arr for arr, _ in extras])


def _rowwise(name, fn, rows, vecs, outs, accs, tm):
    s = rows[0][0].shape[0]
    n_in, n_out = len(rows) + len(vecs), len(outs)

    def body(*refs):
        i = pl.program_id(0)
        res = fn(*[r[...] for r in refs[:n_in]])
        res = res if isinstance(res, tuple) else (res,)
        out_refs, acc_refs = refs[n_in:n_in + n_out], refs[n_in + n_out:]
        for ref, val in zip(out_refs, res[:n_out]):
            ref[...] = val.astype(ref.dtype)

        @pl.when(i == 0)
        def _():
            for ref in acc_refs:
                ref[...] = jnp.zeros_like(ref)

        for ref, val in zip(acc_refs, res[n_out:]):
            ref[...] += val

    in_specs = [pl.BlockSpec((tm, w), functools.partial(lambda i, cb: (i, cb), cb=cb)) for _, w, cb in rows]
    in_specs += [pl.BlockSpec(v.shape, lambda i: (0,) * v.ndim) for v in vecs]
    out_specs = [pl.BlockSpec((tm, w), lambda i: (i, 0)) for w, _ in outs] + [pl.BlockSpec((1, w), lambda i: (0, 0)) for w in accs]
    out_shape = [jax.ShapeDtypeStruct((s, w), dt) for w, dt in outs] + [jax.ShapeDtypeStruct((1, w), F32) for w in accs]
    return pl.pallas_call(
        body, name=name, grid=(s // tm,), in_specs=in_specs, out_specs=out_specs, out_shape=out_shape,
        compiler_params=_params(("arbitrary",)),
    )(*[r for r, _, _ in rows], *vecs)


def _colsum(v):
    return jnp.sum(v, axis=0, keepdims=True)


def _norm_mod(name, h, nw, sc, sh, tm):
    d = h.shape[1]

    def fn(hb, nwb, scb, shb):
        r = lax.rsqrt(jnp.mean(hb * hb, axis=-1, keepdims=True) + EPS)
        return (hb * r) * nwb * (1.0 + scb) + shb

    return _rowwise(name, fn, [(h, d, 0)], [nw, sc, sh], [(d, BF16)], [], tm)[0]


def _norm_mod_bwd(name, dy, h, dh_next, nw, sc, tm):
    d = h.shape[1]

    def fn(dyb, hb, dhb, nwb, scb):
        r = lax.rsqrt(jnp.mean(hb * hb, axis=-1, keepdims=True) + EPS)
        xh = hb * r
        dxh = dyb * (nwb * (1.0 + scb))
        dx = r * (dxh - xh * jnp.mean(dxh * xh, axis=-1, keepdims=True))
        return dhb + dx, _colsum(dyb), _colsum(dyb * xh * nwb), _colsum(dyb * xh * (1.0 + scb))

    return _rowwise(name, fn, [(dy, d, 0), (h, d, 0), (dh_next, d, 0)], [nw, sc], [(d, F32)], [d, d, d], tm)


def _gate_bwd(name, dh, o, g, coef, tm):
    d = dh.shape[1]

    def fn(dhb, ob, gb):
        return dhb * (coef * gb), _colsum(dhb * ob * coef)

    return _rowwise(name, fn, [(dh, d, 0), (o, d, 0)], [g], [(d, BF16)], [d], tm)


def _loss_bwd(name, h, target, nw, tm):
    d = h.shape[1]

    def fn(hb, tb, nwb):
        r = lax.rsqrt(jnp.mean(hb * hb, axis=-1, keepdims=True) + EPS)
        xh = hb * r
        err = xh * nwb - tb
        dy = err * (1.0 / d)
        dxh = dy * nwb
        dx = r * (dxh - xh * jnp.mean(dxh * xh, axis=-1, keepdims=True))
        loss = 0.5 * jnp.sum(jnp.mean(err * err, axis=-1, keepdims=True), axis=0, keepdims=True)
        return dx, jnp.broadcast_to(loss, (1, LANE)), _colsum(dy * xh)

    return _rowwise(name, fn, [(h, d, 0), (target, d, 0)], [nw], [(d, F32)], [LANE, d], tm)


def _attn_fwd(proj, nh, t):
    s = proj.shape[0]
    scale = HEAD_DIM ** -0.5

    def body(q_ref, k_ref, v_ref, y_ref, tot_ref):
        i = pl.program_id(1)
        qb = q_ref[...].astype(BF16)
        row = lax.broadcasted_iota(jnp.int32, (t, t), 0)
        col = lax.broadcasted_iota(jnp.int32, (t, t), 1)
        later = (row > col).astype(BF16)

        def step(jj, carry):
            o, run = carry
            j = i - jj
            ks = pl.ds(pl.multiple_of(j * t, t), t)
            kb = k_ref[ks, :].astype(BF16)
            vb = v_ref[ks, :].astype(BF16)
            z = lax.dot_general(qb, kb, NT, preferred_element_type=F32) * scale
            sp = _softplus(z)
            mask = (col + j * t) < (row + i * t)
            log_keep = jnp.where(mask, -sp, 0.0)
            between = _dot_split(log_keep, later) + run
            w = jnp.where(mask, jnp.exp(z - sp + between), 0.0)
            o = o + jnp.dot(w.astype(BF16), vb, preferred_element_type=F32)
            return o, run + jnp.sum(log_keep, axis=1, keepdims=True)

        o, run = lax.fori_loop(0, i + 1, step, (jnp.zeros((t, HEAD_DIM), F32), jnp.zeros((t, 1), F32)))
        y_ref[...] = o.astype(y_ref.dtype)
        tot_ref[...] = jnp.broadcast_to(run, (t, HEAD_DIM))

    return pl.pallas_call(
        body, name="attn_fwd", grid=(nh, s // t),
        in_specs=[pl.BlockSpec((t, HEAD_DIM), lambda h, i: (i, h)),
                  pl.BlockSpec((s, HEAD_DIM), lambda h, i: (0, nh + h)),
                  pl.BlockSpec((s, HEAD_DIM), lambda h, i: (0, 2 * nh + h))],
        out_specs=[pl.BlockSpec((t, HEAD_DIM), lambda h, i: (i, h)), pl.BlockSpec((t, HEAD_DIM), lambda h, i: (i, h))],
        out_shape=[jax.ShapeDtypeStruct((s, nh * HEAD_DIM), BF16), jax.ShapeDtypeStruct((s, nh * HEAD_DIM), F32)],
        compiler_params=_params(("parallel", "arbitrary")),
    )(proj, proj, proj)


def _attn_bwd(proj, dy, tot, nh, t):
    s = proj.shape[0]
    scale = HEAD_DIM ** -0.5

    def body(q_ref, k_ref, v_ref, dy_ref, tot_ref, dq_ref, dk_ref, dv_ref):
        i = pl.program_id(1)

        @pl.when(i == 0)
        def _():
            dk_ref[...] = jnp.zeros_like(dk_ref)
            dv_ref[...] = jnp.zeros_like(dv_ref)

        qb = q_ref[...].astype(BF16)
        dyb = dy_ref[...].astype(BF16)
        total = tot_ref[:, :1]
        row = lax.broadcasted_iota(jnp.int32, (t, t), 0)
        col = lax.broadcasted_iota(jnp.int32, (t, t), 1)
        upto = (row <= col).astype(BF16)
        before = (row < col).astype(BF16)

        def step(j, carry):
            dq, passed, g_run = carry
            ks = pl.ds(pl.multiple_of(j * t, t), t)
            kb = k_ref[ks, :].astype(BF16)
            vb = v_ref[ks, :].astype(BF16)
            z = lax.dot_general(qb, kb, NT, preferred_element_type=F32) * scale
            sp = _softplus(z)
            mask = (col + j * t) < (row + i * t)
            log_keep = jnp.where(mask, -sp, 0.0)
            between = total - passed - _dot_split(log_keep, upto)
            w = jnp.where(mask, jnp.exp(z - sp + between), 0.0)
            dw = lax.dot_general(dyb, vb, NT, preferred_element_type=F32)
            g = dw * w
            g_before = g_run + _dot_split(g, before)
            dz = jnp.where(mask, g * jnp.exp(-sp) - jnp.exp(z - sp) * g_before, 0.0) * scale
            dzb = dz.astype(BF16)
            dq = dq + jnp.dot(dzb, kb, preferred_element_type=F32)
            dk_ref[ks, :] += lax.dot_general(dzb, qb, TN, preferred_element_type=F32)
            dv_ref[ks, :] += lax.dot_general(w.astype(BF16), dyb, TN, preferred_element_type=F32)
            return (dq, passed + jnp.sum(log_keep, axis=1, keepdims=True), g_run + jnp.sum(g, axis=1, keepdims=True))

        zero = jnp.zeros((t, 1), F32)
        dq, _, _ = lax.fori_loop(0, i + 1, step, (jnp.zeros((t, HEAD_DIM), F32), zero, zero))
        dq_ref[...] = dq

    tile = lambda off: pl.BlockSpec((t, HEAD_DIM), lambda h, i: (i, off + h))
    head = lambda off: pl.BlockSpec((s, HEAD_DIM), lambda h, i: (0, off + h))
    return pl.pallas_call(
        body, name="attn_bwd", grid=(nh, s // t),
        in_specs=[tile(0), head(nh), head(2 * nh), tile(0), tile(0)],
        out_specs=[tile(0), head(0), head(0)],
        out_shape=[jax.ShapeDtypeStruct((s, nh * HEAD_DIM), F32)] * 3,
        compiler_params=_params(("parallel", "arbitrary")),
    )(proj, proj, proj, dy, tot)


def _lru_gates(xc, w_r, b_r, w_i, b_i, lam):
    xb = xc.astype(BF16)
    r = _sigmoid(jnp.dot(xb, w_r.astype(BF16), preferred_element_type=F32) + b_r)
    i = _sigmoid(jnp.dot(xb, w_i.astype(BF16), preferred_element_type=F32) + b_i)
    neg_lam = -lam
    sp_lam = jnp.maximum(neg_lam, 0.0) + _log1p(jnp.exp(-jnp.abs(neg_lam)))
    log_a = -LRU_C * r * sp_lam
    a = jnp.exp(log_a)
    mult = jnp.sqrt(-_expm1(2.0 * log_a))
    return r, i, sp_lam, a, mult


def _conv_taps(xpad_chunk, conv_w, t):
    shifted = [xpad_chunk[CONV_HALO:, :]]
    for d in range(1, CONV_WIDTH):
        shifted.append(pltpu.roll(xpad_chunk, d, 0)[CONV_HALO:, :])
    weights = [conv_w[CONV_WIDTH - 1 - d:CONV_WIDTH - d, :] for d in range(CONV_WIDTH)]
    return shifted, weights


def _lru_fwd(xr_pad, proj, gr_block0, conv_w, conv_b, w_r, b_r, w_i, b_i, lam, t):
    s, w = xr_pad.shape[0] - CONV_HALO, xr_pad.shape[1]
    nblk = w // LANE
    nchunk = s // t
    steps = [1 << p for p in range(t.bit_length() - 1)]
    assert (1 << (t.bit_length() - 1)) == t and w_r.shape[1:] == (LANE, LANE)

    def body(x_ref, gr_ref, cw_ref, cb_ref, wr_ref, br_ref, wi_ref, bi_ref, lam_ref, h_ref, hp_ref, xc_ref, y_ref):
        row = lax.broadcasted_iota(jnp.int32, (t, LANE), 0)

        def chunk(ci, h_in):
            t0 = pl.multiple_of(ci * t, t)
            shifted, weights = _conv_taps(x_ref[pl.ds(t0, t + CONV_HALO), :], cw_ref[...], t)
            xc = cb_ref[...] + sum(wd * xs for wd, xs in zip(weights, shifted))
            r, i, _, a, mult = _lru_gates(xc, wr_ref[...], br_ref[...], wi_ref[...], bi_ref[...], lam_ref[...])
            coef, val = a, mult * (i * xc)
            for d in steps:
                ok = row >= d
                val = jnp.where(ok, coef * pltpu.roll(val, d, 0) + val, val)
                coef = jnp.where(ok, coef * pltpu.roll(coef, d, 0), coef)
            h = val + coef * h_in
            rows = pl.ds(t0, t)
            h_ref[rows, :] = h
            hp_ref[rows, :] = jnp.where(row == 0, h_in, pltpu.roll(h, 1, 0))
            xc_ref[rows, :] = xc
            y_ref[rows, :] = (h * _gelu_and_grad(gr_ref[rows, :])[0]).astype(y_ref.dtype)
            return h[t - 1:t, :]

        lax.fori_loop(0, nchunk, chunk, jnp.zeros((1, LANE), F32))

    col = lambda rows: pl.BlockSpec((rows, LANE), lambda n: (0, n))
    return pl.pallas_call(
        body, name="lru_fwd", grid=(nblk,),
        in_specs=[col(s + CONV_HALO), pl.BlockSpec((s, LANE), lambda n: (0, gr_block0 + n)), col(CONV_WIDTH), col(1),
                  pl.BlockSpec((None, LANE, LANE), lambda n: (n, 0, 0)), col(1),
                  pl.BlockSpec((None, LANE, LANE), lambda n: (n, 0, 0)), col(1), col(1)],
        out_specs=[col(s)] * 4,
        out_shape=[jax.ShapeDtypeStruct((s, w), F32)] * 3 + [jax.ShapeDtypeStruct((s, w), BF16)],
        compiler_params=_params(("parallel",)),
    )(xr_pad, proj, conv_w, conv_b, w_r, b_r, w_i, b_i, lam)


def _lru_bwd(dy, proj, gr_block0, h, h_prev, xc, w_r, b_r, w_i, b_i, lam, t):
    s, w = dy.shape
    nblk = w // LANE
    nchunk = s // t
    steps = [1 << p for p in range(t.bit_length() - 1)]

    def body(dy_ref, gr_ref, h_ref, hp_ref, xc_ref, wr_ref, br_ref, wi_ref, bi_ref, lam_ref,
             dgr_ref, dxc_ref, dwr_ref, dwi_ref, dbr_ref, dbi_ref, dlam_ref):
        row = lax.broadcasted_iota(jnp.int32, (t, LANE), 0)
        for ref in (dwr_ref, dwi_ref, dbr_ref, dbi_ref, dlam_ref):
            ref[...] = jnp.zeros_like(ref)

        def chunk(cc, carry):
            lam_next, a_next = carry
            rows = pl.ds(pl.multiple_of((nchunk - 1 - cc) * t, t), t)
            dyb, hb, xcb = dy_ref[rows, :], h_ref[rows, :], xc_ref[rows, :]
            gel, dgel = _gelu_and_grad(gr_ref[rows, :])
            dgr_ref[rows, :] = dyb * hb * dgel
            w_r, w_i = wr_ref[...], wi_ref[...]
            r, i, sp_lam, a, mult = _lru_gates(xcb, w_r, br_ref[...], w_i, bi_ref[...], lam_ref[...])
            coef = jnp.where(row == t - 1, a_next, pltpu.roll(a, t - 1, 0))
            val = dyb * gel
            for d in steps:
                ok = row < t - d
                val = jnp.where(ok, coef * pltpu.roll(val, t - d, 0) + val, val)
                coef = jnp.where(ok, coef * pltpu.roll(coef, t - d, 0), coef)
            adj = val + coef * lam_next
            da = adj * hp_ref[rows, :]
            v = i * xcb
            dmult, dv = adj * v, adj * mult
            dlog_a = da * a - (a * a) * dmult / mult
            dr_pre = (-LRU_C * sp_lam) * dlog_a * r * (1.0 - r)
            di_pre = dv * xcb * i * (1.0 - i)
            dlam_ref[...] += _colsum(-LRU_C * r * dlog_a)
            dbr_ref[...] += _colsum(dr_pre)
            dbi_ref[...] += _colsum(di_pre)
            xb, drb, dib = xcb.astype(BF16), dr_pre.astype(BF16), di_pre.astype(BF16)
            dwr_ref[...] += lax.dot_general(xb, drb, TN, preferred_element_type=F32)
            dwi_ref[...] += lax.dot_general(xb, dib, TN, preferred_element_type=F32)
            dxc_ref[rows, :] = (dv * i + lax.dot_general(drb, w_r.astype(BF16), NT, preferred_element_type=F32)
                                + lax.dot_general(dib, w_i.astype(BF16), NT, preferred_element_type=F32))
            return adj[0:1, :], a[0:1, :]

        lax.fori_loop(0, nchunk, chunk, (jnp.zeros((1, LANE), F32), jnp.zeros((1, LANE), F32)))
        dlam_ref[...] = dlam_ref[...] * (-_sigmoid(-lam_ref[...]))

    col = lambda rows: pl.BlockSpec((rows, LANE), lambda n: (0, n))
    mat = pl.BlockSpec((None, LANE, LANE), lambda n: (n, 0, 0))
    return pl.pallas_call(
        body, name="lru_bwd", grid=(nblk,),
        in_specs=[col(s), pl.BlockSpec((s, LANE), lambda n: (0, gr_block0 + n)), col(s), col(s), col(s),
                  mat, col(1), mat, col(1), col(1)],
        out_specs=[col(s), col(s), mat, mat, col(1), col(1), col(1)],
        out_shape=[jax.ShapeDtypeStruct((s, w), F32)] * 2 + [jax.ShapeDtypeStruct((nblk, LANE, LANE), F32)] * 2
        + [jax.ShapeDtypeStruct((1, w), F32)] * 3,
        compiler_params=_params(("parallel",)),
    )(dy, proj, h, h_prev, xc, w_r, b_r, w_i, b_i, lam)


def _conv_bwd(xr_pad, dxc_pad, conv_w, t):
    s, w = xr_pad.shape[0] - CONV_HALO, xr_pad.shape[1]
    nchunk = s // t

    def body(x_ref, g_ref, cw_ref, dx_ref, dcw_ref, dcb_ref):
        dcw_ref[...] = jnp.zeros_like(dcw_ref)
        dcb_ref[...] = jnp.zeros_like(dcb_ref)

        def chunk(ci, _):
            t0 = pl.multiple_of(ci * t, t)
            shifted, weights = _conv_taps(x_ref[pl.ds(t0, t + CONV_HALO), :], cw_ref[...], t)
            gpad = g_ref[pl.ds(t0, t + CONV_HALO), :]
            g = gpad[:t, :]
            dx = weights[0] * g
            for d in range(1, CONV_WIDTH):
                dx = dx + weights[d] * pltpu.roll(gpad, t + CONV_HALO - d, 0)[:t, :]
            dx_ref[pl.ds(t0, t), :] = dx
            for d in range(CONV_WIDTH):
                dcw_ref[CONV_WIDTH - 1 - d:CONV_WIDTH - d, :] += _colsum(g * shifted[d])
            dcb_ref[...] += _colsum(g)
            return 0

        lax.fori_loop(0, nchunk, chunk, 0)

    col = lambda rows: pl.BlockSpec((rows, LANE), lambda n: (0, n))
    return pl.pallas_call(
        body, name="conv_bwd", grid=(w // LANE,),
        in_specs=[col(s + CONV_HALO), col(s + CONV_HALO), col(CONV_WIDTH)],
        out_specs=[col(s), col(CONV_WIDTH), col(1)],
        out_shape=[jax.ShapeDtypeStruct((s, w), F32), jax.ShapeDtypeStruct((CONV_WIDTH, w), F32),
                   jax.ShapeDtypeStruct((1, w), F32)],
        compiler_params=_params(("parallel",)),
    )(xr_pad, dxc_pad, conv_w)


def _sum_parts(parts_ref):
    g = parts_ref[0].astype(F32)
    for p in range(1, parts_ref.shape[0]):
        g = g + parts_ref[p].astype(F32)
    return g


def _reduce_parts(name, parts):
    p, r, c = parts.shape
    tr = _tile(r, max(8, (1 << 19) // c), 8)

    def body(parts_ref, g_ref):
        g_ref[...] = _sum_parts(parts_ref)

    return pl.pallas_call(
        body, name=name, grid=(r // tr,), in_specs=[pl.BlockSpec((p, tr, c), lambda i: (0, i, 0))],
        out_specs=pl.BlockSpec((tr, c), lambda i: (i, 0)), out_shape=jax.ShapeDtypeStruct((r, c), F32),
        compiler_params=_params(("parallel",)),
    )(parts)


def _adamw(name, parts, w, m, v):
    p, r, c = parts.shape
    tr = _tile(r, max(8, (1 << 18) // c), 8)

    def body(parts_ref, w_ref, m_ref, v_ref, g_ref, d_ref, nm_ref, nv_ref):
        g = _sum_parts(parts_ref)
        nm = ADAM_B1 * m_ref[...] + (1.0 - ADAM_B1) * g
        nv = ADAM_B2 * v_ref[...] + (1.0 - ADAM_B2) * (g * g)
        m_hat = nm / (1.0 - ADAM_B1 ** ADAM_STEP)
        v_hat = nv / (1.0 - ADAM_B2 ** ADAM_STEP)
        g_ref[...] = g
        d_ref[...] = -ADAM_LR * (m_hat / (jnp.sqrt(v_hat) + ADAM_EPS) + ADAM_WD * w_ref[...])
        nm_ref[...] = nm
        nv_ref[...] = nv

    blk = pl.BlockSpec((tr, c), lambda i: (i, 0))
    return pl.pallas_call(
        body, name=name, grid=(r // tr,), in_specs=[pl.BlockSpec((p, tr, c), lambda i: (0, i, 0)), blk, blk, blk],
        out_specs=[blk] * 4, out_shape=[jax.ShapeDtypeStruct((r, c), F32)] * 4,
        compiler_params=_params(("parallel",)),
    )(parts, w, m, v)


def _ffn_fwd(tag, y, w_in_g, w_out_g, res, gate, tm):
    s, d = y.shape
    half = N_DEV // 2
    cb = w_in_g.shape[2]
    ff = half * cb
    tk = _tile(d, 512)

    def swiglu(acc_ref, extra_refs, out_refs):
        g, u = acc_ref[0], acc_ref[1]
        out_refs[0][0] = g
        out_refs[0][1] = u
        out_refs[1][...] = (g * _sigmoid(g) * u).astype(BF16)

    gu_shape = (2, half, s, cb)
    gu, act = _matmul(
        tag + "_in", y[None], w_in_g.reshape(2, half, d, cb), "nn", tm, cb, tk,
        outs=[(gu_shape, F32, _bspec(gu_shape, tm, cb, _ij)), ((half, s, cb), BF16, _bspec((half, s, cb), tm, cb, _ij))],
        epilogue=swiglu)

    tn = _tile(d, 1024)

    def residual(acc_ref, extra_refs, out_refs):
        o = acc_ref[0]
        out_refs[0][...] = o
        out_refs[1][...] = extra_refs[0][...] + 0.5 * extra_refs[1][...] * o

    plain = _bspec((1, s, d), tm, tn, _ij)
    o, h_new = _matmul(
        tag + "_out", act, w_out_g.reshape(1, ff, d), "nn", tm, tn, cb,
        outs=[((1, s, d), F32, plain), ((1, s, d), F32, plain)], epilogue=residual,
        extras=[(res[None], plain), (gate, _row_spec(tn))])
    return gu, act, o[0], h_new[0]


def _ffn_bwd(tag, dh, y, gu, act, o, gate, w_in_g, w_out_g, tm):
    s, d = dh.shape
    half = N_DEV // 2
    cb = w_in_g.shape[2]
    ff = half * cb
    do, dgate = _gate_bwd(tag + "_gate_bwd", dh, o, gate, 0.5, _tile(s, 256, 8))

    tn_d = _tile(d, 1024)
    dw_out = _matmul(tag + "_dw_out", act, do[None], "tn", cb, tn_d, _tile(s, 512),
                     outs=[((1, ff, d), BF16, _bspec((1, ff, d), cb, tn_d, _ij))])[0]

    def dswiglu(acc_ref, extra_refs, out_refs):
        dact = acc_ref[0]
        g, u = extra_refs[0][0], extra_refs[0][1]
        sg = _sigmoid(g)
        out_refs[0][0] = (dact * u * sg * (1.0 + g * (1.0 - sg))).astype(BF16)
        out_refs[0][1] = (dact * g * sg).astype(BF16)

    gu_shape = (2, half, s, cb)
    gu_spec = _bspec(gu_shape, tm, cb, _ij)
    dgu = _matmul(tag + "_dact", do[None], w_out_g.reshape(1, ff, d), "nt", tm, cb, _tile(d, 1024),
                  outs=[(gu_shape, BF16, gu_spec)], epilogue=dswiglu, extras=[(gu, gu_spec)])[0]
    dgu = dgu.reshape(N_DEV, s, cb)

    tm_d = _tile(d, 512)
    dw_in = _matmul(tag + "_dw_in", y[None], dgu, "tn", tm_d, cb, _tile(s, 512),
                    outs=[((N_DEV, d, cb), BF16, _bspec((N_DEV, d, cb), tm_d, cb, _ij))])[0]
    dy = _matmul(tag + "_dy", dgu, w_in_g, "nt", tm, tn_d, cb,
                 outs=[((1, s, d), F32, _bspec((1, s, d), tm, tn_d, _ij))])[0][0]
    return dy, dgate, dw_in, dw_out.reshape(N_DEV, ff // N_DEV, d)


def kernel(x, c, w_ada, b_ada, norm_ffn1, w_ffn1_in, w_ffn1_out, norm_mix, w_in, conv_w, conv_b, w_rg_gate, b_rg_gate, w_in_gate, b_in_gate, lru_lambda, w_branch_attn, w_branch_lru, w_out, norm_ffn2, w_ffn2_in, w_ffn2_out, norm_final, loss_target, m_w_ada, m_b_ada, m_norm_ffn1, m_w_ffn1_in, m_w_ffn1_out, m_norm_mix, m_w_in, m_conv_w, m_conv_b, m_w_rg_gate, m_b_rg_gate, m_w_in_gate, m_b_in_gate, m_lru_lambda, m_w_branch_attn, m_w_branch_lru, m_w_out, m_norm_ffn2, m_w_ffn2_in, m_w_ffn2_out, m_norm_final, v_w_ada, v_b_ada, v_norm_ffn1, v_w_ffn1_in, v_w_ffn1_out, v_norm_mix, v_w_in, v_conv_w, v_conv_b, v_w_rg_gate, v_b_rg_gate, v_w_in_gate, v_b_in_gate, v_lru_lambda, v_w_branch_attn, v_w_branch_lru, v_w_out, v_norm_ffn2, v_w_ffn2_in, v_w_ffn2_out, v_norm_final):
    xs, target = x[0], loss_target[0]
    s, d = xs.shape
    aw, lw = w_branch_attn.shape[1], w_branch_lru.shape[1]
    nh, nlb = aw // HEAD_DIM, w_rg_gate.shape[1]
    cba, cbi, cbb, cwb = w_ada.shape[2], w_in.shape[2], w_branch_attn.shape[2], conv_w.shape[2]
    assert lw == nlb * LANE and cwb * N_DEV == lw and 3 * aw + 2 * lw + 2 * d == cbi * N_DEV
    me = 4 * lax.axis_index("x") + 2 * lax.axis_index("y") + lax.axis_index("c")
    tm = _tile(s, 512, 8)
    tr = _tile(s, 256, 8)
    t_attn = _tile(s, 256, 8)
    t_lru = _tile(s, 256, 8)

    small = _exchange("gather_c", [jnp.concatenate([c, conv_w.reshape(1, CONV_WIDTH * cwb)], axis=1)], False)[0][:, 0, :]
    c_all = small[:, :d]
    conv_w_full = small[:, d:].reshape(N_DEV, CONV_WIDTH, cwb).transpose(1, 0, 2).reshape(CONV_WIDTH, lw)
    c_act = _rowwise("silu_c", lambda v: v * _sigmoid(v), [(c_all, d, 0)], [], [(d, F32)], [], N_DEV)[0]

    def add_bias(acc_ref, extra_refs, out_refs):
        out_refs[0][...] = acc_ref[0] + extra_refs[0][...]

    b_ada_mine = lax.dynamic_slice(b_ada, (0, me * cba), (1, cba))
    mod_part = _matmul("mod", c_act[None], w_ada, "nn", N_DEV, cba, _tile(d, 512),
                       outs=[((1, N_DEV, cba), F32, _bspec((1, N_DEV, cba), N_DEV, cba, _ij))], epilogue=add_bias,
                       extras=[(b_ada_mine, _row_spec(cba))])[0][0]
    mod_all = _exchange("gather_mod", [mod_part], False)[0]
    mod = lax.dynamic_index_in_dim(mod_all, me, axis=1, keepdims=False).reshape(1, 9 * d)
    sh1, sc1, g1, sh2, sc2, g2, sh3, sc3, g3 = [mod[:, n * d:(n + 1) * d] for n in range(9)]

    shards = [w_ffn1_in[0], w_ffn1_out[0], w_in[0], w_branch_attn[0], w_branch_lru[0], w_out[0], w_ffn2_in[0], w_ffn2_out[0]]
    wf1i, wf1o, wi_g, wba_g, wbl_g, wo_g, wf2i, wf2o = _exchange("gather_w", [w.astype(BF16) for w in shards], False)

    y1 = _norm_mod("norm1", xs, norm_ffn1, sc1, sh1, tr)
    gu1, act1, o1, h1 = _ffn_fwd("ffn1", y1, wf1i, wf1o, xs, g1, tm)

    y2 = _norm_mod("norm2", h1, norm_mix, sc2, sh2, tr)
    tn_i = _tile(cbi, 1152)
    proj = _matmul("mix_in", y2[None], wi_g, "nn", tm, tn_i, _tile(d, 512),
                   outs=[((1, s, N_DEV * cbi), F32, _bspec((1, s, N_DEV * cbi), tm, tn_i, _ij))])[0][0]
    off_xr, off_gr, off_ga, off_gl = 3 * aw, 3 * aw + lw, 3 * aw + 2 * lw, 3 * aw + 2 * lw + d
    y_attn, attn_tot = _attn_fwd(proj, nh, t_attn)
    xr_pad = jnp.pad(proj[:, off_xr:off_xr + lw], ((CONV_HALO, 0), (0, 0)))
    w_r, w_i = w_rg_gate[0], w_in_gate[0]
    h_lru, h_prev, xc, y_lru = _lru_fwd(xr_pad, proj, off_gr // LANE, conv_w_full, conv_b, w_r, b_rg_gate, w_i,
                                        b_in_gate, lru_lambda, t_lru)
    plain_b = _bspec((1, s, d), tm, cbb, _ij)
    ya = _matmul("branch_attn", y_attn[None], wba_g, "nn", tm, cbb, _tile(aw, 1024), outs=[((1, s, d), F32, plain_b)])[0]

    def merge(acc_ref, extra_refs, out_refs):
        yl = acc_ref[0]
        ya_t, ga, gl = extra_refs[0][...], extra_refs[1][...], extra_refs[2][...]
        out_refs[0][...] = yl
        out_refs[1][...] = (_sigmoid(ga) * ya_t + _sigmoid(gl) * yl).astype(BF16)

    proj3 = proj[None]
    ga_spec = _bspec(proj3.shape, tm, cbb, lambda i, j, k: (i, j + off_ga // cbb))
    gl_spec = _bspec(proj3.shape, tm, cbb, lambda i, j, k: (i, j + off_gl // cbb))
    yl, merged = _matmul("branch_lru", y_lru[None], wbl_g, "nn", tm, cbb, _tile(lw, 1024),
                         outs=[((1, s, d), F32, plain_b), ((1, s, d), BF16, plain_b)], epilogue=merge,
                         extras=[(ya, plain_b), (proj3, ga_spec), (proj3, gl_spec)])
    tn_d = _tile(d, 1024)
    plain = _bspec((1, s, d), tm, tn_d, _ij)

    def residual(acc_ref, extra_refs, out_refs):
        o = acc_ref[0]
        out_refs[0][...] = o
        out_refs[1][...] = extra_refs[0][...] + extra_refs[1][...] * o

    mo, h2 = _matmul("mix_out", merged, wo_g.reshape(1, d, d), "nn", tm, tn_d, _tile(d, 512),
                     outs=[((1, s, d), F32, plain), ((1, s, d), F32, plain)], epilogue=residual,
                     extras=[(h1[None], plain), (g2, _row_spec(tn_d))])
    mo, h2 = mo[0], h2[0]

    y3 = _norm_mod("norm3", h2, norm_ffn2, sc3, sh3, tr)
    gu3, act3, o3, h3 = _ffn_fwd("ffn2", y3, wf2i, wf2o, h2, g3, tm)

    nf = norm_final.reshape(1, d)
    dh3, loss_part, d_nf = _loss_bwd("loss", h3, target, nf, tr)
    dy3, dg3, dwf2i, dwf2o = _ffn_bwd("ffn2", dh3, y3, gu3, act3, o3, g3, wf2i, wf2o, tm)
    dh2, dsh3, dsc3, dn3 = _norm_mod_bwd("norm3_bwd", dy3, h2, dh3, norm_ffn2, sc3, tr)

    dmo, dg2 = _gate_bwd("mix_gate_bwd", dh2, mo, g2, 1.0, tr)
    dwo = _matmul("mix_dw_out", merged, dmo[None], "tn", _tile(d, 512), tn_d, _tile(s, 512),
                  outs=[((1, d, d), BF16, _bspec((1, d, d), _tile(d, 512), tn_d, _ij))])[0]

    def dmerge(acc_ref, extra_refs, out_refs):
        dm = acc_ref[0]
        ya_t, yl_t = extra_refs[0][...], extra_refs[1][...]
        sa, sl = _sigmoid(extra_refs[2][...]), _sigmoid(extra_refs[3][...])
        out_refs[0][...] = (dm * sa).astype(BF16)
        out_refs[1][...] = (dm * sl).astype(BF16)
        out_refs[2][...] = (dm * ya_t * sa * (1.0 - sa)).astype(BF16)
        out_refs[3][...] = (dm * yl_t * sl * (1.0 - sl)).astype(BF16)

    bf_plain = ((1, s, d), BF16, plain_b)
    dya, dyl, dga, dgl = _matmul("mix_dmerged", dmo[None], wo_g.reshape(1, d, d), "nt", tm, cbb, _tile(d, 1024),
                                 outs=[bf_plain] * 4, epilogue=dmerge,
                                 extras=[(ya, plain_b), (yl, plain_b), (proj3, ga_spec), (proj3, gl_spec)])
    tm_a, tm_l = _tile(aw, 512), _tile(lw, 512)
    dwba = _matmul("dw_branch_attn", y_attn[None], dya, "tn", tm_a, cbb, _tile(s, 512),
                   outs=[((N_DEV, aw, cbb), BF16, _bspec((N_DEV, aw, cbb), tm_a, cbb, _ij))])[0]
    dwbl = _matmul("dw_branch_lru", y_lru[None], dyl, "tn", tm_l, cbb, _tile(s, 512),
                   outs=[((N_DEV, lw, cbb), BF16, _bspec((N_DEV, lw, cbb), tm_l, cbb, _ij))])[0]
    tn_a, tn_l = _tile(aw, 1024), _tile(lw, 1024)
    dy_attn = _matmul("d_attn_out", dya, wba_g, "nt", tm, tn_a, cbb,
                      outs=[((1, s, aw), F32, _bspec((1, s, aw), tm, tn_a, _ij))])[0][0]
    dy_lru = _matmul("d_lru_out", dyl, wbl_g, "nt", tm, tn_l, cbb,
                     outs=[((1, s, lw), F32, _bspec((1, s, lw), tm, tn_l, _ij))])[0][0]
    dq, dk, dv = _attn_bwd(proj, dy_attn, attn_tot, nh, t_attn)
    dgr, dxc, d_wr, d_wi, d_br, d_bi, d_lam = _lru_bwd(dy_lru, proj, off_gr // LANE, h_lru, h_prev, xc, w_r, b_rg_gate,
                                                       w_i, b_in_gate, lru_lambda, t_lru)
    dxr, d_cw, d_cb = _conv_bwd(xr_pad, jnp.pad(dxc, ((0, CONV_HALO), (0, 0))), conv_w_full, t_lru)
    dproj = jnp.concatenate([dq.astype(BF16), dk.astype(BF16), dv.astype(BF16), dxr.astype(BF16), dgr.astype(BF16),
                             dga[0], dgl[0]], axis=1)
    tm_d = _tile(d, 512)
    dwi = _matmul("mix_dw_in", y2[None], dproj[None], "tn", tm_d, tn_i, _tile(s, 512),
                  outs=[((N_DEV, d, cbi), BF16, _bspec((N_DEV, d, cbi), tm_d, tn_i, _ij))])[0]
    dy2 = _matmul("mix_dy", dproj[None], wi_g, "nt", tm, tn_d, tn_i,
                  outs=[((1, s, d), F32, plain)])[0][0]
    dh1, dsh2, dsc2, dn2 = _norm_mod_bwd("norm2_bwd", dy2, h1, dh2, norm_mix, sc2, tr)

    dy1, dg1, dwf1i, dwf1o = _ffn_bwd("ffn1", dh1, y1, gu1, act1, o1, g1, wf1i, wf1o, tm)
    grad_x, dsh1, dsc1, dn1 = _norm_mod_bwd("norm1_bwd", dy1, xs, dh1, norm_ffn1, sc1, tr)

    lane_pad = jnp.zeros((1, 7 * LANE), F32)
    pack = jnp.concatenate(
        [loss_part, lane_pad, dsh1, dsc1, dg1, dsh2, dsc2, dg2, dsh3, dsc3, dg3, dn1, dn2, dn3, d_nf, d_cb, d_br, d_bi, d_lam,
         d_wr.reshape(1, -1), d_wi.reshape(1, -1), d_cw.reshape(1, -1)], axis=1)
    pack = jnp.pad(pack, ((0, 0), (0, -pack.shape[1] % (8 * LANE))))
    n_pack = pack.shape[1]
    packs = _exchange("gather_small", [pack], False)[0].reshape(N_DEV, n_pack // LANE, LANE)
    g_pack = _reduce_parts("sum_small", packs).reshape(1, n_pack)
    loss = g_pack[0, 0]
    off = 8 * LANE
    n_adam = 9 * d + 4 * d + 4 * lw + 2 * nlb * LANE * LANE
    g_small = g_pack[:, off:off + n_adam].reshape(1, n_adam // LANE, LANE)
    d_cw_sum = g_pack[:, off + n_adam:off + n_adam + CONV_WIDTH * lw].reshape(CONV_WIDTH, lw)
    d_cw_mine = lax.dynamic_slice(d_cw_sum, (0, me * cwb), (CONV_WIDTH, cwb))

    small_names = ["b_ada", "norm_ffn1", "norm_mix", "norm_ffn2", "norm_final", "conv_b", "b_rg_gate", "b_in_gate",
                   "lru_lambda", "w_rg_gate", "w_in_gate"]
    given = dict(b_ada=(b_ada, m_b_ada, v_b_ada), norm_ffn1=(norm_ffn1, m_norm_ffn1, v_norm_ffn1),
                 norm_mix=(norm_mix, m_norm_mix, v_norm_mix), norm_ffn2=(norm_ffn2, m_norm_ffn2, v_norm_ffn2),
                 norm_final=(norm_final, m_norm_final, v_norm_final), conv_b=(conv_b, m_conv_b, v_conv_b),
                 b_rg_gate=(b_rg_gate, m_b_rg_gate, v_b_rg_gate), b_in_gate=(b_in_gate, m_b_in_gate, v_b_in_gate),
                 lru_lambda=(lru_lambda, m_lru_lambda, v_lru_lambda), w_rg_gate=(w_rg_gate, m_w_rg_gate, v_w_rg_gate),
                 w_in_gate=(w_in_gate, m_w_in_gate, v_w_in_gate))
    packed = [jnp.concatenate([given[n][q].reshape(1, -1) for n in small_names], axis=1).reshape(n_adam // LANE, LANE)
              for q in range(3)]
    small_out = _adamw("adamw_small", g_small, *packed)
    results = {}
    pos = 0
    for n in small_names:
        shape = given[n][0].shape
        size = math.prod(shape)
        results[n] = [o.reshape(1, n_adam)[:, pos:pos + size].reshape(shape) for o in small_out]
        pos += size
    results["conv_w"] = [o.reshape(conv_w.shape) for o in
                         _adamw("adamw_conv_w", d_cw_mine[None], conv_w[0], m_conv_w[0], v_conv_w[0])]

    dmod_all = packs.reshape(N_DEV, n_pack)[:, off:off + 9 * d]
    dmod_mine = lax.dynamic_slice(dmod_all, (0, me * cba), (N_DEV, cba))
    dmod_rows = jnp.pad(dmod_mine, ((0, LANE - N_DEV), (0, 0)))
    c_act_t = jnp.pad(c_act.T, ((0, 0), (0, LANE - N_DEV)))
    tm_d2 = _tile(d, 256)
    d_wada = _matmul("dw_ada", c_act_t[None], dmod_rows[None], "nn", tm_d2, cba, LANE,
                     outs=[((1, d, cba), F32, _bspec((1, d, cba), tm_d2, cba, _ij))])[0]
    results["w_ada"] = [o[None] for o in _adamw("adamw_w_ada", d_wada, w_ada[0], m_w_ada[0], v_w_ada[0])]

    big = [("w_ffn1_in", dwf1i, w_ffn1_in, m_w_ffn1_in, v_w_ffn1_in), ("w_ffn1_out", dwf1o, w_ffn1_out, m_w_ffn1_out, v_w_ffn1_out),
           ("w_in", dwi, w_in, m_w_in, v_w_in), ("w_branch_attn", dwba, w_branch_attn, m_w_branch_attn, v_w_branch_attn),
           ("w_branch_lru", dwbl, w_branch_lru, m_w_branch_lru, v_w_branch_lru),
           ("w_out", dwo.reshape(N_DEV, d // N_DEV, d), w_out, m_w_out, v_w_out),
           ("w_ffn2_in", dwf2i, w_ffn2_in, m_w_ffn2_in, v_w_ffn2_in), ("w_ffn2_out", dwf2o, w_ffn2_out, m_w_ffn2_out, v_w_ffn2_out)]
    received = _exchange("scatter_grads", [b[1] for b in big], True)
    for (n, _, w, m, v), parts in zip(big, received):
        results[n] = [o[None] for o in _adamw("adamw_" + n, parts, w[0], m[0], v[0])]

    order = ["w_ada", "b_ada", "norm_ffn1", "w_ffn1_in", "w_ffn1_out", "norm_mix", "w_in", "conv_w", "conv_b", "w_rg_gate",
             "b_rg_gate", "w_in_gate", "b_in_gate", "lru_lambda", "w_branch_attn", "w_branch_lru", "w_out", "norm_ffn2",
             "w_ffn2_in", "w_ffn2_out", "norm_final"]
    return (loss, grad_x[None], *[results[n][0] for n in order], *[results[n][1] for n in order],
            *[results[n][2] for n in order], *[results[n][3] for n in order])
```

```python
import functools
import math

import jax
import jax.numpy as jnp
from jax import lax
from jax.experimental import pallas as pl
from jax.experimental.pallas import tpu as pltpu

F32 = jnp.float32
BF16 = jnp.bfloat16
N_DEV = 8
HEAD_DIM = 128
CONV_WIDTH = 4
CONV_HALO = 8
LRU_C = 8.0
EPS = 1e-6
ADAM_LR, ADAM_B1, ADAM_B2, ADAM_EPS, ADAM_WD, ADAM_STEP = 0.001, 0.9, 0.999, 1e-08, 0.01, 10
LANE = 128
VMEM_LIMIT = 56 * 1024 * 1024
MESH = pl.DeviceIdType.MESH

NT = (((1,), (1,)), ((), ()))
NN = (((1,), (0,)), ((), ()))
TN = (((0,), (0,)), ((), ()))


def _tile(dim, target, align=LANE):
    t = (min(target, dim) // align) * align
    while t >= align:
        if dim % t == 0:
            return t
        t -= align
    return dim


def _params(sem):
    return pltpu.CompilerParams(dimension_semantics=sem, vmem_limit_bytes=VMEM_LIMIT)


def _sigmoid(x):
    return 1.0 / (1.0 + jnp.exp(-x))


def _softplus(x):
    return jnp.maximum(x, 0.0) + jnp.log(1.0 + jnp.exp(-jnp.abs(x)))


def _log1p(z):
    w = 1.0 + z
    return jnp.where(w == 1.0, z, jnp.log(w) * z / jnp.where(w == 1.0, 1.0, w - 1.0))


def _expm1(x):
    poly = x * (1.0 + x * (0.5 + x * (1.0 / 6 + x * (1.0 / 24 + x * (1.0 / 120 + x * (1.0 / 720))))))
    return jnp.where(jnp.abs(x) < 0.25, poly, jnp.exp(x) - 1.0)


_GELU_C = math.sqrt(2.0 / math.pi)


def _gelu_and_grad(x):
    inner = _GELU_C * (x + 0.044715 * x * x * x)
    th = jnp.tanh(inner)
    val = 0.5 * x * (1.0 + th)
    grad = 0.5 * (1.0 + th) + 0.5 * x * (1.0 - th * th) * _GELU_C * (1.0 + 3 * 0.044715 * x * x)
    return val, grad


def _dot_split(x, u):
    hi = x.astype(BF16)
    lo = (x - hi.astype(F32)).astype(BF16)
    return jnp.dot(hi, u, preferred_element_type=F32) + jnp.dot(lo, u, preferred_element_type=F32)


def _mesh_position():
    x, y, c = lax.axis_index("x"), lax.axis_index("y"), lax.axis_index("c")
    return x, y, c, 4 * x + 2 * y + c


def _peers(x, y, c):
    out = []
    for mask in range(1, N_DEV):
        px = 1 - x if mask & 4 else x
        py = 1 - y if mask & 2 else y
        pc = 1 - c if mask & 1 else c
        out.append((mask, (px, py, pc), 4 * px + 2 * py + pc))
    return out


def _exchange(name, arrs, scatter):
    n = len(arrs)

    def body(*refs):
        ins, outs = refs[:n], refs[n:2 * n]
        send_sems, recv_sems, local_sems = refs[2 * n:]
        x, y, c, me = _mesh_position()
        peers = _peers(x, y, c)
        waits = []
        for a in range(n):
            mine = ins[a].at[me] if scatter else ins[a]
            local = pltpu.make_async_copy(mine, outs[a].at[me], local_sems.at[a])
            local.start()
            waits.append(local.wait)
            for mask, dev, idx in peers:
                k = a * (N_DEV - 1) + mask - 1
                src = ins[a].at[idx] if scatter else ins[a]
                send = pltpu.make_async_remote_copy(src_ref=src, dst_ref=outs[a].at[me], send_sem=send_sems.at[k],
                                                    recv_sem=recv_sems.at[k], device_id=dev, device_id_type=MESH)
                send.start()
                arrival = pltpu.make_async_remote_copy(src_ref=src, dst_ref=outs[a].at[idx], send_sem=send_sems.at[k],
                                                       recv_sem=recv_sems.at[k], device_id=dev, device_id_type=MESH)
                waits.append(send.wait_send)
                waits.append(arrival.wait_recv)
        for w in waits:
            w()

    any_spec = pl.BlockSpec(memory_space=pl.ANY)
    out_shape = [jax.ShapeDtypeStruct(a.shape if scatter else (N_DEV,) + a.shape, a.dtype) for a in arrs]
    return pl.pallas_call(
        body, name=name, out_shape=out_shape, in_specs=[any_spec] * n, out_specs=[any_spec] * n,
        scratch_shapes=[pltpu.SemaphoreType.DMA((n * (N_DEV - 1),)), pltpu.SemaphoreType.DMA((n * (N_DEV - 1),)),
                        pltpu.SemaphoreType.DMA((n,))],
    )(*arrs)


HBM_SPEC = pl.BlockSpec(memory_space=pltpu.HBM)
SEM_SPEC = pl.BlockSpec(memory_space=pltpu.SEMAPHORE)
DATAFLOW = pltpu.SideEffectType.DATAFLOW_SIDE_EFFECTING


def _place_own(name, arrs, scatter):
    n = len(arrs)

    def body(*refs):
        ins, outs, sems = refs[:n], refs[n:2 * n], refs[2 * n]
        me = _mesh_position()[3]
        copies = [pltpu.make_async_copy(ins[a].at[me] if scatter else ins[a], outs[a].at[me], sems.at[a]) for a in range(n)]
        for cp in copies:
            cp.start()
        for cp in copies:
            cp.wait()

    any_spec = pl.BlockSpec(memory_space=pl.ANY)
    out_shape = [jax.ShapeDtypeStruct(a.shape if scatter else (N_DEV,) + a.shape, a.dtype) for a in arrs]
    return pl.pallas_call(body, name=name, out_shape=out_shape, in_specs=[any_spec] * n, out_specs=[any_spec] * n,
                          scratch_shapes=[pltpu.SemaphoreType.DMA((n,))])(*arrs)


def _exchange_start(name, arrs, lands, scatter):
    n = len(arrs)

    def body(*refs):
        srcs, zones, outs = refs[:n], refs[n:2 * n], refs[2 * n:]
        x, y, c, me = _mesh_position()
        for a in range(n):
            send_sems, recv_sems = outs[4 * a], outs[4 * a + 1]
            for mask, dev, idx in _peers(x, y, c):
                pltpu.make_async_remote_copy(
                    src_ref=srcs[a].at[idx] if scatter else srcs[a], dst_ref=zones[a].at[me], send_sem=send_sems.at[mask - 1],
                    recv_sem=recv_sems.at[mask - 1], device_id=dev, device_id_type=MESH).start()
        outs[-1][...] = jnp.zeros_like(outs[-1])

    out_shape, out_specs, aliases = [], [], {}
    for a in range(n):
        out_shape += [pltpu.SemaphoreType.DMA((N_DEV - 1,)), pltpu.SemaphoreType.DMA((N_DEV - 1,)),
                      pltpu.HBM(arrs[a].shape, arrs[a].dtype), pltpu.HBM(lands[a].shape, lands[a].dtype)]
        out_specs += [SEM_SPEC, SEM_SPEC, HBM_SPEC, HBM_SPEC]
        aliases[a] = 4 * a + 2
        aliases[n + a] = 4 * a + 3
    out_shape.append(jax.ShapeDtypeStruct((8, LANE), F32))
    out_specs.append(pl.BlockSpec(memory_space=pltpu.VMEM))
    res = pl.pallas_call(
        body, name=name, out_shape=out_shape, in_specs=[HBM_SPEC] * (2 * n), out_specs=out_specs,
        input_output_aliases=aliases, compiler_params=pltpu.CompilerParams(has_side_effects=DATAFLOW),
    )(*[pltpu.with_memory_space_constraint(v, pltpu.HBM) for v in list(arrs) + list(lands)])
    return [tuple(res[4 * a:4 * a + 4]) for a in range(n)], res[-1]


def _exchange_end(name, handles, after, scatter):
    n = len(handles)

    def body(*refs):
        x, y, c, me = _mesh_position()
        for a in range(n):
            src, zone, send_sems, recv_sems = refs[4 * a:4 * a + 4]
            for mask, dev, idx in _peers(x, y, c):
                cp = pltpu.make_async_remote_copy(
                    src_ref=src.at[idx] if scatter else src, dst_ref=zone.at[idx], send_sem=send_sems.at[mask - 1],
                    recv_sem=recv_sems.at[mask - 1], device_id=dev, device_id_type=MESH)
                cp.wait_send()
                cp.wait_recv()

    operands, in_specs, out_shape, aliases = [], [], [], {}
    for a, (send_sems, recv_sems, src, zone) in enumerate(handles):
        operands += [src, zone, send_sems, recv_sems]
        in_specs += [HBM_SPEC, HBM_SPEC, SEM_SPEC, SEM_SPEC]
        out_shape += [pltpu.HBM(src.shape, src.dtype), pltpu.HBM(zone.shape, zone.dtype)]
        aliases[4 * a] = 2 * a
        aliases[4 * a + 1] = 2 * a + 1
    res = pl.pallas_call(
        body, name=name, out_shape=out_shape, in_specs=in_specs + [pl.BlockSpec(memory_space=pl.ANY)],
        out_specs=[HBM_SPEC] * (2 * n), input_output_aliases=aliases,
        compiler_params=pltpu.CompilerParams(has_side_effects=DATAFLOW),
    )(*operands, after)
    return [res[2 * a + 1] for a in range(n)]


def _exchange_begin(name, arrs, scatter):
    return _exchange_start(name, arrs, _place_own(name + "_own", arrs, scatter), scatter)


def _bspec(shape, tr, tc, rc, buffers=None):
    per = shape[-1] // tc
    mode = {} if buffers is None else dict(pipeline_mode=pl.Buffered(buffers))
    if len(shape) == 3:
        return pl.BlockSpec((None, tr, tc), lambda j, i, k: (rc(i, j, k)[1] // per, rc(i, j, k)[0], rc(i, j, k)[1] % per), **mode)
    return pl.BlockSpec((shape[0], None, tr, tc),
                        lambda j, i, k: (0, rc(i, j, k)[1] // per, rc(i, j, k)[0], rc(i, j, k)[1] % per), **mode)


def _ij(i, j, k):
    return i, j


def _row_spec(tn, col_tile_offset=0):
    return pl.BlockSpec((1, tn), lambda j, i, k: (0, j + col_tile_offset))


def _matmul(name, a, b, mode, tm, tn, tk, outs, epilogue=None, extras=(), b_buffers=None):
    groups = b.shape[0] if b.ndim == 4 else 1
    if mode == "nn":
        m, k_dim, n = a.shape[1], a.shape[0] * a.shape[2], b.shape[-3] * b.shape[-1]
        a_spec = _bspec(a.shape, tm, tk, lambda i, j, k: (i, k))
        b_spec = _bspec(b.shape, tk, tn, lambda i, j, k: (k, j), b_buffers)
        dims = NN
    elif mode == "nt":
        m, k_dim, n = a.shape[1], a.shape[0] * a.shape[2], b.shape[-2]
        a_spec = _bspec(a.shape, tm, tk, lambda i, j, k: (i, k))
        b_spec = _bspec(b.shape, tn, tk, lambda i, j, k: (j, k), b_buffers)
        dims = NT
    else:
        m, k_dim, n = a.shape[0] * a.shape[2], a.shape[1], b.shape[-3] * b.shape[-1]
        a_spec = _bspec(a.shape, tk, tm, lambda i, j, k: (k, i))
        b_spec = _bspec(b.shape, tk, tn, lambda i, j, k: (k, j), b_buffers)
        dims = TN
    assert m % tm == 0 and n % tn == 0 and k_dim % tk == 0, (name, m, n, k_dim, tm, tn, tk)
    nk = k_dim // tk
    n_extra, n_out = len(extras), len(outs)

    def finish(acc, extra_refs, out_refs):
        if epilogue is None:
            out_refs[0][...] = acc[0].astype(out_refs[0].dtype)
        else:
            epilogue(acc, extra_refs, out_refs)

    def products(a_ref, b_ref):
        a_tile = a_ref[...].astype(BF16)
        return [lax.dot_general(a_tile, (b_ref[g] if b.ndim == 4 else b_ref[...]).astype(BF16), dims,
                                preferred_element_type=F32) for g in range(groups)]

    def body_whole_k(*refs):
        finish(products(refs[0], refs[1]), refs[2:2 + n_extra], refs[2 + n_extra:])

    def body_k_steps(*refs):
        acc_ref = refs[-1]
        k = pl.program_id(2)

        @pl.when(k == 0)
        def _():
            acc_ref[...] = jnp.zeros_like(acc_ref)

        for g, p in enumerate(products(refs[0], refs[1])):
            acc_ref[g] += p

        @pl.when(k == nk - 1)
        def _():
            finish([acc_ref[g] for g in range(groups)], refs[2:2 + n_extra], refs[2 + n_extra:2 + n_extra + n_out])

    return pl.pallas_call(
        body_whole_k if nk == 1 else body_k_steps, name=name, grid=(n // tn, m // tm, nk),
        in_specs=[a_spec, b_spec] + [s for _, s in extras],
        out_specs=[s for _, _, s in outs],
        out_shape=[jax.ShapeDtypeStruct(shape, dtype) for shape, dtype, _ in outs],
        scratch_shapes=[] if nk == 1 else [pltpu.VMEM((groups, tm, tn), F32)],
        compiler_params=_params(("parallel", "parallel", "arbitrary")),
    )(a, b, *[arr for arr, _ in extras])


def _rowwise(name, fn, rows, vecs, outs, accs, tm):
    s = rows[0][0].shape[0]
    n_in, n_out = len(rows) + len(vecs), len(outs)

    def body(*refs):
        i = pl.program_id(0)
        res = fn(*[r[...] for r in refs[:n_in]])
        res = res if isinstance(res, tuple) else (res,)
        out_refs, acc_refs = refs[n_in:n_in + n_out], refs[n_in + n_out:]
        for ref, val in zip(out_refs, res[:n_out]):
            ref[...] = val.astype(ref.dtype)

        @pl.when(i == 0)
        def _():
            for ref in acc_refs:
                ref[...] = jnp.zeros_like(ref)

        for ref, val in zip(acc_refs, res[n_out:]):
            ref[...] += val

    in_specs = [pl.BlockSpec((tm, w), functools.partial(lambda i, cb: (i, cb), cb=cb)) for _, w, cb in rows]
    in_specs += [pl.BlockSpec(v.shape, lambda i: (0,) * v.ndim) for v in vecs]
    out_specs = [pl.BlockSpec((tm, w), lambda i: (i, 0)) for w, _ in outs] + [pl.BlockSpec((1, w), lambda i: (0, 0)) for w in accs]
    out_shape = [jax.ShapeDtypeStruct((s, w), dt) for w, dt in outs] + [jax.ShapeDtypeStruct((1, w), F32) for w in accs]
    return pl.pallas_call(
        body, name=name, grid=(s // tm,), in_specs=in_specs, out_specs=out_specs, out_shape=out_shape,
        compiler_params=_params(("arbitrary",)),
    )(*[r for r, _, _ in rows], *vecs)


def _colsum(v):
    return jnp.sum(v, axis=0, keepdims=True)


def _norm_mod(name, h, nw, sc, sh, tm):
    d = h.shape[1]

    def fn(hb, nwb, scb, shb):
        r = lax.rsqrt(jnp.mean(hb * hb, axis=-1, keepdims=True) + EPS)
        return (hb * r) * nwb * (1.0 + scb) + shb

    return _rowwise(name, fn, [(h, d, 0)], [nw, sc, sh], [(d, BF16)], [], tm)[0]


def _norm_mod_bwd(name, dy, h, dh_next, nw, sc, tm):
    d = h.shape[1]

    def fn(dyb, hb, dhb, nwb, scb):
        r = lax.rsqrt(jnp.mean(hb * hb, axis=-1, keepdims=True) + EPS)
        xh = hb * r
        dxh = dyb * (nwb * (1.0 + scb))
        dx = r * (dxh - xh * jnp.mean(dxh * xh, axis=-1, keepdims=True))
        return dhb + dx, _colsum(dyb), _colsum(dyb * xh * nwb), _colsum(dyb * xh * (1.0 + scb))

    return _rowwise(name, fn, [(dy, d, 0), (h, d, 0), (dh_next, d, 0)], [nw, sc], [(d, F32)], [d, d, d], tm)


def _gate_bwd(name, dh, o, g, coef, tm):
    d = dh.shape[1]

    def fn(dhb, ob, gb):
        return dhb * (coef * gb), _colsum(dhb * ob * coef)

    return _rowwise(name, fn, [(dh, d, 0), (o, d, 0)], [g], [(d, BF16)], [d], tm)


def _loss_bwd(name, h, target, nw, tm):
    d = h.shape[1]

    def fn(hb, tb, nwb):
        r = lax.rsqrt(jnp.mean(hb * hb, axis=-1, keepdims=True) + EPS)
        xh = hb * r
        err = xh * nwb - tb
        dy = err * (1.0 / d)
        dxh = dy * nwb
        dx = r * (dxh - xh * jnp.mean(dxh * xh, axis=-1, keepdims=True))
        loss = 0.5 * jnp.sum(jnp.mean(err * err, axis=-1, keepdims=True), axis=0, keepdims=True)
        return dx, jnp.broadcast_to(loss, (1, LANE)), _colsum(dy * xh)

    return _rowwise(name, fn, [(h, d, 0), (target, d, 0)], [nw], [(d, F32)], [LANE, d], tm)


def _attn_fwd(proj, nh, t):
    s = proj.shape[0]
    scale = HEAD_DIM ** -0.5

    def body(q_ref, k_ref, v_ref, y_ref, tot_ref):
        i = pl.program_id(1)
        qb = q_ref[...].astype(BF16)
        row = lax.broadcasted_iota(jnp.int32, (t, t), 0)
        col = lax.broadcasted_iota(jnp.int32, (t, t), 1)
        later = (row > col).astype(BF16)

        def step(jj, carry):
            o, run = carry
            j = i - jj
            ks = pl.ds(pl.multiple_of(j * t, t), t)
            kb = k_ref[ks, :].astype(BF16)
            vb = v_ref[ks, :].astype(BF16)
            z = lax.dot_general(qb, kb, NT, preferred_element_type=F32) * scale
            sp = _softplus(z)
            mask = (col + j * t) < (row + i * t)
            log_keep = jnp.where(mask, -sp, 0.0)
            between = _dot_split(log_keep, later) + run
            w = jnp.where(mask, jnp.exp(z - sp + between), 0.0)
            o = o + jnp.dot(w.astype(BF16), vb, preferred_element_type=F32)
            return o, run + jnp.sum(log_keep, axis=1, keepdims=True)

        o, run = lax.fori_loop(0, i + 1, step, (jnp.zeros((t, HEAD_DIM), F32), jnp.zeros((t, 1), F32)))
        y_ref[...] = o.astype(y_ref.dtype)
        tot_ref[...] = jnp.broadcast_to(run, (t, HEAD_DIM))

    return pl.pallas_call(
        body, name="attn_fwd", grid=(nh, s // t),
        in_specs=[pl.BlockSpec((t, HEAD_DIM), lambda h, i: (i, h)),
                  pl.BlockSpec((s, HEAD_DIM), lambda h, i: (0, nh + h)),
                  pl.BlockSpec((s, HEAD_DIM), lambda h, i: (0, 2 * nh + h))],
        out_specs=[pl.BlockSpec((t, HEAD_DIM), lambda h, i: (i, h)), pl.BlockSpec((t, HEAD_DIM), lambda h, i: (i, h))],
        out_shape=[jax.ShapeDtypeStruct((s, nh * HEAD_DIM), BF16), jax.ShapeDtypeStruct((s, nh * HEAD_DIM), F32)],
        compiler_params=_params(("parallel", "arbitrary")),
    )(proj, proj, proj)


def _attn_bwd(proj, dy, tot, nh, t):
    s = proj.shape[0]
    scale = HEAD_DIM ** -0.5

    def body(q_ref, k_ref, v_ref, dy_ref, tot_ref, dq_ref, dk_ref, dv_ref):
        i = pl.program_id(1)

        @pl.when(i == 0)
        def _():
            dk_ref[...] = jnp.zeros_like(dk_ref)
            dv_ref[...] = jnp.zeros_like(dv_ref)

        qb = q_ref[...].astype(BF16)
        dyb = dy_ref[...].astype(BF16)
        total = tot_ref[:, :1]
        row = lax.broadcasted_iota(jnp.int32, (t, t), 0)
        col = lax.broadcasted_iota(jnp.int32, (t, t), 1)
        upto = (row <= col).astype(BF16)
        before = (row < col).astype(BF16)

        def step(j, carry):
            dq, passed, g_run = carry
            ks = pl.ds(pl.multiple_of(j * t, t), t)
            kb = k_ref[ks, :].astype(BF16)
            vb = v_ref[ks, :].astype(BF16)
            z = lax.dot_general(qb, kb, NT, preferred_element_type=F32) * scale
            sp = _softplus(z)
            mask = (col + j * t) < (row + i * t)
            log_keep = jnp.where(mask, -sp, 0.0)
            between = total - passed - _dot_split(log_keep, upto)
            w = jnp.where(mask, jnp.exp(z - sp + between), 0.0)
            dw = lax.dot_general(dyb, vb, NT, preferred_element_type=F32)
            g = dw * w
            g_before = g_run + _dot_split(g, before)
            dz = jnp.where(mask, g * jnp.exp(-sp) - jnp.exp(z - sp) * g_before, 0.0) * scale
            dzb = dz.astype(BF16)
            dq = dq + jnp.dot(dzb, kb, preferred_element_type=F32)
            dk_ref[ks, :] += lax.dot_general(dzb, qb, TN, preferred_element_type=F32)
            dv_ref[ks, :] += lax.dot_general(w.astype(BF16), dyb, TN, preferred_element_type=F32)
            return (dq, passed + jnp.sum(log_keep, axis=1, keepdims=True), g_run + jnp.sum(g, axis=1, keepdims=True))

        zero = jnp.zeros((t, 1), F32)
        dq, _, _ = lax.fori_loop(0, i + 1, step, (jnp.zeros((t, HEAD_DIM), F32), zero, zero))
        dq_ref[...] = dq

    tile = lambda off: pl.BlockSpec((t, HEAD_DIM), lambda h, i: (i, off + h))
    head = lambda off: pl.BlockSpec((s, HEAD_DIM), lambda h, i: (0, off + h))
    return pl.pallas_call(
        body, name="attn_bwd", grid=(nh, s // t),
        in_specs=[tile(0), head(nh), head(2 * nh), tile(0), tile(0)],
        out_specs=[tile(0), head(0), head(0)],
        out_shape=[jax.ShapeDtypeStruct((s, nh * HEAD_DIM), F32)] * 3,
        compiler_params=_params(("parallel", "arbitrary")),
    )(proj, proj, proj, dy, tot)


def _lru_gates(xc, w_r, b_r, w_i, b_i, lam):
    xb = xc.astype(BF16)
    r = _sigmoid(jnp.dot(xb, w_r.astype(BF16), preferred_element_type=F32) + b_r)
    i = _sigmoid(jnp.dot(xb, w_i.astype(BF16), preferred_element_type=F32) + b_i)
    neg_lam = -lam
    sp_lam = jnp.maximum(neg_lam, 0.0) + _log1p(jnp.exp(-jnp.abs(neg_lam)))
    log_a = -LRU_C * r * sp_lam
    a = jnp.exp(log_a)
    mult = jnp.sqrt(-_expm1(2.0 * log_a))
    return r, i, sp_lam, a, mult


def _conv_taps(xpad_chunk, conv_w, t):
    shifted = [xpad_chunk[CONV_HALO:, :]]
    for d in range(1, CONV_WIDTH):
        shifted.append(pltpu.roll(xpad_chunk, d, 0)[CONV_HALO:, :])
    weights = [conv_w[CONV_WIDTH - 1 - d:CONV_WIDTH - d, :] for d in range(CONV_WIDTH)]
    return shifted, weights


def _lru_fwd(xr_pad, proj, gr_block0, conv_w, conv_b, w_r, b_r, w_i, b_i, lam, t):
    s, w = xr_pad.shape[0] - CONV_HALO, xr_pad.shape[1]
    nblk = w // LANE
    nchunk = s // t
    steps = [1 << p for p in range(t.bit_length() - 1)]
    assert (1 << (t.bit_length() - 1)) == t and w_r.shape[1:] == (LANE, LANE)

    def body(x_ref, gr_ref, cw_ref, cb_ref, wr_ref, br_ref, wi_ref, bi_ref, lam_ref, h_ref, hp_ref, xc_ref, y_ref):
        row = lax.broadcasted_iota(jnp.int32, (t, LANE), 0)

        def chunk(ci, h_in):
            t0 = pl.multiple_of(ci * t, t)
            shifted, weights = _conv_taps(x_ref[pl.ds(t0, t + CONV_HALO), :], cw_ref[...], t)
            xc = cb_ref[...] + sum(wd * xs for wd, xs in zip(weights, shifted))
            r, i, _, a, mult = _lru_gates(xc, wr_ref[...], br_ref[...], wi_ref[...], bi_ref[...], lam_ref[...])
            coef, val = a, mult * (i * xc)
            for d in steps:
                ok = row >= d
                val = jnp.where(ok, coef * pltpu.roll(val, d, 0) + val, val)
                coef = jnp.where(ok, coef * pltpu.roll(coef, d, 0), coef)
            h = val + coef * h_in
            rows = pl.ds(t0, t)
            h_ref[rows, :] = h
            hp_ref[rows, :] = jnp.where(row == 0, h_in, pltpu.roll(h, 1, 0))
            xc_ref[rows, :] = xc
            y_ref[rows, :] = (h * _gelu_and_grad(gr_ref[rows, :])[0]).astype(y_ref.dtype)
            return h[t - 1:t, :]

        lax.fori_loop(0, nchunk, chunk, jnp.zeros((1, LANE), F32))

    col = lambda rows: pl.BlockSpec((rows, LANE), lambda n: (0, n))
    return pl.pallas_call(
        body, name="lru_fwd", grid=(nblk,),
        in_specs=[col(s + CONV_HALO), pl.BlockSpec((s, LANE), lambda n: (0, gr_block0 + n)), col(CONV_WIDTH), col(1),
                  pl.BlockSpec((None, LANE, LANE), lambda n: (n, 0, 0)), col(1),
                  pl.BlockSpec((None, LANE, LANE), lambda n: (n, 0, 0)), col(1), col(1)],
        out_specs=[col(s)] * 4,
        out_shape=[jax.ShapeDtypeStruct((s, w), F32)] * 3 + [jax.ShapeDtypeStruct((s, w), BF16)],
        compiler_params=_params(("parallel",)),
    )(xr_pad, proj, conv_w, conv_b, w_r, b_r, w_i, b_i, lam)


def _lru_bwd(dy, proj, gr_block0, h, h_prev, xc, w_r, b_r, w_i, b_i, lam, t):
    s, w = dy.shape
    nblk = w // LANE
    nchunk = s // t
    steps = [1 << p for p in range(t.bit_length() - 1)]

    def body(dy_ref, gr_ref, h_ref, hp_ref, xc_ref, wr_ref, br_ref, wi_ref, bi_ref, lam_ref,
             dgr_ref, dxc_ref, dwr_ref, dwi_ref, dbr_ref, dbi_ref, dlam_ref):
        row = lax.broadcasted_iota(jnp.int32, (t, LANE), 0)
        for ref in (dwr_ref, dwi_ref, dbr_ref, dbi_ref, dlam_ref):
            ref[...] = jnp.zeros_like(ref)

        def chunk(cc, carry):
            lam_next, a_next = carry
            rows = pl.ds(pl.multiple_of((nchunk - 1 - cc) * t, t), t)
            dyb, hb, xcb = dy_ref[rows, :], h_ref[rows, :], xc_ref[rows, :]
            gel, dgel = _gelu_and_grad(gr_ref[rows, :])
            dgr_ref[rows, :] = dyb * hb * dgel
            w_r, w_i = wr_ref[...], wi_ref[...]
            r, i, sp_lam, a, mult = _lru_gates(xcb, w_r, br_ref[...], w_i, bi_ref[...], lam_ref[...])
            coef = jnp.where(row == t - 1, a_next, pltpu.roll(a, t - 1, 0))
            val = dyb * gel
            for d in steps:
                ok = row < t - d
                val = jnp.where(ok, coef * pltpu.roll(val, t - d, 0) + val, val)
                coef = jnp.where(ok, coef * pltpu.roll(coef, t - d, 0), coef)
            adj = val + coef * lam_next
            da = adj * hp_ref[rows, :]
            v = i * xcb
            dmult, dv = adj * v, adj * mult
            dlog_a = da * a - (a * a) * dmult / mult
            dr_pre = (-LRU_C * sp_lam) * dlog_a * r * (1.0 - r)
            di_pre = dv * xcb * i * (1.0 - i)
            dlam_ref[...] += _colsum(-LRU_C * r * dlog_a)
            dbr_ref[...] += _colsum(dr_pre)
            dbi_ref[...] += _colsum(di_pre)
            xb, drb, dib = xcb.astype(BF16), dr_pre.astype(BF16), di_pre.astype(BF16)
            dwr_ref[...] += lax.dot_general(xb, drb, TN, preferred_element_type=F32)
            dwi_ref[...] += lax.dot_general(xb, dib, TN, preferred_element_type=F32)
            dxc_ref[rows, :] = (dv * i + lax.dot_general(drb, w_r.astype(BF16), NT, preferred_element_type=F32)
                                + lax.dot_general(dib, w_i.astype(BF16), NT, preferred_element_type=F32))
            return adj[0:1, :], a[0:1, :]

        lax.fori_loop(0, nchunk, chunk, (jnp.zeros((1, LANE), F32), jnp.zeros((1, LANE), F32)))
        dlam_ref[...] = dlam_ref[...] * (-_sigmoid(-lam_ref[...]))

    col = lambda rows: pl.BlockSpec((rows, LANE), lambda n: (0, n))
    mat = pl.BlockSpec((None, LANE, LANE), lambda n: (n, 0, 0))
    return pl.pallas_call(
        body, name="lru_bwd", grid=(nblk,),
        in_specs=[col(s), pl.BlockSpec((s, LANE), lambda n: (0, gr_block0 + n)), col(s), col(s), col(s),
                  mat, col(1), mat, col(1), col(1)],
        out_specs=[col(s), col(s), mat, mat, col(1), col(1), col(1)],
        out_shape=[jax.ShapeDtypeStruct((s, w), F32)] * 2 + [jax.ShapeDtypeStruct((nblk, LANE, LANE), F32)] * 2
        + [jax.ShapeDtypeStruct((1, w), F32)] * 3,
        compiler_params=_params(("parallel",)),
    )(dy, proj, h, h_prev, xc, w_r, b_r, w_i, b_i, lam)


def _conv_bwd(xr_pad, dxc_pad, conv_w, t):
    s, w = xr_pad.shape[0] - CONV_HALO, xr_pad.shape[1]
    nchunk = s // t

    def body(x_ref, g_ref, cw_ref, dx_ref, dcw_ref, dcb_ref):
        dcw_ref[...] = jnp.zeros_like(dcw_ref)
        dcb_ref[...] = jnp.zeros_like(dcb_ref)

        def chunk(ci, _):
            t0 = pl.multiple_of(ci * t, t)
            shifted, weights = _conv_taps(x_ref[pl.ds(t0, t + CONV_HALO), :], cw_ref[...], t)
            gpad = g_ref[pl.ds(t0, t + CONV_HALO), :]
            g = gpad[:t, :]
            dx = weights[0] * g
            for d in range(1, CONV_WIDTH):
                dx = dx + weights[d] * pltpu.roll(gpad, t + CONV_HALO - d, 0)[:t, :]
            dx_ref[pl.ds(t0, t), :] = dx
            for d in range(CONV_WIDTH):
                dcw_ref[CONV_WIDTH - 1 - d:CONV_WIDTH - d, :] += _colsum(g * shifted[d])
            dcb_ref[...] += _colsum(g)
            return 0

        lax.fori_loop(0, nchunk, chunk, 0)

    col = lambda rows: pl.BlockSpec((rows, LANE), lambda n: (0, n))
    return pl.pallas_call(
        body, name="conv_bwd", grid=(w // LANE,),
        in_specs=[col(s + CONV_HALO), col(s + CONV_HALO), col(CONV_WIDTH)],
        out_specs=[col(s), col(CONV_WIDTH), col(1)],
        out_shape=[jax.ShapeDtypeStruct((s, w), F32), jax.ShapeDtypeStruct((CONV_WIDTH, w), F32),
                   jax.ShapeDtypeStruct((1, w), F32)],
        compiler_params=_params(("parallel",)),
    )(xr_pad, dxc_pad, conv_w)


def _sum_parts(parts_ref):
    g = parts_ref[0].astype(F32)
    for p in range(1, parts_ref.shape[0]):
        g = g + parts_ref[p].astype(F32)
    return g


def _reduce_parts(name, parts):
    p, r, c = parts.shape
    tr = _tile(r, max(8, (1 << 19) // c), 8)

    def body(parts_ref, g_ref):
        g_ref[...] = _sum_parts(parts_ref)

    return pl.pallas_call(
        body, name=name, grid=(r // tr,), in_specs=[pl.BlockSpec((p, tr, c), lambda i: (0, i, 0))],
        out_specs=pl.BlockSpec((tr, c), lambda i: (i, 0)), out_shape=jax.ShapeDtypeStruct((r, c), F32),
        compiler_params=_params(("parallel",)),
    )(parts)


def _adamw(name, parts, w, m, v):
    p, r, c = parts.shape
    tr = _tile(r, max(8, (1 << 18) // c), 8)

    def body(parts_ref, w_ref, m_ref, v_ref, g_ref, d_ref, nm_ref, nv_ref):
        g = _sum_parts(parts_ref)
        nm = ADAM_B1 * m_ref[...] + (1.0 - ADAM_B1) * g
        nv = ADAM_B2 * v_ref[...] + (1.0 - ADAM_B2) * (g * g)
        m_hat = nm / (1.0 - ADAM_B1 ** ADAM_STEP)
        v_hat = nv / (1.0 - ADAM_B2 ** ADAM_STEP)
        g_ref[...] = g
        d_ref[...] = -ADAM_LR * (m_hat / (jnp.sqrt(v_hat) + ADAM_EPS) + ADAM_WD * w_ref[...])
        nm_ref[...] = nm
        nv_ref[...] = nv

    blk = pl.BlockSpec((tr, c), lambda i: (i, 0))
    return pl.pallas_call(
        body, name=name, grid=(r // tr,), in_specs=[pl.BlockSpec((p, tr, c), lambda i: (0, i, 0)), blk, blk, blk],
        out_specs=[blk] * 4, out_shape=[jax.ShapeDtypeStruct((r, c), F32)] * 4,
        compiler_params=_params(("parallel",)),
    )(parts, w, m, v)


def _ffn_in(tag, y, w_in_g, tm):
    s, d = y.shape
    half = N_DEV // 2
    cb = w_in_g.shape[2]
    ff = half * cb

    def swiglu(acc, extra_refs, out_refs):
        g, u = acc
        out_refs[0][0] = g
        out_refs[0][1] = u
        out_refs[1][...] = (g * _sigmoid(g) * u).astype(BF16)

    gu_shape = (2, 1, s, ff)
    gu, act = _matmul(
        tag + "_in", y[None], w_in_g.reshape(2, half, d, cb), "nn", tm, cb, d,
        outs=[(gu_shape, F32, _bspec(gu_shape, tm, cb, _ij)), ((1, s, ff), BF16, _bspec((1, s, ff), tm, cb, _ij))],
        epilogue=swiglu, b_buffers=1)
    return gu, act


def _ffn_out(tag, act, w_out_g, res, gate, tm):
    _, s, ff = act.shape
    d = res.shape[1]
    tn = _tile(d, 1024)

    def residual(acc, extra_refs, out_refs):
        out_refs[0][...] = acc[0]
        out_refs[1][...] = extra_refs[0][...] + 0.5 * extra_refs[1][...] * acc[0]

    plain = _bspec((1, s, d), tm, tn, _ij)
    o, h_new = _matmul(
        tag + "_out", act, w_out_g.reshape(1, ff, d), "nn", tm, tn, ff,
        outs=[((1, s, d), F32, plain), ((1, s, d), F32, plain)], epilogue=residual,
        extras=[(res[None], plain), (gate, _row_spec(tn))], b_buffers=1)
    return o[0], h_new[0]


def _ffn_bwd(tag, dh, y, gu, act, o, gate, w_in_g, w_out_g, tm):
    s, d = dh.shape
    half = N_DEV // 2
    cb = w_in_g.shape[2]
    ff = half * cb
    do, dgate = _gate_bwd(tag + "_gate_bwd", dh, o, gate, 0.5, _tile(s, 256, 8))

    tn_d, tm_f = _tile(d, 1024), _tile(ff, 512)
    dw_out = _matmul(tag + "_dw_out", act, do[None], "tn", tm_f, tn_d, s,
                     outs=[((1, ff, d), BF16, _bspec((1, ff, d), tm_f, tn_d, _ij))], b_buffers=1)[0]

    def dswiglu(acc, extra_refs, out_refs):
        dact = acc[0]
        g, u = extra_refs[0][0], extra_refs[0][1]
        sg = _sigmoid(g)
        out_refs[0][0] = (dact * u * sg * (1.0 + g * (1.0 - sg))).astype(BF16)
        out_refs[0][1] = (dact * g * sg).astype(BF16)

    gu_shape = (2, 1, s, ff)
    gu_spec = _bspec(gu_shape, tm, cb, _ij)
    dgu = _matmul(tag + "_dact", do[None], w_out_g.reshape(1, ff, d), "nt", tm, cb, d,
                  outs=[(gu_shape, BF16, gu_spec)], epilogue=dswiglu, extras=[(gu, gu_spec)], b_buffers=1)[0]
    dgu = dgu.reshape(2, s, ff)

    tm_d = _tile(d, 512)
    dw_in = _matmul(tag + "_dw_in", y[None], dgu, "tn", tm_d, cb, s,
                    outs=[((N_DEV, d, cb), BF16, _bspec((N_DEV, d, cb), tm_d, cb, _ij))], b_buffers=1)[0]
    tm_big = _tile(s, 1024, 8)
    dy = _matmul(tag + "_dy", dgu, w_in_g, "nt", tm_big, d, cb,
                 outs=[((1, s, d), F32, _bspec((1, s, d), tm_big, d, _ij))])[0][0]
    return dy, dgate, dw_in, dw_out.reshape(N_DEV, ff // N_DEV, d)


def kernel(x, c, w_ada, b_ada, norm_ffn1, w_ffn1_in, w_ffn1_out, norm_mix, w_in, conv_w, conv_b, w_rg_gate, b_rg_gate, w_in_gate, b_in_gate, lru_lambda, w_branch_attn, w_branch_lru, w_out, norm_ffn2, w_ffn2_in, w_ffn2_out, norm_final, loss_target, m_w_ada, m_b_ada, m_norm_ffn1, m_w_ffn1_in, m_w_ffn1_out, m_norm_mix, m_w_in, m_conv_w, m_conv_b, m_w_rg_gate, m_b_rg_gate, m_w_in_gate, m_b_in_gate, m_lru_lambda, m_w_branch_attn, m_w_branch_lru, m_w_out, m_norm_ffn2, m_w_ffn2_in, m_w_ffn2_out, m_norm_final, v_w_ada, v_b_ada, v_norm_ffn1, v_w_ffn1_in, v_w_ffn1_out, v_norm_mix, v_w_in, v_conv_w, v_conv_b, v_w_rg_gate, v_b_rg_gate, v_w_in_gate, v_b_in_gate, v_lru_lambda, v_w_branch_attn, v_w_branch_lru, v_w_out, v_norm_ffn2, v_w_ffn2_in, v_w_ffn2_out, v_norm_final):
    xs, target = x[0], loss_target[0]
    s, d = xs.shape
    aw, lw = w_branch_attn.shape[1], w_branch_lru.shape[1]
    nh, nlb = aw // HEAD_DIM, w_rg_gate.shape[1]
    cba, cbi, cbb, cwb = w_ada.shape[2], w_in.shape[2], w_branch_attn.shape[2], conv_w.shape[2]
    assert lw == nlb * LANE and cwb * N_DEV == lw and 3 * aw + 2 * lw + 2 * d == cbi * N_DEV
    me = 4 * lax.axis_index("x") + 2 * lax.axis_index("y") + lax.axis_index("c")
    tm = _tile(s, 512, 8)
    tr = _tile(s, 256, 8)
    t_attn = _tile(s, 256, 8)
    t_lru = _tile(s, 256, 8)

    small = _exchange("gather_c", [jnp.concatenate([c, conv_w.reshape(1, CONV_WIDTH * cwb)], axis=1)], False)[0][:, 0, :]
    c_all = small[:, :d]
    conv_w_full = small[:, d:].reshape(N_DEV, CONV_WIDTH, cwb).transpose(1, 0, 2).reshape(CONV_WIDTH, lw)
    c_act = _rowwise("silu_c", lambda v: v * _sigmoid(v), [(c_all, d, 0)], [], [(d, F32)], [], N_DEV)[0]

    def add_bias(acc_ref, extra_refs, out_refs):
        out_refs[0][...] = acc_ref[0] + extra_refs[0][...]

    b_ada_mine = lax.dynamic_slice(b_ada, (0, me * cba), (1, cba))
    mod_part = _matmul("mod", c_act[None], w_ada, "nn", N_DEV, cba, _tile(d, 512),
                       outs=[((1, N_DEV, cba), F32, _bspec((1, N_DEV, cba), N_DEV, cba, _ij))], epilogue=add_bias,
                       extras=[(b_ada_mine, _row_spec(cba))])[0][0]
    mod_all = _exchange("gather_mod", [mod_part], False)[0]
    mod = lax.dynamic_index_in_dim(mod_all, me, axis=1, keepdims=False).reshape(1, 9 * d)
    sh1, sc1, g1, sh2, sc2, g2, sh3, sc3, g3 = [mod[:, n * d:(n + 1) * d] for n in range(9)]

    shards = [w_ffn1_in[0], w_ffn1_out[0], w_in[0], w_branch_attn[0], w_branch_lru[0], w_out[0], w_ffn2_in[0], w_ffn2_out[0]]
    gathers, token = _exchange_begin("gather_w", [w.astype(BF16) for w in shards], False)

    def gathered(n, after):
        return _exchange_end("gathered_w%d" % n, [gathers[n]], after, False)[0]

    y1 = _norm_mod("norm1", xs, norm_ffn1 + token[:1, :1], sc1, sh1, tr)
    wf1i = gathered(0, y1)
    gu1, act1 = _ffn_in("ffn1", y1, wf1i, tm)
    wf1o = gathered(1, act1)
    o1, h1 = _ffn_out("ffn1", act1, wf1o, xs, g1, tm)

    y2 = _norm_mod("norm2", h1, norm_mix, sc2, sh2, tr)
    wi_g = gathered(2, y2)
    tn_i = _tile(cbi, 1152)
    proj = _matmul("mix_in", y2[None], wi_g, "nn", tm, tn_i, d,
                   outs=[((1, s, N_DEV * cbi), F32, _bspec((1, s, N_DEV * cbi), tm, tn_i, _ij))], b_buffers=1)[0][0]
    off_xr, off_gr, off_ga, off_gl = 3 * aw, 3 * aw + lw, 3 * aw + 2 * lw, 3 * aw + 2 * lw + d
    y_attn, attn_tot = _attn_fwd(proj, nh, t_attn)
    xr_pad = jnp.pad(proj[:, off_xr:off_xr + lw], ((CONV_HALO, 0), (0, 0)))
    w_r, w_i = w_rg_gate[0], w_in_gate[0]
    h_lru, h_prev, xc, y_lru = _lru_fwd(xr_pad, proj, off_gr // LANE, conv_w_full, conv_b, w_r, b_rg_gate, w_i,
                                        b_in_gate, lru_lambda, t_lru)
    plain_b = _bspec((1, s, d), tm, cbb, _ij)
    wba_g = gathered(3, y_attn)
    wbl_g = gathered(4, y_lru)
    ya = _matmul("branch_attn", y_attn[None], wba_g, "nn", tm, cbb, _tile(aw, 1024), outs=[((1, s, d), F32, plain_b)])[0]

    def merge(acc_ref, extra_refs, out_refs):
        yl = acc_ref[0]
        ya_t, ga, gl = extra_refs[0][...], extra_refs[1][...], extra_refs[2][...]
        out_refs[0][...] = yl
        out_refs[1][...] = (_sigmoid(ga) * ya_t + _sigmoid(gl) * yl).astype(BF16)

    proj3 = proj[None]
    ga_spec = _bspec(proj3.shape, tm, cbb, lambda i, j, k: (i, j + off_ga // cbb))
    gl_spec = _bspec(proj3.shape, tm, cbb, lambda i, j, k: (i, j + off_gl // cbb))
    yl, merged = _matmul("branch_lru", y_lru[None], wbl_g, "nn", tm, cbb, _tile(lw, 1024),
                         outs=[((1, s, d), F32, plain_b), ((1, s, d), BF16, plain_b)], epilogue=merge,
                         extras=[(ya, plain_b), (proj3, ga_spec), (proj3, gl_spec)])
    tn_d = _tile(d, 1024)
    plain = _bspec((1, s, d), tm, tn_d, _ij)

    def residual(acc_ref, extra_refs, out_refs):
        o = acc_ref[0]
        out_refs[0][...] = o
        out_refs[1][...] = extra_refs[0][...] + extra_refs[1][...] * o

    wo_g = gathered(5, merged)
    mo, h2 = _matmul("mix_out", merged, wo_g.reshape(1, d, d), "nn", tm, tn_d, d,
                     outs=[((1, s, d), F32, plain), ((1, s, d), F32, plain)], epilogue=residual,
                     extras=[(h1[None], plain), (g2, _row_spec(tn_d))], b_buffers=1)
    mo, h2 = mo[0], h2[0]

    y3 = _norm_mod("norm3", h2, norm_ffn2, sc3, sh3, tr)
    wf2i = gathered(6, y3)
    gu3, act3 = _ffn_in("ffn2", y3, wf2i, tm)
    wf2o = gathered(7, act3)
    o3, h3 = _ffn_out("ffn2", act3, wf2o, h2, g3, tm)

    nf = norm_final.reshape(1, d)
    dh3, loss_part, d_nf = _loss_bwd("loss", h3, target, nf, tr)
    dy3, dg3, dwf2i, dwf2o = _ffn_bwd("ffn2", dh3, y3, gu3, act3, o3, g3, wf2i, wf2o, tm)
    scatter_ffn2, token = _exchange_begin("scatter_ffn2", [dwf2i, dwf2o], True)
    dh2, dsh3, dsc3, dn3 = _norm_mod_bwd("norm3_bwd", dy3, h2, dh3, norm_ffn2 + token[:1, :1], sc3, tr)

    dmo, dg2 = _gate_bwd("mix_gate_bwd", dh2, mo, g2, 1.0, tr)
    dwo = _matmul("mix_dw_out", merged, dmo[None], "tn", _tile(d, 512), tn_d, s,
                  outs=[((1, d, d), BF16, _bspec((1, d, d), _tile(d, 512), tn_d, _ij))], b_buffers=1)[0]

    def dmerge(acc_ref, extra_refs, out_refs):
        dm = acc_ref[0]
        ya_t, yl_t = extra_refs[0][...], extra_refs[1][...]
        sa, sl = _sigmoid(extra_refs[2][...]), _sigmoid(extra_refs[3][...])
        out_refs[0][...] = (dm * sa).astype(BF16)
        out_refs[1][...] = (dm * sl).astype(BF16)
        out_refs[2][...] = (dm * ya_t * sa * (1.0 - sa)).astype(BF16)
        out_refs[3][...] = (dm * yl_t * sl * (1.0 - sl)).astype(BF16)

    tn_m = _tile(math.gcd(d, off_ga, off_gl), 1024)
    plain_m = _bspec((1, s, d), tm, tn_m, _ij)
    gate_specs = [_bspec(proj3.shape, tm, tn_m, functools.partial(lambda i, j, k, o: (i, j + o), o=o // tn_m))
                  for o in (off_ga, off_gl)]
    dya, dyl, dga, dgl = _matmul("mix_dmerged", dmo[None], wo_g.reshape(1, d, d), "nt", tm, tn_m, d,
                                 outs=[((1, s, d), BF16, plain_m)] * 4, epilogue=dmerge,
                                 extras=[(ya, plain_m), (yl, plain_m), (proj3, gate_specs[0]), (proj3, gate_specs[1])],
                                 b_buffers=1)
    tm_a, tm_l = _tile(aw, 1024), _tile(lw, 1024)
    dwba = _matmul("dw_branch_attn", y_attn[None], dya, "tn", tm_a, cbb, s,
                   outs=[((N_DEV, aw, cbb), BF16, _bspec((N_DEV, aw, cbb), tm_a, cbb, _ij))])[0]
    dwbl = _matmul("dw_branch_lru", y_lru[None], dyl, "tn", tm_l, cbb, s,
                   outs=[((N_DEV, lw, cbb), BF16, _bspec((N_DEV, lw, cbb), tm_l, cbb, _ij))])[0]
    tn_a, tn_l = _tile(aw, 1024), _tile(lw, 1024)
    dy_attn = _matmul("d_attn_out", dya, wba_g, "nt", tm, tn_a, cbb,
                      outs=[((1, s, aw), F32, _bspec((1, s, aw), tm, tn_a, _ij))])[0][0]
    dy_lru = _matmul("d_lru_out", dyl, wbl_g, "nt", tm, tn_l, cbb,
                     outs=[((1, s, lw), F32, _bspec((1, s, lw), tm, tn_l, _ij))])[0][0]
    dq, dk, dv = _attn_bwd(proj, dy_attn, attn_tot, nh, t_attn)
    dgr, dxc, d_wr, d_wi, d_br, d_bi, d_lam = _lru_bwd(dy_lru, proj, off_gr // LANE, h_lru, h_prev, xc, w_r, b_rg_gate,
                                                       w_i, b_in_gate, lru_lambda, t_lru)
    dxr, d_cw, d_cb = _conv_bwd(xr_pad, jnp.pad(dxc, ((0, CONV_HALO), (0, 0))), conv_w_full, t_lru)
    dproj = jnp.concatenate([dq.astype(BF16), dk.astype(BF16), dv.astype(BF16), dxr.astype(BF16), dgr.astype(BF16),
                             dga[0], dgl[0]], axis=1)
    tm_d = _tile(d, 512)
    dwi = _matmul("mix_dw_in", y2[None], dproj[None], "tn", tm_d, tn_i, s,
                  outs=[((N_DEV, d, cbi), BF16, _bspec((N_DEV, d, cbi), tm_d, tn_i, _ij))], b_buffers=1)[0]
    tm_big = _tile(s, 1024, 8)
    dy2 = _matmul("mix_dy", dproj[None], wi_g, "nt", tm_big, d, tn_i,
                  outs=[((1, s, d), F32, _bspec((1, s, d), tm_big, d, _ij))])[0][0]
    scatter_mix, token = _exchange_begin(
        "scatter_mix", [dwi, dwba, dwbl, dwo.reshape(N_DEV, d // N_DEV, d)], True)
    dh1, dsh2, dsc2, dn2 = _norm_mod_bwd("norm2_bwd", dy2, h1, dh2, norm_mix + token[:1, :1], sc2, tr)

    dy1, dg1, dwf1i, dwf1o = _ffn_bwd("ffn1", dh1, y1, gu1, act1, o1, g1, wf1i, wf1o, tm)
    scatter_ffn1, token = _exchange_begin("scatter_ffn1", [dwf1i, dwf1o], True)
    grad_x, dsh1, dsc1, dn1 = _norm_mod_bwd("norm1_bwd", dy1, xs, dh1, norm_ffn1 + token[:1, :1], sc1, tr)

    lane_pad = jnp.zeros((1, 7 * LANE), F32)
    pack = jnp.concatenate(
        [loss_part, lane_pad, dsh1, dsc1, dg1, dsh2, dsc2, dg2, dsh3, dsc3, dg3, dn1, dn2, dn3, d_nf, d_cb, d_br, d_bi, d_lam,
         d_wr.reshape(1, -1), d_wi.reshape(1, -1), d_cw.reshape(1, -1)], axis=1)
    pack = jnp.pad(pack, ((0, 0), (0, -pack.shape[1] % (8 * LANE))))
    n_pack = pack.shape[1]
    packs = _exchange("gather_small", [pack], False)[0].reshape(N_DEV, n_pack // LANE, LANE)
    g_pack = _reduce_parts("sum_small", packs).reshape(1, n_pack)
    loss = g_pack[0, 0]
    off = 8 * LANE
    n_adam = 9 * d + 4 * d + 4 * lw + 2 * nlb * LANE * LANE
    g_small = g_pack[:, off:off + n_adam].reshape(1, n_adam // LANE, LANE)
    d_cw_sum = g_pack[:, off + n_adam:off + n_adam + CONV_WIDTH * lw].reshape(CONV_WIDTH, lw)
    d_cw_mine = lax.dynamic_slice(d_cw_sum, (0, me * cwb), (CONV_WIDTH, cwb))

    small_names = ["b_ada", "norm_ffn1", "norm_mix", "norm_ffn2", "norm_final", "conv_b", "b_rg_gate", "b_in_gate",
                   "lru_lambda", "w_rg_gate", "w_in_gate"]
    given = dict(b_ada=(b_ada, m_b_ada, v_b_ada), norm_ffn1=(norm_ffn1, m_norm_ffn1, v_norm_ffn1),
                 norm_mix=(norm_mix, m_norm_mix, v_norm_mix), norm_ffn2=(norm_ffn2, m_norm_ffn2, v_norm_ffn2),
                 norm_final=(norm_final, m_norm_final, v_norm_final), conv_b=(conv_b, m_conv_b, v_conv_b),
                 b_rg_gate=(b_rg_gate, m_b_rg_gate, v_b_rg_gate), b_in_gate=(b_in_gate, m_b_in_gate, v_b_in_gate),
                 lru_lambda=(lru_lambda, m_lru_lambda, v_lru_lambda), w_rg_gate=(w_rg_gate, m_w_rg_gate, v_w_rg_gate),
                 w_in_gate=(w_in_gate, m_w_in_gate, v_w_in_gate))
    packed = [jnp.concatenate([given[n][q].reshape(1, -1) for n in small_names], axis=1).reshape(n_adam // LANE, LANE)
              for q in range(3)]
    small_out = _adamw("adamw_small", g_small, *packed)
    results = {}
    pos = 0
    for n in small_names:
        shape = given[n][0].shape
        size = math.prod(shape)
        results[n] = [o.reshape(1, n_adam)[:, pos:pos + size].reshape(shape) for o in small_out]
        pos += size
    results["conv_w"] = [o.reshape(conv_w.shape) for o in
                         _adamw("adamw_conv_w", d_cw_mine[None], conv_w[0], m_conv_w[0], v_conv_w[0])]

    dmod_all = packs.reshape(N_DEV, n_pack)[:, off:off + 9 * d]
    dmod_mine = lax.dynamic_slice(dmod_all, (0, me * cba), (N_DEV, cba))
    dmod_rows = jnp.pad(dmod_mine, ((0, LANE - N_DEV), (0, 0)))
    c_act_t = jnp.pad(c_act.T, ((0, 0), (0, LANE - N_DEV)))
    tm_d2 = _tile(d, 256)
    d_wada = _matmul("dw_ada", c_act_t[None], dmod_rows[None], "nn", tm_d2, cba, LANE,
                     outs=[((1, d, cba), F32, _bspec((1, d, cba), tm_d2, cba, _ij))])[0]
    results["w_ada"] = [o[None] for o in _adamw("adamw_w_ada", d_wada, w_ada[0], m_w_ada[0], v_w_ada[0])]

    groups = [("scattered_ffn2", scatter_ffn2, [("w_ffn2_in", w_ffn2_in, m_w_ffn2_in, v_w_ffn2_in),
                                                ("w_ffn2_out", w_ffn2_out, m_w_ffn2_out, v_w_ffn2_out)]),
              ("scattered_mix", scatter_mix, [("w_in", w_in, m_w_in, v_w_in),
                                              ("w_branch_attn", w_branch_attn, m_w_branch_attn, v_w_branch_attn),
                                              ("w_branch_lru", w_branch_lru, m_w_branch_lru, v_w_branch_lru),
                                              ("w_out", w_out, m_w_out, v_w_out)]),
              ("scattered_ffn1", scatter_ffn1, [("w_ffn1_in", w_ffn1_in, m_w_ffn1_in, v_w_ffn1_in),
                                                ("w_ffn1_out", w_ffn1_out, m_w_ffn1_out, v_w_ffn1_out)])]
    after = results["w_ada"][0]
    for wait_name, handles, leaves in groups:
        for (n, w, m, v), parts in zip(leaves, _exchange_end(wait_name, handles, after, True)):
            results[n] = [o[None] for o in _adamw("adamw_" + n, parts, w[0], m[0], v[0])]
        after = results[leaves[-1][0]][0]

    order = ["w_ada", "b_ada", "norm_ffn1", "w_ffn1_in", "w_ffn1_out", "norm_mix", "w_in", "conv_w", "conv_b", "w_rg_gate",
             "b_rg_gate", "w_in_gate", "b_in_gate", "lru_lambda", "w_branch_attn", "w_branch_lru", "w_out", "norm_ffn2",
             "w_ffn2_in", "w_ffn2_out", "norm_final"]
    return (loss, grad_x[None], *[results[n][0] for n in order], *[results[n][1] for n in order],
            *[results[n][2] for n in order], *[results[n][3] for n in order])
```

```python
import functools
import math

import jax
import jax.numpy as jnp
from jax import lax
from jax.experimental import pallas as pl
from jax.experimental.pallas import tpu as pltpu

F32 = jnp.float32
BF16 = jnp.bfloat16
N_DEV = 8
HEAD_DIM = 128
CONV_WIDTH = 4
CONV_HALO = 8
LRU_C = 8.0
EPS = 1e-6
ADAM_LR, ADAM_B1, ADAM_B2, ADAM_EPS, ADAM_WD, ADAM_STEP = 0.001, 0.9, 0.999, 1e-08, 0.01, 10
LANE = 128
VMEM_LIMIT = 56 * 1024 * 1024
MESH = pl.DeviceIdType.MESH

NT = (((1,), (1,)), ((), ()))
NN = (((1,), (0,)), ((), ()))
TN = (((0,), (0,)), ((), ()))


def _tile(dim, target, align=LANE):
    t = (min(target, dim) // align) * align
    while t >= align:
        if dim % t == 0:
            return t
        t -= align
    return dim


def _params(sem):
    return pltpu.CompilerParams(dimension_semantics=sem, vmem_limit_bytes=VMEM_LIMIT)


def _sigmoid(x):
    return 1.0 / (1.0 + jnp.exp(-x))


def _softplus(x):
    return jnp.maximum(x, 0.0) + jnp.log(1.0 + jnp.exp(-jnp.abs(x)))


def _log1p(z):
    w = 1.0 + z
    return jnp.where(w == 1.0, z, jnp.log(w) * z / jnp.where(w == 1.0, 1.0, w - 1.0))


def _expm1(x):
    poly = x * (1.0 + x * (0.5 + x * (1.0 / 6 + x * (1.0 / 24 + x * (1.0 / 120 + x * (1.0 / 720))))))
    return jnp.where(jnp.abs(x) < 0.25, poly, jnp.exp(x) - 1.0)


_GELU_C = math.sqrt(2.0 / math.pi)


def _gelu_and_grad(x):
    inner = _GELU_C * (x + 0.044715 * x * x * x)
    th = jnp.tanh(inner)
    val = 0.5 * x * (1.0 + th)
    grad = 0.5 * (1.0 + th) + 0.5 * x * (1.0 - th * th) * _GELU_C * (1.0 + 3 * 0.044715 * x * x)
    return val, grad


def _dot_split(x, u):
    hi = x.astype(BF16)
    lo = (x - hi.astype(F32)).astype(BF16)
    return jnp.dot(hi, u, preferred_element_type=F32) + jnp.dot(lo, u, preferred_element_type=F32)


def _mesh_position():
    x, y, c = lax.axis_index("x"), lax.axis_index("y"), lax.axis_index("c")
    return x, y, c, 4 * x + 2 * y + c


def _peers(x, y, c):
    out = []
    for mask in range(1, N_DEV):
        px = 1 - x if mask & 4 else x
        py = 1 - y if mask & 2 else y
        pc = 1 - c if mask & 1 else c
        out.append((mask, (px, py, pc), 4 * px + 2 * py + pc))
    return out


def _exchange(name, arrs, scatter, after=None):
    n = len(arrs)
    behind = [] if after is None else [after]

    def body(*refs):
        ins, outs = refs[:n], refs[n + len(behind):2 * n + len(behind)]
        send_sems, recv_sems, local_sems = refs[2 * n + len(behind):]
        x, y, c, me = _mesh_position()
        peers = _peers(x, y, c)
        waits = []
        for a in range(n):
            mine = ins[a].at[me] if scatter else ins[a]
            local = pltpu.make_async_copy(mine, outs[a].at[me], local_sems.at[a])
            local.start()
            waits.append(local.wait)
            for mask, dev, idx in peers:
                k = a * (N_DEV - 1) + mask - 1
                src = ins[a].at[idx] if scatter else ins[a]
                send = pltpu.make_async_remote_copy(src_ref=src, dst_ref=outs[a].at[me], send_sem=send_sems.at[k],
                                                    recv_sem=recv_sems.at[k], device_id=dev, device_id_type=MESH)
                send.start()
                arrival = pltpu.make_async_remote_copy(src_ref=src, dst_ref=outs[a].at[idx], send_sem=send_sems.at[k],
                                                       recv_sem=recv_sems.at[k], device_id=dev, device_id_type=MESH)
                waits.append(send.wait_send)
                waits.append(arrival.wait_recv)
        for w in waits:
            w()

    any_spec = pl.BlockSpec(memory_space=pl.ANY)
    out_shape = [jax.ShapeDtypeStruct(a.shape if scatter else (N_DEV,) + a.shape, a.dtype) for a in arrs]
    return pl.pallas_call(
        body, name=name, out_shape=out_shape, in_specs=[any_spec] * (n + len(behind)), out_specs=[any_spec] * n,
        scratch_shapes=[pltpu.SemaphoreType.DMA((n * (N_DEV - 1),)), pltpu.SemaphoreType.DMA((n * (N_DEV - 1),)),
                        pltpu.SemaphoreType.DMA((n,))],
    )(*arrs, *behind)


HBM_SPEC = pl.BlockSpec(memory_space=pltpu.HBM)
SEM_SPEC = pl.BlockSpec(memory_space=pltpu.SEMAPHORE)
DATAFLOW = pltpu.SideEffectType.DATAFLOW_SIDE_EFFECTING


def _exchange_begin(name, arrs, scatter, after):
    n = len(arrs)
    lands = [lax.empty(a.shape if scatter else (N_DEV,) + a.shape, a.dtype) for a in arrs]

    def body(*refs):
        srcs, zones, outs = refs[:n], refs[n:2 * n], refs[2 * n + 1:]
        x, y, c, me = _mesh_position()
        for a in range(n):
            send_sems, recv_sems = outs[4 * a], outs[4 * a + 1]
            for mask, dev, idx in _peers(x, y, c):
                pltpu.make_async_remote_copy(
                    src_ref=srcs[a].at[idx] if scatter else srcs[a], dst_ref=zones[a].at[me], send_sem=send_sems.at[mask - 1],
                    recv_sem=recv_sems.at[mask - 1], device_id=dev, device_id_type=MESH).start()
        outs[-1][...] = jnp.zeros_like(outs[-1])

    out_shape, out_specs, aliases = [], [], {}
    for a in range(n):
        out_shape += [pltpu.SemaphoreType.DMA((N_DEV - 1,)), pltpu.SemaphoreType.DMA((N_DEV - 1,)),
                      pltpu.HBM(arrs[a].shape, arrs[a].dtype), pltpu.HBM(lands[a].shape, lands[a].dtype)]
        out_specs += [SEM_SPEC, SEM_SPEC, HBM_SPEC, HBM_SPEC]
        aliases[a] = 4 * a + 2
        aliases[n + a] = 4 * a + 3
    out_shape.append(jax.ShapeDtypeStruct((8, LANE), F32))
    out_specs.append(pl.BlockSpec(memory_space=pltpu.VMEM))
    res = pl.pallas_call(
        body, name=name, out_shape=out_shape, in_specs=[HBM_SPEC] * (2 * n) + [pl.BlockSpec(memory_space=pl.ANY)],
        out_specs=out_specs, input_output_aliases=aliases, compiler_params=pltpu.CompilerParams(has_side_effects=DATAFLOW),
    )(*[pltpu.with_memory_space_constraint(v, pltpu.HBM) for v in list(arrs) + lands], after)
    return [tuple(res[4 * a:4 * a + 4]) for a in range(n)], res[-1]


def _exchange_end(name, handles, after, scatter):
    n = len(handles)
    me = 4 * lax.axis_index("x") + 2 * lax.axis_index("y") + lax.axis_index("c")

    def body(*refs):
        x, y, c, me = _mesh_position()
        for a in range(n):
            src, zone, send_sems, recv_sems = refs[4 * a:4 * a + 4]
            for mask, dev, idx in _peers(x, y, c):
                cp = pltpu.make_async_remote_copy(
                    src_ref=src.at[idx] if scatter else src, dst_ref=zone.at[idx], send_sem=send_sems.at[mask - 1],
                    recv_sem=recv_sems.at[mask - 1], device_id=dev, device_id_type=MESH)
                cp.wait_send()
                cp.wait_recv()

    operands, in_specs, out_shape, aliases = [], [], [], {}
    for a, (send_sems, recv_sems, src, zone) in enumerate(handles):
        operands += [src, zone, send_sems, recv_sems]
        in_specs += [HBM_SPEC, HBM_SPEC, SEM_SPEC, SEM_SPEC]
        out_shape += [pltpu.HBM(src.shape, src.dtype), pltpu.HBM(zone.shape, zone.dtype)]
        aliases[4 * a] = 2 * a
        aliases[4 * a + 1] = 2 * a + 1
    res = pl.pallas_call(
        body, name=name, out_shape=out_shape, in_specs=in_specs + [pl.BlockSpec(memory_space=pl.ANY)],
        out_specs=[HBM_SPEC] * (2 * n), input_output_aliases=aliases,
        compiler_params=pltpu.CompilerParams(has_side_effects=DATAFLOW),
    )(*operands, after)
    full = []
    for a in range(n):
        src, zone = res[2 * a], res[2 * a + 1]
        own = lax.dynamic_index_in_dim(src, me, 0, keepdims=False) if scatter else src
        full.append(lax.dynamic_update_index_in_dim(zone, own, me, 0))
    return full


def _bspec(shape, tr, tc, rc, buffers=None):
    per = shape[-1] // tc
    mode = {} if buffers is None else dict(pipeline_mode=pl.Buffered(buffers))
    if len(shape) == 3:
        return pl.BlockSpec((None, tr, tc), lambda j, i, k: (rc(i, j, k)[1] // per, rc(i, j, k)[0], rc(i, j, k)[1] % per), **mode)
    return pl.BlockSpec((shape[0], None, tr, tc),
                        lambda j, i, k: (0, rc(i, j, k)[1] // per, rc(i, j, k)[0], rc(i, j, k)[1] % per), **mode)


def _ij(i, j, k):
    return i, j


def _row_spec(tn, col_tile_offset=0):
    return pl.BlockSpec((1, tn), lambda j, i, k: (0, j + col_tile_offset))


def _matmul(name, a, b, mode, tm, tn, tk, outs, epilogue=None, extras=(), b_buffers=None):
    groups = b.shape[0] if b.ndim == 4 else 1
    if mode == "nn":
        m, k_dim, n = a.shape[1], a.shape[0] * a.shape[2], b.shape[-3] * b.shape[-1]
        a_spec = _bspec(a.shape, tm, tk, lambda i, j, k: (i, k))
        b_spec = _bspec(b.shape, tk, tn, lambda i, j, k: (k, j), b_buffers)
        dims = NN
    elif mode == "nt":
        m, k_dim, n = a.shape[1], a.shape[0] * a.shape[2], b.shape[-2]
        a_spec = _bspec(a.shape, tm, tk, lambda i, j, k: (i, k))
        b_spec = _bspec(b.shape, tn, tk, lambda i, j, k: (j, k), b_buffers)
        dims = NT
    else:
        m, k_dim, n = a.shape[0] * a.shape[2], a.shape[1], b.shape[-3] * b.shape[-1]
        a_spec = _bspec(a.shape, tk, tm, lambda i, j, k: (k, i))
        b_spec = _bspec(b.shape, tk, tn, lambda i, j, k: (k, j), b_buffers)
        dims = TN
    assert m % tm == 0 and n % tn == 0 and k_dim % tk == 0, (name, m, n, k_dim, tm, tn, tk)
    nk = k_dim // tk
    n_extra, n_out = len(extras), len(outs)

    def finish(acc, extra_refs, out_refs):
        if epilogue is None:
            out_refs[0][...] = acc[0].astype(out_refs[0].dtype)
        else:
            epilogue(acc, extra_refs, out_refs)

    def products(a_ref, b_ref):
        a_tile = a_ref[...].astype(BF16)
        return [lax.dot_general(a_tile, (b_ref[g] if b.ndim == 4 else b_ref[...]).astype(BF16), dims,
                                preferred_element_type=F32) for g in range(groups)]

    def body_whole_k(*refs):
        finish(products(refs[0], refs[1]), refs[2:2 + n_extra], refs[2 + n_extra:])

    def body_k_steps(*refs):
        acc_ref = refs[-1]
        k = pl.program_id(2)

        @pl.when(k == 0)
        def _():
            acc_ref[...] = jnp.zeros_like(acc_ref)

        for g, p in enumerate(products(refs[0], refs[1])):
            acc_ref[g] += p

        @pl.when(k == nk - 1)
        def _():
            finish([acc_ref[g] for g in range(groups)], refs[2:2 + n_extra], refs[2 + n_extra:2 + n_extra + n_out])

    return pl.pallas_call(
        body_whole_k if nk == 1 else body_k_steps, name=name, grid=(n // tn, m // tm, nk),
        in_specs=[a_spec, b_spec] + [s for _, s in extras],
        out_specs=[s for _, _, s in outs],
        out_shape=[jax.ShapeDtypeStruct(shape, dtype) for shape, dtype, _ in outs],
        scratch_shapes=[] if nk == 1 else [pltpu.VMEM((groups, tm, tn), F32)],
        compiler_params=_params(("parallel", "parallel", "arbitrary")),
    )(a, b, *[arr for arr, _ in extras])


def _rowwise(name, fn, rows, vecs, outs, accs, tm):
    s = rows[0][0].shape[0]
    n_in, n_out = len(rows) + len(vecs), len(outs)

    def body(*refs):
        i = pl.program_id(0)
        res = fn(*[r[...] for r in refs[:n_in]])
        res = res if isinstance(res, tuple) else (res,)
        out_refs, acc_refs = refs[n_in:n_in + n_out], refs[n_in + n_out:]
        for ref, val in zip(out_refs, res[:n_out]):
            ref[...] = val.astype(ref.dtype)

        @pl.when(i == 0)
        def _():
            for ref in acc_refs:
                ref[...] = jnp.zeros_like(ref)

        for ref, val in zip(acc_refs, res[n_out:]):
            ref[...] += val

    in_specs = [pl.BlockSpec((tm, w), functools.partial(lambda i, cb: (i, cb), cb=cb)) for _, w, cb in rows]
    in_specs += [pl.BlockSpec(v.shape, lambda i: (0,) * v.ndim) for v in vecs]
    out_specs = [pl.BlockSpec((tm, w), lambda i: (i, 0)) for w, _ in outs] + [pl.BlockSpec((1, w), lambda i: (0, 0)) for w in accs]
    out_shape = [jax.ShapeDtypeStruct((s, w), dt) for w, dt in outs] + [jax.ShapeDtypeStruct((1, w), F32) for w in accs]
    return pl.pallas_call(
        body, name=name, grid=(s // tm,), in_specs=in_specs, out_specs=out_specs, out_shape=out_shape,
        compiler_params=_params(("arbitrary",)),
    )(*[r for r, _, _ in rows], *vecs)


def _colsum(v):
    return jnp.sum(v, axis=0, keepdims=True)


def _norm_mod(name, h, nw, sc, sh, tm):
    d = h.shape[1]

    def fn(hb, nwb, scb, shb):
        r = lax.rsqrt(jnp.mean(hb * hb, axis=-1, keepdims=True) + EPS)
        return (hb * r) * nwb * (1.0 + scb) + shb

    return _rowwise(name, fn, [(h, d, 0)], [nw, sc, sh], [(d, BF16)], [], tm)[0]


def _norm_mod_bwd(name, dy, h, dh_next, nw, sc, tm):
    d = h.shape[1]

    def fn(dyb, hb, dhb, nwb, scb):
        r = lax.rsqrt(jnp.mean(hb * hb, axis=-1, keepdims=True) + EPS)
        xh = hb * r
        dxh = dyb * (nwb * (1.0 + scb))
        dx = r * (dxh - xh * jnp.mean(dxh * xh, axis=-1, keepdims=True))
        return dhb + dx, _colsum(dyb), _colsum(dyb * xh * nwb), _colsum(dyb * xh * (1.0 + scb))

    return _rowwise(name, fn, [(dy, d, 0), (h, d, 0), (dh_next, d, 0)], [nw, sc], [(d, F32)], [d, d, d], tm)


def _gate_bwd(name, dh, o, g, coef, tm):
    d = dh.shape[1]

    def fn(dhb, ob, gb):
        return dhb * (coef * gb), _colsum(dhb * ob * coef)

    return _rowwise(name, fn, [(dh, d, 0), (o, d, 0)], [g], [(d, BF16)], [d], tm)


def _loss_bwd(name, h, target, nw, tm):
    d = h.shape[1]

    def fn(hb, tb, nwb):
        r = lax.rsqrt(jnp.mean(hb * hb, axis=-1, keepdims=True) + EPS)
        xh = hb * r
        err = xh * nwb - tb
        dy = err * (1.0 / d)
        dxh = dy * nwb
        dx = r * (dxh - xh * jnp.mean(dxh * xh, axis=-1, keepdims=True))
        loss = 0.5 * jnp.sum(jnp.mean(err * err, axis=-1, keepdims=True), axis=0, keepdims=True)
        return dx, jnp.broadcast_to(loss, (1, LANE)), _colsum(dy * xh)

    return _rowwise(name, fn, [(h, d, 0), (target, d, 0)], [nw], [(d, F32)], [LANE, d], tm)


def _head_group(nh):
    return 2 if nh % 2 == 0 else 1


def _attn_fwd(qkv, nh, t):
    s = qkv.shape[0]
    scale = HEAD_DIM ** -0.5
    hp = _head_group(nh)
    wide = hp * HEAD_DIM
    lanes = [slice(u * HEAD_DIM, (u + 1) * HEAD_DIM) for u in range(hp)]

    def body(q_ref, k_ref, v_ref, y_ref, tot_ref):
        i = pl.program_id(1)
        row = lax.broadcasted_iota(jnp.int32, (t, t), 0)
        col = lax.broadcasted_iota(jnp.int32, (t, t), 1)
        later = (row > col).astype(BF16)
        causal = col < row
        qs = [q_ref[:, ln] for ln in lanes]

        def block(j, carry, diagonal):
            ks = pl.ds(pl.multiple_of(j * t, t), t)
            out = []
            for u, ln in enumerate(lanes):
                o, run = carry[u]
                z = lax.dot_general(qs[u], k_ref[ks, ln], NT, preferred_element_type=F32) * scale
                sp = _softplus(z)
                log_keep = jnp.where(causal, -sp, 0.0) if diagonal else -sp
                w = jnp.exp(z - sp + _dot_split(log_keep, later) + run)
                if diagonal:
                    w = jnp.where(causal, w, 0.0)
                o = o + jnp.dot(w.astype(BF16), v_ref[ks, ln], preferred_element_type=F32)
                out.append((o, run + jnp.sum(log_keep, axis=1, keepdims=True)))
            return tuple(out)

        carry = tuple((jnp.zeros((t, HEAD_DIM), F32), jnp.zeros((t, 1), F32)) for _ in lanes)
        carry = block(i, carry, True)
        carry = lax.fori_loop(0, i, lambda jj, cr: block(i - 1 - jj, cr, False), carry)
        for u, ln in enumerate(lanes):
            y_ref[:, ln] = carry[u][0].astype(y_ref.dtype)
            tot_ref[:, ln] = jnp.broadcast_to(carry[u][1], (t, HEAD_DIM))

    g = nh // hp
    return pl.pallas_call(
        body, name="attn_fwd", grid=(g, s // t),
        in_specs=[pl.BlockSpec((t, wide), lambda h, i: (i, h)),
                  pl.BlockSpec((s, wide), lambda h, i: (0, g + h)),
                  pl.BlockSpec((s, wide), lambda h, i: (0, 2 * g + h))],
        out_specs=[pl.BlockSpec((t, wide), lambda h, i: (i, h)), pl.BlockSpec((t, wide), lambda h, i: (i, h))],
        out_shape=[jax.ShapeDtypeStruct((s, nh * HEAD_DIM), BF16), jax.ShapeDtypeStruct((s, nh * HEAD_DIM), F32)],
        compiler_params=_params(("parallel", "arbitrary")),
    )(qkv, qkv, qkv)


def _attn_bwd(qkv, dy, tot, nh, t):
    s = qkv.shape[0]
    scale = HEAD_DIM ** -0.5
    hp = _head_group(nh)
    wide = hp * HEAD_DIM
    lanes = [slice(u * HEAD_DIM, (u + 1) * HEAD_DIM) for u in range(hp)]

    def body(q_ref, k_ref, v_ref, dy_ref, tot_ref, dq_ref, dk_ref, dv_ref):
        i = pl.program_id(1)

        @pl.when(i == 0)
        def _():
            dk_ref[...] = jnp.zeros_like(dk_ref)
            dv_ref[...] = jnp.zeros_like(dv_ref)

        row = lax.broadcasted_iota(jnp.int32, (t, t), 0)
        col = lax.broadcasted_iota(jnp.int32, (t, t), 1)
        upto = (row <= col).astype(BF16)
        before = (row < col).astype(BF16)
        causal = col < row
        qs = [q_ref[:, ln] for ln in lanes]
        dys = [dy_ref[:, ln] for ln in lanes]
        totals = [tot_ref[:, u * HEAD_DIM:u * HEAD_DIM + 1] for u in range(hp)]

        def block(j, carry, diagonal):
            ks = pl.ds(pl.multiple_of(j * t, t), t)
            out = []
            for u, ln in enumerate(lanes):
                dq, passed, g_run = carry[u]
                kb, vb = k_ref[ks, ln], v_ref[ks, ln]
                z = lax.dot_general(qs[u], kb, NT, preferred_element_type=F32) * scale
                sp = _softplus(z)
                log_keep = jnp.where(causal, -sp, 0.0) if diagonal else -sp
                w = jnp.exp(z - sp + (totals[u] - passed - _dot_split(log_keep, upto)))
                if diagonal:
                    w = jnp.where(causal, w, 0.0)
                g = lax.dot_general(dys[u], vb, NT, preferred_element_type=F32) * w
                dz = (g * jnp.exp(-sp) - jnp.exp(z - sp) * (g_run + _dot_split(g, before))) * scale
                if diagonal:
                    dz = jnp.where(causal, dz, 0.0)
                dzb = dz.astype(BF16)
                dq = dq + jnp.dot(dzb, kb, preferred_element_type=F32)
                dk_ref[ks, ln] += lax.dot_general(dzb, qs[u], TN, preferred_element_type=F32)
                dv_ref[ks, ln] += lax.dot_general(w.astype(BF16), dys[u], TN, preferred_element_type=F32)
                out.append((dq, passed + jnp.sum(log_keep, axis=1, keepdims=True), g_run + jnp.sum(g, axis=1, keepdims=True)))
            return tuple(out)

        zero = jnp.zeros((t, 1), F32)
        carry = tuple((jnp.zeros((t, HEAD_DIM), F32), zero, zero) for _ in lanes)
        carry = lax.fori_loop(0, i, lambda j, cr: block(j, cr, False), carry)
        carry = block(i, carry, True)
        for u, ln in enumerate(lanes):
            dq_ref[:, ln] = carry[u][0]

    g = nh // hp
    tile = lambda off: pl.BlockSpec((t, wide), lambda h, i: (i, off + h))
    head = lambda off: pl.BlockSpec((s, wide), lambda h, i: (0, off + h))
    return pl.pallas_call(
        body, name="attn_bwd", grid=(g, s // t),
        in_specs=[tile(0), head(g), head(2 * g), tile(0), tile(0)],
        out_specs=[tile(0), head(0), head(0)],
        out_shape=[jax.ShapeDtypeStruct((s, nh * HEAD_DIM), F32)] * 3,
        compiler_params=_params(("parallel", "arbitrary")),
    )(qkv, qkv, qkv, dy, tot)


def _lru_gates(xc, w_r, b_r, w_i, b_i, lam):
    xb = xc.astype(BF16)
    r = _sigmoid(jnp.dot(xb, w_r.astype(BF16), preferred_element_type=F32) + b_r)
    i = _sigmoid(jnp.dot(xb, w_i.astype(BF16), preferred_element_type=F32) + b_i)
    neg_lam = -lam
    sp_lam = jnp.maximum(neg_lam, 0.0) + _log1p(jnp.exp(-jnp.abs(neg_lam)))
    log_a = -LRU_C * r * sp_lam
    a = jnp.exp(log_a)
    mult = jnp.sqrt(-_expm1(2.0 * log_a))
    return r, i, sp_lam, a, mult


def _conv_taps(xpad_chunk, conv_w, t):
    shifted = [xpad_chunk[CONV_HALO:, :]]
    for d in range(1, CONV_WIDTH):
        shifted.append(pltpu.roll(xpad_chunk, d, 0)[CONV_HALO:, :])
    weights = [conv_w[CONV_WIDTH - 1 - d:CONV_WIDTH - d, :] for d in range(CONV_WIDTH)]
    return shifted, weights


def _lru_fwd(xr_pad, proj, gr_block0, conv_w, conv_b, w_r, b_r, w_i, b_i, lam, t):
    s, w = xr_pad.shape[0] - CONV_HALO, xr_pad.shape[1]
    nblk = w // LANE
    nchunk = s // t
    steps = [1 << p for p in range(t.bit_length() - 1)]
    assert (1 << (t.bit_length() - 1)) == t and w_r.shape[1:] == (LANE, LANE)

    def body(x_ref, gr_ref, cw_ref, cb_ref, wr_ref, br_ref, wi_ref, bi_ref, lam_ref, h_ref, hp_ref, xc_ref, y_ref):
        row = lax.broadcasted_iota(jnp.int32, (t, LANE), 0)

        def chunk(ci, h_in):
            t0 = pl.multiple_of(ci * t, t)
            shifted, weights = _conv_taps(x_ref[pl.ds(t0, t + CONV_HALO), :], cw_ref[...], t)
            xc = cb_ref[...] + sum(wd * xs for wd, xs in zip(weights, shifted))
            r, i, _, a, mult = _lru_gates(xc, wr_ref[...], br_ref[...], wi_ref[...], bi_ref[...], lam_ref[...])
            coef, val = a, mult * (i * xc)
            for d in steps:
                ok = row >= d
                val = jnp.where(ok, coef * pltpu.roll(val, d, 0) + val, val)
                coef = jnp.where(ok, coef * pltpu.roll(coef, d, 0), coef)
            h = val + coef * h_in
            rows = pl.ds(t0, t)
            h_ref[rows, :] = h
            hp_ref[rows, :] = jnp.where(row == 0, h_in, pltpu.roll(h, 1, 0))
            xc_ref[rows, :] = xc
            y_ref[rows, :] = (h * _gelu_and_grad(gr_ref[rows, :])[0]).astype(y_ref.dtype)
            return h[t - 1:t, :]

        lax.fori_loop(0, nchunk, chunk, jnp.zeros((1, LANE), F32))

    col = lambda rows: pl.BlockSpec((rows, LANE), lambda n: (0, n))
    return pl.pallas_call(
        body, name="lru_fwd", grid=(nblk,),
        in_specs=[col(s + CONV_HALO), pl.BlockSpec((s, LANE), lambda n: (0, gr_block0 + n)), col(CONV_WIDTH), col(1),
                  pl.BlockSpec((None, LANE, LANE), lambda n: (n, 0, 0)), col(1),
                  pl.BlockSpec((None, LANE, LANE), lambda n: (n, 0, 0)), col(1), col(1)],
        out_specs=[col(s)] * 4,
        out_shape=[jax.ShapeDtypeStruct((s, w), F32)] * 3 + [jax.ShapeDtypeStruct((s, w), BF16)],
        compiler_params=_params(("parallel",)),
    )(xr_pad, proj, conv_w, conv_b, w_r, b_r, w_i, b_i, lam)


def _lru_bwd(dy, proj, gr_block0, h, h_prev, xc, w_r, b_r, w_i, b_i, lam, t):
    s, w = dy.shape
    nblk = w // LANE
    nchunk = s // t
    steps = [1 << p for p in range(t.bit_length() - 1)]

    def body(dy_ref, gr_ref, h_ref, hp_ref, xc_ref, wr_ref, br_ref, wi_ref, bi_ref, lam_ref,
             dgr_ref, dxc_ref, dwr_ref, dwi_ref, dbr_ref, dbi_ref, dlam_ref):
        row = lax.broadcasted_iota(jnp.int32, (t, LANE), 0)
        for ref in (dwr_ref, dwi_ref, dbr_ref, dbi_ref, dlam_ref):
            ref[...] = jnp.zeros_like(ref)

        def chunk(cc, carry):
            lam_next, a_next = carry
            rows = pl.ds(pl.multiple_of((nchunk - 1 - cc) * t, t), t)
            dyb, hb, xcb = dy_ref[rows, :], h_ref[rows, :], xc_ref[rows, :]
            gel, dgel = _gelu_and_grad(gr_ref[rows, :])
            dgr_ref[rows, :] = dyb * hb * dgel
            w_r, w_i = wr_ref[...], wi_ref[...]
            r, i, sp_lam, a, mult = _lru_gates(xcb, w_r, br_ref[...], w_i, bi_ref[...], lam_ref[...])
            coef = jnp.where(row == t - 1, a_next, pltpu.roll(a, t - 1, 0))
            val = dyb * gel
            for d in steps:
                ok = row < t - d
                val = jnp.where(ok, coef * pltpu.roll(val, t - d, 0) + val, val)
                coef = jnp.where(ok, coef * pltpu.roll(coef, t - d, 0), coef)
            adj = val + coef * lam_next
            da = adj * hp_ref[rows, :]
            v = i * xcb
            dmult, dv = adj * v, adj * mult
            dlog_a = da * a - (a * a) * dmult / mult
            dr_pre = (-LRU_C * sp_lam) * dlog_a * r * (1.0 - r)
            di_pre = dv * xcb * i * (1.0 - i)
            dlam_ref[...] += _colsum(-LRU_C * r * dlog_a)
            dbr_ref[...] += _colsum(dr_pre)
            dbi_ref[...] += _colsum(di_pre)
            xb, drb, dib = xcb.astype(BF16), dr_pre.astype(BF16), di_pre.astype(BF16)
            dwr_ref[...] += lax.dot_general(xb, drb, TN, preferred_element_type=F32)
            dwi_ref[...] += lax.dot_general(xb, dib, TN, preferred_element_type=F32)
            dxc_ref[rows, :] = (dv * i + lax.dot_general(drb, w_r.astype(BF16), NT, preferred_element_type=F32)
                                + lax.dot_general(dib, w_i.astype(BF16), NT, preferred_element_type=F32))
            return adj[0:1, :], a[0:1, :]

        lax.fori_loop(0, nchunk, chunk, (jnp.zeros((1, LANE), F32), jnp.zeros((1, LANE), F32)))
        dlam_ref[...] = dlam_ref[...] * (-_sigmoid(-lam_ref[...]))

    col = lambda rows: pl.BlockSpec((rows, LANE), lambda n: (0, n))
    mat = pl.BlockSpec((None, LANE, LANE), lambda n: (n, 0, 0))
    return pl.pallas_call(
        body, name="lru_bwd", grid=(nblk,),
        in_specs=[col(s), pl.BlockSpec((s, LANE), lambda n: (0, gr_block0 + n)), col(s), col(s), col(s),
                  mat, col(1), mat, col(1), col(1)],
        out_specs=[col(s), col(s), mat, mat, col(1), col(1), col(1)],
        out_shape=[jax.ShapeDtypeStruct((s, w), F32)] * 2 + [jax.ShapeDtypeStruct((nblk, LANE, LANE), F32)] * 2
        + [jax.ShapeDtypeStruct((1, w), F32)] * 3,
        compiler_params=_params(("parallel",)),
    )(dy, proj, h, h_prev, xc, w_r, b_r, w_i, b_i, lam)


def _conv_bwd(xr_pad, dxc_pad, conv_w, t):
    s, w = xr_pad.shape[0] - CONV_HALO, xr_pad.shape[1]
    nchunk = s // t

    def body(x_ref, g_ref, cw_ref, dx_ref, dcw_ref, dcb_ref):
        dcw_ref[...] = jnp.zeros_like(dcw_ref)
        dcb_ref[...] = jnp.zeros_like(dcb_ref)

        def chunk(ci, _):
            t0 = pl.multiple_of(ci * t, t)
            shifted, weights = _conv_taps(x_ref[pl.ds(t0, t + CONV_HALO), :], cw_ref[...], t)
            gpad = g_ref[pl.ds(t0, t + CONV_HALO), :]
            g = gpad[:t, :]
            dx = weights[0] * g
            for d in range(1, CONV_WIDTH):
                dx = dx + weights[d] * pltpu.roll(gpad, t + CONV_HALO - d, 0)[:t, :]
            dx_ref[pl.ds(t0, t), :] = dx
            for d in range(CONV_WIDTH):
                dcw_ref[CONV_WIDTH - 1 - d:CONV_WIDTH - d, :] += _colsum(g * shifted[d])
            dcb_ref[...] += _colsum(g)
            return 0

        lax.fori_loop(0, nchunk, chunk, 0)

    col = lambda rows: pl.BlockSpec((rows, LANE), lambda n: (0, n))
    return pl.pallas_call(
        body, name="conv_bwd", grid=(w // LANE,),
        in_specs=[col(s + CONV_HALO), col(s + CONV_HALO), col(CONV_WIDTH)],
        out_specs=[col(s), col(CONV_WIDTH), col(1)],
        out_shape=[jax.ShapeDtypeStruct((s, w), F32), jax.ShapeDtypeStruct((CONV_WIDTH, w), F32),
                   jax.ShapeDtypeStruct((1, w), F32)],
        compiler_params=_params(("parallel",)),
    )(xr_pad, dxc_pad, conv_w)


def _sum_parts(parts_ref):
    g = parts_ref[0].astype(F32)
    for p in range(1, parts_ref.shape[0]):
        g = g + parts_ref[p].astype(F32)
    return g


def _reduce_parts(name, parts):
    p, r, c = parts.shape
    tr = _tile(r, max(8, (1 << 19) // c), 8)

    def body(parts_ref, g_ref):
        g_ref[...] = _sum_parts(parts_ref)

    return pl.pallas_call(
        body, name=name, grid=(r // tr,), in_specs=[pl.BlockSpec((p, tr, c), lambda i: (0, i, 0))],
        out_specs=pl.BlockSpec((tr, c), lambda i: (i, 0)), out_shape=jax.ShapeDtypeStruct((r, c), F32),
        compiler_params=_params(("parallel",)),
    )(parts)


def _adamw(name, parts, w, m, v):
    p, r, c = parts.shape
    tr = _tile(r, max(8, (1 << 18) // c), 8)

    def body(parts_ref, w_ref, m_ref, v_ref, g_ref, d_ref, nm_ref, nv_ref):
        g = _sum_parts(parts_ref)
        nm = ADAM_B1 * m_ref[...] + (1.0 - ADAM_B1) * g
        nv = ADAM_B2 * v_ref[...] + (1.0 - ADAM_B2) * (g * g)
        m_hat = nm / (1.0 - ADAM_B1 ** ADAM_STEP)
        v_hat = nv / (1.0 - ADAM_B2 ** ADAM_STEP)
        g_ref[...] = g
        d_ref[...] = -ADAM_LR * (m_hat / (jnp.sqrt(v_hat) + ADAM_EPS) + ADAM_WD * w_ref[...])
        nm_ref[...] = nm
        nv_ref[...] = nv

    blk = pl.BlockSpec((tr, c), lambda i: (i, 0))
    return pl.pallas_call(
        body, name=name, grid=(r // tr,), in_specs=[pl.BlockSpec((p, tr, c), lambda i: (0, i, 0)), blk, blk, blk],
        out_specs=[blk] * 4, out_shape=[jax.ShapeDtypeStruct((r, c), F32)] * 4,
        compiler_params=_params(("parallel",)),
    )(parts, w, m, v)


def _ffn_in(tag, y, w_in_g, tm):
    s, d = y.shape
    half = N_DEV // 2
    cb = w_in_g.shape[2]
    ff = half * cb

    def swiglu(acc, extra_refs, out_refs):
        g, u = acc
        out_refs[0][0] = g
        out_refs[0][1] = u
        out_refs[1][...] = (g * _sigmoid(g) * u).astype(BF16)

    gu_shape = (2, 1, s, ff)
    gu, act = _matmul(
        tag + "_in", y[None], w_in_g.reshape(2, half, d, cb), "nn", tm, cb, d,
        outs=[(gu_shape, F32, _bspec(gu_shape, tm, cb, _ij)), ((1, s, ff), BF16, _bspec((1, s, ff), tm, cb, _ij))],
        epilogue=swiglu, b_buffers=1)
    return gu, act


def _ffn_out(tag, act, w_out_g, res, gate, tm):
    _, s, ff = act.shape
    d = res.shape[1]
    tn = _tile(d, 1024)

    def residual(acc, extra_refs, out_refs):
        out_refs[0][...] = acc[0]
        out_refs[1][...] = extra_refs[0][...] + 0.5 * extra_refs[1][...] * acc[0]

    plain = _bspec((1, s, d), tm, tn, _ij)
    o, h_new = _matmul(
        tag + "_out", act, w_out_g.reshape(1, ff, d), "nn", tm, tn, ff,
        outs=[((1, s, d), F32, plain), ((1, s, d), F32, plain)], epilogue=residual,
        extras=[(res[None], plain), (gate, _row_spec(tn))], b_buffers=1)
    return o[0], h_new[0]


def _after_token(token):
    return token, pl.BlockSpec(token.shape, lambda j, i, k: (0, 0))


def _ffn_bwd_weights(tag, dh, y, gu, act, o, gate, w_in_g, w_out_g, tm):
    s, d = dh.shape
    half = N_DEV // 2
    cb = w_in_g.shape[2]
    ff = half * cb
    do, dgate = _gate_bwd(tag + "_gate_bwd", dh, o, gate, 0.5, _tile(s, 256, 8))

    tn_d, tm_f = _tile(d, 1024), _tile(ff, 512)
    dw_out = _matmul(tag + "_dw_out", act, do[None], "tn", tm_f, tn_d, s,
                     outs=[((1, ff, d), BF16, _bspec((1, ff, d), tm_f, tn_d, _ij))], b_buffers=1)[0]

    def dswiglu(acc, extra_refs, out_refs):
        dact = acc[0]
        g, u = extra_refs[0][0], extra_refs[0][1]
        sg = _sigmoid(g)
        out_refs[0][0] = (dact * u * sg * (1.0 + g * (1.0 - sg))).astype(BF16)
        out_refs[0][1] = (dact * g * sg).astype(BF16)

    gu_shape = (2, 1, s, ff)
    gu_spec = _bspec(gu_shape, tm, cb, _ij)
    dgu = _matmul(tag + "_dact", do[None], w_out_g.reshape(1, ff, d), "nt", tm, cb, d,
                  outs=[(gu_shape, BF16, gu_spec)], epilogue=dswiglu, extras=[(gu, gu_spec)], b_buffers=1)[0]
    dgu = dgu.reshape(2, s, ff)

    tm_d = _tile(d, 512)
    dw_in = _matmul(tag + "_dw_in", y[None], dgu, "tn", tm_d, cb, s,
                    outs=[((N_DEV, d, cb), BF16, _bspec((N_DEV, d, cb), tm_d, cb, _ij))], b_buffers=1)[0]
    return dgu, dgate, dw_in, dw_out.reshape(N_DEV, ff // N_DEV, d)


def _ffn_bwd_input(tag, dgu, w_in_g, token):
    s, d = dgu.shape[1], w_in_g.shape[1]
    tm_big = _tile(s, 1024, 8)
    return _matmul(tag + "_dy", dgu, w_in_g, "nt", tm_big, d, w_in_g.shape[2],
                   outs=[((1, s, d), F32, _bspec((1, s, d), tm_big, d, _ij))], extras=[_after_token(token)])[0][0]


def kernel(x, c, w_ada, b_ada, norm_ffn1, w_ffn1_in, w_ffn1_out, norm_mix, w_in, conv_w, conv_b, w_rg_gate, b_rg_gate, w_in_gate, b_in_gate, lru_lambda, w_branch_attn, w_branch_lru, w_out, norm_ffn2, w_ffn2_in, w_ffn2_out, norm_final, loss_target, m_w_ada, m_b_ada, m_norm_ffn1, m_w_ffn1_in, m_w_ffn1_out, m_norm_mix, m_w_in, m_conv_w, m_conv_b, m_w_rg_gate, m_b_rg_gate, m_w_in_gate, m_b_in_gate, m_lru_lambda, m_w_branch_attn, m_w_branch_lru, m_w_out, m_norm_ffn2, m_w_ffn2_in, m_w_ffn2_out, m_norm_final, v_w_ada, v_b_ada, v_norm_ffn1, v_w_ffn1_in, v_w_ffn1_out, v_norm_mix, v_w_in, v_conv_w, v_conv_b, v_w_rg_gate, v_b_rg_gate, v_w_in_gate, v_b_in_gate, v_lru_lambda, v_w_branch_attn, v_w_branch_lru, v_w_out, v_norm_ffn2, v_w_ffn2_in, v_w_ffn2_out, v_norm_final):
    xs, target = x[0], loss_target[0]
    s, d = xs.shape
    aw, lw = w_branch_attn.shape[1], w_branch_lru.shape[1]
    nh, nlb = aw // HEAD_DIM, w_rg_gate.shape[1]
    cba, cbi, cbb, cwb = w_ada.shape[2], w_in.shape[2], w_branch_attn.shape[2], conv_w.shape[2]
    assert lw == nlb * LANE and cwb * N_DEV == lw and 3 * aw + 2 * lw + 2 * d == cbi * N_DEV
    me = 4 * lax.axis_index("x") + 2 * lax.axis_index("y") + lax.axis_index("c")
    tm = _tile(s, 512, 8)
    tr = _tile(s, 256, 8)
    t_attn = _tile(s, 256, 8)
    t_lru = _tile(s, 256, 8)

    small = _exchange("gather_c", [jnp.concatenate([c, conv_w.reshape(1, CONV_WIDTH * cwb)], axis=1)], False)[0][:, 0, :]
    c_all = small[:, :d]
    conv_w_full = small[:, d:].reshape(N_DEV, CONV_WIDTH, cwb).transpose(1, 0, 2).reshape(CONV_WIDTH, lw)
    c_act = _rowwise("silu_c", lambda v: v * _sigmoid(v), [(c_all, d, 0)], [], [(d, F32)], [], N_DEV)[0]

    def add_bias(acc_ref, extra_refs, out_refs):
        out_refs[0][...] = acc_ref[0] + extra_refs[0][...]

    b_ada_mine = lax.dynamic_slice(b_ada, (0, me * cba), (1, cba))
    mod_part = _matmul("mod", c_act[None], w_ada, "nn", N_DEV, cba, _tile(d, 512),
                       outs=[((1, N_DEV, cba), F32, _bspec((1, N_DEV, cba), N_DEV, cba, _ij))], epilogue=add_bias,
                       extras=[(b_ada_mine, _row_spec(cba))])[0][0]
    mod_all = _exchange("gather_mod", [mod_part], False)[0]
    mod = lax.dynamic_index_in_dim(mod_all, me, axis=1, keepdims=False).reshape(1, 9 * d)
    sh1, sc1, g1, sh2, sc2, g2, sh3, sc3, g3 = [mod[:, n * d:(n + 1) * d] for n in range(9)]

    shards = [w_ffn1_in[0], w_ffn1_out[0], w_in[0], w_branch_attn[0], w_branch_lru[0], w_out[0], w_ffn2_in[0], w_ffn2_out[0]]
    gathers, token = _exchange_begin("gather_w", [w.astype(BF16) for w in shards], False, mod)

    def gathered(n, after):
        return _exchange_end("gathered_w%d" % n, [gathers[n]], after, False)[0]

    y1 = _norm_mod("norm1", xs, norm_ffn1 + token[:1, :1], sc1, sh1, tr)
    wf1i = gathered(0, y1)
    gu1, act1 = _ffn_in("ffn1", y1, wf1i, tm)
    wf1o = gathered(1, act1)
    o1, h1 = _ffn_out("ffn1", act1, wf1o, xs, g1, tm)

    y2 = _norm_mod("norm2", h1, norm_mix, sc2, sh2, tr)
    wi_g = gathered(2, y2)
    tn_i = _tile(cbi, 1152)
    proj = _matmul("mix_in", y2[None], wi_g, "nn", tm, tn_i, d,
                   outs=[((1, s, N_DEV * cbi), F32, _bspec((1, s, N_DEV * cbi), tm, tn_i, _ij))], b_buffers=1)[0][0]
    off_xr, off_gr, off_ga, off_gl = 3 * aw, 3 * aw + lw, 3 * aw + 2 * lw, 3 * aw + 2 * lw + d
    qkv = proj[:, :3 * aw].astype(BF16)
    y_attn, attn_tot = _attn_fwd(qkv, nh, t_attn)
    xr_pad = jnp.pad(proj[:, off_xr:off_xr + lw], ((CONV_HALO, 0), (0, 0)))
    w_r, w_i = w_rg_gate[0], w_in_gate[0]
    h_lru, h_prev, xc, y_lru = _lru_fwd(xr_pad, proj, off_gr // LANE, conv_w_full, conv_b, w_r, b_rg_gate, w_i,
                                        b_in_gate, lru_lambda, t_lru)
    plain_b = _bspec((1, s, d), tm, cbb, _ij)
    wba_g = gathered(3, y_attn)
    wbl_g = gathered(4, y_lru)
    ya = _matmul("branch_attn", y_attn[None], wba_g, "nn", tm, cbb, _tile(aw, 1024), outs=[((1, s, d), F32, plain_b)])[0]

    def merge(acc_ref, extra_refs, out_refs):
        yl = acc_ref[0]
        ya_t, ga, gl = extra_refs[0][...], extra_refs[1][...], extra_refs[2][...]
        out_refs[0][...] = yl
        out_refs[1][...] = (_sigmoid(ga) * ya_t + _sigmoid(gl) * yl).astype(BF16)

    proj3 = proj[None]
    ga_spec = _bspec(proj3.shape, tm, cbb, lambda i, j, k: (i, j + off_ga // cbb))
    gl_spec = _bspec(proj3.shape, tm, cbb, lambda i, j, k: (i, j + off_gl // cbb))
    yl, merged = _matmul("branch_lru", y_lru[None], wbl_g, "nn", tm, cbb, _tile(lw, 1024),
                         outs=[((1, s, d), F32, plain_b), ((1, s, d), BF16, plain_b)], epilogue=merge,
                         extras=[(ya, plain_b), (proj3, ga_spec), (proj3, gl_spec)])
    tn_d = _tile(d, 1024)
    plain = _bspec((1, s, d), tm, tn_d, _ij)

    def residual(acc_ref, extra_refs, out_refs):
        o = acc_ref[0]
        out_refs[0][...] = o
        out_refs[1][...] = extra_refs[0][...] + extra_refs[1][...] * o

    wo_g = gathered(5, merged)
    mo, h2 = _matmul("mix_out", merged, wo_g.reshape(1, d, d), "nn", tm, tn_d, d,
                     outs=[((1, s, d), F32, plain), ((1, s, d), F32, plain)], epilogue=residual,
                     extras=[(h1[None], plain), (g2, _row_spec(tn_d))], b_buffers=1)
    mo, h2 = mo[0], h2[0]

    y3 = _norm_mod("norm3", h2, norm_ffn2, sc3, sh3, tr)
    wf2i = gathered(6, y3)
    gu3, act3 = _ffn_in("ffn2", y3, wf2i, tm)
    wf2o = gathered(7, act3)
    o3, h3 = _ffn_out("ffn2", act3, wf2o, h2, g3, tm)

    nf = norm_final.reshape(1, d)
    dh3, loss_part, d_nf = _loss_bwd("loss", h3, target, nf, tr)
    dgu3, dg3, dwf2i, dwf2o = _ffn_bwd_weights("ffn2", dh3, y3, gu3, act3, o3, g3, wf2i, wf2o, tm)
    scatter_ffn2, token = _exchange_begin("scatter_ffn2", [dwf2i, dwf2o], True, dwf2i)
    dy3 = _ffn_bwd_input("ffn2", dgu3, wf2i, token)
    dh2, dsh3, dsc3, dn3 = _norm_mod_bwd("norm3_bwd", dy3, h2, dh3, norm_ffn2, sc3, tr)

    dmo, dg2 = _gate_bwd("mix_gate_bwd", dh2, mo, g2, 1.0, tr)
    dwo = _matmul("mix_dw_out", merged, dmo[None], "tn", _tile(d, 512), tn_d, s,
                  outs=[((1, d, d), BF16, _bspec((1, d, d), _tile(d, 512), tn_d, _ij))], b_buffers=1)[0]

    def dmerge(acc_ref, extra_refs, out_refs):
        dm = acc_ref[0]
        ya_t, yl_t = extra_refs[0][...], extra_refs[1][...]
        sa, sl = _sigmoid(extra_refs[2][...]), _sigmoid(extra_refs[3][...])
        out_refs[0][...] = (dm * sa).astype(BF16)
        out_refs[1][...] = (dm * sl).astype(BF16)
        out_refs[2][...] = (dm * ya_t * sa * (1.0 - sa)).astype(BF16)
        out_refs[3][...] = (dm * yl_t * sl * (1.0 - sl)).astype(BF16)

    tn_m = _tile(math.gcd(d, off_ga, off_gl), 1024)
    plain_m = _bspec((1, s, d), tm, tn_m, _ij)
    gate_specs = [_bspec(proj3.shape, tm, tn_m, functools.partial(lambda i, j, k, o: (i, j + o), o=o // tn_m))
                  for o in (off_ga, off_gl)]
    dya, dyl, dga, dgl = _matmul("mix_dmerged", dmo[None], wo_g.reshape(1, d, d), "nt", tm, tn_m, d,
                                 outs=[((1, s, d), BF16, plain_m)] * 4, epilogue=dmerge,
                                 extras=[(ya, plain_m), (yl, plain_m), (proj3, gate_specs[0]), (proj3, gate_specs[1])],
                                 b_buffers=1)
    tm_a, tm_l = _tile(aw, 1024), _tile(lw, 1024)
    dwba = _matmul("dw_branch_attn", y_attn[None], dya, "tn", tm_a, cbb, s,
                   outs=[((N_DEV, aw, cbb), BF16, _bspec((N_DEV, aw, cbb), tm_a, cbb, _ij))])[0]
    dwbl = _matmul("dw_branch_lru", y_lru[None], dyl, "tn", tm_l, cbb, s,
                   outs=[((N_DEV, lw, cbb), BF16, _bspec((N_DEV, lw, cbb), tm_l, cbb, _ij))])[0]
    tn_a, tn_l = _tile(aw, 1024), _tile(lw, 1024)
    dy_attn = _matmul("d_attn_out", dya, wba_g, "nt", tm, tn_a, cbb,
                      outs=[((1, s, aw), BF16, _bspec((1, s, aw), tm, tn_a, _ij))])[0][0]
    dy_lru = _matmul("d_lru_out", dyl, wbl_g, "nt", tm, tn_l, cbb,
                     outs=[((1, s, lw), F32, _bspec((1, s, lw), tm, tn_l, _ij))])[0][0]
    dq, dk, dv = _attn_bwd(qkv, dy_attn, attn_tot, nh, t_attn)
    dgr, dxc, d_wr, d_wi, d_br, d_bi, d_lam = _lru_bwd(dy_lru, proj, off_gr // LANE, h_lru, h_prev, xc, w_r, b_rg_gate,
                                                       w_i, b_in_gate, lru_lambda, t_lru)
    dxr, d_cw, d_cb = _conv_bwd(xr_pad, jnp.pad(dxc, ((0, CONV_HALO), (0, 0))), conv_w_full, t_lru)
    dproj = jnp.concatenate([dq.astype(BF16), dk.astype(BF16), dv.astype(BF16), dxr.astype(BF16), dgr.astype(BF16),
                             dga[0], dgl[0]], axis=1)
    tm_d = _tile(d, 512)
    dwi = _matmul("mix_dw_in", y2[None], dproj[None], "tn", tm_d, tn_i, s,
                  outs=[((N_DEV, d, cbi), BF16, _bspec((N_DEV, d, cbi), tm_d, tn_i, _ij))], b_buffers=1)[0]
    scatter_mix, token = _exchange_begin(
        "scatter_mix", [dwi, dwba, dwbl, dwo.reshape(N_DEV, d // N_DEV, d)], True, dwi)
    tm_big = _tile(s, 1024, 8)
    dy2 = _matmul("mix_dy", dproj[None], wi_g, "nt", tm_big, d, tn_i,
                  outs=[((1, s, d), F32, _bspec((1, s, d), tm_big, d, _ij))], extras=[_after_token(token)])[0][0]
    dh1, dsh2, dsc2, dn2 = _norm_mod_bwd("norm2_bwd", dy2, h1, dh2, norm_mix, sc2, tr)

    dgu1, dg1, dwf1i, dwf1o = _ffn_bwd_weights("ffn1", dh1, y1, gu1, act1, o1, g1, wf1i, wf1o, tm)
    scatter_ffn1, token = _exchange_begin("scatter_ffn1", [dwf1i, dwf1o], True, dwf1i)
    dy1 = _ffn_bwd_input("ffn1", dgu1, wf1i, token)
    grad_x, dsh1, dsc1, dn1 = _norm_mod_bwd("norm1_bwd", dy1, xs, dh1, norm_ffn1, sc1, tr)

    results = {}

    def update_group(wait_name, handles, leaves, after):
        for (n, w, m, v), parts in zip(leaves, _exchange_end(wait_name, handles, after, True)):
            results[n] = [o[None] for o in _adamw("adamw_" + n, parts, w[0], m[0], v[0])]
        return results[leaves[-1][0]][0]

    done = update_group("scattered_ffn2", scatter_ffn2, [("w_ffn2_in", w_ffn2_in, m_w_ffn2_in, v_w_ffn2_in),
                                                         ("w_ffn2_out", w_ffn2_out, m_w_ffn2_out, v_w_ffn2_out)], grad_x)
    done = update_group("scattered_mix", scatter_mix, [("w_in", w_in, m_w_in, v_w_in),
                                                       ("w_branch_attn", w_branch_attn, m_w_branch_attn, v_w_branch_attn),
                                                       ("w_branch_lru", w_branch_lru, m_w_branch_lru, v_w_branch_lru),
                                                       ("w_out", w_out, m_w_out, v_w_out)], done)

    lane_pad = jnp.zeros((1, 7 * LANE), F32)
    pack = jnp.concatenate(
        [loss_part, lane_pad, dsh1, dsc1, dg1, dsh2, dsc2, dg2, dsh3, dsc3, dg3, dn1, dn2, dn3, d_nf, d_cb, d_br, d_bi, d_lam,
         d_wr.reshape(1, -1), d_wi.reshape(1, -1), d_cw.reshape(1, -1)], axis=1)
    pack = jnp.pad(pack, ((0, 0), (0, -pack.shape[1] % (8 * LANE))))
    n_pack = pack.shape[1]
    packs = _exchange("gather_small", [pack], False, done)[0].reshape(N_DEV, n_pack // LANE, LANE)
    g_pack = _reduce_parts("sum_small", packs).reshape(1, n_pack)
    loss = g_pack[0, 0]
    off = 8 * LANE
    n_adam = 9 * d + 4 * d + 4 * lw + 2 * nlb * LANE * LANE
    g_small = g_pack[:, off:off + n_adam].reshape(1, n_adam // LANE, LANE)
    d_cw_sum = g_pack[:, off + n_adam:off + n_adam + CONV_WIDTH * lw].reshape(CONV_WIDTH, lw)
    d_cw_mine = lax.dynamic_slice(d_cw_sum, (0, me * cwb), (CONV_WIDTH, cwb))

    small_names = ["b_ada", "norm_ffn1", "norm_mix", "norm_ffn2", "norm_final", "conv_b", "b_rg_gate", "b_in_gate",
                   "lru_lambda", "w_rg_gate", "w_in_gate"]
    given = dict(b_ada=(b_ada, m_b_ada, v_b_ada), norm_ffn1=(norm_ffn1, m_norm_ffn1, v_norm_ffn1),
                 norm_mix=(norm_mix, m_norm_mix, v_norm_mix), norm_ffn2=(norm_ffn2, m_norm_ffn2, v_norm_ffn2),
                 norm_final=(norm_final, m_norm_final, v_norm_final), conv_b=(conv_b, m_conv_b, v_conv_b),
                 b_rg_gate=(b_rg_gate, m_b_rg_gate, v_b_rg_gate), b_in_gate=(b_in_gate, m_b_in_gate, v_b_in_gate),
                 lru_lambda=(lru_lambda, m_lru_lambda, v_lru_lambda), w_rg_gate=(w_rg_gate, m_w_rg_gate, v_w_rg_gate),
                 w_in_gate=(w_in_gate, m_w_in_gate, v_w_in_gate))
    packed = [jnp.concatenate([given[n][q].reshape(1, -1) for n in small_names], axis=1).reshape(n_adam // LANE, LANE)
              for q in range(3)]
    small_out = _adamw("adamw_small", g_small, *packed)
    pos = 0
    for n in small_names:
        shape = given[n][0].shape
        size = math.prod(shape)
        results[n] = [o.reshape(1, n_adam)[:, pos:pos + size].reshape(shape) for o in small_out]
        pos += size
    results["conv_w"] = [o.reshape(conv_w.shape) for o in
                         _adamw("adamw_conv_w", d_cw_mine[None], conv_w[0], m_conv_w[0], v_conv_w[0])]

    dmod_all = packs.reshape(N_DEV, n_pack)[:, off:off + 9 * d]
    dmod_mine = lax.dynamic_slice(dmod_all, (0, me * cba), (N_DEV, cba))
    dmod_rows = jnp.pad(dmod_mine, ((0, LANE - N_DEV), (0, 0)))
    c_act_t = jnp.pad(c_act.T, ((0, 0), (0, LANE - N_DEV)))
    tm_d2 = _tile(d, 256)
    d_wada = _matmul("dw_ada", c_act_t[None], dmod_rows[None], "nn", tm_d2, cba, LANE,
                     outs=[((1, d, cba), F32, _bspec((1, d, cba), tm_d2, cba, _ij))])[0]
    results["w_ada"] = [o[None] for o in _adamw("adamw_w_ada", d_wada, w_ada[0], m_w_ada[0], v_w_ada[0])]

    update_group("scattered_ffn1", scatter_ffn1, [("w_ffn1_in", w_ffn1_in, m_w_ffn1_in, v_w_ffn1_in),
                                                  ("w_ffn1_out", w_ffn1_out, m_w_ffn1_out, v_w_ffn1_out)], results["w_ada"][0])

    order = ["w_ada", "b_ada", "norm_ffn1", "w_ffn1_in", "w_ffn1_out", "norm_mix", "w_in", "conv_w", "conv_b", "w_rg_gate",
             "b_rg_gate", "w_in_gate", "b_in_gate", "lru_lambda", "w_branch_attn", "w_branch_lru", "w_out", "norm_ffn2",
             "w_ffn2_in", "w_ffn2_out", "norm_final"]
    return (loss, grad_x[None], *[results[n][0] for n in order], *[results[n][1] for n in order],
            *[results[n][2] for n in order], *[results[n][3] for n in order])
```

```python
import functools
import math

import jax
import jax.numpy as jnp
from jax import lax
from jax.experimental import pallas as pl
from jax.experimental.pallas import tpu as pltpu

F32 = jnp.float32
BF16 = jnp.bfloat16
N_DEV = 8
HEAD_DIM = 128
CONV_WIDTH = 4
CONV_HALO = 8
LRU_C = 8.0
EPS = 1e-6
ADAM_LR, ADAM_B1, ADAM_B2, ADAM_EPS, ADAM_WD, ADAM_STEP = 0.001, 0.9, 0.999, 1e-08, 0.01, 10
LANE = 128
VMEM_LIMIT = 56 * 1024 * 1024
MESH = pl.DeviceIdType.MESH

NT = (((1,), (1,)), ((), ()))
NN = (((1,), (0,)), ((), ()))
TN = (((0,), (0,)), ((), ()))


def _tile(dim, target, align=LANE):
    t = (min(target, dim) // align) * align
    while t >= align:
        if dim % t == 0:
            return t
        t -= align
    return dim


def _params(sem):
    return pltpu.CompilerParams(dimension_semantics=sem, vmem_limit_bytes=VMEM_LIMIT)


def _sigmoid(x):
    return 1.0 / (1.0 + jnp.exp(-x))


def _softplus(x):
    return jnp.maximum(x, 0.0) + jnp.log(1.0 + jnp.exp(-jnp.abs(x)))


def _log1p(z):
    w = 1.0 + z
    return jnp.where(w == 1.0, z, jnp.log(w) * z / jnp.where(w == 1.0, 1.0, w - 1.0))


def _expm1(x):
    poly = x * (1.0 + x * (0.5 + x * (1.0 / 6 + x * (1.0 / 24 + x * (1.0 / 120 + x * (1.0 / 720))))))
    return jnp.where(jnp.abs(x) < 0.25, poly, jnp.exp(x) - 1.0)


_GELU_C = math.sqrt(2.0 / math.pi)


def _gelu_and_grad(x):
    inner = _GELU_C * (x + 0.044715 * x * x * x)
    th = jnp.tanh(inner)
    val = 0.5 * x * (1.0 + th)
    grad = 0.5 * (1.0 + th) + 0.5 * x * (1.0 - th * th) * _GELU_C * (1.0 + 3 * 0.044715 * x * x)
    return val, grad


def _dot_split(x, u):
    hi = x.astype(BF16)
    lo = (x - hi.astype(F32)).astype(BF16)
    return jnp.dot(hi, u, preferred_element_type=F32) + jnp.dot(lo, u, preferred_element_type=F32)


def _mesh_position():
    x, y, c = lax.axis_index("x"), lax.axis_index("y"), lax.axis_index("c")
    return x, y, c, 4 * x + 2 * y + c


def _peers(x, y, c):
    out = []
    for mask in range(1, N_DEV):
        px = 1 - x if mask & 4 else x
        py = 1 - y if mask & 2 else y
        pc = 1 - c if mask & 1 else c
        out.append((mask, (px, py, pc), 4 * px + 2 * py + pc))
    return out


def _exchange(name, arrs, scatter, after=None):
    n = len(arrs)
    behind = [] if after is None else [after]

    def body(*refs):
        ins, outs = refs[:n], refs[n + len(behind):2 * n + len(behind)]
        send_sems, recv_sems, local_sems = refs[2 * n + len(behind):]
        x, y, c, me = _mesh_position()
        peers = _peers(x, y, c)
        waits = []
        for a in range(n):
            mine = ins[a].at[me] if scatter else ins[a]
            local = pltpu.make_async_copy(mine, outs[a].at[me], local_sems.at[a])
            local.start()
            waits.append(local.wait)
            for mask, dev, idx in peers:
                k = a * (N_DEV - 1) + mask - 1
                src = ins[a].at[idx] if scatter else ins[a]
                send = pltpu.make_async_remote_copy(src_ref=src, dst_ref=outs[a].at[me], send_sem=send_sems.at[k],
                                                    recv_sem=recv_sems.at[k], device_id=dev, device_id_type=MESH)
                send.start()
                arrival = pltpu.make_async_remote_copy(src_ref=src, dst_ref=outs[a].at[idx], send_sem=send_sems.at[k],
                                                       recv_sem=recv_sems.at[k], device_id=dev, device_id_type=MESH)
                waits.append(send.wait_send)
                waits.append(arrival.wait_recv)
        for w in waits:
            w()

    any_spec = pl.BlockSpec(memory_space=pl.ANY)
    out_shape = [jax.ShapeDtypeStruct(a.shape if scatter else (N_DEV,) + a.shape, a.dtype) for a in arrs]
    return pl.pallas_call(
        body, name=name, out_shape=out_shape, in_specs=[any_spec] * (n + len(behind)), out_specs=[any_spec] * n,
        scratch_shapes=[pltpu.SemaphoreType.DMA((n * (N_DEV - 1),)), pltpu.SemaphoreType.DMA((n * (N_DEV - 1),)),
                        pltpu.SemaphoreType.DMA((n,))],
    )(*arrs, *behind)


HBM_SPEC = pl.BlockSpec(memory_space=pltpu.HBM)
SEM_SPEC = pl.BlockSpec(memory_space=pltpu.SEMAPHORE)
DATAFLOW = pltpu.SideEffectType.DATAFLOW_SIDE_EFFECTING


def _exchange_begin(name, arrs, scatter, after=None):
    n = len(arrs)
    lands = [lax.empty(a.shape if scatter else (N_DEV,) + a.shape, a.dtype) for a in arrs]
    behind = [] if after is None else [after]

    def body(*refs):
        srcs, zones, outs = refs[:n], refs[n:2 * n], refs[2 * n + len(behind):]
        x, y, c, me = _mesh_position()
        for a in range(n):
            send_sems, recv_sems = outs[4 * a], outs[4 * a + 1]
            for mask, dev, idx in _peers(x, y, c):
                pltpu.make_async_remote_copy(
                    src_ref=srcs[a].at[idx] if scatter else srcs[a], dst_ref=zones[a].at[me], send_sem=send_sems.at[mask - 1],
                    recv_sem=recv_sems.at[mask - 1], device_id=dev, device_id_type=MESH).start()
        outs[-1][...] = jnp.zeros_like(outs[-1])

    out_shape, out_specs, aliases = [], [], {}
    for a in range(n):
        out_shape += [pltpu.SemaphoreType.DMA((N_DEV - 1,)), pltpu.SemaphoreType.DMA((N_DEV - 1,)),
                      pltpu.HBM(arrs[a].shape, arrs[a].dtype), pltpu.HBM(lands[a].shape, lands[a].dtype)]
        out_specs += [SEM_SPEC, SEM_SPEC, HBM_SPEC, HBM_SPEC]
        aliases[a] = 4 * a + 2
        aliases[n + a] = 4 * a + 3
    out_shape.append(jax.ShapeDtypeStruct((8, LANE), F32))
    out_specs.append(pl.BlockSpec(memory_space=pltpu.VMEM))
    res = pl.pallas_call(
        body, name=name, out_shape=out_shape,
        in_specs=[HBM_SPEC] * (2 * n) + [pl.BlockSpec(memory_space=pl.ANY)] * len(behind),
        out_specs=out_specs, input_output_aliases=aliases, compiler_params=pltpu.CompilerParams(has_side_effects=DATAFLOW),
    )(*[pltpu.with_memory_space_constraint(v, pltpu.HBM) for v in list(arrs) + lands], *behind)
    return [tuple(res[4 * a:4 * a + 4]) for a in range(n)], res[-1]


def _exchange_end(name, handles, after, scatter):
    n = len(handles)
    me = 4 * lax.axis_index("x") + 2 * lax.axis_index("y") + lax.axis_index("c")

    def body(*refs):
        x, y, c, me = _mesh_position()
        for a in range(n):
            src, zone, send_sems, recv_sems = refs[4 * a:4 * a + 4]
            for mask, dev, idx in _peers(x, y, c):
                cp = pltpu.make_async_remote_copy(
                    src_ref=src.at[idx] if scatter else src, dst_ref=zone.at[idx], send_sem=send_sems.at[mask - 1],
                    recv_sem=recv_sems.at[mask - 1], device_id=dev, device_id_type=MESH)
                cp.wait_send()
                cp.wait_recv()

    operands, in_specs, out_shape, aliases = [], [], [], {}
    for a, (send_sems, recv_sems, src, zone) in enumerate(handles):
        operands += [src, zone, send_sems, recv_sems]
        in_specs += [HBM_SPEC, HBM_SPEC, SEM_SPEC, SEM_SPEC]
        out_shape += [pltpu.HBM(src.shape, src.dtype), pltpu.HBM(zone.shape, zone.dtype)]
        aliases[4 * a] = 2 * a
        aliases[4 * a + 1] = 2 * a + 1
    res = pl.pallas_call(
        body, name=name, out_shape=out_shape, in_specs=in_specs + [pl.BlockSpec(memory_space=pl.ANY)],
        out_specs=[HBM_SPEC] * (2 * n), input_output_aliases=aliases,
        compiler_params=pltpu.CompilerParams(has_side_effects=DATAFLOW),
    )(*operands, after)
    full = []
    for a in range(n):
        src, zone = res[2 * a], res[2 * a + 1]
        own = lax.dynamic_index_in_dim(src, me, 0, keepdims=False) if scatter else src
        full.append(lax.dynamic_update_index_in_dim(zone, own, me, 0))
    return full


def _bspec(shape, tr, tc, rc, buffers=None):
    per = shape[-1] // tc
    mode = {} if buffers is None else dict(pipeline_mode=pl.Buffered(buffers))
    if len(shape) == 3:
        return pl.BlockSpec((None, tr, tc), lambda j, i, k: (rc(i, j, k)[1] // per, rc(i, j, k)[0], rc(i, j, k)[1] % per), **mode)
    return pl.BlockSpec((shape[0], None, tr, tc),
                        lambda j, i, k: (0, rc(i, j, k)[1] // per, rc(i, j, k)[0], rc(i, j, k)[1] % per), **mode)


def _ij(i, j, k):
    return i, j


def _row_spec(tn, col_tile_offset=0):
    return pl.BlockSpec((1, tn), lambda j, i, k: (0, j + col_tile_offset))


def _matmul(name, a, b, mode, tm, tn, tk, outs, epilogue=None, extras=(), b_buffers=None):
    groups = b.shape[0] if b.ndim == 4 else 1
    if mode == "nn":
        m, k_dim, n = a.shape[1], a.shape[0] * a.shape[2], b.shape[-3] * b.shape[-1]
        a_spec = _bspec(a.shape, tm, tk, lambda i, j, k: (i, k))
        b_spec = _bspec(b.shape, tk, tn, lambda i, j, k: (k, j), b_buffers)
        dims = NN
    elif mode == "nt":
        m, k_dim, n = a.shape[1], a.shape[0] * a.shape[2], b.shape[-2]
        a_spec = _bspec(a.shape, tm, tk, lambda i, j, k: (i, k))
        b_spec = _bspec(b.shape, tn, tk, lambda i, j, k: (j, k), b_buffers)
        dims = NT
    else:
        m, k_dim, n = a.shape[0] * a.shape[2], a.shape[1], b.shape[-3] * b.shape[-1]
        a_spec = _bspec(a.shape, tk, tm, lambda i, j, k: (k, i))
        b_spec = _bspec(b.shape, tk, tn, lambda i, j, k: (k, j), b_buffers)
        dims = TN
    assert m % tm == 0 and n % tn == 0 and k_dim % tk == 0, (name, m, n, k_dim, tm, tn, tk)
    nk = k_dim // tk
    n_extra, n_out = len(extras), len(outs)

    def finish(acc, extra_refs, out_refs):
        if epilogue is None:
            out_refs[0][...] = acc[0].astype(out_refs[0].dtype)
        else:
            epilogue(acc, extra_refs, out_refs)

    def products(a_ref, b_ref):
        a_tile = a_ref[...].astype(BF16)
        return [lax.dot_general(a_tile, (b_ref[g] if b.ndim == 4 else b_ref[...]).astype(BF16), dims,
                                preferred_element_type=F32) for g in range(groups)]

    def body_whole_k(*refs):
        finish(products(refs[0], refs[1]), refs[2:2 + n_extra], refs[2 + n_extra:])

    def body_k_steps(*refs):
        acc_ref = refs[-1]
        k = pl.program_id(2)

        @pl.when(k == 0)
        def _():
            acc_ref[...] = jnp.zeros_like(acc_ref)

        for g, p in enumerate(products(refs[0], refs[1])):
            acc_ref[g] += p

        @pl.when(k == nk - 1)
        def _():
            finish([acc_ref[g] for g in range(groups)], refs[2:2 + n_extra], refs[2 + n_extra:2 + n_extra + n_out])

    return pl.pallas_call(
        body_whole_k if nk == 1 else body_k_steps, name=name, grid=(n // tn, m // tm, nk),
        in_specs=[a_spec, b_spec] + [s for _, s in extras],
        out_specs=[s for _, _, s in outs],
        out_shape=[jax.ShapeDtypeStruct(shape, dtype) for shape, dtype, _ in outs],
        scratch_shapes=[] if nk == 1 else [pltpu.VMEM((groups, tm, tn), F32)],
        compiler_params=_params(("parallel", "parallel", "arbitrary")),
    )(a, b, *[arr for arr, _ in extras])


def _rowwise(name, fn, rows, vecs, outs, accs, tm):
    s = rows[0][0].shape[0]
    n_in, n_out = len(rows) + len(vecs), len(outs)

    def body(*refs):
        i = pl.program_id(0)
        res = fn(*[r[...] for r in refs[:n_in]])
        res = res if isinstance(res, tuple) else (res,)
        out_refs, acc_refs = refs[n_in:n_in + n_out], refs[n_in + n_out:]
        for ref, val in zip(out_refs, res[:n_out]):
            ref[...] = val.astype(ref.dtype)

        @pl.when(i == 0)
        def _():
            for ref in acc_refs:
                ref[...] = jnp.zeros_like(ref)

        for ref, val in zip(acc_refs, res[n_out:]):
            ref[...] += val

    in_specs = [pl.BlockSpec((tm, w), functools.partial(lambda i, cb: (i, cb), cb=cb)) for _, w, cb in rows]
    in_specs += [pl.BlockSpec(v.shape, lambda i: (0,) * v.ndim) for v in vecs]
    out_specs = [pl.BlockSpec((tm, w), lambda i: (i, 0)) for w, _ in outs] + [pl.BlockSpec((1, w), lambda i: (0, 0)) for w in accs]
    out_shape = [jax.ShapeDtypeStruct((s, w), dt) for w, dt in outs] + [jax.ShapeDtypeStruct((1, w), F32) for w in accs]
    return pl.pallas_call(
        body, name=name, grid=(s // tm,), in_specs=in_specs, out_specs=out_specs, out_shape=out_shape,
        compiler_params=_params(("arbitrary",)),
    )(*[r for r, _, _ in rows], *vecs)


def _colsum(v):
    return jnp.sum(v, axis=0, keepdims=True)


def _norm_mod(name, h, nw, sc, sh, tm):
    d = h.shape[1]

    def fn(hb, nwb, scb, shb):
        r = lax.rsqrt(jnp.mean(hb * hb, axis=-1, keepdims=True) + EPS)
        return (hb * r) * nwb * (1.0 + scb) + shb

    return _rowwise(name, fn, [(h, d, 0)], [nw, sc, sh], [(d, BF16)], [], tm)[0]


def _norm_mod_bwd(name, dy, h, dh_next, nw, sc, tm):
    d = h.shape[1]

    def fn(dyb, hb, dhb, nwb, scb):
        r = lax.rsqrt(jnp.mean(hb * hb, axis=-1, keepdims=True) + EPS)
        xh = hb * r
        dxh = dyb * (nwb * (1.0 + scb))
        dx = r * (dxh - xh * jnp.mean(dxh * xh, axis=-1, keepdims=True))
        return dhb + dx, _colsum(dyb), _colsum(dyb * xh * nwb), _colsum(dyb * xh * (1.0 + scb))

    return _rowwise(name, fn, [(dy, d, 0), (h, d, 0), (dh_next, d, 0)], [nw, sc], [(d, F32)], [d, d, d], tm)


def _gate_bwd(name, dh, o, g, coef, tm):
    d = dh.shape[1]

    def fn(dhb, ob, gb):
        return dhb * (coef * gb), _colsum(dhb * ob * coef)

    return _rowwise(name, fn, [(dh, d, 0), (o, d, 0)], [g], [(d, BF16)], [d], tm)


def _loss_bwd(name, h, target, nw, tm):
    d = h.shape[1]

    def fn(hb, tb, nwb):
        r = lax.rsqrt(jnp.mean(hb * hb, axis=-1, keepdims=True) + EPS)
        xh = hb * r
        err = xh * nwb - tb
        dy = err * (1.0 / d)
        dxh = dy * nwb
        dx = r * (dxh - xh * jnp.mean(dxh * xh, axis=-1, keepdims=True))
        loss = 0.5 * jnp.sum(jnp.mean(err * err, axis=-1, keepdims=True), axis=0, keepdims=True)
        return dx, jnp.broadcast_to(loss, (1, LANE)), _colsum(dy * xh)

    return _rowwise(name, fn, [(h, d, 0), (target, d, 0)], [nw], [(d, F32)], [LANE, d], tm)


def _head_group(nh, most):
    return max(g for g in (1, 2, 4) if g <= most and nh % g == 0)


def _attn_fwd(qkv, nh, t):
    s = qkv.shape[0]
    scale = HEAD_DIM ** -0.5
    hp = _head_group(nh, 4)
    wide = hp * HEAD_DIM
    lanes = [slice(u * HEAD_DIM, (u + 1) * HEAD_DIM) for u in range(hp)]

    def body(q_ref, k_ref, v_ref, y_ref, tot_ref):
        i = pl.program_id(1)
        row = lax.broadcasted_iota(jnp.int32, (t, t), 0)
        col = lax.broadcasted_iota(jnp.int32, (t, t), 1)
        later = (row > col).astype(BF16)
        causal = col < row
        qs = [q_ref[:, ln] for ln in lanes]

        def block(j, carry, diagonal):
            ks = pl.ds(pl.multiple_of(j * t, t), t)
            heads = range(hp)
            z = [lax.dot_general(qs[u], k_ref[ks, lanes[u]], NT, preferred_element_type=F32) * scale for u in heads]
            sp = [_softplus(z[u]) for u in heads]
            log_keep = [jnp.where(causal, -sp[u], 0.0) if diagonal else -sp[u] for u in heads]
            between = [_dot_split(log_keep[u], later) for u in heads]
            w = [jnp.exp(z[u] - sp[u] + between[u] + carry[u][1]) for u in heads]
            if diagonal:
                w = [jnp.where(causal, w[u], 0.0) for u in heads]
            o = [carry[u][0] + jnp.dot(w[u].astype(BF16), v_ref[ks, lanes[u]], preferred_element_type=F32) for u in heads]
            return tuple((o[u], carry[u][1] + jnp.sum(log_keep[u], axis=1, keepdims=True)) for u in heads)

        carry = tuple((jnp.zeros((t, HEAD_DIM), F32), jnp.zeros((t, 1), F32)) for _ in lanes)
        carry = block(i, carry, True)
        carry = lax.fori_loop(0, i, lambda jj, cr: block(i - 1 - jj, cr, False), carry)
        for u, ln in enumerate(lanes):
            y_ref[:, ln] = carry[u][0].astype(y_ref.dtype)
            tot_ref[:, ln] = jnp.broadcast_to(carry[u][1], (t, HEAD_DIM))

    g = nh // hp
    return pl.pallas_call(
        body, name="attn_fwd", grid=(g, s // t),
        in_specs=[pl.BlockSpec((t, wide), lambda h, i: (i, h)),
                  pl.BlockSpec((s, wide), lambda h, i: (0, g + h)),
                  pl.BlockSpec((s, wide), lambda h, i: (0, 2 * g + h))],
        out_specs=[pl.BlockSpec((t, wide), lambda h, i: (i, h)), pl.BlockSpec((t, wide), lambda h, i: (i, h))],
        out_shape=[jax.ShapeDtypeStruct((s, nh * HEAD_DIM), BF16), jax.ShapeDtypeStruct((s, nh * HEAD_DIM), F32)],
        compiler_params=_params(("parallel", "arbitrary")),
    )(qkv, qkv, qkv)


def _attn_bwd(qkv, dy, tot, nh, t):
    s = qkv.shape[0]
    scale = HEAD_DIM ** -0.5
    hp = _head_group(nh, 4)
    wide = hp * HEAD_DIM
    lanes = [slice(u * HEAD_DIM, (u + 1) * HEAD_DIM) for u in range(hp)]

    def body(q_ref, k_ref, v_ref, dy_ref, tot_ref, dq_ref, dk_out, dv_out, dk_ref, dv_ref):
        i = pl.program_id(1)

        @pl.when(i == 0)
        def _():
            dk_ref[...] = jnp.zeros_like(dk_ref)
            dv_ref[...] = jnp.zeros_like(dv_ref)

        row = lax.broadcasted_iota(jnp.int32, (t, t), 0)
        col = lax.broadcasted_iota(jnp.int32, (t, t), 1)
        upto = (row <= col).astype(BF16)
        before = (row < col).astype(BF16)
        causal = col < row
        qs = [q_ref[:, ln] for ln in lanes]
        dys = [dy_ref[:, ln] for ln in lanes]
        totals = [tot_ref[:, u * HEAD_DIM:u * HEAD_DIM + 1] for u in range(hp)]

        def block(j, carry, diagonal):
            ks = pl.ds(pl.multiple_of(j * t, t), t)
            heads = range(hp)
            kb = [k_ref[ks, ln] for ln in lanes]
            vb = [v_ref[ks, ln] for ln in lanes]
            z = [lax.dot_general(qs[u], kb[u], NT, preferred_element_type=F32) * scale for u in heads]
            dw = [lax.dot_general(dys[u], vb[u], NT, preferred_element_type=F32) for u in heads]
            sp = [_softplus(z[u]) for u in heads]
            log_keep = [jnp.where(causal, -sp[u], 0.0) if diagonal else -sp[u] for u in heads]
            upto_sum = [_dot_split(log_keep[u], upto) for u in heads]
            w = [jnp.exp(z[u] - sp[u] + (totals[u] - carry[u][1] - upto_sum[u])) for u in heads]
            if diagonal:
                w = [jnp.where(causal, w[u], 0.0) for u in heads]
            g = [dw[u] * w[u] for u in heads]
            g_before = [_dot_split(g[u], before) for u in heads]
            dz = [(g[u] * jnp.exp(-sp[u]) - jnp.exp(z[u] - sp[u]) * (carry[u][2] + g_before[u])) * scale for u in heads]
            if diagonal:
                dz = [jnp.where(causal, dz[u], 0.0) for u in heads]
            dzb = [dz[u].astype(BF16) for u in heads]
            dq = [carry[u][0] + jnp.dot(dzb[u], kb[u], preferred_element_type=F32) for u in heads]
            for u in heads:
                dk_ref[ks, lanes[u]] += lax.dot_general(dzb[u], qs[u], TN, preferred_element_type=F32)
            for u in heads:
                dv_ref[ks, lanes[u]] += lax.dot_general(w[u].astype(BF16), dys[u], TN, preferred_element_type=F32)
            return tuple((dq[u], carry[u][1] + jnp.sum(log_keep[u], axis=1, keepdims=True),
                          carry[u][2] + jnp.sum(g[u], axis=1, keepdims=True)) for u in heads)

        zero = jnp.zeros((t, 1), F32)
        carry = tuple((jnp.zeros((t, HEAD_DIM), F32), zero, zero) for _ in lanes)
        carry = lax.fori_loop(0, i, lambda j, cr: block(j, cr, False), carry)
        carry = block(i, carry, True)
        for u, ln in enumerate(lanes):
            dq_ref[:, ln] = carry[u][0].astype(dq_ref.dtype)

        @pl.when(i == pl.num_programs(1) - 1)
        def _():
            dk_out[...] = dk_ref[...].astype(dk_out.dtype)
            dv_out[...] = dv_ref[...].astype(dv_out.dtype)

    g = nh // hp
    tile = lambda off: pl.BlockSpec((t, wide), lambda h, i: (i, off + h))
    head = lambda off, **mode: pl.BlockSpec((s, wide), lambda h, i: (0, off + h), **mode)
    once = dict(pipeline_mode=pl.Buffered(1))
    return pl.pallas_call(
        body, name="attn_bwd", grid=(g, s // t),
        in_specs=[tile(0), head(g, **once), head(2 * g, **once), tile(0), tile(0)],
        out_specs=[tile(0), head(0), head(0)],
        out_shape=[jax.ShapeDtypeStruct((s, nh * HEAD_DIM), BF16)] * 3,
        scratch_shapes=[pltpu.VMEM((s, wide), F32), pltpu.VMEM((s, wide), F32)],
        compiler_params=_params(("parallel", "arbitrary")),
    )(qkv, qkv, qkv, dy, tot)


def _lru_gates(xc, w_r, b_r, w_i, b_i, lam):
    xb = xc.astype(BF16)
    r = _sigmoid(jnp.dot(xb, w_r.astype(BF16), preferred_element_type=F32) + b_r)
    i = _sigmoid(jnp.dot(xb, w_i.astype(BF16), preferred_element_type=F32) + b_i)
    neg_lam = -lam
    sp_lam = jnp.maximum(neg_lam, 0.0) + _log1p(jnp.exp(-jnp.abs(neg_lam)))
    log_a = -LRU_C * r * sp_lam
    a = jnp.exp(log_a)
    mult = jnp.sqrt(-_expm1(2.0 * log_a))
    return r, i, sp_lam, a, mult


def _conv_taps(xpad_chunk, conv_w, t):
    shifted = [xpad_chunk[CONV_HALO:, :]]
    for d in range(1, CONV_WIDTH):
        shifted.append(pltpu.roll(xpad_chunk, d, 0)[CONV_HALO:, :])
    weights = [conv_w[CONV_WIDTH - 1 - d:CONV_WIDTH - d, :] for d in range(CONV_WIDTH)]
    return shifted, weights


def _lru_fwd(xr_pad, proj, gr_block0, conv_w, conv_b, w_r, b_r, w_i, b_i, lam, t):
    s, w = xr_pad.shape[0] - CONV_HALO, xr_pad.shape[1]
    nblk = w // LANE
    nchunk = s // t
    steps = [1 << p for p in range(t.bit_length() - 1)]
    assert (1 << (t.bit_length() - 1)) == t and w_r.shape[1:] == (LANE, LANE)

    def body(x_ref, gr_ref, cw_ref, cb_ref, wr_ref, br_ref, wi_ref, bi_ref, lam_ref, h_ref, hp_ref, xc_ref, y_ref):
        row = lax.broadcasted_iota(jnp.int32, (t, LANE), 0)

        def chunk(ci, h_in):
            t0 = pl.multiple_of(ci * t, t)
            shifted, weights = _conv_taps(x_ref[pl.ds(t0, t + CONV_HALO), :], cw_ref[...], t)
            xc = cb_ref[...] + sum(wd * xs for wd, xs in zip(weights, shifted))
            r, i, _, a, mult = _lru_gates(xc, wr_ref[...], br_ref[...], wi_ref[...], bi_ref[...], lam_ref[...])
            coef, val = a, mult * (i * xc)
            for d in steps:
                ok = row >= d
                val = jnp.where(ok, coef * pltpu.roll(val, d, 0) + val, val)
                coef = jnp.where(ok, coef * pltpu.roll(coef, d, 0), coef)
            h = val + coef * h_in
            rows = pl.ds(t0, t)
            h_ref[rows, :] = h
            hp_ref[rows, :] = jnp.where(row == 0, h_in, pltpu.roll(h, 1, 0))
            xc_ref[rows, :] = xc
            y_ref[rows, :] = (h * _gelu_and_grad(gr_ref[rows, :])[0]).astype(y_ref.dtype)
            return h[t - 1:t, :]

        lax.fori_loop(0, nchunk, chunk, jnp.zeros((1, LANE), F32))

    col = lambda rows: pl.BlockSpec((rows, LANE), lambda n: (0, n))
    return pl.pallas_call(
        body, name="lru_fwd", grid=(nblk,),
        in_specs=[col(s + CONV_HALO), pl.BlockSpec((s, LANE), lambda n: (0, gr_block0 + n)), col(CONV_WIDTH), col(1),
                  pl.BlockSpec((None, LANE, LANE), lambda n: (n, 0, 0)), col(1),
                  pl.BlockSpec((None, LANE, LANE), lambda n: (n, 0, 0)), col(1), col(1)],
        out_specs=[col(s)] * 4,
        out_shape=[jax.ShapeDtypeStruct((s, w), F32)] * 3 + [jax.ShapeDtypeStruct((s, w), BF16)],
        compiler_params=_params(("parallel",)),
    )(xr_pad, proj, conv_w, conv_b, w_r, b_r, w_i, b_i, lam)


def _lru_bwd(dy, proj, gr_block0, h, h_prev, xc, w_r, b_r, w_i, b_i, lam, t):
    s, w = dy.shape
    nblk = w // LANE
    nchunk = s // t
    steps = [1 << p for p in range(t.bit_length() - 1)]

    def body(dy_ref, gr_ref, h_ref, hp_ref, xc_ref, wr_ref, br_ref, wi_ref, bi_ref, lam_ref,
             dgr_ref, dxc_ref, dwr_ref, dwi_ref, dbr_ref, dbi_ref, dlam_ref):
        row = lax.broadcasted_iota(jnp.int32, (t, LANE), 0)
        for ref in (dwr_ref, dwi_ref, dbr_ref, dbi_ref, dlam_ref):
            ref[...] = jnp.zeros_like(ref)

        def chunk(cc, carry):
            lam_next, a_next = carry
            rows = pl.ds(pl.multiple_of((nchunk - 1 - cc) * t, t), t)
            dyb, hb, xcb = dy_ref[rows, :], h_ref[rows, :], xc_ref[rows, :]
            gel, dgel = _gelu_and_grad(gr_ref[rows, :])
            dgr_ref[rows, :] = dyb * hb * dgel
            w_r, w_i = wr_ref[...], wi_ref[...]
            r, i, sp_lam, a, mult = _lru_gates(xcb, w_r, br_ref[...], w_i, bi_ref[...], lam_ref[...])
            coef = jnp.where(row == t - 1, a_next, pltpu.roll(a, t - 1, 0))
            val = dyb * gel
            for d in steps:
                ok = row < t - d
                val = jnp.where(ok, coef * pltpu.roll(val, t - d, 0) + val, val)
                coef = jnp.where(ok, coef * pltpu.roll(coef, t - d, 0), coef)
            adj = val + coef * lam_next
            da = adj * hp_ref[rows, :]
            v = i * xcb
            dmult, dv = adj * v, adj * mult
            dlog_a = da * a - (a * a) * dmult / mult
            dr_pre = (-LRU_C * sp_lam) * dlog_a * r * (1.0 - r)
            di_pre = dv * xcb * i * (1.0 - i)
            dlam_ref[...] += _colsum(-LRU_C * r * dlog_a)
            dbr_ref[...] += _colsum(dr_pre)
            dbi_ref[...] += _colsum(di_pre)
            xb, drb, dib = xcb.astype(BF16), dr_pre.astype(BF16), di_pre.astype(BF16)
            dwr_ref[...] += lax.dot_general(xb, drb, TN, preferred_element_type=F32)
            dwi_ref[...] += lax.dot_general(xb, dib, TN, preferred_element_type=F32)
            dxc_ref[rows, :] = (dv * i + lax.dot_general(drb, w_r.astype(BF16), NT, preferred_element_type=F32)
                                + lax.dot_general(dib, w_i.astype(BF16), NT, preferred_element_type=F32))
            return adj[0:1, :], a[0:1, :]

        lax.fori_loop(0, nchunk, chunk, (jnp.zeros((1, LANE), F32), jnp.zeros((1, LANE), F32)))
        dlam_ref[...] = dlam_ref[...] * (-_sigmoid(-lam_ref[...]))

    col = lambda rows: pl.BlockSpec((rows, LANE), lambda n: (0, n))
    mat = pl.BlockSpec((None, LANE, LANE), lambda n: (n, 0, 0))
    return pl.pallas_call(
        body, name="lru_bwd", grid=(nblk,),
        in_specs=[col(s), pl.BlockSpec((s, LANE), lambda n: (0, gr_block0 + n)), col(s), col(s), col(s),
                  mat, col(1), mat, col(1), col(1)],
        out_specs=[col(s), col(s), mat, mat, col(1), col(1), col(1)],
        out_shape=[jax.ShapeDtypeStruct((s, w), F32)] * 2 + [jax.ShapeDtypeStruct((nblk, LANE, LANE), F32)] * 2
        + [jax.ShapeDtypeStruct((1, w), F32)] * 3,
        compiler_params=_params(("parallel",)),
    )(dy, proj, h, h_prev, xc, w_r, b_r, w_i, b_i, lam)


def _conv_bwd(xr_pad, dxc_pad, conv_w, t):
    s, w = xr_pad.shape[0] - CONV_HALO, xr_pad.shape[1]
    nchunk = s // t

    def body(x_ref, g_ref, cw_ref, dx_ref, dcw_ref, dcb_ref):
        dcw_ref[...] = jnp.zeros_like(dcw_ref)
        dcb_ref[...] = jnp.zeros_like(dcb_ref)

        def chunk(ci, _):
            t0 = pl.multiple_of(ci * t, t)
            shifted, weights = _conv_taps(x_ref[pl.ds(t0, t + CONV_HALO), :], cw_ref[...], t)
            gpad = g_ref[pl.ds(t0, t + CONV_HALO), :]
            g = gpad[:t, :]
            dx = weights[0] * g
            for d in range(1, CONV_WIDTH):
                dx = dx + weights[d] * pltpu.roll(gpad, t + CONV_HALO - d, 0)[:t, :]
            dx_ref[pl.ds(t0, t), :] = dx
            for d in range(CONV_WIDTH):
                dcw_ref[CONV_WIDTH - 1 - d:CONV_WIDTH - d, :] += _colsum(g * shifted[d])
            dcb_ref[...] += _colsum(g)
            return 0

        lax.fori_loop(0, nchunk, chunk, 0)

    col = lambda rows: pl.BlockSpec((rows, LANE), lambda n: (0, n))
    return pl.pallas_call(
        body, name="conv_bwd", grid=(w // LANE,),
        in_specs=[col(s + CONV_HALO), col(s + CONV_HALO), col(CONV_WIDTH)],
        out_specs=[col(s), col(CONV_WIDTH), col(1)],
        out_shape=[jax.ShapeDtypeStruct((s, w), F32), jax.ShapeDtypeStruct((CONV_WIDTH, w), F32),
                   jax.ShapeDtypeStruct((1, w), F32)],
        compiler_params=_params(("parallel",)),
    )(xr_pad, dxc_pad, conv_w)


def _sum_parts(parts_ref):
    g = parts_ref[0].astype(F32)
    for p in range(1, parts_ref.shape[0]):
        g = g + parts_ref[p].astype(F32)
    return g


def _reduce_parts(name, parts):
    p, r, c = parts.shape
    tr = _tile(r, max(8, (1 << 19) // c), 8)

    def body(parts_ref, g_ref):
        g_ref[...] = _sum_parts(parts_ref)

    return pl.pallas_call(
        body, name=name, grid=(r // tr,), in_specs=[pl.BlockSpec((p, tr, c), lambda i: (0, i, 0))],
        out_specs=pl.BlockSpec((tr, c), lambda i: (i, 0)), out_shape=jax.ShapeDtypeStruct((r, c), F32),
        compiler_params=_params(("parallel",)),
    )(parts)


def _adamw(name, parts, w, m, v):
    p, r, c = parts.shape
    tr = _tile(r, max(8, (1 << 18) // c), 8)

    def body(parts_ref, w_ref, m_ref, v_ref, g_ref, d_ref, nm_ref, nv_ref):
        g = _sum_parts(parts_ref)
        nm = ADAM_B1 * m_ref[...] + (1.0 - ADAM_B1) * g
        nv = ADAM_B2 * v_ref[...] + (1.0 - ADAM_B2) * (g * g)
        m_hat = nm / (1.0 - ADAM_B1 ** ADAM_STEP)
        v_hat = nv / (1.0 - ADAM_B2 ** ADAM_STEP)
        g_ref[...] = g
        d_ref[...] = -ADAM_LR * (m_hat / (jnp.sqrt(v_hat) + ADAM_EPS) + ADAM_WD * w_ref[...])
        nm_ref[...] = nm
        nv_ref[...] = nv

    blk = pl.BlockSpec((tr, c), lambda i: (i, 0))
    return pl.pallas_call(
        body, name=name, grid=(r // tr,), in_specs=[pl.BlockSpec((p, tr, c), lambda i: (0, i, 0)), blk, blk, blk],
        out_specs=[blk] * 4, out_shape=[jax.ShapeDtypeStruct((r, c), F32)] * 4,
        compiler_params=_params(("parallel",)),
    )(parts, w, m, v)


def _ffn_in(tag, y, w_in_g, tm):
    s, d = y.shape
    half = N_DEV // 2
    cb = w_in_g.shape[2]
    ff = half * cb

    def swiglu(acc, extra_refs, out_refs):
        g, u = acc
        out_refs[0][0] = g
        out_refs[0][1] = u
        out_refs[1][...] = (g * _sigmoid(g) * u).astype(BF16)

    gu_shape = (2, 1, s, ff)
    gu, act = _matmul(
        tag + "_in", y[None], w_in_g.reshape(2, half, d, cb), "nn", tm, cb, d,
        outs=[(gu_shape, F32, _bspec(gu_shape, tm, cb, _ij)), ((1, s, ff), BF16, _bspec((1, s, ff), tm, cb, _ij))],
        epilogue=swiglu, b_buffers=1)
    return gu, act


def _ffn_out(tag, act, w_out_g, res, gate, tm):
    _, s, ff = act.shape
    d = res.shape[1]
    tn = _tile(d, 1024)

    def residual(acc, extra_refs, out_refs):
        out_refs[0][...] = acc[0]
        out_refs[1][...] = extra_refs[0][...] + 0.5 * extra_refs[1][...] * acc[0]

    plain = _bspec((1, s, d), tm, tn, _ij)
    o, h_new = _matmul(
        tag + "_out", act, w_out_g.reshape(1, ff, d), "nn", tm, tn, ff,
        outs=[((1, s, d), F32, plain), ((1, s, d), F32, plain)], epilogue=residual,
        extras=[(res[None], plain), (gate, _row_spec(tn))], b_buffers=1)
    return o[0], h_new[0]


def _after_token(token):
    return token, pl.BlockSpec(token.shape, lambda j, i, k: (0, 0))


def _ffn_bwd_weights(tag, dh, y, gu, act, o, gate, w_in_g, w_out_g, tm):
    s, d = dh.shape
    half = N_DEV // 2
    cb = w_in_g.shape[2]
    ff = half * cb
    do, dgate = _gate_bwd(tag + "_gate_bwd", dh, o, gate, 0.5, _tile(s, 256, 8))

    tn_d, tm_f = _tile(d, 1024), _tile(ff, 512)
    dw_out = _matmul(tag + "_dw_out", act, do[None], "tn", tm_f, tn_d, s,
                     outs=[((1, ff, d), BF16, _bspec((1, ff, d), tm_f, tn_d, _ij))], b_buffers=1)[0]
    dw_out = dw_out.reshape(N_DEV, ff // N_DEV, d)
    out_handles, token = _exchange_begin(tag + "_scatter_out", [dw_out], True)

    def dswiglu(acc, extra_refs, out_refs):
        dact = acc[0]
        g, u = extra_refs[0][0], extra_refs[0][1]
        sg = _sigmoid(g)
        out_refs[0][0] = (dact * u * sg * (1.0 + g * (1.0 - sg))).astype(BF16)
        out_refs[0][1] = (dact * g * sg).astype(BF16)

    gu_shape = (2, 1, s, ff)
    gu_spec = _bspec(gu_shape, tm, cb, _ij)
    dgu = _matmul(tag + "_dact", do[None], w_out_g.reshape(1, ff, d), "nt", tm, cb, d,
                  outs=[(gu_shape, BF16, gu_spec)], epilogue=dswiglu, extras=[(gu, gu_spec), _after_token(token)],
                  b_buffers=1)[0]
    dgu = dgu.reshape(2, s, ff)

    tm_d = _tile(d, 512)
    dw_in = _matmul(tag + "_dw_in", y[None], dgu, "tn", tm_d, cb, s,
                    outs=[((N_DEV, d, cb), BF16, _bspec((N_DEV, d, cb), tm_d, cb, _ij))], b_buffers=1)[0]
    in_handles, token = _exchange_begin(tag + "_scatter_in", [dw_in], True)
    return dgu, dgate, in_handles + out_handles, token


def _ffn_bwd_input(tag, dgu, w_in_g, token):
    s, d = dgu.shape[1], w_in_g.shape[1]
    tm_big = _tile(s, 1024, 8)
    return _matmul(tag + "_dy", dgu, w_in_g, "nt", tm_big, d, w_in_g.shape[2],
                   outs=[((1, s, d), F32, _bspec((1, s, d), tm_big, d, _ij))], extras=[_after_token(token)])[0][0]


def kernel(x, c, w_ada, b_ada, norm_ffn1, w_ffn1_in, w_ffn1_out, norm_mix, w_in, conv_w, conv_b, w_rg_gate, b_rg_gate, w_in_gate, b_in_gate, lru_lambda, w_branch_attn, w_branch_lru, w_out, norm_ffn2, w_ffn2_in, w_ffn2_out, norm_final, loss_target, m_w_ada, m_b_ada, m_norm_ffn1, m_w_ffn1_in, m_w_ffn1_out, m_norm_mix, m_w_in, m_conv_w, m_conv_b, m_w_rg_gate, m_b_rg_gate, m_w_in_gate, m_b_in_gate, m_lru_lambda, m_w_branch_attn, m_w_branch_lru, m_w_out, m_norm_ffn2, m_w_ffn2_in, m_w_ffn2_out, m_norm_final, v_w_ada, v_b_ada, v_norm_ffn1, v_w_ffn1_in, v_w_ffn1_out, v_norm_mix, v_w_in, v_conv_w, v_conv_b, v_w_rg_gate, v_b_rg_gate, v_w_in_gate, v_b_in_gate, v_lru_lambda, v_w_branch_attn, v_w_branch_lru, v_w_out, v_norm_ffn2, v_w_ffn2_in, v_w_ffn2_out, v_norm_final):
    xs, target = x[0], loss_target[0]
    s, d = xs.shape
    aw, lw = w_branch_attn.shape[1], w_branch_lru.shape[1]
    nh, nlb = aw // HEAD_DIM, w_rg_gate.shape[1]
    cba, cbi, cbb, cwb = w_ada.shape[2], w_in.shape[2], w_branch_attn.shape[2], conv_w.shape[2]
    assert lw == nlb * LANE and cwb * N_DEV == lw and 3 * aw + 2 * lw + 2 * d == cbi * N_DEV
    me = 4 * lax.axis_index("x") + 2 * lax.axis_index("y") + lax.axis_index("c")
    tm = _tile(s, 512, 8)
    tr = _tile(s, 256, 8)
    t_attn = _tile(s, 256, 8)
    t_lru = _tile(s, 256, 8)

    small = _exchange("gather_c", [jnp.concatenate([c, conv_w.reshape(1, CONV_WIDTH * cwb)], axis=1)], False)[0][:, 0, :]
    c_all = small[:, :d]
    conv_w_full = small[:, d:].reshape(N_DEV, CONV_WIDTH, cwb).transpose(1, 0, 2).reshape(CONV_WIDTH, lw)
    c_act = _rowwise("silu_c", lambda v: v * _sigmoid(v), [(c_all, d, 0)], [], [(d, F32)], [], N_DEV)[0]

    def add_bias(acc_ref, extra_refs, out_refs):
        out_refs[0][...] = acc_ref[0] + extra_refs[0][...]

    b_ada_mine = lax.dynamic_slice(b_ada, (0, me * cba), (1, cba))
    mod_part = _matmul("mod", c_act[None], w_ada, "nn", N_DEV, cba, _tile(d, 512),
                       outs=[((1, N_DEV, cba), F32, _bspec((1, N_DEV, cba), N_DEV, cba, _ij))], epilogue=add_bias,
                       extras=[(b_ada_mine, _row_spec(cba))])[0][0]
    mod_all = _exchange("gather_mod", [mod_part], False)[0]
    mod = lax.dynamic_index_in_dim(mod_all, me, axis=1, keepdims=False).reshape(1, 9 * d)
    sh1, sc1, g1, sh2, sc2, g2, sh3, sc3, g3 = [mod[:, n * d:(n + 1) * d] for n in range(9)]

    shards = [w_ffn1_in[0], w_ffn1_out[0], w_in[0], w_branch_attn[0], w_branch_lru[0], w_out[0], w_ffn2_in[0], w_ffn2_out[0]]
    gathers, token = _exchange_begin("gather_w", [w.astype(BF16) for w in shards], False, mod)

    def gathered(n, after):
        return _exchange_end("gathered_w%d" % n, [gathers[n]], after, False)[0]

    y1 = _norm_mod("norm1", xs, norm_ffn1 + token[:1, :1], sc1, sh1, tr)
    wf1i = gathered(0, y1)
    gu1, act1 = _ffn_in("ffn1", y1, wf1i, tm)
    wf1o = gathered(1, act1)
    o1, h1 = _ffn_out("ffn1", act1, wf1o, xs, g1, tm)

    y2 = _norm_mod("norm2", h1, norm_mix, sc2, sh2, tr)
    wi_g = gathered(2, y2)
    tn_i = _tile(cbi, 1152)
    proj = _matmul("mix_in", y2[None], wi_g, "nn", tm, tn_i, d,
                   outs=[((1, s, N_DEV * cbi), F32, _bspec((1, s, N_DEV * cbi), tm, tn_i, _ij))], b_buffers=1)[0][0]
    off_xr, off_gr, off_ga, off_gl = 3 * aw, 3 * aw + lw, 3 * aw + 2 * lw, 3 * aw + 2 * lw + d
    qkv = proj[:, :3 * aw].astype(BF16)
    y_attn, attn_tot = _attn_fwd(qkv, nh, t_attn)
    xr_pad = jnp.pad(proj[:, off_xr:off_xr + lw], ((CONV_HALO, 0), (0, 0)))
    w_r, w_i = w_rg_gate[0], w_in_gate[0]
    h_lru, h_prev, xc, y_lru = _lru_fwd(xr_pad, proj, off_gr // LANE, conv_w_full, conv_b, w_r, b_rg_gate, w_i,
                                        b_in_gate, lru_lambda, t_lru)
    plain_b = _bspec((1, s, d), tm, cbb, _ij)
    wba_g = gathered(3, y_attn)
    wbl_g = gathered(4, y_lru)
    ya = _matmul("branch_attn", y_attn[None], wba_g, "nn", tm, cbb, _tile(aw, 1024), outs=[((1, s, d), F32, plain_b)])[0]

    def merge(acc_ref, extra_refs, out_refs):
        yl = acc_ref[0]
        ya_t, ga, gl = extra_refs[0][...], extra_refs[1][...], extra_refs[2][...]
        out_refs[0][...] = yl
        out_refs[1][...] = (_sigmoid(ga) * ya_t + _sigmoid(gl) * yl).astype(BF16)

    proj3 = proj[None]
    ga_spec = _bspec(proj3.shape, tm, cbb, lambda i, j, k: (i, j + off_ga // cbb))
    gl_spec = _bspec(proj3.shape, tm, cbb, lambda i, j, k: (i, j + off_gl // cbb))
    yl, merged = _matmul("branch_lru", y_lru[None], wbl_g, "nn", tm, cbb, _tile(lw, 1024),
                         outs=[((1, s, d), F32, plain_b), ((1, s, d), BF16, plain_b)], epilogue=merge,
                         extras=[(ya, plain_b), (proj3, ga_spec), (proj3, gl_spec)])
    tn_d = _tile(d, 1024)
    plain = _bspec((1, s, d), tm, tn_d, _ij)

    def residual(acc_ref, extra_refs, out_refs):
        o = acc_ref[0]
        out_refs[0][...] = o
        out_refs[1][...] = extra_refs[0][...] + extra_refs[1][...] * o

    wo_g = gathered(5, merged)
    mo, h2 = _matmul("mix_out", merged, wo_g.reshape(1, d, d), "nn", tm, tn_d, d,
                     outs=[((1, s, d), F32, plain), ((1, s, d), F32, plain)], epilogue=residual,
                     extras=[(h1[None], plain), (g2, _row_spec(tn_d))], b_buffers=1)
    mo, h2 = mo[0], h2[0]

    y3 = _norm_mod("norm3", h2, norm_ffn2, sc3, sh3, tr)
    wf2i = gathered(6, y3)
    gu3, act3 = _ffn_in("ffn2", y3, wf2i, tm)
    wf2o = gathered(7, act3)
    o3, h3 = _ffn_out("ffn2", act3, wf2o, h2, g3, tm)

    nf = norm_final.reshape(1, d)
    dh3, loss_part, d_nf = _loss_bwd("loss", h3, target, nf, tr)
    dgu3, dg3, scatter_ffn2, token = _ffn_bwd_weights("ffn2", dh3, y3, gu3, act3, o3, g3, wf2i, wf2o, tm)
    dy3 = _ffn_bwd_input("ffn2", dgu3, wf2i, token)
    dh2, dsh3, dsc3, dn3 = _norm_mod_bwd("norm3_bwd", dy3, h2, dh3, norm_ffn2, sc3, tr)

    dmo, dg2 = _gate_bwd("mix_gate_bwd", dh2, mo, g2, 1.0, tr)
    dwo = _matmul("mix_dw_out", merged, dmo[None], "tn", _tile(d, 512), tn_d, s,
                  outs=[((1, d, d), BF16, _bspec((1, d, d), _tile(d, 512), tn_d, _ij))], b_buffers=1)[0]

    def dmerge(acc_ref, extra_refs, out_refs):
        dm = acc_ref[0]
        ya_t, yl_t = extra_refs[0][...], extra_refs[1][...]
        sa, sl = _sigmoid(extra_refs[2][...]), _sigmoid(extra_refs[3][...])
        out_refs[0][...] = (dm * sa).astype(BF16)
        out_refs[1][...] = (dm * sl).astype(BF16)
        out_refs[2][...] = (dm * ya_t * sa * (1.0 - sa)).astype(BF16)
        out_refs[3][...] = (dm * yl_t * sl * (1.0 - sl)).astype(BF16)

    tn_m = _tile(math.gcd(d, off_ga, off_gl), 1024)
    plain_m = _bspec((1, s, d), tm, tn_m, _ij)
    gate_specs = [_bspec(proj3.shape, tm, tn_m, functools.partial(lambda i, j, k, o: (i, j + o), o=o // tn_m))
                  for o in (off_ga, off_gl)]
    dya, dyl, dga, dgl = _matmul("mix_dmerged", dmo[None], wo_g.reshape(1, d, d), "nt", tm, tn_m, d,
                                 outs=[((1, s, d), BF16, plain_m)] * 4, epilogue=dmerge,
                                 extras=[(ya, plain_m), (yl, plain_m), (proj3, gate_specs[0]), (proj3, gate_specs[1])],
                                 b_buffers=1)
    tm_a, tm_l = _tile(aw, 1024), _tile(lw, 1024)
    dwba = _matmul("dw_branch_attn", y_attn[None], dya, "tn", tm_a, cbb, s,
                   outs=[((N_DEV, aw, cbb), BF16, _bspec((N_DEV, aw, cbb), tm_a, cbb, _ij))])[0]
    dwbl = _matmul("dw_branch_lru", y_lru[None], dyl, "tn", tm_l, cbb, s,
                   outs=[((N_DEV, lw, cbb), BF16, _bspec((N_DEV, lw, cbb), tm_l, cbb, _ij))])[0]
    scatter_branch, token = _exchange_begin("scatter_branch", [dwba, dwbl, dwo.reshape(N_DEV, d // N_DEV, d)], True)
    tn_a, tn_l = _tile(aw, 1024), _tile(lw, 1024)
    dy_attn = _matmul("d_attn_out", dya, wba_g, "nt", tm, tn_a, cbb,
                      outs=[((1, s, aw), BF16, _bspec((1, s, aw), tm, tn_a, _ij))], extras=[_after_token(token)])[0][0]
    dy_lru = _matmul("d_lru_out", dyl, wbl_g, "nt", tm, tn_l, cbb,
                     outs=[((1, s, lw), F32, _bspec((1, s, lw), tm, tn_l, _ij))])[0][0]
    dq, dk, dv = _attn_bwd(qkv, dy_attn, attn_tot, nh, t_attn)
    dgr, dxc, d_wr, d_wi, d_br, d_bi, d_lam = _lru_bwd(dy_lru, proj, off_gr // LANE, h_lru, h_prev, xc, w_r, b_rg_gate,
                                                       w_i, b_in_gate, lru_lambda, t_lru)
    dxr, d_cw, d_cb = _conv_bwd(xr_pad, jnp.pad(dxc, ((0, CONV_HALO), (0, 0))), conv_w_full, t_lru)
    dproj = jnp.concatenate([dq.astype(BF16), dk.astype(BF16), dv.astype(BF16), dxr.astype(BF16), dgr.astype(BF16),
                             dga[0], dgl[0]], axis=1)
    tm_d = _tile(d, 512)
    dwi = _matmul("mix_dw_in", y2[None], dproj[None], "tn", tm_d, tn_i, s,
                  outs=[((N_DEV, d, cbi), BF16, _bspec((N_DEV, d, cbi), tm_d, tn_i, _ij))], b_buffers=1)[0]
    scatter_mix, token = _exchange_begin("scatter_mix", [dwi], True)
    tm_big = _tile(s, 1024, 8)
    dy2 = _matmul("mix_dy", dproj[None], wi_g, "nt", tm_big, d, tn_i,
                  outs=[((1, s, d), F32, _bspec((1, s, d), tm_big, d, _ij))], extras=[_after_token(token)])[0][0]
    dh1, dsh2, dsc2, dn2 = _norm_mod_bwd("norm2_bwd", dy2, h1, dh2, norm_mix, sc2, tr)

    dgu1, dg1, scatter_ffn1, token = _ffn_bwd_weights("ffn1", dh1, y1, gu1, act1, o1, g1, wf1i, wf1o, tm)
    dy1 = _ffn_bwd_input("ffn1", dgu1, wf1i, token)
    grad_x, dsh1, dsc1, dn1 = _norm_mod_bwd("norm1_bwd", dy1, xs, dh1, norm_ffn1, sc1, tr)

    results = {}

    def update_group(wait_name, handles, leaves, after):
        for (n, w, m, v), parts in zip(leaves, _exchange_end(wait_name, handles, after, True)):
            results[n] = [o[None] for o in _adamw("adamw_" + n, parts, w[0], m[0], v[0])]
        return results[leaves[-1][0]][0]

    done = update_group("scattered_ffn2", scatter_ffn2, [("w_ffn2_in", w_ffn2_in, m_w_ffn2_in, v_w_ffn2_in),
                                                         ("w_ffn2_out", w_ffn2_out, m_w_ffn2_out, v_w_ffn2_out)], grad_x)
    done = update_group("scattered_mix", scatter_branch + scatter_mix,
                        [("w_branch_attn", w_branch_attn, m_w_branch_attn, v_w_branch_attn),
                         ("w_branch_lru", w_branch_lru, m_w_branch_lru, v_w_branch_lru),
                         ("w_out", w_out, m_w_out, v_w_out), ("w_in", w_in, m_w_in, v_w_in)], done)

    lane_pad = jnp.zeros((1, 7 * LANE), F32)
    pack = jnp.concatenate(
        [loss_part, lane_pad, dsh1, dsc1, dg1, dsh2, dsc2, dg2, dsh3, dsc3, dg3, dn1, dn2, dn3, d_nf, d_cb, d_br, d_bi, d_lam,
         d_wr.reshape(1, -1), d_wi.reshape(1, -1), d_cw.reshape(1, -1)], axis=1)
    pack = jnp.pad(pack, ((0, 0), (0, -pack.shape[1] % (8 * LANE))))
    n_pack = pack.shape[1]
    packs = _exchange("gather_small", [pack], False, done)[0].reshape(N_DEV, n_pack // LANE, LANE)
    g_pack = _reduce_parts("sum_small", packs).reshape(1, n_pack)
    loss = g_pack[0, 0]
    off = 8 * LANE
    n_adam = 9 * d + 4 * d + 4 * lw + 2 * nlb * LANE * LANE
    g_small = g_pack[:, off:off + n_adam].reshape(1, n_adam // LANE, LANE)
    d_cw_sum = g_pack[:, off + n_adam:off + n_adam + CONV_WIDTH * lw].reshape(CONV_WIDTH, lw)
    d_cw_mine = lax.dynamic_slice(d_cw_sum, (0, me * cwb), (CONV_WIDTH, cwb))

    small_names = ["b_ada", "norm_ffn1", "norm_mix", "norm_ffn2", "norm_final", "conv_b", "b_rg_gate", "b_in_gate",
                   "lru_lambda", "w_rg_gate", "w_in_gate"]
    given = dict(b_ada=(b_ada, m_b_ada, v_b_ada), norm_ffn1=(norm_ffn1, m_norm_ffn1, v_norm_ffn1),
                 norm_mix=(norm_mix, m_norm_mix, v_norm_mix), norm_ffn2=(norm_ffn2, m_norm_ffn2, v_norm_ffn2),
                 norm_final=(norm_final, m_norm_final, v_norm_final), conv_b=(conv_b, m_conv_b, v_conv_b),
                 b_rg_gate=(b_rg_gate, m_b_rg_gate, v_b_rg_gate), b_in_gate=(b_in_gate, m_b_in_gate, v_b_in_gate),
                 lru_lambda=(lru_lambda, m_lru_lambda, v_lru_lambda), w_rg_gate=(w_rg_gate, m_w_rg_gate, v_w_rg_gate),
                 w_in_gate=(w_in_gate, m_w_in_gate, v_w_in_gate))
    packed = [jnp.concatenate([given[n][q].reshape(1, -1) for n in small_names], axis=1).reshape(n_adam // LANE, LANE)
              for q in range(3)]
    small_out = _adamw("adamw_small", g_small, *packed)
    pos = 0
    for n in small_names:
        shape = given[n][0].shape
        size = math.prod(shape)
        results[n] = [o.reshape(1, n_adam)[:, pos:pos + size].reshape(shape) for o in small_out]
        pos += size
    results["conv_w"] = [o.reshape(conv_w.shape) for o in
                         _adamw("adamw_conv_w", d_cw_mine[None], conv_w[0], m_conv_w[0], v_conv_w[0])]

    dmod_all = packs.reshape(N_DEV, n_pack)[:, off:off + 9 * d]
    dmod_mine = lax.dynamic_slice(dmod_all, (0, me * cba), (N_DEV, cba))
    dmod_rows = jnp.pad(dmod_mine, ((0, LANE - N_DEV), (0, 0)))
    c_act_t = jnp.pad(c_act.T, ((0, 0), (0, LANE - N_DEV)))
    tm_d2 = _tile(d, 256)
    d_wada = _matmul("dw_ada", c_act_t[None], dmod_rows[None], "nn", tm_d2, cba, LANE,
                     outs=[((1, d, cba), F32, _bspec((1, d, cba), tm_d2, cba, _ij))])[0]
    results["w_ada"] = [o[None] for o in _adamw("adamw_w_ada", d_wada, w_ada[0], m_w_ada[0], v_w_ada[0])]

    update_group("scattered_ffn1", scatter_ffn1, [("w_ffn1_in", w_ffn1_in, m_w_ffn1_in, v_w_ffn1_in),
                                                  ("w_ffn1_out", w_ffn1_out, m_w_ffn1_out, v_w_ffn1_out)], results["w_ada"][0])

    order = ["w_ada", "b_ada", "norm_ffn1", "w_ffn1_in", "w_ffn1_out", "norm_mix", "w_in", "conv_w", "conv_b", "w_rg_gate",
             "b_rg_gate", "w_in_gate", "b_in_gate", "lru_lambda", "w_branch_attn", "w_branch_lru", "w_out", "norm_ffn2",
             "w_ffn2_in", "w_ffn2_out", "norm_final"]
    return (loss, grad_x[None], *[results[n][0] for n in order], *[results[n][1] for n in order],
            *[results[n][2] for n in order], *[results[n][3] for n in order])
```

```python
import functools
import math

import jax
import jax.numpy as jnp
from jax import lax
from jax.experimental import pallas as pl
from jax.experimental.pallas import tpu as pltpu

F32 = jnp.float32
BF16 = jnp.bfloat16
N_DEV = 8
HEAD_DIM = 128
CONV_WIDTH = 4
CONV_HALO = 8
LRU_C = 8.0
EPS = 1e-6
ADAM_LR, ADAM_B1, ADAM_B2, ADAM_EPS, ADAM_WD, ADAM_STEP = 0.001, 0.9, 0.999, 1e-08, 0.01, 10
LANE = 128
VMEM_LIMIT = 56 * 1024 * 1024
MESH = pl.DeviceIdType.MESH

NT = (((1,), (1,)), ((), ()))
NN = (((1,), (0,)), ((), ()))
TN = (((0,), (0,)), ((), ()))


def _tile(dim, target, align=LANE):
    t = (min(target, dim) // align) * align
    while t >= align:
        if dim % t == 0:
            return t
        t -= align
    return dim


def _params(sem):
    return pltpu.CompilerParams(dimension_semantics=sem, vmem_limit_bytes=VMEM_LIMIT)


def _sigmoid(x):
    return 1.0 / (1.0 + jnp.exp(-x))


def _softplus(x):
    return jnp.maximum(x, 0.0) + jnp.log(1.0 + jnp.exp(-jnp.abs(x)))


def _log1p(z):
    w = 1.0 + z
    return jnp.where(w == 1.0, z, jnp.log(w) * z / jnp.where(w == 1.0, 1.0, w - 1.0))


def _expm1(x):
    poly = x * (1.0 + x * (0.5 + x * (1.0 / 6 + x * (1.0 / 24 + x * (1.0 / 120 + x * (1.0 / 720))))))
    return jnp.where(jnp.abs(x) < 0.25, poly, jnp.exp(x) - 1.0)


_GELU_C = math.sqrt(2.0 / math.pi)


def _gelu_and_grad(x):
    inner = _GELU_C * (x + 0.044715 * x * x * x)
    th = jnp.tanh(inner)
    val = 0.5 * x * (1.0 + th)
    grad = 0.5 * (1.0 + th) + 0.5 * x * (1.0 - th * th) * _GELU_C * (1.0 + 3 * 0.044715 * x * x)
    return val, grad


def _dot_split(x, u):
    hi = x.astype(BF16)
    lo = (x - hi.astype(F32)).astype(BF16)
    return jnp.dot(hi, u, preferred_element_type=F32) + jnp.dot(lo, u, preferred_element_type=F32)


def _mesh_position():
    x, y, c = lax.axis_index("x"), lax.axis_index("y"), lax.axis_index("c")
    return x, y, c, 4 * x + 2 * y + c


def _peers(x, y, c):
    out = []
    for mask in range(1, N_DEV):
        px = 1 - x if mask & 4 else x
        py = 1 - y if mask & 2 else y
        pc = 1 - c if mask & 1 else c
        out.append((mask, (px, py, pc), 4 * px + 2 * py + pc))
    return out


def _exchange(name, arrs, scatter, after=None):
    n = len(arrs)
    behind = [] if after is None else [after]

    def body(*refs):
        ins, outs = refs[:n], refs[n + len(behind):2 * n + len(behind)]
        send_sems, recv_sems, local_sems = refs[2 * n + len(behind):]
        x, y, c, me = _mesh_position()
        peers = _peers(x, y, c)
        waits = []
        for a in range(n):
            mine = ins[a].at[me] if scatter else ins[a]
            local = pltpu.make_async_copy(mine, outs[a].at[me], local_sems.at[a])
            local.start()
            waits.append(local.wait)
            for mask, dev, idx in peers:
                k = a * (N_DEV - 1) + mask - 1
                src = ins[a].at[idx] if scatter else ins[a]
                send = pltpu.make_async_remote_copy(src_ref=src, dst_ref=outs[a].at[me], send_sem=send_sems.at[k],
                                                    recv_sem=recv_sems.at[k], device_id=dev, device_id_type=MESH)
                send.start()
                arrival = pltpu.make_async_remote_copy(src_ref=src, dst_ref=outs[a].at[idx], send_sem=send_sems.at[k],
                                                       recv_sem=recv_sems.at[k], device_id=dev, device_id_type=MESH)
                waits.append(send.wait_send)
                waits.append(arrival.wait_recv)
        for w in waits:
            w()

    any_spec = pl.BlockSpec(memory_space=pl.ANY)
    out_shape = [jax.ShapeDtypeStruct(a.shape if scatter else (N_DEV,) + a.shape, a.dtype) for a in arrs]
    return pl.pallas_call(
        body, name=name, out_shape=out_shape, in_specs=[any_spec] * (n + len(behind)), out_specs=[any_spec] * n,
        scratch_shapes=[pltpu.SemaphoreType.DMA((n * (N_DEV - 1),)), pltpu.SemaphoreType.DMA((n * (N_DEV - 1),)),
                        pltpu.SemaphoreType.DMA((n,))],
    )(*arrs, *behind)


HBM_SPEC = pl.BlockSpec(memory_space=pltpu.HBM)
SEM_SPEC = pl.BlockSpec(memory_space=pltpu.SEMAPHORE)
DATAFLOW = pltpu.SideEffectType.DATAFLOW_SIDE_EFFECTING


ALL_MASKS = tuple(range(1, N_DEV))
SAME_CORE_MASKS = (1, 2, 4, 6)


def _exchange_begin(name, arrs, scatter, after=None, once_per_chip=()):
    n = len(arrs)
    lands = [lax.empty(a.shape if scatter else (N_DEV,) + a.shape, a.dtype) for a in arrs]
    behind = [] if after is None else [after]
    masks = [SAME_CORE_MASKS if a in once_per_chip else ALL_MASKS for a in range(n)]

    def body(*refs):
        srcs, zones, outs = refs[:n], refs[n:2 * n], refs[2 * n + len(behind):]
        x, y, c, me = _mesh_position()
        for a in range(n):
            send_sems, recv_sems = outs[4 * a], outs[4 * a + 1]
            for mask, dev, idx in _peers(x, y, c):
                if mask not in masks[a]:
                    continue
                pltpu.make_async_remote_copy(
                    src_ref=srcs[a].at[idx] if scatter else srcs[a], dst_ref=zones[a].at[me], send_sem=send_sems.at[mask - 1],
                    recv_sem=recv_sems.at[mask - 1], device_id=dev, device_id_type=MESH).start()
        outs[-1][...] = jnp.zeros_like(outs[-1])

    out_shape, out_specs, aliases = [], [], {}
    for a in range(n):
        out_shape += [pltpu.SemaphoreType.DMA((N_DEV - 1,)), pltpu.SemaphoreType.DMA((N_DEV - 1,)),
                      pltpu.HBM(arrs[a].shape, arrs[a].dtype), pltpu.HBM(lands[a].shape, lands[a].dtype)]
        out_specs += [SEM_SPEC, SEM_SPEC, HBM_SPEC, HBM_SPEC]
        aliases[a] = 4 * a + 2
        aliases[n + a] = 4 * a + 3
    out_shape.append(jax.ShapeDtypeStruct((8, LANE), F32))
    out_specs.append(pl.BlockSpec(memory_space=pltpu.VMEM))
    res = pl.pallas_call(
        body, name=name, out_shape=out_shape,
        in_specs=[HBM_SPEC] * (2 * n) + [pl.BlockSpec(memory_space=pl.ANY)] * len(behind),
        out_specs=out_specs, input_output_aliases=aliases, compiler_params=pltpu.CompilerParams(has_side_effects=DATAFLOW),
    )(*[pltpu.with_memory_space_constraint(v, pltpu.HBM) for v in list(arrs) + lands], *behind)
    return [tuple(res[4 * a:4 * a + 4]) + (masks[a],) for a in range(n)], res[-1]


def _exchange_end(name, handles, after, scatter):
    n = len(handles)
    me = 4 * lax.axis_index("x") + 2 * lax.axis_index("y") + lax.axis_index("c")

    def body(*refs):
        x, y, c, me = _mesh_position()
        for a in range(n):
            src, zone, send_sems, recv_sems = refs[4 * a:4 * a + 4]
            for mask, dev, idx in _peers(x, y, c):
                if mask not in handles[a][4]:
                    continue
                cp = pltpu.make_async_remote_copy(
                    src_ref=src.at[idx] if scatter else src, dst_ref=zone.at[idx], send_sem=send_sems.at[mask - 1],
                    recv_sem=recv_sems.at[mask - 1], device_id=dev, device_id_type=MESH)
                cp.wait_send()
                cp.wait_recv()

    operands, in_specs, out_shape, aliases = [], [], [], {}
    for a, (send_sems, recv_sems, src, zone, _) in enumerate(handles):
        operands += [src, zone, send_sems, recv_sems]
        in_specs += [HBM_SPEC, HBM_SPEC, SEM_SPEC, SEM_SPEC]
        out_shape += [pltpu.HBM(src.shape, src.dtype), pltpu.HBM(zone.shape, zone.dtype)]
        aliases[4 * a] = 2 * a
        aliases[4 * a + 1] = 2 * a + 1
    res = pl.pallas_call(
        body, name=name, out_shape=out_shape, in_specs=in_specs + [pl.BlockSpec(memory_space=pl.ANY)],
        out_specs=[HBM_SPEC] * (2 * n), input_output_aliases=aliases,
        compiler_params=pltpu.CompilerParams(has_side_effects=DATAFLOW),
    )(*operands, after)
    full = []
    for a in range(n):
        src, zone = res[2 * a], res[2 * a + 1]
        own = lax.dynamic_index_in_dim(src, me, 0, keepdims=False) if scatter else src
        full.append(lax.dynamic_update_index_in_dim(zone, own, me, 0))
    return full


def _sibling_forward(name, zones):
    n = len(zones)
    hops = (2, 4, 6)

    def body(*refs):
        outs, send_sems, recv_sems = refs[n:2 * n], refs[2 * n], refs[2 * n + 1]
        x, y, c, me = _mesh_position()
        sibling = (x, y, 1 - c)
        waits = []
        for a in range(n):
            for q, mask in enumerate(hops):
                chip = 4 * (1 - x if mask & 4 else x) + 2 * (1 - y if mask & 2 else y)
                k = a * len(hops) + q
                held, missing = outs[a].at[chip + c], outs[a].at[chip + 1 - c]
                send = pltpu.make_async_remote_copy(src_ref=held, dst_ref=held, send_sem=send_sems.at[k],
                                                    recv_sem=recv_sems.at[k], device_id=sibling, device_id_type=MESH)
                send.start()
                arrival = pltpu.make_async_remote_copy(src_ref=missing, dst_ref=missing, send_sem=send_sems.at[k],
                                                       recv_sem=recv_sems.at[k], device_id=sibling, device_id_type=MESH)
                waits += [send.wait_send, arrival.wait_recv]
        for w in waits:
            w()

    any_spec = pl.BlockSpec(memory_space=pl.ANY)
    return pl.pallas_call(
        body, name=name, out_shape=[jax.ShapeDtypeStruct(z.shape, z.dtype) for z in zones], in_specs=[any_spec] * n,
        out_specs=[any_spec] * n, input_output_aliases={a: a for a in range(n)},
        scratch_shapes=[pltpu.SemaphoreType.DMA((n * len(hops),)), pltpu.SemaphoreType.DMA((n * len(hops),))],
    )(*zones)


def _bspec(shape, tr, tc, rc, buffers=None):
    per = shape[-1] // tc
    mode = {} if buffers is None else dict(pipeline_mode=pl.Buffered(buffers))
    if len(shape) == 3:
        return pl.BlockSpec((None, tr, tc), lambda j, i, k: (rc(i, j, k)[1] // per, rc(i, j, k)[0], rc(i, j, k)[1] % per), **mode)
    return pl.BlockSpec((shape[0], None, tr, tc),
                        lambda j, i, k: (0, rc(i, j, k)[1] // per, rc(i, j, k)[0], rc(i, j, k)[1] % per), **mode)


def _ij(i, j, k):
    return i, j


def _row_spec(tn, col_tile_offset=0):
    return pl.BlockSpec((1, tn), lambda j, i, k: (0, j + col_tile_offset))


def _matmul(name, a, b, mode, tm, tn, tk, outs, epilogue=None, extras=(), b_buffers=None):
    groups = b.shape[0] if b.ndim == 4 else 1
    if mode == "nn":
        m, k_dim, n = a.shape[1], a.shape[0] * a.shape[2], b.shape[-3] * b.shape[-1]
        a_spec = _bspec(a.shape, tm, tk, lambda i, j, k: (i, k))
        b_spec = _bspec(b.shape, tk, tn, lambda i, j, k: (k, j), b_buffers)
        dims = NN
    elif mode == "nt":
        m, k_dim, n = a.shape[1], a.shape[0] * a.shape[2], b.shape[-2]
        a_spec = _bspec(a.shape, tm, tk, lambda i, j, k: (i, k))
        b_spec = _bspec(b.shape, tn, tk, lambda i, j, k: (j, k), b_buffers)
        dims = NT
    else:
        m, k_dim, n = a.shape[0] * a.shape[2], a.shape[1], b.shape[-3] * b.shape[-1]
        a_spec = _bspec(a.shape, tk, tm, lambda i, j, k: (k, i))
        b_spec = _bspec(b.shape, tk, tn, lambda i, j, k: (k, j), b_buffers)
        dims = TN
    assert m % tm == 0 and n % tn == 0 and k_dim % tk == 0, (name, m, n, k_dim, tm, tn, tk)
    nk = k_dim // tk
    n_extra, n_out = len(extras), len(outs)

    def finish(acc, extra_refs, out_refs):
        if epilogue is None:
            out_refs[0][...] = acc[0].astype(out_refs[0].dtype)
        else:
            epilogue(acc, extra_refs, out_refs)

    def products(a_ref, b_ref):
        a_tile = a_ref[...].astype(BF16)
        return [lax.dot_general(a_tile, (b_ref[g] if b.ndim == 4 else b_ref[...]).astype(BF16), dims,
                                preferred_element_type=F32) for g in range(groups)]

    def body_whole_k(*refs):
        finish(products(refs[0], refs[1]), refs[2:2 + n_extra], refs[2 + n_extra:])

    def body_k_steps(*refs):
        acc_ref = refs[-1]
        k = pl.program_id(2)

        @pl.when(k == 0)
        def _():
            acc_ref[...] = jnp.zeros_like(acc_ref)

        for g, p in enumerate(products(refs[0], refs[1])):
            acc_ref[g] += p

        @pl.when(k == nk - 1)
        def _():
            finish([acc_ref[g] for g in range(groups)], refs[2:2 + n_extra], refs[2 + n_extra:2 + n_extra + n_out])

    return pl.pallas_call(
        body_whole_k if nk == 1 else body_k_steps, name=name, grid=(n // tn, m // tm, nk),
        in_specs=[a_spec, b_spec] + [s for _, s in extras],
        out_specs=[s for _, _, s in outs],
        out_shape=[jax.ShapeDtypeStruct(shape, dtype) for shape, dtype, _ in outs],
        scratch_shapes=[] if nk == 1 else [pltpu.VMEM((groups, tm, tn), F32)],
        compiler_params=_params(("parallel", "parallel", "arbitrary")),
    )(a, b, *[arr for arr, _ in extras])


def _rowwise(name, fn, rows, vecs, outs, accs, tm):
    s = rows[0][0].shape[0]
    n_in, n_out = len(rows) + len(vecs), len(outs)

    def body(*refs):
        i = pl.program_id(0)
        res = fn(*[r[...] for r in refs[:n_in]])
        res = res if isinstance(res, tuple) else (res,)
        out_refs, acc_refs = refs[n_in:n_in + n_out], refs[n_in + n_out:]
        for ref, val in zip(out_refs, res[:n_out]):
            ref[...] = val.astype(ref.dtype)

        @pl.when(i == 0)
        def _():
            for ref in acc_refs:
                ref[...] = jnp.zeros_like(ref)

        for ref, val in zip(acc_refs, res[n_out:]):
            ref[...] += val

    in_specs = [pl.BlockSpec((tm, w), functools.partial(lambda i, cb: (i, cb), cb=cb)) for _, w, cb in rows]
    in_specs += [pl.BlockSpec(v.shape, lambda i: (0,) * v.ndim) for v in vecs]
    out_specs = [pl.BlockSpec((tm, w), lambda i: (i, 0)) for w, _ in outs] + [pl.BlockSpec((1, w), lambda i: (0, 0)) for w in accs]
    out_shape = [jax.ShapeDtypeStruct((s, w), dt) for w, dt in outs] + [jax.ShapeDtypeStruct((1, w), F32) for w in accs]
    return pl.pallas_call(
        body, name=name, grid=(s // tm,), in_specs=in_specs, out_specs=out_specs, out_shape=out_shape,
        compiler_params=_params(("arbitrary",)),
    )(*[r for r, _, _ in rows], *vecs)


def _colsum(v):
    return jnp.sum(v, axis=0, keepdims=True)


def _norm_mod(name, h, nw, sc, sh, tm):
    d = h.shape[1]

    def fn(hb, nwb, scb, shb):
        r = lax.rsqrt(jnp.mean(hb * hb, axis=-1, keepdims=True) + EPS)
        return (hb * r) * nwb * (1.0 + scb) + shb

    return _rowwise(name, fn, [(h, d, 0)], [nw, sc, sh], [(d, BF16)], [], tm)[0]


def _norm_mod_bwd(name, dy, h, dh_next, nw, sc, tm):
    d = h.shape[1]

    def fn(dyb, hb, dhb, nwb, scb):
        r = lax.rsqrt(jnp.mean(hb * hb, axis=-1, keepdims=True) + EPS)
        xh = hb * r
        dxh = dyb * (nwb * (1.0 + scb))
        dx = r * (dxh - xh * jnp.mean(dxh * xh, axis=-1, keepdims=True))
        return dhb + dx, _colsum(dyb), _colsum(dyb * xh * nwb), _colsum(dyb * xh * (1.0 + scb))

    return _rowwise(name, fn, [(dy, d, 0), (h, d, 0), (dh_next, d, 0)], [nw, sc], [(d, F32)], [d, d, d], tm)


def _gate_bwd(name, dh, o, g, coef, tm):
    d = dh.shape[1]

    def fn(dhb, ob, gb):
        return dhb * (coef * gb), _colsum(dhb * ob * coef)

    return _rowwise(name, fn, [(dh, d, 0), (o, d, 0)], [g], [(d, BF16)], [d], tm)


def _loss_bwd(name, h, target, nw, tm):
    d = h.shape[1]

    def fn(hb, tb, nwb):
        r = lax.rsqrt(jnp.mean(hb * hb, axis=-1, keepdims=True) + EPS)
        xh = hb * r
        err = xh * nwb - tb
        dy = err * (1.0 / d)
        dxh = dy * nwb
        dx = r * (dxh - xh * jnp.mean(dxh * xh, axis=-1, keepdims=True))
        loss = 0.5 * jnp.sum(jnp.mean(err * err, axis=-1, keepdims=True), axis=0, keepdims=True)
        return dx, jnp.broadcast_to(loss, (1, LANE)), _colsum(dy * xh)

    return _rowwise(name, fn, [(h, d, 0), (target, d, 0)], [nw], [(d, F32)], [LANE, d], tm)


def _head_group(nh, most):
    return max(g for g in (1, 2, 4) if g <= most and nh % g == 0)


def _attn_fwd(qkv, nh, t):
    s = qkv.shape[0]
    scale = HEAD_DIM ** -0.5
    hp = _head_group(nh, 4)
    wide = hp * HEAD_DIM
    lanes = [slice(u * HEAD_DIM, (u + 1) * HEAD_DIM) for u in range(hp)]

    def body(q_ref, k_ref, v_ref, y_ref, tot_ref):
        i = pl.program_id(1)
        row = lax.broadcasted_iota(jnp.int32, (t, t), 0)
        col = lax.broadcasted_iota(jnp.int32, (t, t), 1)
        later = (row > col).astype(BF16)
        causal = col < row
        qs = [q_ref[:, ln] for ln in lanes]

        def block(j, carry, diagonal):
            ks = pl.ds(pl.multiple_of(j * t, t), t)
            heads = range(hp)
            z = [lax.dot_general(qs[u], k_ref[ks, lanes[u]], NT, preferred_element_type=F32) * scale for u in heads]
            sp = [_softplus(z[u]) for u in heads]
            log_keep = [jnp.where(causal, -sp[u], 0.0) if diagonal else -sp[u] for u in heads]
            between = [_dot_split(log_keep[u], later) for u in heads]
            w = [jnp.exp(z[u] - sp[u] + between[u] + carry[u][1]) for u in heads]
            if diagonal:
                w = [jnp.where(causal, w[u], 0.0) for u in heads]
            o = [carry[u][0] + jnp.dot(w[u].astype(BF16), v_ref[ks, lanes[u]], preferred_element_type=F32) for u in heads]
            return tuple((o[u], carry[u][1] + jnp.sum(log_keep[u], axis=1, keepdims=True)) for u in heads)

        carry = tuple((jnp.zeros((t, HEAD_DIM), F32), jnp.zeros((t, 1), F32)) for _ in lanes)
        carry = block(i, carry, True)
        carry = lax.fori_loop(0, i, lambda jj, cr: block(i - 1 - jj, cr, False), carry)
        for u, ln in enumerate(lanes):
            y_ref[:, ln] = carry[u][0].astype(y_ref.dtype)
            tot_ref[:, ln] = jnp.broadcast_to(carry[u][1], (t, HEAD_DIM))

    g = nh // hp
    return pl.pallas_call(
        body, name="attn_fwd", grid=(g, s // t),
        in_specs=[pl.BlockSpec((t, wide), lambda h, i: (i, h)),
                  pl.BlockSpec((s, wide), lambda h, i: (0, g + h)),
                  pl.BlockSpec((s, wide), lambda h, i: (0, 2 * g + h))],
        out_specs=[pl.BlockSpec((t, wide), lambda h, i: (i, h)), pl.BlockSpec((t, wide), lambda h, i: (i, h))],
        out_shape=[jax.ShapeDtypeStruct((s, nh * HEAD_DIM), BF16), jax.ShapeDtypeStruct((s, nh * HEAD_DIM), F32)],
        compiler_params=_params(("parallel", "arbitrary")),
    )(qkv, qkv, qkv)


def _attn_bwd(qkv, dy, tot, nh, t):
    s = qkv.shape[0]
    scale = HEAD_DIM ** -0.5
    hp = _head_group(nh, 4)
    wide = hp * HEAD_DIM
    lanes = [slice(u * HEAD_DIM, (u + 1) * HEAD_DIM) for u in range(hp)]

    def body(q_ref, k_ref, v_ref, dy_ref, tot_ref, dq_ref, dk_out, dv_out, dk_ref, dv_ref):
        i = pl.program_id(1)

        @pl.when(i == 0)
        def _():
            dk_ref[...] = jnp.zeros_like(dk_ref)
            dv_ref[...] = jnp.zeros_like(dv_ref)

        row = lax.broadcasted_iota(jnp.int32, (t, t), 0)
        col = lax.broadcasted_iota(jnp.int32, (t, t), 1)
        upto = (row <= col).astype(BF16)
        before = (row < col).astype(BF16)
        causal = col < row
        qs = [q_ref[:, ln] for ln in lanes]
        dys = [dy_ref[:, ln] for ln in lanes]
        totals = [tot_ref[:, u * HEAD_DIM:u * HEAD_DIM + 1] for u in range(hp)]

        def block(j, carry, diagonal):
            ks = pl.ds(pl.multiple_of(j * t, t), t)
            heads = range(hp)
            kb = [k_ref[ks, ln] for ln in lanes]
            vb = [v_ref[ks, ln] for ln in lanes]
            z = [lax.dot_general(qs[u], kb[u], NT, preferred_element_type=F32) * scale for u in heads]
            dw = [lax.dot_general(dys[u], vb[u], NT, preferred_element_type=F32) for u in heads]
            sp = [_softplus(z[u]) for u in heads]
            log_keep = [jnp.where(causal, -sp[u], 0.0) if diagonal else -sp[u] for u in heads]
            upto_sum = [_dot_split(log_keep[u], upto) for u in heads]
            w = [jnp.exp(z[u] - sp[u] + (totals[u] - carry[u][1] - upto_sum[u])) for u in heads]
            if diagonal:
                w = [jnp.where(causal, w[u], 0.0) for u in heads]
            g = [dw[u] * w[u] for u in heads]
            g_before = [_dot_split(g[u], before) for u in heads]
            dz = [(g[u] * jnp.exp(-sp[u]) - jnp.exp(z[u] - sp[u]) * (carry[u][2] + g_before[u])) * scale for u in heads]
            if diagonal:
                dz = [jnp.where(causal, dz[u], 0.0) for u in heads]
            dzb = [dz[u].astype(BF16) for u in heads]
            dq = [carry[u][0] + jnp.dot(dzb[u], kb[u], preferred_element_type=F32) for u in heads]
            for u in heads:
                dk_ref[ks, lanes[u]] += lax.dot_general(dzb[u], qs[u], TN, preferred_element_type=F32)
            for u in heads:
                dv_ref[ks, lanes[u]] += lax.dot_general(w[u].astype(BF16), dys[u], TN, preferred_element_type=F32)
            return tuple((dq[u], carry[u][1] + jnp.sum(log_keep[u], axis=1, keepdims=True),
                          carry[u][2] + jnp.sum(g[u], axis=1, keepdims=True)) for u in heads)

        zero = jnp.zeros((t, 1), F32)
        carry = tuple((jnp.zeros((t, HEAD_DIM), F32), zero, zero) for _ in lanes)
        carry = lax.fori_loop(0, i, lambda j, cr: block(j, cr, False), carry)
        carry = block(i, carry, True)
        for u, ln in enumerate(lanes):
            dq_ref[:, ln] = carry[u][0].astype(dq_ref.dtype)

        @pl.when(i == pl.num_programs(1) - 1)
        def _():
            dk_out[...] = dk_ref[...].astype(dk_out.dtype)
            dv_out[...] = dv_ref[...].astype(dv_out.dtype)

    g = nh // hp
    tile = lambda off: pl.BlockSpec((t, wide), lambda h, i: (i, off + h))
    head = lambda off, **mode: pl.BlockSpec((s, wide), lambda h, i: (0, off + h), **mode)
    once = dict(pipeline_mode=pl.Buffered(1))
    return pl.pallas_call(
        body, name="attn_bwd", grid=(g, s // t),
        in_specs=[tile(0), head(g, **once), head(2 * g, **once), tile(0), tile(0)],
        out_specs=[tile(0), head(0), head(0)],
        out_shape=[jax.ShapeDtypeStruct((s, nh * HEAD_DIM), BF16)] * 3,
        scratch_shapes=[pltpu.VMEM((s, wide), F32), pltpu.VMEM((s, wide), F32)],
        compiler_params=_params(("parallel", "arbitrary")),
    )(qkv, qkv, qkv, dy, tot)


def _lru_gates(xc, w_r, b_r, w_i, b_i, lam):
    xb = xc.astype(BF16)
    r = _sigmoid(jnp.dot(xb, w_r.astype(BF16), preferred_element_type=F32) + b_r)
    i = _sigmoid(jnp.dot(xb, w_i.astype(BF16), preferred_element_type=F32) + b_i)
    neg_lam = -lam
    sp_lam = jnp.maximum(neg_lam, 0.0) + _log1p(jnp.exp(-jnp.abs(neg_lam)))
    log_a = -LRU_C * r * sp_lam
    a = jnp.exp(log_a)
    mult = jnp.sqrt(-_expm1(2.0 * log_a))
    return r, i, sp_lam, a, mult


def _conv_taps(xpad_chunk, conv_w, t):
    shifted = [xpad_chunk[CONV_HALO:, :]]
    for d in range(1, CONV_WIDTH):
        shifted.append(pltpu.roll(xpad_chunk, d, 0)[CONV_HALO:, :])
    weights = [conv_w[CONV_WIDTH - 1 - d:CONV_WIDTH - d, :] for d in range(CONV_WIDTH)]
    return shifted, weights


def _lru_fwd(xr_pad, proj, gr_block0, conv_w, conv_b, w_r, b_r, w_i, b_i, lam, t):
    s, w = xr_pad.shape[0] - CONV_HALO, xr_pad.shape[1]
    nblk = w // LANE
    nchunk = s // t
    steps = [1 << p for p in range(t.bit_length() - 1)]
    assert (1 << (t.bit_length() - 1)) == t and w_r.shape[1:] == (LANE, LANE)

    def body(x_ref, gr_ref, cw_ref, cb_ref, wr_ref, br_ref, wi_ref, bi_ref, lam_ref, h_ref, hp_ref, xc_ref, y_ref):
        row = lax.broadcasted_iota(jnp.int32, (t, LANE), 0)

        def chunk(ci, h_in):
            t0 = pl.multiple_of(ci * t, t)
            shifted, weights = _conv_taps(x_ref[pl.ds(t0, t + CONV_HALO), :], cw_ref[...], t)
            xc = cb_ref[...] + sum(wd * xs for wd, xs in zip(weights, shifted))
            r, i, _, a, mult = _lru_gates(xc, wr_ref[...], br_ref[...], wi_ref[...], bi_ref[...], lam_ref[...])
            coef, val = a, mult * (i * xc)
            for d in steps:
                ok = row >= d
                val = jnp.where(ok, coef * pltpu.roll(val, d, 0) + val, val)
                coef = jnp.where(ok, coef * pltpu.roll(coef, d, 0), coef)
            h = val + coef * h_in
            rows = pl.ds(t0, t)
            h_ref[rows, :] = h
            hp_ref[rows, :] = jnp.where(row == 0, h_in, pltpu.roll(h, 1, 0))
            xc_ref[rows, :] = xc
            y_ref[rows, :] = (h * _gelu_and_grad(gr_ref[rows, :])[0]).astype(y_ref.dtype)
            return h[t - 1:t, :]

        lax.fori_loop(0, nchunk, chunk, jnp.zeros((1, LANE), F32))

    col = lambda rows: pl.BlockSpec((rows, LANE), lambda n: (0, n))
    return pl.pallas_call(
        body, name="lru_fwd", grid=(nblk,),
        in_specs=[col(s + CONV_HALO), pl.BlockSpec((s, LANE), lambda n: (0, gr_block0 + n)), col(CONV_WIDTH), col(1),
                  pl.BlockSpec((None, LANE, LANE), lambda n: (n, 0, 0)), col(1),
                  pl.BlockSpec((None, LANE, LANE), lambda n: (n, 0, 0)), col(1), col(1)],
        out_specs=[col(s)] * 4,
        out_shape=[jax.ShapeDtypeStruct((s, w), F32)] * 3 + [jax.ShapeDtypeStruct((s, w), BF16)],
        compiler_params=_params(("parallel",)),
    )(xr_pad, proj, conv_w, conv_b, w_r, b_r, w_i, b_i, lam)


def _lru_bwd(dy, proj, gr_block0, h, h_prev, xc, w_r, b_r, w_i, b_i, lam, t):
    s, w = dy.shape
    nblk = w // LANE
    nchunk = s // t
    steps = [1 << p for p in range(t.bit_length() - 1)]

    def body(dy_ref, gr_ref, h_ref, hp_ref, xc_ref, wr_ref, br_ref, wi_ref, bi_ref, lam_ref,
             dgr_ref, dxc_ref, dwr_ref, dwi_ref, dbr_ref, dbi_ref, dlam_ref):
        row = lax.broadcasted_iota(jnp.int32, (t, LANE), 0)
        for ref in (dwr_ref, dwi_ref, dbr_ref, dbi_ref, dlam_ref):
            ref[...] = jnp.zeros_like(ref)

        def chunk(cc, carry):
            lam_next, a_next = carry
            rows = pl.ds(pl.multiple_of((nchunk - 1 - cc) * t, t), t)
            dyb, hb, xcb = dy_ref[rows, :], h_ref[rows, :], xc_ref[rows, :]
            gel, dgel = _gelu_and_grad(gr_ref[rows, :])
            dgr_ref[rows, :] = dyb * hb * dgel
            w_r, w_i = wr_ref[...], wi_ref[...]
            r, i, sp_lam, a, mult = _lru_gates(xcb, w_r, br_ref[...], w_i, bi_ref[...], lam_ref[...])
            coef = jnp.where(row == t - 1, a_next, pltpu.roll(a, t - 1, 0))
            val = dyb * gel
            for d in steps:
                ok = row < t - d
                val = jnp.where(ok, coef * pltpu.roll(val, t - d, 0) + val, val)
                coef = jnp.where(ok, coef * pltpu.roll(coef, t - d, 0), coef)
            adj = val + coef * lam_next
            da = adj * hp_ref[rows, :]
            v = i * xcb
            dmult, dv = adj * v, adj * mult
            dlog_a = da * a - (a * a) * dmult / mult
            dr_pre = (-LRU_C * sp_lam) * dlog_a * r * (1.0 - r)
            di_pre = dv * xcb * i * (1.0 - i)
            dlam_ref[...] += _colsum(-LRU_C * r * dlog_a)
            dbr_ref[...] += _colsum(dr_pre)
            dbi_ref[...] += _colsum(di_pre)
            xb, drb, dib = xcb.astype(BF16), dr_pre.astype(BF16), di_pre.astype(BF16)
            dwr_ref[...] += lax.dot_general(xb, drb, TN, preferred_element_type=F32)
            dwi_ref[...] += lax.dot_general(xb, dib, TN, preferred_element_type=F32)
            dxc_ref[rows, :] = (dv * i + lax.dot_general(drb, w_r.astype(BF16), NT, preferred_element_type=F32)
                                + lax.dot_general(dib, w_i.astype(BF16), NT, preferred_element_type=F32))
            return adj[0:1, :], a[0:1, :]

        lax.fori_loop(0, nchunk, chunk, (jnp.zeros((1, LANE), F32), jnp.zeros((1, LANE), F32)))
        dlam_ref[...] = dlam_ref[...] * (-_sigmoid(-lam_ref[...]))

    col = lambda rows: pl.BlockSpec((rows, LANE), lambda n: (0, n))
    mat = pl.BlockSpec((None, LANE, LANE), lambda n: (n, 0, 0))
    return pl.pallas_call(
        body, name="lru_bwd", grid=(nblk,),
        in_specs=[col(s), pl.BlockSpec((s, LANE), lambda n: (0, gr_block0 + n)), col(s), col(s), col(s),
                  mat, col(1), mat, col(1), col(1)],
        out_specs=[col(s), col(s), mat, mat, col(1), col(1), col(1)],
        out_shape=[jax.ShapeDtypeStruct((s, w), F32)] * 2 + [jax.ShapeDtypeStruct((nblk, LANE, LANE), F32)] * 2
        + [jax.ShapeDtypeStruct((1, w), F32)] * 3,
        compiler_params=_params(("parallel",)),
    )(dy, proj, h, h_prev, xc, w_r, b_r, w_i, b_i, lam)


def _conv_bwd(xr_pad, dxc_pad, conv_w, t):
    s, w = xr_pad.shape[0] - CONV_HALO, xr_pad.shape[1]
    nchunk = s // t

    def body(x_ref, g_ref, cw_ref, dx_ref, dcw_ref, dcb_ref):
        dcw_ref[...] = jnp.zeros_like(dcw_ref)
        dcb_ref[...] = jnp.zeros_like(dcb_ref)

        def chunk(ci, _):
            t0 = pl.multiple_of(ci * t, t)
            shifted, weights = _conv_taps(x_ref[pl.ds(t0, t + CONV_HALO), :], cw_ref[...], t)
            gpad = g_ref[pl.ds(t0, t + CONV_HALO), :]
            g = gpad[:t, :]
            dx = weights[0] * g
            for d in range(1, CONV_WIDTH):
                dx = dx + weights[d] * pltpu.roll(gpad, t + CONV_HALO - d, 0)[:t, :]
            dx_ref[pl.ds(t0, t), :] = dx
            for d in range(CONV_WIDTH):
                dcw_ref[CONV_WIDTH - 1 - d:CONV_WIDTH - d, :] += _colsum(g * shifted[d])
            dcb_ref[...] += _colsum(g)
            return 0

        lax.fori_loop(0, nchunk, chunk, 0)

    col = lambda rows: pl.BlockSpec((rows, LANE), lambda n: (0, n))
    return pl.pallas_call(
        body, name="conv_bwd", grid=(w // LANE,),
        in_specs=[col(s + CONV_HALO), col(s + CONV_HALO), col(CONV_WIDTH)],
        out_specs=[col(s), col(CONV_WIDTH), col(1)],
        out_shape=[jax.ShapeDtypeStruct((s, w), F32), jax.ShapeDtypeStruct((CONV_WIDTH, w), F32),
                   jax.ShapeDtypeStruct((1, w), F32)],
        compiler_params=_params(("parallel",)),
    )(xr_pad, dxc_pad, conv_w)


def _sum_parts(parts_ref):
    g = parts_ref[0].astype(F32)
    for p in range(1, parts_ref.shape[0]):
        g = g + parts_ref[p].astype(F32)
    return g


def _reduce_parts(name, parts):
    p, r, c = parts.shape
    tr = _tile(r, max(8, (1 << 19) // c), 8)

    def body(parts_ref, g_ref):
        g_ref[...] = _sum_parts(parts_ref)

    return pl.pallas_call(
        body, name=name, grid=(r // tr,), in_specs=[pl.BlockSpec((p, tr, c), lambda i: (0, i, 0))],
        out_specs=pl.BlockSpec((tr, c), lambda i: (i, 0)), out_shape=jax.ShapeDtypeStruct((r, c), F32),
        compiler_params=_params(("parallel",)),
    )(parts)


def _adamw(name, parts, w, m, v):
    p, r, c = parts.shape
    tr = _tile(r, max(8, (1 << 18) // c), 8)

    def body(parts_ref, w_ref, m_ref, v_ref, g_ref, d_ref, nm_ref, nv_ref):
        g = _sum_parts(parts_ref)
        nm = ADAM_B1 * m_ref[...] + (1.0 - ADAM_B1) * g
        nv = ADAM_B2 * v_ref[...] + (1.0 - ADAM_B2) * (g * g)
        m_hat = nm / (1.0 - ADAM_B1 ** ADAM_STEP)
        v_hat = nv / (1.0 - ADAM_B2 ** ADAM_STEP)
        g_ref[...] = g
        d_ref[...] = -ADAM_LR * (m_hat / (jnp.sqrt(v_hat) + ADAM_EPS) + ADAM_WD * w_ref[...])
        nm_ref[...] = nm
        nv_ref[...] = nv

    blk = pl.BlockSpec((tr, c), lambda i: (i, 0))
    return pl.pallas_call(
        body, name=name, grid=(r // tr,), in_specs=[pl.BlockSpec((p, tr, c), lambda i: (0, i, 0)), blk, blk, blk],
        out_specs=[blk] * 4, out_shape=[jax.ShapeDtypeStruct((r, c), F32)] * 4,
        compiler_params=_params(("parallel",)),
    )(parts, w, m, v)


def _ffn_in(tag, y, w_in_g, tm):
    s, d = y.shape
    half = N_DEV // 2
    cb = w_in_g.shape[2]
    ff = half * cb

    def swiglu(acc, extra_refs, out_refs):
        g, u = acc
        out_refs[0][0] = g
        out_refs[0][1] = u
        out_refs[1][...] = (g * _sigmoid(g) * u).astype(BF16)

    gu_shape = (2, 1, s, ff)
    gu, act = _matmul(
        tag + "_in", y[None], w_in_g.reshape(2, half, d, cb), "nn", tm, cb, d,
        outs=[(gu_shape, F32, _bspec(gu_shape, tm, cb, _ij)), ((1, s, ff), BF16, _bspec((1, s, ff), tm, cb, _ij))],
        epilogue=swiglu, b_buffers=1)
    return gu, act


def _ffn_out(tag, act, w_out_g, res, gate, tm):
    _, s, ff = act.shape
    d = res.shape[1]
    tn = _tile(d, 1024)

    def residual(acc, extra_refs, out_refs):
        out_refs[0][...] = acc[0]
        out_refs[1][...] = extra_refs[0][...] + 0.5 * extra_refs[1][...] * acc[0]

    plain = _bspec((1, s, d), tm, tn, _ij)
    o, h_new = _matmul(
        tag + "_out", act, w_out_g.reshape(1, ff, d), "nn", tm, tn, ff,
        outs=[((1, s, d), F32, plain), ((1, s, d), F32, plain)], epilogue=residual,
        extras=[(res[None], plain), (gate, _row_spec(tn))], b_buffers=1)
    return o[0], h_new[0]


def _after_token(token):
    return token, pl.BlockSpec(token.shape, lambda j, i, k: (0, 0))


def _ffn_bwd_weights(tag, dh, y, gu, act, o, gate, w_in_g, w_out_g, tm):
    s, d = dh.shape
    half = N_DEV // 2
    cb = w_in_g.shape[2]
    ff = half * cb
    do, dgate = _gate_bwd(tag + "_gate_bwd", dh, o, gate, 0.5, _tile(s, 256, 8))

    tn_d, tm_f = _tile(d, 1024), _tile(ff, 512)
    dw_out = _matmul(tag + "_dw_out", act, do[None], "tn", tm_f, tn_d, s,
                     outs=[((1, ff, d), BF16, _bspec((1, ff, d), tm_f, tn_d, _ij))], b_buffers=1)[0]
    dw_out = dw_out.reshape(N_DEV, ff // N_DEV, d)
    out_handles, token = _exchange_begin(tag + "_scatter_out", [dw_out], True)

    def dswiglu(acc, extra_refs, out_refs):
        dact = acc[0]
        g, u = extra_refs[0][0], extra_refs[0][1]
        sg = _sigmoid(g)
        out_refs[0][0] = (dact * u * sg * (1.0 + g * (1.0 - sg))).astype(BF16)
        out_refs[0][1] = (dact * g * sg).astype(BF16)

    gu_shape = (2, 1, s, ff)
    gu_spec = _bspec(gu_shape, tm, cb, _ij)
    dgu = _matmul(tag + "_dact", do[None], w_out_g.reshape(1, ff, d), "nt", tm, cb, d,
                  outs=[(gu_shape, BF16, gu_spec)], epilogue=dswiglu, extras=[(gu, gu_spec), _after_token(token)],
                  b_buffers=1)[0]
    dgu = dgu.reshape(2, s, ff)

    tm_d = _tile(d, 512)
    dw_in = _matmul(tag + "_dw_in", y[None], dgu, "tn", tm_d, cb, s,
                    outs=[((N_DEV, d, cb), BF16, _bspec((N_DEV, d, cb), tm_d, cb, _ij))], b_buffers=1)[0]
    in_handles, token = _exchange_begin(tag + "_scatter_in", [dw_in], True)
    return dgu, dgate, in_handles + out_handles, token


def _ffn_bwd_input(tag, dgu, w_in_g, token):
    s, d = dgu.shape[1], w_in_g.shape[1]
    tm_big = _tile(s, 1024, 8)
    return _matmul(tag + "_dy", dgu, w_in_g, "nt", tm_big, d, w_in_g.shape[2],
                   outs=[((1, s, d), F32, _bspec((1, s, d), tm_big, d, _ij))], extras=[_after_token(token)])[0][0]


def kernel(x, c, w_ada, b_ada, norm_ffn1, w_ffn1_in, w_ffn1_out, norm_mix, w_in, conv_w, conv_b, w_rg_gate, b_rg_gate, w_in_gate, b_in_gate, lru_lambda, w_branch_attn, w_branch_lru, w_out, norm_ffn2, w_ffn2_in, w_ffn2_out, norm_final, loss_target, m_w_ada, m_b_ada, m_norm_ffn1, m_w_ffn1_in, m_w_ffn1_out, m_norm_mix, m_w_in, m_conv_w, m_conv_b, m_w_rg_gate, m_b_rg_gate, m_w_in_gate, m_b_in_gate, m_lru_lambda, m_w_branch_attn, m_w_branch_lru, m_w_out, m_norm_ffn2, m_w_ffn2_in, m_w_ffn2_out, m_norm_final, v_w_ada, v_b_ada, v_norm_ffn1, v_w_ffn1_in, v_w_ffn1_out, v_norm_mix, v_w_in, v_conv_w, v_conv_b, v_w_rg_gate, v_b_rg_gate, v_w_in_gate, v_b_in_gate, v_lru_lambda, v_w_branch_attn, v_w_branch_lru, v_w_out, v_norm_ffn2, v_w_ffn2_in, v_w_ffn2_out, v_norm_final):
    xs, target = x[0], loss_target[0]
    s, d = xs.shape
    aw, lw = w_branch_attn.shape[1], w_branch_lru.shape[1]
    nh, nlb = aw // HEAD_DIM, w_rg_gate.shape[1]
    cba, cbi, cbb, cwb = w_ada.shape[2], w_in.shape[2], w_branch_attn.shape[2], conv_w.shape[2]
    assert lw == nlb * LANE and cwb * N_DEV == lw and 3 * aw + 2 * lw + 2 * d == cbi * N_DEV
    me = 4 * lax.axis_index("x") + 2 * lax.axis_index("y") + lax.axis_index("c")
    tm = _tile(s, 512, 8)
    tr = _tile(s, 256, 8)
    t_attn = _tile(s, 256, 8)
    t_lru = _tile(s, 256, 8)

    small = _exchange("gather_c", [jnp.concatenate([c, conv_w.reshape(1, CONV_WIDTH * cwb)], axis=1)], False)[0][:, 0, :]
    c_all = small[:, :d]
    conv_w_full = small[:, d:].reshape(N_DEV, CONV_WIDTH, cwb).transpose(1, 0, 2).reshape(CONV_WIDTH, lw)
    c_act = _rowwise("silu_c", lambda v: v * _sigmoid(v), [(c_all, d, 0)], [], [(d, F32)], [], N_DEV)[0]

    def add_bias(acc_ref, extra_refs, out_refs):
        out_refs[0][...] = acc_ref[0] + extra_refs[0][...]

    b_ada_mine = lax.dynamic_slice(b_ada, (0, me * cba), (1, cba))
    mod_part = _matmul("mod", c_act[None], w_ada, "nn", N_DEV, cba, _tile(d, 512),
                       outs=[((1, N_DEV, cba), F32, _bspec((1, N_DEV, cba), N_DEV, cba, _ij))], epilogue=add_bias,
                       extras=[(b_ada_mine, _row_spec(cba))])[0][0]
    mod_all = _exchange("gather_mod", [mod_part], False)[0]
    mod = lax.dynamic_index_in_dim(mod_all, me, axis=1, keepdims=False).reshape(1, 9 * d)
    sh1, sc1, g1, sh2, sc2, g2, sh3, sc3, g3 = [mod[:, n * d:(n + 1) * d] for n in range(9)]

    shards = [w_ffn1_in[0], w_ffn1_out[0], w_in[0], w_branch_attn[0], w_branch_lru[0], w_out[0], w_ffn2_in[0], w_ffn2_out[0]]
    early = (0, 1, 2)
    gathers, token = _exchange_begin("gather_w", [w.astype(BF16) for w in shards], False, mod, early)

    def gathered(n, after):
        full = _exchange_end("gathered_w%d" % n, [gathers[n]], after, False)
        return (_sibling_forward("forwarded_w%d" % n, full) if n in early else full)[0]

    y1 = _norm_mod("norm1", xs, norm_ffn1 + token[:1, :1], sc1, sh1, tr)
    wf1i = gathered(0, y1)
    gu1, act1 = _ffn_in("ffn1", y1, wf1i, tm)
    wf1o = gathered(1, act1)
    o1, h1 = _ffn_out("ffn1", act1, wf1o, xs, g1, tm)

    y2 = _norm_mod("norm2", h1, norm_mix, sc2, sh2, tr)
    wi_g = gathered(2, y2)
    tn_i = _tile(cbi, 1152)
    proj = _matmul("mix_in", y2[None], wi_g, "nn", tm, tn_i, d,
                   outs=[((1, s, N_DEV * cbi), F32, _bspec((1, s, N_DEV * cbi), tm, tn_i, _ij))], b_buffers=1)[0][0]
    off_xr, off_gr, off_ga, off_gl = 3 * aw, 3 * aw + lw, 3 * aw + 2 * lw, 3 * aw + 2 * lw + d
    qkv = proj[:, :3 * aw].astype(BF16)
    y_attn, attn_tot = _attn_fwd(qkv, nh, t_attn)
    xr_pad = jnp.pad(proj[:, off_xr:off_xr + lw], ((CONV_HALO, 0), (0, 0)))
    w_r, w_i = w_rg_gate[0], w_in_gate[0]
    h_lru, h_prev, xc, y_lru = _lru_fwd(xr_pad, proj, off_gr // LANE, conv_w_full, conv_b, w_r, b_rg_gate, w_i,
                                        b_in_gate, lru_lambda, t_lru)
    plain_b = _bspec((1, s, d), tm, cbb, _ij)
    wba_g = gathered(3, y_attn)
    wbl_g = gathered(4, y_lru)
    ya = _matmul("branch_attn", y_attn[None], wba_g, "nn", tm, cbb, _tile(aw, 1024), outs=[((1, s, d), F32, plain_b)])[0]

    def merge(acc_ref, extra_refs, out_refs):
        yl = acc_ref[0]
        ya_t, ga, gl = extra_refs[0][...], extra_refs[1][...], extra_refs[2][...]
        out_refs[0][...] = yl
        out_refs[1][...] = (_sigmoid(ga) * ya_t + _sigmoid(gl) * yl).astype(BF16)

    proj3 = proj[None]
    ga_spec = _bspec(proj3.shape, tm, cbb, lambda i, j, k: (i, j + off_ga // cbb))
    gl_spec = _bspec(proj3.shape, tm, cbb, lambda i, j, k: (i, j + off_gl // cbb))
    yl, merged = _matmul("branch_lru", y_lru[None], wbl_g, "nn", tm, cbb, _tile(lw, 1024),
                         outs=[((1, s, d), F32, plain_b), ((1, s, d), BF16, plain_b)], epilogue=merge,
                         extras=[(ya, plain_b), (proj3, ga_spec), (proj3, gl_spec)])
    tn_d = _tile(d, 1024)
    plain = _bspec((1, s, d), tm, tn_d, _ij)

    def residual(acc_ref, extra_refs, out_refs):
        o = acc_ref[0]
        out_refs[0][...] = o
        out_refs[1][...] = extra_refs[0][...] + extra_refs[1][...] * o

    wo_g = gathered(5, merged)
    mo, h2 = _matmul("mix_out", merged, wo_g.reshape(1, d, d), "nn", tm, tn_d, d,
                     outs=[((1, s, d), F32, plain), ((1, s, d), F32, plain)], epilogue=residual,
                     extras=[(h1[None], plain), (g2, _row_spec(tn_d))], b_buffers=1)
    mo, h2 = mo[0], h2[0]

    y3 = _norm_mod("norm3", h2, norm_ffn2, sc3, sh3, tr)
    wf2i = gathered(6, y3)
    gu3, act3 = _ffn_in("ffn2", y3, wf2i, tm)
    wf2o = gathered(7, act3)
    o3, h3 = _ffn_out("ffn2", act3, wf2o, h2, g3, tm)

    nf = norm_final.reshape(1, d)
    dh3, loss_part, d_nf = _loss_bwd("loss", h3, target, nf, tr)
    dgu3, dg3, scatter_ffn2, token = _ffn_bwd_weights("ffn2", dh3, y3, gu3, act3, o3, g3, wf2i, wf2o, tm)
    dy3 = _ffn_bwd_input("ffn2", dgu3, wf2i, token)
    dh2, dsh3, dsc3, dn3 = _norm_mod_bwd("norm3_bwd", dy3, h2, dh3, norm_ffn2, sc3, tr)

    dmo, dg2 = _gate_bwd("mix_gate_bwd", dh2, mo, g2, 1.0, tr)
    dwo = _matmul("mix_dw_out", merged, dmo[None], "tn", _tile(d, 512), tn_d, s,
                  outs=[((1, d, d), BF16, _bspec((1, d, d), _tile(d, 512), tn_d, _ij))], b_buffers=1)[0]

    def dmerge(acc_ref, extra_refs, out_refs):
        dm = acc_ref[0]
        ya_t, yl_t = extra_refs[0][...], extra_refs[1][...]
        sa, sl = _sigmoid(extra_refs[2][...]), _sigmoid(extra_refs[3][...])
        out_refs[0][...] = (dm * sa).astype(BF16)
        out_refs[1][...] = (dm * sl).astype(BF16)
        out_refs[2][...] = (dm * ya_t * sa * (1.0 - sa)).astype(BF16)
        out_refs[3][...] = (dm * yl_t * sl * (1.0 - sl)).astype(BF16)

    tn_m = _tile(math.gcd(d, off_ga, off_gl), 1024)
    plain_m = _bspec((1, s, d), tm, tn_m, _ij)
    gate_specs = [_bspec(proj3.shape, tm, tn_m, functools.partial(lambda i, j, k, o: (i, j + o), o=o // tn_m))
                  for o in (off_ga, off_gl)]
    dya, dyl, dga, dgl = _matmul("mix_dmerged", dmo[None], wo_g.reshape(1, d, d), "nt", tm, tn_m, d,
                                 outs=[((1, s, d), BF16, plain_m)] * 4, epilogue=dmerge,
                                 extras=[(ya, plain_m), (yl, plain_m), (proj3, gate_specs[0]), (proj3, gate_specs[1])],
                                 b_buffers=1)
    tm_a, tm_l = _tile(aw, 1024), _tile(lw, 1024)
    dwba = _matmul("dw_branch_attn", y_attn[None], dya, "tn", tm_a, cbb, s,
                   outs=[((N_DEV, aw, cbb), BF16, _bspec((N_DEV, aw, cbb), tm_a, cbb, _ij))])[0]
    dwbl = _matmul("dw_branch_lru", y_lru[None], dyl, "tn", tm_l, cbb, s,
                   outs=[((N_DEV, lw, cbb), BF16, _bspec((N_DEV, lw, cbb), tm_l, cbb, _ij))])[0]
    scatter_branch, token = _exchange_begin("scatter_branch", [dwba, dwbl, dwo.reshape(N_DEV, d // N_DEV, d)], True)
    tn_a, tn_l = _tile(aw, 1024), _tile(lw, 1024)
    dy_attn = _matmul("d_attn_out", dya, wba_g, "nt", tm, tn_a, cbb,
                      outs=[((1, s, aw), BF16, _bspec((1, s, aw), tm, tn_a, _ij))], extras=[_after_token(token)])[0][0]
    dy_lru = _matmul("d_lru_out", dyl, wbl_g, "nt", tm, tn_l, cbb,
                     outs=[((1, s, lw), F32, _bspec((1, s, lw), tm, tn_l, _ij))])[0][0]
    dq, dk, dv = _attn_bwd(qkv, dy_attn, attn_tot, nh, t_attn)
    dgr, dxc, d_wr, d_wi, d_br, d_bi, d_lam = _lru_bwd(dy_lru, proj, off_gr // LANE, h_lru, h_prev, xc, w_r, b_rg_gate,
                                                       w_i, b_in_gate, lru_lambda, t_lru)
    dxr, d_cw, d_cb = _conv_bwd(xr_pad, jnp.pad(dxc, ((0, CONV_HALO), (0, 0))), conv_w_full, t_lru)
    dproj = jnp.concatenate([dq.astype(BF16), dk.astype(BF16), dv.astype(BF16), dxr.astype(BF16), dgr.astype(BF16),
                             dga[0], dgl[0]], axis=1)
    tm_d = _tile(d, 512)
    dwi = _matmul("mix_dw_in", y2[None], dproj[None], "tn", tm_d, tn_i, s,
                  outs=[((N_DEV, d, cbi), BF16, _bspec((N_DEV, d, cbi), tm_d, tn_i, _ij))], b_buffers=1)[0]
    scatter_mix, token = _exchange_begin("scatter_mix", [dwi], True)
    tm_big = _tile(s, 1024, 8)
    dy2 = _matmul("mix_dy", dproj[None], wi_g, "nt", tm_big, d, tn_i,
                  outs=[((1, s, d), F32, _bspec((1, s, d), tm_big, d, _ij))], extras=[_after_token(token)])[0][0]
    dh1, dsh2, dsc2, dn2 = _norm_mod_bwd("norm2_bwd", dy2, h1, dh2, norm_mix, sc2, tr)

    dgu1, dg1, scatter_ffn1, token = _ffn_bwd_weights("ffn1", dh1, y1, gu1, act1, o1, g1, wf1i, wf1o, tm)
    dy1 = _ffn_bwd_input("ffn1", dgu1, wf1i, token)
    grad_x, dsh1, dsc1, dn1 = _norm_mod_bwd("norm1_bwd", dy1, xs, dh1, norm_ffn1, sc1, tr)

    results = {}

    def update_group(wait_name, handles, leaves, after):
        for (n, w, m, v), parts in zip(leaves, _exchange_end(wait_name, handles, after, True)):
            results[n] = [o[None] for o in _adamw("adamw_" + n, parts, w[0], m[0], v[0])]
        return results[leaves[-1][0]][0]

    done = update_group("scattered_ffn2", scatter_ffn2, [("w_ffn2_in", w_ffn2_in, m_w_ffn2_in, v_w_ffn2_in),
                                                         ("w_ffn2_out", w_ffn2_out, m_w_ffn2_out, v_w_ffn2_out)], grad_x)
    done = update_group("scattered_mix", scatter_branch + scatter_mix,
                        [("w_branch_attn", w_branch_attn, m_w_branch_attn, v_w_branch_attn),
                         ("w_branch_lru", w_branch_lru, m_w_branch_lru, v_w_branch_lru),
                         ("w_out", w_out, m_w_out, v_w_out), ("w_in", w_in, m_w_in, v_w_in)], done)

    lane_pad = jnp.zeros((1, 7 * LANE), F32)
    pack = jnp.concatenate(
        [loss_part, lane_pad, dsh1, dsc1, dg1, dsh2, dsc2, dg2, dsh3, dsc3, dg3, dn1, dn2, dn3, d_nf, d_cb, d_br, d_bi, d_lam,
         d_wr.reshape(1, -1), d_wi.reshape(1, -1), d_cw.reshape(1, -1)], axis=1)
    pack = jnp.pad(pack, ((0, 0), (0, -pack.shape[1] % (8 * LANE))))
    n_pack = pack.shape[1]
    packs = _exchange("gather_small", [pack], False, done)[0].reshape(N_DEV, n_pack // LANE, LANE)
    g_pack = _reduce_parts("sum_small", packs).reshape(1, n_pack)
    loss = g_pack[0, 0]
    off = 8 * LANE
    n_adam = 9 * d + 4 * d + 4 * lw + 2 * nlb * LANE * LANE
    g_small = g_pack[:, off:off + n_adam].reshape(1, n_adam // LANE, LANE)
    d_cw_sum = g_pack[:, off + n_adam:off + n_adam + CONV_WIDTH * lw].reshape(CONV_WIDTH, lw)
    d_cw_mine = lax.dynamic_slice(d_cw_sum, (0, me * cwb), (CONV_WIDTH, cwb))

    small_names = ["b_ada", "norm_ffn1", "norm_mix", "norm_ffn2", "norm_final", "conv_b", "b_rg_gate", "b_in_gate",
                   "lru_lambda", "w_rg_gate", "w_in_gate"]
    given = dict(b_ada=(b_ada, m_b_ada, v_b_ada), norm_ffn1=(norm_ffn1, m_norm_ffn1, v_norm_ffn1),
                 norm_mix=(norm_mix, m_norm_mix, v_norm_mix), norm_ffn2=(norm_ffn2, m_norm_ffn2, v_norm_ffn2),
                 norm_final=(norm_final, m_norm_final, v_norm_final), conv_b=(conv_b, m_conv_b, v_conv_b),
                 b_rg_gate=(b_rg_gate, m_b_rg_gate, v_b_rg_gate), b_in_gate=(b_in_gate, m_b_in_gate, v_b_in_gate),
                 lru_lambda=(lru_lambda, m_lru_lambda, v_lru_lambda), w_rg_gate=(w_rg_gate, m_w_rg_gate, v_w_rg_gate),
                 w_in_gate=(w_in_gate, m_w_in_gate, v_w_in_gate))
    packed = [jnp.concatenate([given[n][q].reshape(1, -1) for n in small_names], axis=1).reshape(n_adam // LANE, LANE)
              for q in range(3)]
    small_out = _adamw("adamw_small", g_small, *packed)
    pos = 0
    for n in small_names:
        shape = given[n][0].shape
        size = math.prod(shape)
        results[n] = [o.reshape(1, n_adam)[:, pos:pos + size].reshape(shape) for o in small_out]
        pos += size
    results["conv_w"] = [o.reshape(conv_w.shape) for o in
                         _adamw("adamw_conv_w", d_cw_mine[None], conv_w[0], m_conv_w[0], v_conv_w[0])]

    dmod_all = packs.reshape(N_DEV, n_pack)[:, off:off + 9 * d]
    dmod_mine = lax.dynamic_slice(dmod_all, (0, me * cba), (N_DEV, cba))
    dmod_rows = jnp.pad(dmod_mine, ((0, LANE - N_DEV), (0, 0)))
    c_act_t = jnp.pad(c_act.T, ((0, 0), (0, LANE - N_DEV)))
    tm_d2 = _tile(d, 256)
    d_wada = _matmul("dw_ada", c_act_t[None], dmod_rows[None], "nn", tm_d2, cba, LANE,
                     outs=[((1, d, cba), F32, _bspec((1, d, cba), tm_d2, cba, _ij))])[0]
    results["w_ada"] = [o[None] for o in _adamw("adamw_w_ada", d_wada, w_ada[0], m_w_ada[0], v_w_ada[0])]

    update_group("scattered_ffn1", scatter_ffn1, [("w_ffn1_in", w_ffn1_in, m_w_ffn1_in, v_w_ffn1_in),
                                                  ("w_ffn1_out", w_ffn1_out, m_w_ffn1_out, v_w_ffn1_out)], results["w_ada"][0])

    order = ["w_ada", "b_ada", "norm_ffn1", "w_ffn1_in", "w_ffn1_out", "norm_mix", "w_in", "conv_w", "conv_b", "w_rg_gate",
             "b_rg_gate", "w_in_gate", "b_in_gate", "lru_lambda", "w_branch_attn", "w_branch_lru", "w_out", "norm_ffn2",
             "w_ffn2_in", "w_ffn2_out", "norm_final"]
    return (loss, grad_x[None], *[results[n][0] for n in order], *[results[n][1] for n in order],
            *[results[n][2] for n in order], *[results[n][3] for n in order])
```

```python
import functools
import math

import jax
import jax.numpy as jnp
from jax import lax
from jax.experimental import pallas as pl
from jax.experimental.pallas import tpu as pltpu

F32 = jnp.float32
BF16 = jnp.bfloat16
N_DEV = 8
HEAD_DIM = 128
CONV_WIDTH = 4
CONV_HALO = 8
LRU_C = 8.0
EPS = 1e-6
ADAM_LR, ADAM_B1, ADAM_B2, ADAM_EPS, ADAM_WD, ADAM_STEP = 0.001, 0.9, 0.999, 1e-08, 0.01, 10
LANE = 128
VMEM_LIMIT = 56 * 1024 * 1024
MESH = pl.DeviceIdType.MESH

NT = (((1,), (1,)), ((), ()))
NN = (((1,), (0,)), ((), ()))
TN = (((0,), (0,)), ((), ()))


def _tile(dim, target, align=LANE):
    t = (min(target, dim) // align) * align
    while t >= align:
        if dim % t == 0:
            return t
        t -= align
    return dim


def _params(sem):
    return pltpu.CompilerParams(dimension_semantics=sem, vmem_limit_bytes=VMEM_LIMIT)


def _sigmoid(x):
    return 1.0 / (1.0 + jnp.exp(-x))


def _softplus(x):
    return jnp.maximum(x, 0.0) + jnp.log(1.0 + jnp.exp(-jnp.abs(x)))


def _log1p(z):
    w = 1.0 + z
    return jnp.where(w == 1.0, z, jnp.log(w) * z / jnp.where(w == 1.0, 1.0, w - 1.0))


def _expm1(x):
    poly = x * (1.0 + x * (0.5 + x * (1.0 / 6 + x * (1.0 / 24 + x * (1.0 / 120 + x * (1.0 / 720))))))
    return jnp.where(jnp.abs(x) < 0.25, poly, jnp.exp(x) - 1.0)


_GELU_C = math.sqrt(2.0 / math.pi)


def _gelu_and_grad(x):
    inner = _GELU_C * (x + 0.044715 * x * x * x)
    th = jnp.tanh(inner)
    val = 0.5 * x * (1.0 + th)
    grad = 0.5 * (1.0 + th) + 0.5 * x * (1.0 - th * th) * _GELU_C * (1.0 + 3 * 0.044715 * x * x)
    return val, grad


def _dot_split(x, u):
    hi = x.astype(BF16)
    lo = (x - hi.astype(F32)).astype(BF16)
    return jnp.dot(hi, u, preferred_element_type=F32) + jnp.dot(lo, u, preferred_element_type=F32)


def _mesh_position():
    x, y, c = lax.axis_index("x"), lax.axis_index("y"), lax.axis_index("c")
    return x, y, c, 4 * x + 2 * y + c


def _peers(x, y, c):
    out = []
    for mask in range(1, N_DEV):
        px = 1 - x if mask & 4 else x
        py = 1 - y if mask & 2 else y
        pc = 1 - c if mask & 1 else c
        out.append((mask, (px, py, pc), 4 * px + 2 * py + pc))
    return out


def _exchange(name, arrs, scatter, after=None):
    n = len(arrs)
    behind = [] if after is None else [after]

    def body(*refs):
        ins, outs = refs[:n], refs[n + len(behind):2 * n + len(behind)]
        send_sems, recv_sems, local_sems = refs[2 * n + len(behind):]
        x, y, c, me = _mesh_position()
        peers = _peers(x, y, c)
        waits = []
        for a in range(n):
            mine = ins[a].at[me] if scatter else ins[a]
            local = pltpu.make_async_copy(mine, outs[a].at[me], local_sems.at[a])
            local.start()
            waits.append(local.wait)
            for mask, dev, idx in peers:
                k = a * (N_DEV - 1) + mask - 1
                src = ins[a].at[idx] if scatter else ins[a]
                send = pltpu.make_async_remote_copy(src_ref=src, dst_ref=outs[a].at[me], send_sem=send_sems.at[k],
                                                    recv_sem=recv_sems.at[k], device_id=dev, device_id_type=MESH)
                send.start()
                arrival = pltpu.make_async_remote_copy(src_ref=src, dst_ref=outs[a].at[idx], send_sem=send_sems.at[k],
                                                       recv_sem=recv_sems.at[k], device_id=dev, device_id_type=MESH)
                waits.append(send.wait_send)
                waits.append(arrival.wait_recv)
        for w in waits:
            w()

    any_spec = pl.BlockSpec(memory_space=pl.ANY)
    out_shape = [jax.ShapeDtypeStruct(a.shape if scatter else (N_DEV,) + a.shape, a.dtype) for a in arrs]
    return pl.pallas_call(
        body, name=name, out_shape=out_shape, in_specs=[any_spec] * (n + len(behind)), out_specs=[any_spec] * n,
        scratch_shapes=[pltpu.SemaphoreType.DMA((n * (N_DEV - 1),)), pltpu.SemaphoreType.DMA((n * (N_DEV - 1),)),
                        pltpu.SemaphoreType.DMA((n,))],
    )(*arrs, *behind)


HBM_SPEC = pl.BlockSpec(memory_space=pltpu.HBM)
SEM_SPEC = pl.BlockSpec(memory_space=pltpu.SEMAPHORE)
DATAFLOW = pltpu.SideEffectType.DATAFLOW_SIDE_EFFECTING


ALL_MASKS = tuple(range(1, N_DEV))
SAME_CORE_MASKS = (1, 2, 4, 6)


def _exchange_begin(name, arrs, scatter, after=None, once_per_chip=()):
    n = len(arrs)
    lands = [lax.empty(a.shape if scatter else (N_DEV,) + a.shape, a.dtype) for a in arrs]
    behind = [] if after is None else [after]
    masks = [SAME_CORE_MASKS if a in once_per_chip else ALL_MASKS for a in range(n)]

    def body(*refs):
        srcs, zones, outs = refs[:n], refs[n:2 * n], refs[2 * n + len(behind):]
        x, y, c, me = _mesh_position()
        for a in range(n):
            send_sems, recv_sems = outs[4 * a], outs[4 * a + 1]
            for mask, dev, idx in _peers(x, y, c):
                if mask not in masks[a]:
                    continue
                pltpu.make_async_remote_copy(
                    src_ref=srcs[a].at[idx] if scatter else srcs[a], dst_ref=zones[a].at[me], send_sem=send_sems.at[mask - 1],
                    recv_sem=recv_sems.at[mask - 1], device_id=dev, device_id_type=MESH).start()
        outs[-1][...] = jnp.zeros_like(outs[-1])

    out_shape, out_specs, aliases = [], [], {}
    for a in range(n):
        out_shape += [pltpu.SemaphoreType.DMA((N_DEV - 1,)), pltpu.SemaphoreType.DMA((N_DEV - 1,)),
                      pltpu.HBM(arrs[a].shape, arrs[a].dtype), pltpu.HBM(lands[a].shape, lands[a].dtype)]
        out_specs += [SEM_SPEC, SEM_SPEC, HBM_SPEC, HBM_SPEC]
        aliases[a] = 4 * a + 2
        aliases[n + a] = 4 * a + 3
    out_shape.append(jax.ShapeDtypeStruct((8, LANE), F32))
    out_specs.append(pl.BlockSpec(memory_space=pltpu.VMEM))
    res = pl.pallas_call(
        body, name=name, out_shape=out_shape,
        in_specs=[HBM_SPEC] * (2 * n) + [pl.BlockSpec(memory_space=pl.ANY)] * len(behind),
        out_specs=out_specs, input_output_aliases=aliases, compiler_params=pltpu.CompilerParams(has_side_effects=DATAFLOW),
    )(*[pltpu.with_memory_space_constraint(v, pltpu.HBM) for v in list(arrs) + lands], *behind)
    return [tuple(res[4 * a:4 * a + 4]) + (masks[a],) for a in range(n)], res[-1]


def _exchange_end(name, handles, after, scatter):
    n = len(handles)
    me = 4 * lax.axis_index("x") + 2 * lax.axis_index("y") + lax.axis_index("c")

    def body(*refs):
        x, y, c, me = _mesh_position()
        for a in range(n):
            src, zone, send_sems, recv_sems = refs[4 * a:4 * a + 4]
            for mask, dev, idx in _peers(x, y, c):
                if mask not in handles[a][4]:
                    continue
                cp = pltpu.make_async_remote_copy(
                    src_ref=src.at[idx] if scatter else src, dst_ref=zone.at[idx], send_sem=send_sems.at[mask - 1],
                    recv_sem=recv_sems.at[mask - 1], device_id=dev, device_id_type=MESH)
                cp.wait_send()
                cp.wait_recv()

    operands, in_specs, out_shape, aliases = [], [], [], {}
    for a, (send_sems, recv_sems, src, zone, _) in enumerate(handles):
        operands += [src, zone, send_sems, recv_sems]
        in_specs += [HBM_SPEC, HBM_SPEC, SEM_SPEC, SEM_SPEC]
        out_shape += [pltpu.HBM(src.shape, src.dtype), pltpu.HBM(zone.shape, zone.dtype)]
        aliases[4 * a] = 2 * a
        aliases[4 * a + 1] = 2 * a + 1
    res = pl.pallas_call(
        body, name=name, out_shape=out_shape, in_specs=in_specs + [pl.BlockSpec(memory_space=pl.ANY)],
        out_specs=[HBM_SPEC] * (2 * n), input_output_aliases=aliases,
        compiler_params=pltpu.CompilerParams(has_side_effects=DATAFLOW),
    )(*operands, after)
    full = []
    for a in range(n):
        src, zone = res[2 * a], res[2 * a + 1]
        own = lax.dynamic_index_in_dim(src, me, 0, keepdims=False) if scatter else src
        full.append(lax.dynamic_update_index_in_dim(zone, own, me, 0))
    return full


def _sibling_forward(name, zones):
    n = len(zones)
    hops = (2, 4, 6)

    def body(*refs):
        outs, send_sems, recv_sems = refs[n:2 * n], refs[2 * n], refs[2 * n + 1]
        x, y, c, me = _mesh_position()
        sibling = (x, y, 1 - c)
        waits = []
        for a in range(n):
            for q, mask in enumerate(hops):
                chip = 4 * (1 - x if mask & 4 else x) + 2 * (1 - y if mask & 2 else y)
                k = a * len(hops) + q
                held, missing = outs[a].at[chip + c], outs[a].at[chip + 1 - c]
                send = pltpu.make_async_remote_copy(src_ref=held, dst_ref=held, send_sem=send_sems.at[k],
                                                    recv_sem=recv_sems.at[k], device_id=sibling, device_id_type=MESH)
                send.start()
                arrival = pltpu.make_async_remote_copy(src_ref=missing, dst_ref=missing, send_sem=send_sems.at[k],
                                                       recv_sem=recv_sems.at[k], device_id=sibling, device_id_type=MESH)
                waits += [send.wait_send, arrival.wait_recv]
        for w in waits:
            w()

    any_spec = pl.BlockSpec(memory_space=pl.ANY)
    return pl.pallas_call(
        body, name=name, out_shape=[jax.ShapeDtypeStruct(z.shape, z.dtype) for z in zones], in_specs=[any_spec] * n,
        out_specs=[any_spec] * n, input_output_aliases={a: a for a in range(n)},
        scratch_shapes=[pltpu.SemaphoreType.DMA((n * len(hops),)), pltpu.SemaphoreType.DMA((n * len(hops),))],
    )(*zones)


def _bspec(shape, tr, tc, rc, buffers=None):
    per = shape[-1] // tc
    mode = {} if buffers is None else dict(pipeline_mode=pl.Buffered(buffers))
    if len(shape) == 3:
        return pl.BlockSpec((None, tr, tc), lambda j, i, k: (rc(i, j, k)[1] // per, rc(i, j, k)[0], rc(i, j, k)[1] % per), **mode)
    return pl.BlockSpec((shape[0], None, tr, tc),
                        lambda j, i, k: (0, rc(i, j, k)[1] // per, rc(i, j, k)[0], rc(i, j, k)[1] % per), **mode)


def _ij(i, j, k):
    return i, j


def _row_spec(tn, col_tile_offset=0):
    return pl.BlockSpec((1, tn), lambda j, i, k: (0, j + col_tile_offset))


def _matmul(name, a, b, mode, tm, tn, tk, outs, epilogue=None, extras=(), b_buffers=None):
    groups = b.shape[0] if b.ndim == 4 else 1
    if mode == "nn":
        m, k_dim, n = a.shape[1], a.shape[0] * a.shape[2], b.shape[-3] * b.shape[-1]
        a_spec = _bspec(a.shape, tm, tk, lambda i, j, k: (i, k))
        b_spec = _bspec(b.shape, tk, tn, lambda i, j, k: (k, j), b_buffers)
        dims = NN
    elif mode == "nt":
        m, k_dim, n = a.shape[1], a.shape[0] * a.shape[2], b.shape[-2]
        a_spec = _bspec(a.shape, tm, tk, lambda i, j, k: (i, k))
        b_spec = _bspec(b.shape, tn, tk, lambda i, j, k: (j, k), b_buffers)
        dims = NT
    else:
        m, k_dim, n = a.shape[0] * a.shape[2], a.shape[1], b.shape[-3] * b.shape[-1]
        a_spec = _bspec(a.shape, tk, tm, lambda i, j, k: (k, i))
        b_spec = _bspec(b.shape, tk, tn, lambda i, j, k: (k, j), b_buffers)
        dims = TN
    assert m % tm == 0 and n % tn == 0 and k_dim % tk == 0, (name, m, n, k_dim, tm, tn, tk)
    nk = k_dim // tk
    n_extra, n_out = len(extras), len(outs)

    def finish(acc, extra_refs, out_refs):
        if epilogue is None:
            out_refs[0][...] = acc[0].astype(out_refs[0].dtype)
        else:
            epilogue(acc, extra_refs, out_refs)

    def products(a_ref, b_ref):
        a_tile = a_ref[...].astype(BF16)
        return [lax.dot_general(a_tile, (b_ref[g] if b.ndim == 4 else b_ref[...]).astype(BF16), dims,
                                preferred_element_type=F32) for g in range(groups)]

    def body_whole_k(*refs):
        finish(products(refs[0], refs[1]), refs[2:2 + n_extra], refs[2 + n_extra:])

    def body_k_steps(*refs):
        acc_ref = refs[-1]
        k = pl.program_id(2)

        @pl.when(k == 0)
        def _():
            acc_ref[...] = jnp.zeros_like(acc_ref)

        for g, p in enumerate(products(refs[0], refs[1])):
            acc_ref[g] += p

        @pl.when(k == nk - 1)
        def _():
            finish([acc_ref[g] for g in range(groups)], refs[2:2 + n_extra], refs[2 + n_extra:2 + n_extra + n_out])

    return pl.pallas_call(
        body_whole_k if nk == 1 else body_k_steps, name=name, grid=(n // tn, m // tm, nk),
        in_specs=[a_spec, b_spec] + [s for _, s in extras],
        out_specs=[s for _, _, s in outs],
        out_shape=[jax.ShapeDtypeStruct(shape, dtype) for shape, dtype, _ in outs],
        scratch_shapes=[] if nk == 1 else [pltpu.VMEM((groups, tm, tn), F32)],
        compiler_params=_params(("parallel", "parallel", "arbitrary")),
    )(a, b, *[arr for arr, _ in extras])


def _rowwise(name, fn, rows, vecs, outs, accs, tm):
    s = rows[0][0].shape[0]
    n_in, n_out = len(rows) + len(vecs), len(outs)

    def body(*refs):
        i = pl.program_id(0)
        res = fn(*[r[...] for r in refs[:n_in]])
        res = res if isinstance(res, tuple) else (res,)
        out_refs, acc_refs = refs[n_in:n_in + n_out], refs[n_in + n_out:]
        for ref, val in zip(out_refs, res[:n_out]):
            ref[...] = val.astype(ref.dtype)

        @pl.when(i == 0)
        def _():
            for ref in acc_refs:
                ref[...] = jnp.zeros_like(ref)

        for ref, val in zip(acc_refs, res[n_out:]):
            ref[...] += val

    in_specs = [pl.BlockSpec((tm, w), functools.partial(lambda i, cb: (i, cb), cb=cb)) for _, w, cb in rows]
    in_specs += [pl.BlockSpec(v.shape, lambda i: (0,) * v.ndim) for v in vecs]
    out_specs = [pl.BlockSpec((tm, w), lambda i: (i, 0)) for w, _ in outs] + [pl.BlockSpec((1, w), lambda i: (0, 0)) for w in accs]
    out_shape = [jax.ShapeDtypeStruct((s, w), dt) for w, dt in outs] + [jax.ShapeDtypeStruct((1, w), F32) for w in accs]
    return pl.pallas_call(
        body, name=name, grid=(s // tm,), in_specs=in_specs, out_specs=out_specs, out_shape=out_shape,
        compiler_params=_params(("arbitrary",)),
    )(*[r for r, _, _ in rows], *vecs)


def _colsum(v):
    return jnp.sum(v, axis=0, keepdims=True)


def _norm_mod(name, h, nw, sc, sh, tm):
    d = h.shape[1]

    def fn(hb, nwb, scb, shb):
        r = lax.rsqrt(jnp.mean(hb * hb, axis=-1, keepdims=True) + EPS)
        return (hb * r) * nwb * (1.0 + scb) + shb

    return _rowwise(name, fn, [(h, d, 0)], [nw, sc, sh], [(d, BF16)], [], tm)[0]


def _norm_mod_bwd(name, dy, h, dh_next, nw, sc, tm, below=None):
    d = h.shape[1]

    def fn(dyb, hb, dhb, *rest):
        nwb, scb = rest[-2:] if below is None else rest[1:3]
        r = lax.rsqrt(jnp.mean(hb * hb, axis=-1, keepdims=True) + EPS)
        xh = hb * r
        dxh = dyb * (nwb * (1.0 + scb))
        dx = r * (dxh - xh * jnp.mean(dxh * xh, axis=-1, keepdims=True))
        dh = dhb + dx
        sums = (_colsum(dyb), _colsum(dyb * xh * nwb), _colsum(dyb * xh * (1.0 + scb)))
        if below is None:
            return (dh,) + sums
        ob, gb = rest[0], rest[3]
        return (dh, dh * (below[2] * gb)) + sums + (_colsum(dh * ob * below[2]),)

    rows = [(dy, d, 0), (h, d, 0), (dh_next, d, 0)]
    if below is None:
        return _rowwise(name, fn, rows, [nw, sc], [(d, F32)], [d, d, d], tm)
    return _rowwise(name, fn, rows + [(below[0], d, 0)], [nw, sc, below[1]], [(d, F32), (d, BF16)], [d, d, d, d], tm)


def _loss_bwd(name, h, target, nw, o, g, tm):
    d = h.shape[1]

    def fn(hb, tb, ob, nwb, gb):
        r = lax.rsqrt(jnp.mean(hb * hb, axis=-1, keepdims=True) + EPS)
        xh = hb * r
        err = xh * nwb - tb
        dy = err * (1.0 / d)
        dxh = dy * nwb
        dx = r * (dxh - xh * jnp.mean(dxh * xh, axis=-1, keepdims=True))
        loss = 0.5 * jnp.sum(jnp.mean(err * err, axis=-1, keepdims=True), axis=0, keepdims=True)
        return dx, dx * (0.5 * gb), jnp.broadcast_to(loss, (1, LANE)), _colsum(dy * xh), _colsum(dx * ob * 0.5)

    return _rowwise(name, fn, [(h, d, 0), (target, d, 0), (o, d, 0)], [nw, g], [(d, F32), (d, BF16)], [LANE, d, d], tm)


def _head_group(nh, most):
    return max(g for g in (1, 2, 4) if g <= most and nh % g == 0)


def _attn_fwd(qkv, nh, t):
    s = qkv.shape[0]
    scale = HEAD_DIM ** -0.5
    hp = _head_group(nh, 4)
    wide = hp * HEAD_DIM
    lanes = [slice(u * HEAD_DIM, (u + 1) * HEAD_DIM) for u in range(hp)]

    def body(q_ref, k_ref, v_ref, y_ref, tot_ref):
        i = pl.program_id(1)
        row = lax.broadcasted_iota(jnp.int32, (t, t), 0)
        col = lax.broadcasted_iota(jnp.int32, (t, t), 1)
        later = (row > col).astype(BF16)
        causal = col < row
        qs = [q_ref[:, ln] for ln in lanes]

        def block(j, carry, diagonal):
            ks = pl.ds(pl.multiple_of(j * t, t), t)
            heads = range(hp)
            z = [lax.dot_general(qs[u], k_ref[ks, lanes[u]], NT, preferred_element_type=F32) * scale for u in heads]
            sp = [_softplus(z[u]) for u in heads]
            log_keep = [jnp.where(causal, -sp[u], 0.0) if diagonal else -sp[u] for u in heads]
            between = [_dot_split(log_keep[u], later) for u in heads]
            w = [jnp.exp(z[u] - sp[u] + between[u] + carry[u][1]) for u in heads]
            if diagonal:
                w = [jnp.where(causal, w[u], 0.0) for u in heads]
            o = [carry[u][0] + jnp.dot(w[u].astype(BF16), v_ref[ks, lanes[u]], preferred_element_type=F32) for u in heads]
            return tuple((o[u], carry[u][1] + jnp.sum(log_keep[u], axis=1, keepdims=True)) for u in heads)

        carry = tuple((jnp.zeros((t, HEAD_DIM), F32), jnp.zeros((t, 1), F32)) for _ in lanes)
        carry = block(i, carry, True)
        carry = lax.fori_loop(0, i, lambda jj, cr: block(i - 1 - jj, cr, False), carry)
        for u, ln in enumerate(lanes):
            y_ref[:, ln] = carry[u][0].astype(y_ref.dtype)
            tot_ref[:, ln] = jnp.broadcast_to(carry[u][1], (t, HEAD_DIM))

    g = nh // hp
    return pl.pallas_call(
        body, name="attn_fwd", grid=(g, s // t),
        in_specs=[pl.BlockSpec((t, wide), lambda h, i: (i, h)),
                  pl.BlockSpec((s, wide), lambda h, i: (0, g + h)),
                  pl.BlockSpec((s, wide), lambda h, i: (0, 2 * g + h))],
        out_specs=[pl.BlockSpec((t, wide), lambda h, i: (i, h)), pl.BlockSpec((t, wide), lambda h, i: (i, h))],
        out_shape=[jax.ShapeDtypeStruct((s, nh * HEAD_DIM), BF16), jax.ShapeDtypeStruct((s, nh * HEAD_DIM), F32)],
        compiler_params=_params(("parallel", "arbitrary")),
    )(qkv, qkv, qkv)


def _attn_bwd(qkv, dy, tot, nh, t):
    s = qkv.shape[0]
    scale = HEAD_DIM ** -0.5
    hp = _head_group(nh, 4)
    wide = hp * HEAD_DIM
    lanes = [slice(u * HEAD_DIM, (u + 1) * HEAD_DIM) for u in range(hp)]

    def body(q_ref, k_ref, v_ref, dy_ref, tot_ref, dq_ref, dk_out, dv_out, dk_ref, dv_ref):
        i = pl.program_id(1)

        @pl.when(i == 0)
        def _():
            dk_ref[...] = jnp.zeros_like(dk_ref)
            dv_ref[...] = jnp.zeros_like(dv_ref)

        row = lax.broadcasted_iota(jnp.int32, (t, t), 0)
        col = lax.broadcasted_iota(jnp.int32, (t, t), 1)
        upto = (row <= col).astype(BF16)
        before = (row < col).astype(BF16)
        causal = col < row
        qs = [q_ref[:, ln] for ln in lanes]
        dys = [dy_ref[:, ln] for ln in lanes]
        totals = [tot_ref[:, u * HEAD_DIM:u * HEAD_DIM + 1] for u in range(hp)]

        def block(j, carry, diagonal):
            ks = pl.ds(pl.multiple_of(j * t, t), t)
            heads = range(hp)
            kb = [k_ref[ks, ln] for ln in lanes]
            vb = [v_ref[ks, ln] for ln in lanes]
            z = [lax.dot_general(qs[u], kb[u], NT, preferred_element_type=F32) * scale for u in heads]
            dw = [lax.dot_general(dys[u], vb[u], NT, preferred_element_type=F32) for u in heads]
            sp = [_softplus(z[u]) for u in heads]
            log_keep = [jnp.where(causal, -sp[u], 0.0) if diagonal else -sp[u] for u in heads]
            upto_sum = [_dot_split(log_keep[u], upto) for u in heads]
            w = [jnp.exp(z[u] - sp[u] + (totals[u] - carry[u][1] - upto_sum[u])) for u in heads]
            if diagonal:
                w = [jnp.where(causal, w[u], 0.0) for u in heads]
            g = [dw[u] * w[u] for u in heads]
            g_before = [_dot_split(g[u], before) for u in heads]
            dz = [(g[u] * jnp.exp(-sp[u]) - jnp.exp(z[u] - sp[u]) * (carry[u][2] + g_before[u])) * scale for u in heads]
            if diagonal:
                dz = [jnp.where(causal, dz[u], 0.0) for u in heads]
            dzb = [dz[u].astype(BF16) for u in heads]
            dq = [carry[u][0] + jnp.dot(dzb[u], kb[u], preferred_element_type=F32) for u in heads]
            for u in heads:
                dk_ref[ks, lanes[u]] += lax.dot_general(dzb[u], qs[u], TN, preferred_element_type=F32)
            for u in heads:
                dv_ref[ks, lanes[u]] += lax.dot_general(w[u].astype(BF16), dys[u], TN, preferred_element_type=F32)
            return tuple((dq[u], carry[u][1] + jnp.sum(log_keep[u], axis=1, keepdims=True),
                          carry[u][2] + jnp.sum(g[u], axis=1, keepdims=True)) for u in heads)

        zero = jnp.zeros((t, 1), F32)
        carry = tuple((jnp.zeros((t, HEAD_DIM), F32), zero, zero) for _ in lanes)
        carry = lax.fori_loop(0, i, lambda j, cr: block(j, cr, False), carry)
        carry = block(i, carry, True)
        for u, ln in enumerate(lanes):
            dq_ref[:, ln] = carry[u][0].astype(dq_ref.dtype)

        @pl.when(i == pl.num_programs(1) - 1)
        def _():
            dk_out[...] = dk_ref[...].astype(dk_out.dtype)
            dv_out[...] = dv_ref[...].astype(dv_out.dtype)

    g = nh // hp
    tile = lambda off: pl.BlockSpec((t, wide), lambda h, i: (i, off + h))
    head = lambda off, **mode: pl.BlockSpec((s, wide), lambda h, i: (0, off + h), **mode)
    once = dict(pipeline_mode=pl.Buffered(1))
    return pl.pallas_call(
        body, name="attn_bwd", grid=(g, s // t),
        in_specs=[tile(0), head(g, **once), head(2 * g, **once), tile(0), tile(0)],
        out_specs=[tile(0), head(0), head(0)],
        out_shape=[jax.ShapeDtypeStruct((s, nh * HEAD_DIM), BF16)] * 3,
        scratch_shapes=[pltpu.VMEM((s, wide), F32), pltpu.VMEM((s, wide), F32)],
        compiler_params=_params(("parallel", "arbitrary")),
    )(qkv, qkv, qkv, dy, tot)


def _lru_gates(xc, w_r, b_r, w_i, b_i, lam):
    xb = xc.astype(BF16)
    r = _sigmoid(jnp.dot(xb, w_r.astype(BF16), preferred_element_type=F32) + b_r)
    i = _sigmoid(jnp.dot(xb, w_i.astype(BF16), preferred_element_type=F32) + b_i)
    neg_lam = -lam
    sp_lam = jnp.maximum(neg_lam, 0.0) + _log1p(jnp.exp(-jnp.abs(neg_lam)))
    log_a = -LRU_C * r * sp_lam
    a = jnp.exp(log_a)
    mult = jnp.sqrt(-_expm1(2.0 * log_a))
    return r, i, sp_lam, a, mult


def _conv_taps(xpad_chunk, conv_w, t):
    shifted = [xpad_chunk[CONV_HALO:, :]]
    for d in range(1, CONV_WIDTH):
        shifted.append(pltpu.roll(xpad_chunk, d, 0)[CONV_HALO:, :])
    weights = [conv_w[CONV_WIDTH - 1 - d:CONV_WIDTH - d, :] for d in range(CONV_WIDTH)]
    return shifted, weights


def _lru_fwd(xr_pad, proj, gr_block0, conv_w, conv_b, w_r, b_r, w_i, b_i, lam, t):
    s, w = xr_pad.shape[0] - CONV_HALO, xr_pad.shape[1]
    nblk = w // LANE
    nchunk = s // t
    steps = [1 << p for p in range(t.bit_length() - 1)]
    assert (1 << (t.bit_length() - 1)) == t and w_r.shape[1:] == (LANE, LANE)

    def body(x_ref, gr_ref, cw_ref, cb_ref, wr_ref, br_ref, wi_ref, bi_ref, lam_ref, h_ref, hp_ref, xc_ref, y_ref):
        row = lax.broadcasted_iota(jnp.int32, (t, LANE), 0)

        def chunk(ci, h_in):
            t0 = pl.multiple_of(ci * t, t)
            shifted, weights = _conv_taps(x_ref[pl.ds(t0, t + CONV_HALO), :], cw_ref[...], t)
            xc = cb_ref[...] + sum(wd * xs for wd, xs in zip(weights, shifted))
            r, i, _, a, mult = _lru_gates(xc, wr_ref[...], br_ref[...], wi_ref[...], bi_ref[...], lam_ref[...])
            coef, val = a, mult * (i * xc)
            for d in steps:
                ok = row >= d
                val = jnp.where(ok, coef * pltpu.roll(val, d, 0) + val, val)
                coef = jnp.where(ok, coef * pltpu.roll(coef, d, 0), coef)
            h = val + coef * h_in
            rows = pl.ds(t0, t)
            h_ref[rows, :] = h
            hp_ref[rows, :] = jnp.where(row == 0, h_in, pltpu.roll(h, 1, 0))
            xc_ref[rows, :] = xc
            y_ref[rows, :] = (h * _gelu_and_grad(gr_ref[rows, :])[0]).astype(y_ref.dtype)
            return h[t - 1:t, :]

        lax.fori_loop(0, nchunk, chunk, jnp.zeros((1, LANE), F32))

    col = lambda rows: pl.BlockSpec((rows, LANE), lambda n: (0, n))
    return pl.pallas_call(
        body, name="lru_fwd", grid=(nblk,),
        in_specs=[col(s + CONV_HALO), pl.BlockSpec((s, LANE), lambda n: (0, gr_block0 + n)), col(CONV_WIDTH), col(1),
                  pl.BlockSpec((None, LANE, LANE), lambda n: (n, 0, 0)), col(1),
                  pl.BlockSpec((None, LANE, LANE), lambda n: (n, 0, 0)), col(1), col(1)],
        out_specs=[col(s)] * 4,
        out_shape=[jax.ShapeDtypeStruct((s, w), F32)] * 3 + [jax.ShapeDtypeStruct((s, w), BF16)],
        compiler_params=_params(("parallel",)),
    )(xr_pad, proj, conv_w, conv_b, w_r, b_r, w_i, b_i, lam)


def _lru_bwd(dy, proj, gr_block0, h, h_prev, xc, w_r, b_r, w_i, b_i, lam, t):
    s, w = dy.shape
    nblk = w // LANE
    nchunk = s // t
    steps = [1 << p for p in range(t.bit_length() - 1)]

    def body(dy_ref, gr_ref, h_ref, hp_ref, xc_ref, wr_ref, br_ref, wi_ref, bi_ref, lam_ref,
             dgr_ref, dxc_ref, dwr_ref, dwi_ref, dbr_ref, dbi_ref, dlam_ref):
        row = lax.broadcasted_iota(jnp.int32, (t, LANE), 0)
        for ref in (dwr_ref, dwi_ref, dbr_ref, dbi_ref, dlam_ref):
            ref[...] = jnp.zeros_like(ref)

        def chunk(cc, carry):
            lam_next, a_next = carry
            rows = pl.ds(pl.multiple_of((nchunk - 1 - cc) * t, t), t)
            dyb, hb, xcb = dy_ref[rows, :], h_ref[rows, :], xc_ref[rows, :]
            gel, dgel = _gelu_and_grad(gr_ref[rows, :])
            dgr_ref[rows, :] = dyb * hb * dgel
            w_r, w_i = wr_ref[...], wi_ref[...]
            r, i, sp_lam, a, mult = _lru_gates(xcb, w_r, br_ref[...], w_i, bi_ref[...], lam_ref[...])
            coef = jnp.where(row == t - 1, a_next, pltpu.roll(a, t - 1, 0))
            val = dyb * gel
            for d in steps:
                ok = row < t - d
                val = jnp.where(ok, coef * pltpu.roll(val, t - d, 0) + val, val)
                coef = jnp.where(ok, coef * pltpu.roll(coef, t - d, 0), coef)
            adj = val + coef * lam_next
            da = adj * hp_ref[rows, :]
            v = i * xcb
            dmult, dv = adj * v, adj * mult
            dlog_a = da * a - (a * a) * dmult / mult
            dr_pre = (-LRU_C * sp_lam) * dlog_a * r * (1.0 - r)
            di_pre = dv * xcb * i * (1.0 - i)
            dlam_ref[...] += _colsum(-LRU_C * r * dlog_a)
            dbr_ref[...] += _colsum(dr_pre)
            dbi_ref[...] += _colsum(di_pre)
            xb, drb, dib = xcb.astype(BF16), dr_pre.astype(BF16), di_pre.astype(BF16)
            dwr_ref[...] += lax.dot_general(xb, drb, TN, preferred_element_type=F32)
            dwi_ref[...] += lax.dot_general(xb, dib, TN, preferred_element_type=F32)
            dxc_ref[rows, :] = (dv * i + lax.dot_general(drb, w_r.astype(BF16), NT, preferred_element_type=F32)
                                + lax.dot_general(dib, w_i.astype(BF16), NT, preferred_element_type=F32))
            return adj[0:1, :], a[0:1, :]

        lax.fori_loop(0, nchunk, chunk, (jnp.zeros((1, LANE), F32), jnp.zeros((1, LANE), F32)))
        dlam_ref[...] = dlam_ref[...] * (-_sigmoid(-lam_ref[...]))

    col = lambda rows: pl.BlockSpec((rows, LANE), lambda n: (0, n))
    mat = pl.BlockSpec((None, LANE, LANE), lambda n: (n, 0, 0))
    return pl.pallas_call(
        body, name="lru_bwd", grid=(nblk,),
        in_specs=[col(s), pl.BlockSpec((s, LANE), lambda n: (0, gr_block0 + n)), col(s), col(s), col(s),
                  mat, col(1), mat, col(1), col(1)],
        out_specs=[col(s), col(s), mat, mat, col(1), col(1), col(1)],
        out_shape=[jax.ShapeDtypeStruct((s, w), F32)] * 2 + [jax.ShapeDtypeStruct((nblk, LANE, LANE), F32)] * 2
        + [jax.ShapeDtypeStruct((1, w), F32)] * 3,
        compiler_params=_params(("parallel",)),
    )(dy, proj, h, h_prev, xc, w_r, b_r, w_i, b_i, lam)


def _conv_bwd(xr_pad, dxc_pad, conv_w, t):
    s, w = xr_pad.shape[0] - CONV_HALO, xr_pad.shape[1]
    nchunk = s // t

    def body(x_ref, g_ref, cw_ref, dx_ref, dcw_ref, dcb_ref):
        dcw_ref[...] = jnp.zeros_like(dcw_ref)
        dcb_ref[...] = jnp.zeros_like(dcb_ref)

        def chunk(ci, _):
            t0 = pl.multiple_of(ci * t, t)
            shifted, weights = _conv_taps(x_ref[pl.ds(t0, t + CONV_HALO), :], cw_ref[...], t)
            gpad = g_ref[pl.ds(t0, t + CONV_HALO), :]
            g = gpad[:t, :]
            dx = weights[0] * g
            for d in range(1, CONV_WIDTH):
                dx = dx + weights[d] * pltpu.roll(gpad, t + CONV_HALO - d, 0)[:t, :]
            dx_ref[pl.ds(t0, t), :] = dx
            for d in range(CONV_WIDTH):
                dcw_ref[CONV_WIDTH - 1 - d:CONV_WIDTH - d, :] += _colsum(g * shifted[d])
            dcb_ref[...] += _colsum(g)
            return 0

        lax.fori_loop(0, nchunk, chunk, 0)

    col = lambda rows: pl.BlockSpec((rows, LANE), lambda n: (0, n))
    return pl.pallas_call(
        body, name="conv_bwd", grid=(w // LANE,),
        in_specs=[col(s + CONV_HALO), col(s + CONV_HALO), col(CONV_WIDTH)],
        out_specs=[col(s), col(CONV_WIDTH), col(1)],
        out_shape=[jax.ShapeDtypeStruct((s, w), F32), jax.ShapeDtypeStruct((CONV_WIDTH, w), F32),
                   jax.ShapeDtypeStruct((1, w), F32)],
        compiler_params=_params(("parallel",)),
    )(xr_pad, dxc_pad, conv_w)


def _sum_parts(parts_ref):
    g = parts_ref[0].astype(F32)
    for p in range(1, parts_ref.shape[0]):
        g = g + parts_ref[p].astype(F32)
    return g


def _reduce_parts(name, parts):
    p, r, c = parts.shape
    tr = _tile(r, max(8, (1 << 19) // c), 8)

    def body(parts_ref, g_ref):
        g_ref[...] = _sum_parts(parts_ref)

    return pl.pallas_call(
        body, name=name, grid=(r // tr,), in_specs=[pl.BlockSpec((p, tr, c), lambda i: (0, i, 0))],
        out_specs=pl.BlockSpec((tr, c), lambda i: (i, 0)), out_shape=jax.ShapeDtypeStruct((r, c), F32),
        compiler_params=_params(("parallel",)),
    )(parts)


def _adamw(name, parts, w, m, v):
    p, r, c = parts.shape
    tr = _tile(r, max(8, (1 << 18) // c), 8)

    def body(parts_ref, w_ref, m_ref, v_ref, g_ref, d_ref, nm_ref, nv_ref):
        g = _sum_parts(parts_ref)
        nm = ADAM_B1 * m_ref[...] + (1.0 - ADAM_B1) * g
        nv = ADAM_B2 * v_ref[...] + (1.0 - ADAM_B2) * (g * g)
        m_hat = nm / (1.0 - ADAM_B1 ** ADAM_STEP)
        v_hat = nv / (1.0 - ADAM_B2 ** ADAM_STEP)
        g_ref[...] = g
        d_ref[...] = -ADAM_LR * (m_hat / (jnp.sqrt(v_hat) + ADAM_EPS) + ADAM_WD * w_ref[...])
        nm_ref[...] = nm
        nv_ref[...] = nv

    blk = pl.BlockSpec((tr, c), lambda i: (i, 0))
    return pl.pallas_call(
        body, name=name, grid=(r // tr,), in_specs=[pl.BlockSpec((p, tr, c), lambda i: (0, i, 0)), blk, blk, blk],
        out_specs=[blk] * 4, out_shape=[jax.ShapeDtypeStruct((r, c), F32)] * 4,
        compiler_params=_params(("parallel",)),
    )(parts, w, m, v)


def _ffn_in(tag, y, w_in_g, tm):
    s, d = y.shape
    half = N_DEV // 2
    cb = w_in_g.shape[2]
    ff = half * cb

    def swiglu(acc, extra_refs, out_refs):
        g, u = acc
        out_refs[0][0] = g.astype(BF16)
        out_refs[0][1] = u.astype(BF16)
        out_refs[1][...] = (g * _sigmoid(g) * u).astype(BF16)

    gu_shape = (2, 1, s, ff)
    gu, act = _matmul(
        tag + "_in", y[None], w_in_g.reshape(2, half, d, cb), "nn", tm, cb, d,
        outs=[(gu_shape, BF16, _bspec(gu_shape, tm, cb, _ij)), ((1, s, ff), BF16, _bspec((1, s, ff), tm, cb, _ij))],
        epilogue=swiglu, b_buffers=1)
    return gu, act


def _ffn_out(tag, act, w_out_g, res, gate, tm):
    _, s, ff = act.shape
    d = res.shape[1]
    tn = _tile(d, 1024)

    def residual(acc, extra_refs, out_refs):
        out_refs[0][...] = acc[0]
        out_refs[1][...] = extra_refs[0][...] + 0.5 * extra_refs[1][...] * acc[0]

    plain = _bspec((1, s, d), tm, tn, _ij)
    o, h_new = _matmul(
        tag + "_out", act, w_out_g.reshape(1, ff, d), "nn", tm, tn, ff,
        outs=[((1, s, d), F32, plain), ((1, s, d), F32, plain)], epilogue=residual,
        extras=[(res[None], plain), (gate, _row_spec(tn))], b_buffers=1)
    return o[0], h_new[0]


def _after_token(token):
    return token, pl.BlockSpec(token.shape, lambda j, i, k: (0, 0))


def _ffn_bwd_weights(tag, do, y, gu, act, w_in_g, w_out_g, tm):
    s, d = do.shape
    half = N_DEV // 2
    cb = w_in_g.shape[2]
    ff = half * cb

    tn_d, tm_f = _tile(d, 1024), _tile(ff, 512)
    dw_out = _matmul(tag + "_dw_out", act, do[None], "tn", tm_f, tn_d, s,
                     outs=[((1, ff, d), BF16, _bspec((1, ff, d), tm_f, tn_d, _ij))], b_buffers=1)[0]
    dw_out = dw_out.reshape(N_DEV, ff // N_DEV, d)
    out_handles, token = _exchange_begin(tag + "_scatter_out", [dw_out], True)

    def dswiglu(acc, extra_refs, out_refs):
        dact = acc[0]
        g, u = extra_refs[0][0].astype(F32), extra_refs[0][1].astype(F32)
        sg = _sigmoid(g)
        out_refs[0][0] = (dact * u * sg * (1.0 + g * (1.0 - sg))).astype(BF16)
        out_refs[0][1] = (dact * g * sg).astype(BF16)

    gu_shape = (2, 1, s, ff)
    gu_spec = _bspec(gu_shape, tm, cb, _ij)
    dgu = _matmul(tag + "_dact", do[None], w_out_g.reshape(1, ff, d), "nt", tm, cb, d,
                  outs=[(gu_shape, BF16, gu_spec)], epilogue=dswiglu, extras=[(gu, gu_spec), _after_token(token)],
                  b_buffers=1)[0]
    dgu = dgu.reshape(2, s, ff)

    tm_d = _tile(d, 512)
    dw_in = _matmul(tag + "_dw_in", y[None], dgu, "tn", tm_d, cb, s,
                    outs=[((N_DEV, d, cb), BF16, _bspec((N_DEV, d, cb), tm_d, cb, _ij))], b_buffers=1)[0]
    in_handles, token = _exchange_begin(tag + "_scatter_in", [dw_in], True)
    return dgu, in_handles + out_handles, token


def _ffn_bwd_input(tag, dgu, w_in_g, token):
    s, d = dgu.shape[1], w_in_g.shape[1]
    tm_big = _tile(s, 1024, 8)
    return _matmul(tag + "_dy", dgu, w_in_g, "nt", tm_big, d, w_in_g.shape[2],
                   outs=[((1, s, d), F32, _bspec((1, s, d), tm_big, d, _ij))], extras=[_after_token(token)])[0][0]


def kernel(x, c, w_ada, b_ada, norm_ffn1, w_ffn1_in, w_ffn1_out, norm_mix, w_in, conv_w, conv_b, w_rg_gate, b_rg_gate, w_in_gate, b_in_gate, lru_lambda, w_branch_attn, w_branch_lru, w_out, norm_ffn2, w_ffn2_in, w_ffn2_out, norm_final, loss_target, m_w_ada, m_b_ada, m_norm_ffn1, m_w_ffn1_in, m_w_ffn1_out, m_norm_mix, m_w_in, m_conv_w, m_conv_b, m_w_rg_gate, m_b_rg_gate, m_w_in_gate, m_b_in_gate, m_lru_lambda, m_w_branch_attn, m_w_branch_lru, m_w_out, m_norm_ffn2, m_w_ffn2_in, m_w_ffn2_out, m_norm_final, v_w_ada, v_b_ada, v_norm_ffn1, v_w_ffn1_in, v_w_ffn1_out, v_norm_mix, v_w_in, v_conv_w, v_conv_b, v_w_rg_gate, v_b_rg_gate, v_w_in_gate, v_b_in_gate, v_lru_lambda, v_w_branch_attn, v_w_branch_lru, v_w_out, v_norm_ffn2, v_w_ffn2_in, v_w_ffn2_out, v_norm_final):
    xs, target = x[0], loss_target[0]
    s, d = xs.shape
    aw, lw = w_branch_attn.shape[1], w_branch_lru.shape[1]
    nh, nlb = aw // HEAD_DIM, w_rg_gate.shape[1]
    cba, cbi, cbb, cwb = w_ada.shape[2], w_in.shape[2], w_branch_attn.shape[2], conv_w.shape[2]
    assert lw == nlb * LANE and cwb * N_DEV == lw and 3 * aw + 2 * lw + 2 * d == cbi * N_DEV
    me = 4 * lax.axis_index("x") + 2 * lax.axis_index("y") + lax.axis_index("c")
    tm = _tile(s, 512, 8)
    tr = _tile(s, 256, 8)
    t_attn = _tile(s, 256, 8)
    t_lru = _tile(s, 256, 8)

    small = _exchange("gather_c", [jnp.concatenate([c, conv_w.reshape(1, CONV_WIDTH * cwb)], axis=1)], False)[0][:, 0, :]
    c_all = small[:, :d]
    conv_w_full = small[:, d:].reshape(N_DEV, CONV_WIDTH, cwb).transpose(1, 0, 2).reshape(CONV_WIDTH, lw)
    c_act = _rowwise("silu_c", lambda v: v * _sigmoid(v), [(c_all, d, 0)], [], [(d, F32)], [], N_DEV)[0]

    def add_bias(acc_ref, extra_refs, out_refs):
        out_refs[0][...] = acc_ref[0] + extra_refs[0][...]

    b_ada_mine = lax.dynamic_slice(b_ada, (0, me * cba), (1, cba))
    mod_part = _matmul("mod", c_act[None], w_ada, "nn", N_DEV, cba, _tile(d, 512),
                       outs=[((1, N_DEV, cba), F32, _bspec((1, N_DEV, cba), N_DEV, cba, _ij))], epilogue=add_bias,
                       extras=[(b_ada_mine, _row_spec(cba))])[0][0]
    mod_all = _exchange("gather_mod", [mod_part], False)[0]
    mod = lax.dynamic_index_in_dim(mod_all, me, axis=1, keepdims=False).reshape(1, 9 * d)
    sh1, sc1, g1, sh2, sc2, g2, sh3, sc3, g3 = [mod[:, n * d:(n + 1) * d] for n in range(9)]

    shards = [w_ffn1_in[0], w_ffn1_out[0], w_in[0], w_branch_attn[0], w_branch_lru[0], w_out[0], w_ffn2_in[0], w_ffn2_out[0]]
    early = (0, 1, 2)
    gathers, token = _exchange_begin("gather_w", [w.astype(BF16) for w in shards], False, mod, early)

    def gathered(n, after):
        full = _exchange_end("gathered_w%d" % n, [gathers[n]], after, False)
        return (_sibling_forward("forwarded_w%d" % n, full) if n in early else full)[0]

    y1 = _norm_mod("norm1", xs, norm_ffn1 + token[:1, :1], sc1, sh1, tr)
    wf1i = gathered(0, y1)
    gu1, act1 = _ffn_in("ffn1", y1, wf1i, tm)
    wf1o = gathered(1, act1)
    o1, h1 = _ffn_out("ffn1", act1, wf1o, xs, g1, tm)

    y2 = _norm_mod("norm2", h1, norm_mix, sc2, sh2, tr)
    wi_g = gathered(2, y2)
    tn_i = _tile(cbi, 1152)
    proj = _matmul("mix_in", y2[None], wi_g, "nn", tm, tn_i, d,
                   outs=[((1, s, N_DEV * cbi), F32, _bspec((1, s, N_DEV * cbi), tm, tn_i, _ij))], b_buffers=1)[0][0]
    off_xr, off_gr, off_ga, off_gl = 3 * aw, 3 * aw + lw, 3 * aw + 2 * lw, 3 * aw + 2 * lw + d
    qkv = proj[:, :3 * aw].astype(BF16)
    y_attn, attn_tot = _attn_fwd(qkv, nh, t_attn)
    xr_pad = jnp.pad(proj[:, off_xr:off_xr + lw], ((CONV_HALO, 0), (0, 0)))
    w_r, w_i = w_rg_gate[0], w_in_gate[0]
    h_lru, h_prev, xc, y_lru = _lru_fwd(xr_pad, proj, off_gr // LANE, conv_w_full, conv_b, w_r, b_rg_gate, w_i,
                                        b_in_gate, lru_lambda, t_lru)
    wba_p = gathered(3, y_attn).transpose(1, 0, 2).reshape(1, aw, d)
    wbl_p = gathered(4, y_lru).transpose(1, 0, 2).reshape(1, lw, d)
    proj3 = proj[None]
    tm_b = _tile(s, 1024, 8)
    tn_m = _tile(math.gcd(d, off_ga, off_gl), 1024)
    plain_m = _bspec((1, s, d), tm_b, tn_m, _ij)
    gate_specs = [_bspec(proj3.shape, tm_b, tn_m, functools.partial(lambda i, j, k, o: (i, j + o), o=o // tn_m))
                  for o in (off_ga, off_gl)]
    ya = _matmul("branch_attn", y_attn[None], wba_p, "nn", tm_b, tn_m, aw, outs=[((1, s, d), F32, plain_m)], b_buffers=1)[0]

    def merge(acc_ref, extra_refs, out_refs):
        yl = acc_ref[0]
        ya_t, ga, gl = extra_refs[0][...], extra_refs[1][...], extra_refs[2][...]
        out_refs[0][...] = yl
        out_refs[1][...] = (_sigmoid(ga) * ya_t + _sigmoid(gl) * yl).astype(BF16)

    yl, merged = _matmul("branch_lru", y_lru[None], wbl_p, "nn", tm_b, tn_m, lw,
                         outs=[((1, s, d), F32, plain_m), ((1, s, d), BF16, plain_m)], epilogue=merge,
                         extras=[(ya, plain_m), (proj3, gate_specs[0]), (proj3, gate_specs[1])], b_buffers=1)
    tn_d = _tile(d, 1024)
    plain = _bspec((1, s, d), tm, tn_d, _ij)

    def residual(acc_ref, extra_refs, out_refs):
        o = acc_ref[0]
        out_refs[0][...] = o
        out_refs[1][...] = extra_refs[0][...] + extra_refs[1][...] * o

    wo_g = gathered(5, merged)
    mo, h2 = _matmul("mix_out", merged, wo_g.reshape(1, d, d), "nn", tm, tn_d, d,
                     outs=[((1, s, d), F32, plain), ((1, s, d), F32, plain)], epilogue=residual,
                     extras=[(h1[None], plain), (g2, _row_spec(tn_d))], b_buffers=1)
    mo, h2 = mo[0], h2[0]

    y3 = _norm_mod("norm3", h2, norm_ffn2, sc3, sh3, tr)
    wf2i = gathered(6, y3)
    gu3, act3 = _ffn_in("ffn2", y3, wf2i, tm)
    wf2o = gathered(7, act3)
    o3, h3 = _ffn_out("ffn2", act3, wf2o, h2, g3, tm)

    nf = norm_final.reshape(1, d)
    dh3, do3, loss_part, d_nf, dg3 = _loss_bwd("loss", h3, target, nf, o3, g3, tr)
    dgu3, scatter_ffn2, token = _ffn_bwd_weights("ffn2", do3, y3, gu3, act3, wf2i, wf2o, tm)
    dy3 = _ffn_bwd_input("ffn2", dgu3, wf2i, token)
    dh2, dmo, dsh3, dsc3, dn3, dg2 = _norm_mod_bwd("norm3_bwd", dy3, h2, dh3, norm_ffn2, sc3, tr, below=(mo, g2, 1.0))

    dwo = _matmul("mix_dw_out", merged, dmo[None], "tn", _tile(d, 512), tn_d, s,
                  outs=[((1, d, d), BF16, _bspec((1, d, d), _tile(d, 512), tn_d, _ij))], b_buffers=1)[0]

    def dmerge(acc_ref, extra_refs, out_refs):
        dm = acc_ref[0]
        ya_t, yl_t = extra_refs[0][...], extra_refs[1][...]
        sa, sl = _sigmoid(extra_refs[2][...]), _sigmoid(extra_refs[3][...])
        out_refs[0][...] = (dm * sa).astype(BF16)
        out_refs[1][...] = (dm * sl).astype(BF16)
        out_refs[2][...] = (dm * ya_t * sa * (1.0 - sa)).astype(BF16)
        out_refs[3][...] = (dm * yl_t * sl * (1.0 - sl)).astype(BF16)

    tn_m = _tile(math.gcd(d, off_ga, off_gl), 1024)
    plain_m = _bspec((1, s, d), tm, tn_m, _ij)
    gate_specs = [_bspec(proj3.shape, tm, tn_m, functools.partial(lambda i, j, k, o: (i, j + o), o=o // tn_m))
                  for o in (off_ga, off_gl)]
    dya, dyl, dga, dgl = _matmul("mix_dmerged", dmo[None], wo_g.reshape(1, d, d), "nt", tm, tn_m, d,
                                 outs=[((1, s, d), BF16, plain_m)] * 4, epilogue=dmerge,
                                 extras=[(ya, plain_m), (yl, plain_m), (proj3, gate_specs[0]), (proj3, gate_specs[1])],
                                 b_buffers=1)
    tm_a, tm_l = _tile(aw, 1024), _tile(lw, 1024)
    dwba = _matmul("dw_branch_attn", y_attn[None], dya, "tn", tm_a, cbb, s,
                   outs=[((N_DEV, aw, cbb), BF16, _bspec((N_DEV, aw, cbb), tm_a, cbb, _ij))])[0]
    dwbl = _matmul("dw_branch_lru", y_lru[None], dyl, "tn", tm_l, cbb, s,
                   outs=[((N_DEV, lw, cbb), BF16, _bspec((N_DEV, lw, cbb), tm_l, cbb, _ij))])[0]
    scatter_branch, token = _exchange_begin("scatter_branch", [dwba, dwbl, dwo.reshape(N_DEV, d // N_DEV, d)], True)
    tn_a, tn_l = _tile(aw, 1024), _tile(lw, 1024)
    dy_attn = _matmul("d_attn_out", dya, wba_p, "nt", tm_b, tn_a, d,
                      outs=[((1, s, aw), BF16, _bspec((1, s, aw), tm_b, tn_a, _ij))], extras=[_after_token(token)],
                      b_buffers=1)[0][0]
    dy_lru = _matmul("d_lru_out", dyl, wbl_p, "nt", tm_b, tn_l, d,
                     outs=[((1, s, lw), F32, _bspec((1, s, lw), tm_b, tn_l, _ij))], b_buffers=1)[0][0]
    dq, dk, dv = _attn_bwd(qkv, dy_attn, attn_tot, nh, t_attn)
    dgr, dxc, d_wr, d_wi, d_br, d_bi, d_lam = _lru_bwd(dy_lru, proj, off_gr // LANE, h_lru, h_prev, xc, w_r, b_rg_gate,
                                                       w_i, b_in_gate, lru_lambda, t_lru)
    dxr, d_cw, d_cb = _conv_bwd(xr_pad, jnp.pad(dxc, ((0, CONV_HALO), (0, 0))), conv_w_full, t_lru)
    dproj = jnp.concatenate([dq.astype(BF16), dk.astype(BF16), dv.astype(BF16), dxr.astype(BF16), dgr.astype(BF16),
                             dga[0], dgl[0]], axis=1)
    tm_d = _tile(d, 512)
    dwi = _matmul("mix_dw_in", y2[None], dproj[None], "tn", tm_d, tn_i, s,
                  outs=[((N_DEV, d, cbi), BF16, _bspec((N_DEV, d, cbi), tm_d, tn_i, _ij))], b_buffers=1)[0]
    scatter_mix, token = _exchange_begin("scatter_mix", [dwi], True)
    tm_big = _tile(s, 1024, 8)
    dy2 = _matmul("mix_dy", dproj[None], wi_g, "nt", tm_big, d, tn_i,
                  outs=[((1, s, d), F32, _bspec((1, s, d), tm_big, d, _ij))], extras=[_after_token(token)])[0][0]
    dh1, do1, dsh2, dsc2, dn2, dg1 = _norm_mod_bwd("norm2_bwd", dy2, h1, dh2, norm_mix, sc2, tr, below=(o1, g1, 0.5))

    dgu1, scatter_ffn1, token = _ffn_bwd_weights("ffn1", do1, y1, gu1, act1, wf1i, wf1o, tm)
    dy1 = _ffn_bwd_input("ffn1", dgu1, wf1i, token)
    grad_x, dsh1, dsc1, dn1 = _norm_mod_bwd("norm1_bwd", dy1, xs, dh1, norm_ffn1, sc1, tr)

    results = {}

    def update_group(wait_name, handles, leaves, after):
        for (n, w, m, v), parts in zip(leaves, _exchange_end(wait_name, handles, after, True)):
            results[n] = [o[None] for o in _adamw("adamw_" + n, parts, w[0], m[0], v[0])]
        return results[leaves[-1][0]][0]

    done = update_group("scattered_ffn2", scatter_ffn2, [("w_ffn2_in", w_ffn2_in, m_w_ffn2_in, v_w_ffn2_in),
                                                         ("w_ffn2_out", w_ffn2_out, m_w_ffn2_out, v_w_ffn2_out)], grad_x)
    done = update_group("scattered_mix", scatter_branch + scatter_mix,
                        [("w_branch_attn", w_branch_attn, m_w_branch_attn, v_w_branch_attn),
                         ("w_branch_lru", w_branch_lru, m_w_branch_lru, v_w_branch_lru),
                         ("w_out", w_out, m_w_out, v_w_out), ("w_in", w_in, m_w_in, v_w_in)], done)

    lane_pad = jnp.zeros((1, 7 * LANE), F32)
    pack = jnp.concatenate(
        [loss_part, lane_pad, dsh1, dsc1, dg1, dsh2, dsc2, dg2, dsh3, dsc3, dg3, dn1, dn2, dn3, d_nf, d_cb, d_br, d_bi, d_lam,
         d_cw.reshape(1, -1)], axis=1)
    pack = jnp.pad(pack, ((0, 0), (0, -pack.shape[1] % (8 * LANE))))
    gate_pack = jnp.concatenate([d_wr.reshape(1, -1), d_wi.reshape(1, -1)], axis=1).astype(BF16)
    n_pack, n_gate = pack.shape[1], gate_pack.shape[1]
    packs, gate_packs = _exchange("gather_small", [pack, gate_pack], False, done)
    packs = packs.reshape(N_DEV, n_pack // LANE, LANE)
    g_pack = _reduce_parts("sum_small", packs).reshape(1, n_pack)
    g_gate = _reduce_parts("sum_gates", gate_packs.reshape(N_DEV, n_gate // LANE, LANE)).reshape(1, n_gate)
    loss = g_pack[0, 0]
    off = 8 * LANE
    n_vec = 9 * d + 4 * d + 4 * lw
    n_adam = n_vec + n_gate
    g_small = jnp.concatenate([g_pack[:, off:off + n_vec], g_gate], axis=1).reshape(1, n_adam // LANE, LANE)
    d_cw_sum = g_pack[:, off + n_vec:off + n_vec + CONV_WIDTH * lw].reshape(CONV_WIDTH, lw)
    d_cw_mine = lax.dynamic_slice(d_cw_sum, (0, me * cwb), (CONV_WIDTH, cwb))

    small_names = ["b_ada", "norm_ffn1", "norm_mix", "norm_ffn2", "norm_final", "conv_b", "b_rg_gate", "b_in_gate",
                   "lru_lambda", "w_rg_gate", "w_in_gate"]
    given = dict(b_ada=(b_ada, m_b_ada, v_b_ada), norm_ffn1=(norm_ffn1, m_norm_ffn1, v_norm_ffn1),
                 norm_mix=(norm_mix, m_norm_mix, v_norm_mix), norm_ffn2=(norm_ffn2, m_norm_ffn2, v_norm_ffn2),
                 norm_final=(norm_final, m_norm_final, v_norm_final), conv_b=(conv_b, m_conv_b, v_conv_b),
                 b_rg_gate=(b_rg_gate, m_b_rg_gate, v_b_rg_gate), b_in_gate=(b_in_gate, m_b_in_gate, v_b_in_gate),
                 lru_lambda=(lru_lambda, m_lru_lambda, v_lru_lambda), w_rg_gate=(w_rg_gate, m_w_rg_gate, v_w_rg_gate),
                 w_in_gate=(w_in_gate, m_w_in_gate, v_w_in_gate))
    packed = [jnp.concatenate([given[n][q].reshape(1, -1) for n in small_names], axis=1).reshape(n_adam // LANE, LANE)
              for q in range(3)]
    small_out = _adamw("adamw_small", g_small, *packed)
    pos = 0
    for n in small_names:
        shape = given[n][0].shape
        size = math.prod(shape)
        results[n] = [o.reshape(1, n_adam)[:, pos:pos + size].reshape(shape) for o in small_out]
        pos += size
    results["conv_w"] = [o.reshape(conv_w.shape) for o in
                         _adamw("adamw_conv_w", d_cw_mine[None], conv_w[0], m_conv_w[0], v_conv_w[0])]

    dmod_all = packs.reshape(N_DEV, n_pack)[:, off:off + 9 * d]
    dmod_mine = lax.dynamic_slice(dmod_all, (0, me * cba), (N_DEV, cba))
    dmod_rows = jnp.pad(dmod_mine, ((0, LANE - N_DEV), (0, 0)))
    c_act_t = jnp.pad(c_act.T, ((0, 0), (0, LANE - N_DEV)))
    tm_d2 = _tile(d, 256)
    d_wada = _matmul("dw_ada", c_act_t[None], dmod_rows[None], "nn", tm_d2, cba, LANE,
                     outs=[((1, d, cba), F32, _bspec((1, d, cba), tm_d2, cba, _ij))])[0]
    results["w_ada"] = [o[None] for o in _adamw("adamw_w_ada", d_wada, w_ada[0], m_w_ada[0], v_w_ada[0])]

    update_group("scattered_ffn1", scatter_ffn1, [("w_ffn1_in", w_ffn1_in, m_w_ffn1_in, v_w_ffn1_in),
                                                  ("w_ffn1_out", w_ffn1_out, m_w_ffn1_out, v_w_ffn1_out)], results["w_ada"][0])

    order = ["w_ada", "b_ada", "norm_ffn1", "w_ffn1_in", "w_ffn1_out", "norm_mix", "w_in", "conv_w", "conv_b", "w_rg_gate",
             "b_rg_gate", "w_in_gate", "b_in_gate", "lru_lambda", "w_branch_attn", "w_branch_lru", "w_out", "norm_ffn2",
             "w_ffn2_in", "w_ffn2_out", "norm_final"]
    return (loss, grad_x[None], *[results[n][0] for n in order], *[results[n][1] for n in order],
            *[results[n][2] for n in order], *[results[n][3] for n in order])
```

```python
import functools
import math

import jax
import jax.numpy as jnp
from jax import lax
from jax.experimental import pallas as pl
from jax.experimental.pallas import tpu as pltpu

F32 = jnp.float32
BF16 = jnp.bfloat16
N_DEV = 8
HEAD_DIM = 128
CONV_WIDTH = 4
CONV_HALO = 8
LRU_C = 8.0
EPS = 1e-6
ADAM_LR, ADAM_B1, ADAM_B2, ADAM_EPS, ADAM_WD, ADAM_STEP = 0.001, 0.9, 0.999, 1e-08, 0.01, 10
LANE = 128
VMEM_LIMIT = 56 * 1024 * 1024
MESH = pl.DeviceIdType.MESH

NT = (((1,), (1,)), ((), ()))
NN = (((1,), (0,)), ((), ()))
TN = (((0,), (0,)), ((), ()))


def _tile(dim, target, align=LANE):
    t = (min(target, dim) // align) * align
    while t >= align:
        if dim % t == 0:
            return t
        t -= align
    return dim


def _params(sem):
    return pltpu.CompilerParams(dimension_semantics=sem, vmem_limit_bytes=VMEM_LIMIT)


def _sigmoid(x):
    return 1.0 / (1.0 + jnp.exp(-x))


def _softplus(x):
    return jnp.maximum(x, 0.0) + jnp.log(1.0 + jnp.exp(-jnp.abs(x)))


def _log1p(z):
    w = 1.0 + z
    return jnp.where(w == 1.0, z, jnp.log(w) * z / jnp.where(w == 1.0, 1.0, w - 1.0))


def _expm1(x):
    poly = x * (1.0 + x * (0.5 + x * (1.0 / 6 + x * (1.0 / 24 + x * (1.0 / 120 + x * (1.0 / 720))))))
    return jnp.where(jnp.abs(x) < 0.25, poly, jnp.exp(x) - 1.0)


_GELU_C = math.sqrt(2.0 / math.pi)


def _gelu_and_grad(x):
    inner = _GELU_C * (x + 0.044715 * x * x * x)
    th = jnp.tanh(inner)
    val = 0.5 * x * (1.0 + th)
    grad = 0.5 * (1.0 + th) + 0.5 * x * (1.0 - th * th) * _GELU_C * (1.0 + 3 * 0.044715 * x * x)
    return val, grad


def _dot_split(x, u):
    hi = x.astype(BF16)
    lo = (x - hi.astype(F32)).astype(BF16)
    return jnp.dot(hi, u, preferred_element_type=F32) + jnp.dot(lo, u, preferred_element_type=F32)


def _mesh_position():
    x, y, c = lax.axis_index("x"), lax.axis_index("y"), lax.axis_index("c")
    return x, y, c, 4 * x + 2 * y + c


def _peers(x, y, c):
    out = []
    for mask in range(1, N_DEV):
        px = 1 - x if mask & 4 else x
        py = 1 - y if mask & 2 else y
        pc = 1 - c if mask & 1 else c
        out.append((mask, (px, py, pc), 4 * px + 2 * py + pc))
    return out


def _exchange(name, arrs, scatter, after=None):
    n = len(arrs)
    behind = [] if after is None else [after]

    def body(*refs):
        ins, outs = refs[:n], refs[n + len(behind):2 * n + len(behind)]
        send_sems, recv_sems, local_sems = refs[2 * n + len(behind):]
        x, y, c, me = _mesh_position()
        peers = _peers(x, y, c)
        waits = []
        for a in range(n):
            mine = ins[a].at[me] if scatter else ins[a]
            local = pltpu.make_async_copy(mine, outs[a].at[me], local_sems.at[a])
            local.start()
            waits.append(local.wait)
            for mask, dev, idx in peers:
                k = a * (N_DEV - 1) + mask - 1
                src = ins[a].at[idx] if scatter else ins[a]
                send = pltpu.make_async_remote_copy(src_ref=src, dst_ref=outs[a].at[me], send_sem=send_sems.at[k],
                                                    recv_sem=recv_sems.at[k], device_id=dev, device_id_type=MESH)
                send.start()
                arrival = pltpu.make_async_remote_copy(src_ref=src, dst_ref=outs[a].at[idx], send_sem=send_sems.at[k],
                                                       recv_sem=recv_sems.at[k], device_id=dev, device_id_type=MESH)
                waits.append(send.wait_send)
                waits.append(arrival.wait_recv)
        for w in waits:
            w()

    any_spec = pl.BlockSpec(memory_space=pl.ANY)
    out_shape = [jax.ShapeDtypeStruct(a.shape if scatter else (N_DEV,) + a.shape, a.dtype) for a in arrs]
    return pl.pallas_call(
        body, name=name, out_shape=out_shape, in_specs=[any_spec] * (n + len(behind)), out_specs=[any_spec] * n,
        scratch_shapes=[pltpu.SemaphoreType.DMA((n * (N_DEV - 1),)), pltpu.SemaphoreType.DMA((n * (N_DEV - 1),)),
                        pltpu.SemaphoreType.DMA((n,))],
    )(*arrs, *behind)


HBM_SPEC = pl.BlockSpec(memory_space=pltpu.HBM)
SEM_SPEC = pl.BlockSpec(memory_space=pltpu.SEMAPHORE)
DATAFLOW = pltpu.SideEffectType.DATAFLOW_SIDE_EFFECTING


ALL_MASKS = tuple(range(1, N_DEV))
SAME_CORE_MASKS = (1, 2, 4, 6)


def _exchange_begin(name, arrs, scatter, after=None, once_per_chip=()):
    n = len(arrs)
    lands = [lax.empty(a.shape if scatter else (N_DEV,) + a.shape, a.dtype) for a in arrs]
    behind = [] if after is None else [after]
    masks = [SAME_CORE_MASKS if a in once_per_chip else ALL_MASKS for a in range(n)]

    def body(*refs):
        srcs, zones, outs = refs[:n], refs[n:2 * n], refs[2 * n + len(behind):]
        x, y, c, me = _mesh_position()
        for a in range(n):
            send_sems, recv_sems = outs[4 * a], outs[4 * a + 1]
            for mask, dev, idx in _peers(x, y, c):
                if mask not in masks[a]:
                    continue
                pltpu.make_async_remote_copy(
                    src_ref=srcs[a].at[idx] if scatter else srcs[a], dst_ref=zones[a].at[me], send_sem=send_sems.at[mask - 1],
                    recv_sem=recv_sems.at[mask - 1], device_id=dev, device_id_type=MESH).start()
        outs[-1][...] = jnp.zeros_like(outs[-1])

    out_shape, out_specs, aliases = [], [], {}
    for a in range(n):
        out_shape += [pltpu.SemaphoreType.DMA((N_DEV - 1,)), pltpu.SemaphoreType.DMA((N_DEV - 1,)),
                      pltpu.HBM(arrs[a].shape, arrs[a].dtype), pltpu.HBM(lands[a].shape, lands[a].dtype)]
        out_specs += [SEM_SPEC, SEM_SPEC, HBM_SPEC, HBM_SPEC]
        aliases[a] = 4 * a + 2
        aliases[n + a] = 4 * a + 3
    out_shape.append(jax.ShapeDtypeStruct((8, LANE), F32))
    out_specs.append(pl.BlockSpec(memory_space=pltpu.VMEM))
    res = pl.pallas_call(
        body, name=name, out_shape=out_shape,
        in_specs=[HBM_SPEC] * (2 * n) + [pl.BlockSpec(memory_space=pl.ANY)] * len(behind),
        out_specs=out_specs, input_output_aliases=aliases, compiler_params=pltpu.CompilerParams(has_side_effects=DATAFLOW),
    )(*[pltpu.with_memory_space_constraint(v, pltpu.HBM) for v in list(arrs) + lands], *behind)
    return [tuple(res[4 * a:4 * a + 4]) + (masks[a],) for a in range(n)], res[-1]


def _exchange_end(name, handles, after, scatter):
    n = len(handles)
    me = 4 * lax.axis_index("x") + 2 * lax.axis_index("y") + lax.axis_index("c")

    def body(*refs):
        x, y, c, me = _mesh_position()
        for a in range(n):
            src, zone, send_sems, recv_sems = refs[4 * a:4 * a + 4]
            for mask, dev, idx in _peers(x, y, c):
                if mask not in handles[a][4]:
                    continue
                cp = pltpu.make_async_remote_copy(
                    src_ref=src.at[idx] if scatter else src, dst_ref=zone.at[idx], send_sem=send_sems.at[mask - 1],
                    recv_sem=recv_sems.at[mask - 1], device_id=dev, device_id_type=MESH)
                cp.wait_send()
                cp.wait_recv()

    operands, in_specs, out_shape, aliases = [], [], [], {}
    for a, (send_sems, recv_sems, src, zone, _) in enumerate(handles):
        operands += [src, zone, send_sems, recv_sems]
        in_specs += [HBM_SPEC, HBM_SPEC, SEM_SPEC, SEM_SPEC]
        out_shape += [pltpu.HBM(src.shape, src.dtype), pltpu.HBM(zone.shape, zone.dtype)]
        aliases[4 * a] = 2 * a
        aliases[4 * a + 1] = 2 * a + 1
    res = pl.pallas_call(
        body, name=name, out_shape=out_shape, in_specs=in_specs + [pl.BlockSpec(memory_space=pl.ANY)],
        out_specs=[HBM_SPEC] * (2 * n), input_output_aliases=aliases,
        compiler_params=pltpu.CompilerParams(has_side_effects=DATAFLOW),
    )(*operands, after)
    full = []
    for a in range(n):
        src, zone = res[2 * a], res[2 * a + 1]
        own = lax.dynamic_index_in_dim(src, me, 0, keepdims=False) if scatter else src
        full.append(lax.dynamic_update_index_in_dim(zone, own, me, 0))
    return full


def _sibling_forward(name, zones):
    n = len(zones)
    hops = (2, 4, 6)

    def body(*refs):
        outs, send_sems, recv_sems = refs[n:2 * n], refs[2 * n], refs[2 * n + 1]
        x, y, c, me = _mesh_position()
        sibling = (x, y, 1 - c)
        waits = []
        for a in range(n):
            for q, mask in enumerate(hops):
                chip = 4 * (1 - x if mask & 4 else x) + 2 * (1 - y if mask & 2 else y)
                k = a * len(hops) + q
                held, missing = outs[a].at[chip + c], outs[a].at[chip + 1 - c]
                send = pltpu.make_async_remote_copy(src_ref=held, dst_ref=held, send_sem=send_sems.at[k],
                                                    recv_sem=recv_sems.at[k], device_id=sibling, device_id_type=MESH)
                send.start()
                arrival = pltpu.make_async_remote_copy(src_ref=missing, dst_ref=missing, send_sem=send_sems.at[k],
                                                       recv_sem=recv_sems.at[k], device_id=sibling, device_id_type=MESH)
                waits += [send.wait_send, arrival.wait_recv]
        for w in waits:
            w()

    any_spec = pl.BlockSpec(memory_space=pl.ANY)
    return pl.pallas_call(
        body, name=name, out_shape=[jax.ShapeDtypeStruct(z.shape, z.dtype) for z in zones], in_specs=[any_spec] * n,
        out_specs=[any_spec] * n, input_output_aliases={a: a for a in range(n)},
        scratch_shapes=[pltpu.SemaphoreType.DMA((n * len(hops),)), pltpu.SemaphoreType.DMA((n * len(hops),))],
    )(*zones)


def _bspec(shape, tr, tc, rc, buffers=None):
    per = shape[-1] // tc
    mode = {} if buffers is None else dict(pipeline_mode=pl.Buffered(buffers))
    if len(shape) == 3:
        return pl.BlockSpec((None, tr, tc), lambda j, i, k: (rc(i, j, k)[1] // per, rc(i, j, k)[0], rc(i, j, k)[1] % per), **mode)
    return pl.BlockSpec((shape[0], None, tr, tc),
                        lambda j, i, k: (0, rc(i, j, k)[1] // per, rc(i, j, k)[0], rc(i, j, k)[1] % per), **mode)


def _ij(i, j, k):
    return i, j


def _row_spec(tn, col_tile_offset=0):
    return pl.BlockSpec((1, tn), lambda j, i, k: (0, j + col_tile_offset))


def _matmul(name, a, b, mode, tm, tn, tk, outs, epilogue=None, extras=(), b_buffers=None):
    groups = b.shape[0] if b.ndim == 4 else 1
    if mode == "nn":
        m, k_dim, n = a.shape[1], a.shape[0] * a.shape[2], b.shape[-3] * b.shape[-1]
        a_spec = _bspec(a.shape, tm, tk, lambda i, j, k: (i, k))
        b_spec = _bspec(b.shape, tk, tn, lambda i, j, k: (k, j), b_buffers)
        dims = NN
    elif mode == "nt":
        m, k_dim, n = a.shape[1], a.shape[0] * a.shape[2], b.shape[-2]
        a_spec = _bspec(a.shape, tm, tk, lambda i, j, k: (i, k))
        b_spec = _bspec(b.shape, tn, tk, lambda i, j, k: (j, k), b_buffers)
        dims = NT
    else:
        m, k_dim, n = a.shape[0] * a.shape[2], a.shape[1], b.shape[-3] * b.shape[-1]
        a_spec = _bspec(a.shape, tk, tm, lambda i, j, k: (k, i))
        b_spec = _bspec(b.shape, tk, tn, lambda i, j, k: (k, j), b_buffers)
        dims = TN
    assert m % tm == 0 and n % tn == 0 and k_dim % tk == 0, (name, m, n, k_dim, tm, tn, tk)
    nk = k_dim // tk
    n_extra, n_out = len(extras), len(outs)

    def finish(acc, extra_refs, out_refs):
        if epilogue is None:
            out_refs[0][...] = acc[0].astype(out_refs[0].dtype)
        else:
            epilogue(acc, extra_refs, out_refs)

    def products(a_ref, b_ref):
        a_tile = a_ref[...].astype(BF16)
        return [lax.dot_general(a_tile, (b_ref[g] if b.ndim == 4 else b_ref[...]).astype(BF16), dims,
                                preferred_element_type=F32) for g in range(groups)]

    def body_whole_k(*refs):
        finish(products(refs[0], refs[1]), refs[2:2 + n_extra], refs[2 + n_extra:])

    def body_k_steps(*refs):
        acc_ref = refs[-1]
        k = pl.program_id(2)

        @pl.when(k == 0)
        def _():
            acc_ref[...] = jnp.zeros_like(acc_ref)

        for g, p in enumerate(products(refs[0], refs[1])):
            acc_ref[g] += p

        @pl.when(k == nk - 1)
        def _():
            finish([acc_ref[g] for g in range(groups)], refs[2:2 + n_extra], refs[2 + n_extra:2 + n_extra + n_out])

    return pl.pallas_call(
        body_whole_k if nk == 1 else body_k_steps, name=name, grid=(n // tn, m // tm, nk),
        in_specs=[a_spec, b_spec] + [s for _, s in extras],
        out_specs=[s for _, _, s in outs],
        out_shape=[jax.ShapeDtypeStruct(shape, dtype) for shape, dtype, _ in outs],
        scratch_shapes=[] if nk == 1 else [pltpu.VMEM((groups, tm, tn), F32)],
        compiler_params=_params(("parallel", "parallel", "arbitrary")),
    )(a, b, *[arr for arr, _ in extras])


def _rowwise(name, fn, rows, vecs, outs, accs, tm):
    s = rows[0][0].shape[0]
    n_in, n_out = len(rows) + len(vecs), len(outs)

    def body(*refs):
        i = pl.program_id(0)
        res = fn(*[r[...] for r in refs[:n_in]])
        res = res if isinstance(res, tuple) else (res,)
        out_refs, acc_refs = refs[n_in:n_in + n_out], refs[n_in + n_out:]
        for ref, val in zip(out_refs, res[:n_out]):
            ref[...] = val.astype(ref.dtype)

        @pl.when(i == 0)
        def _():
            for ref in acc_refs:
                ref[...] = jnp.zeros_like(ref)

        for ref, val in zip(acc_refs, res[n_out:]):
            ref[...] += val

    in_specs = [pl.BlockSpec((tm, w), functools.partial(lambda i, cb: (i, cb), cb=cb)) for _, w, cb in rows]
    in_specs += [pl.BlockSpec(v.shape, lambda i: (0,) * v.ndim) for v in vecs]
    out_specs = [pl.BlockSpec((tm, w), lambda i: (i, 0)) for w, _ in outs] + [pl.BlockSpec((1, w), lambda i: (0, 0)) for w in accs]
    out_shape = [jax.ShapeDtypeStruct((s, w), dt) for w, dt in outs] + [jax.ShapeDtypeStruct((1, w), F32) for w in accs]
    return pl.pallas_call(
        body, name=name, grid=(s // tm,), in_specs=in_specs, out_specs=out_specs, out_shape=out_shape,
        compiler_params=_params(("arbitrary",)),
    )(*[r for r, _, _ in rows], *vecs)


def _colsum(v):
    return jnp.sum(v, axis=0, keepdims=True)


def _norm_mod(name, h, nw, sc, sh, tm):
    d = h.shape[1]

    def fn(hb, nwb, scb, shb):
        r = lax.rsqrt(jnp.mean(hb * hb, axis=-1, keepdims=True) + EPS)
        return (hb * r) * nwb * (1.0 + scb) + shb

    return _rowwise(name, fn, [(h, d, 0)], [nw, sc, sh], [(d, BF16)], [], tm)[0]


def _norm_mod_bwd(name, dy, h, dh_next, nw, sc, tm, below=None):
    d = h.shape[1]

    def fn(dyb, hb, dhb, *rest):
        nwb, scb = rest[-2:] if below is None else rest[1:3]
        r = lax.rsqrt(jnp.mean(hb * hb, axis=-1, keepdims=True) + EPS)
        xh = hb * r
        dxh = dyb * (nwb * (1.0 + scb))
        dx = r * (dxh - xh * jnp.mean(dxh * xh, axis=-1, keepdims=True))
        dh = dhb + dx
        sums = (_colsum(dyb), _colsum(dyb * xh * nwb), _colsum(dyb * xh * (1.0 + scb)))
        if below is None:
            return (dh,) + sums
        ob, gb = rest[0], rest[3]
        return (dh, dh * (below[2] * gb)) + sums + (_colsum(dh * ob * below[2]),)

    rows = [(dy, d, 0), (h, d, 0), (dh_next, d, 0)]
    if below is None:
        return _rowwise(name, fn, rows, [nw, sc], [(d, F32)], [d, d, d], tm)
    return _rowwise(name, fn, rows + [(below[0], d, 0)], [nw, sc, below[1]], [(d, F32), (d, BF16)], [d, d, d, d], tm)


def _loss_bwd(name, h, target, nw, o, g, tm):
    d = h.shape[1]

    def fn(hb, tb, ob, nwb, gb):
        r = lax.rsqrt(jnp.mean(hb * hb, axis=-1, keepdims=True) + EPS)
        xh = hb * r
        err = xh * nwb - tb
        dy = err * (1.0 / d)
        dxh = dy * nwb
        dx = r * (dxh - xh * jnp.mean(dxh * xh, axis=-1, keepdims=True))
        loss = 0.5 * jnp.sum(jnp.mean(err * err, axis=-1, keepdims=True), axis=0, keepdims=True)
        return dx, dx * (0.5 * gb), jnp.broadcast_to(loss, (1, LANE)), _colsum(dy * xh), _colsum(dx * ob * 0.5)

    return _rowwise(name, fn, [(h, d, 0), (target, d, 0), (o, d, 0)], [nw, g], [(d, F32), (d, BF16)], [LANE, d, d], tm)


def _head_group(nh, most):
    return max(g for g in (1, 2, 4) if g <= most and nh % g == 0)


def _attn_fwd(qkv, nh, t):
    s = qkv.shape[0]
    scale = HEAD_DIM ** -0.5
    hp = _head_group(nh, 4)
    wide = hp * HEAD_DIM
    lanes = [slice(u * HEAD_DIM, (u + 1) * HEAD_DIM) for u in range(hp)]

    def body(q_ref, k_ref, v_ref, y_ref, tot_ref):
        i = pl.program_id(1)
        row = lax.broadcasted_iota(jnp.int32, (t, t), 0)
        col = lax.broadcasted_iota(jnp.int32, (t, t), 1)
        later = (row > col).astype(BF16)
        causal = col < row
        qs = [q_ref[:, ln] for ln in lanes]

        def block(j, carry, diagonal):
            ks = pl.ds(pl.multiple_of(j * t, t), t)
            heads = range(hp)
            z = [lax.dot_general(qs[u], k_ref[ks, lanes[u]], NT, preferred_element_type=F32) * scale for u in heads]
            sp = [_softplus(z[u]) for u in heads]
            log_keep = [jnp.where(causal, -sp[u], 0.0) if diagonal else -sp[u] for u in heads]
            between = [_dot_split(log_keep[u], later) for u in heads]
            w = [jnp.exp(z[u] - sp[u] + between[u] + carry[u][1]) for u in heads]
            if diagonal:
                w = [jnp.where(causal, w[u], 0.0) for u in heads]
            o = [carry[u][0] + jnp.dot(w[u].astype(BF16), v_ref[ks, lanes[u]], preferred_element_type=F32) for u in heads]
            return tuple((o[u], carry[u][1] + jnp.sum(log_keep[u], axis=1, keepdims=True)) for u in heads)

        carry = tuple((jnp.zeros((t, HEAD_DIM), F32), jnp.zeros((t, 1), F32)) for _ in lanes)
        carry = block(i, carry, True)
        carry = lax.fori_loop(0, i, lambda jj, cr: block(i - 1 - jj, cr, False), carry)
        for u, ln in enumerate(lanes):
            y_ref[:, ln] = carry[u][0].astype(y_ref.dtype)
            tot_ref[:, ln] = jnp.broadcast_to(carry[u][1], (t, HEAD_DIM))

    g = nh // hp
    return pl.pallas_call(
        body, name="attn_fwd", grid=(g, s // t),
        in_specs=[pl.BlockSpec((t, wide), lambda h, i: (i, h)),
                  pl.BlockSpec((s, wide), lambda h, i: (0, g + h)),
                  pl.BlockSpec((s, wide), lambda h, i: (0, 2 * g + h))],
        out_specs=[pl.BlockSpec((t, wide), lambda h, i: (i, h)), pl.BlockSpec((t, wide), lambda h, i: (i, h))],
        out_shape=[jax.ShapeDtypeStruct((s, nh * HEAD_DIM), BF16), jax.ShapeDtypeStruct((s, nh * HEAD_DIM), F32)],
        compiler_params=_params(("parallel", "arbitrary")),
    )(qkv, qkv, qkv)


def _attn_bwd(qkv, dy, tot, nh, t):
    s = qkv.shape[0]
    scale = HEAD_DIM ** -0.5
    hp = _head_group(nh, 4)
    wide = hp * HEAD_DIM
    lanes = [slice(u * HEAD_DIM, (u + 1) * HEAD_DIM) for u in range(hp)]

    def body(q_ref, k_ref, v_ref, dy_ref, tot_ref, dq_ref, dk_out, dv_out, dk_ref, dv_ref):
        i = pl.program_id(1)

        @pl.when(i == 0)
        def _():
            dk_ref[...] = jnp.zeros_like(dk_ref)
            dv_ref[...] = jnp.zeros_like(dv_ref)

        row = lax.broadcasted_iota(jnp.int32, (t, t), 0)
        col = lax.broadcasted_iota(jnp.int32, (t, t), 1)
        upto = (row <= col).astype(BF16)
        before = (row < col).astype(BF16)
        causal = col < row
        qs = [q_ref[:, ln] for ln in lanes]
        dys = [dy_ref[:, ln] for ln in lanes]
        totals = [tot_ref[:, u * HEAD_DIM:u * HEAD_DIM + 1] for u in range(hp)]

        def block(j, carry, diagonal):
            ks = pl.ds(pl.multiple_of(j * t, t), t)
            heads = range(hp)
            kb = [k_ref[ks, ln] for ln in lanes]
            vb = [v_ref[ks, ln] for ln in lanes]
            z = [lax.dot_general(qs[u], kb[u], NT, preferred_element_type=F32) * scale for u in heads]
            dw = [lax.dot_general(dys[u], vb[u], NT, preferred_element_type=F32) for u in heads]
            sp = [_softplus(z[u]) for u in heads]
            log_keep = [jnp.where(causal, -sp[u], 0.0) if diagonal else -sp[u] for u in heads]
            upto_sum = [_dot_split(log_keep[u], upto) for u in heads]
            w = [jnp.exp(z[u] - sp[u] + (totals[u] - carry[u][1] - upto_sum[u])) for u in heads]
            if diagonal:
                w = [jnp.where(causal, w[u], 0.0) for u in heads]
            g = [dw[u] * w[u] for u in heads]
            g_before = [_dot_split(g[u], before) for u in heads]
            dz = [(g[u] * jnp.exp(-sp[u]) - jnp.exp(z[u] - sp[u]) * (carry[u][2] + g_before[u])) * scale for u in heads]
            if diagonal:
                dz = [jnp.where(causal, dz[u], 0.0) for u in heads]
            dzb = [dz[u].astype(BF16) for u in heads]
            dq = [carry[u][0] + jnp.dot(dzb[u], kb[u], preferred_element_type=F32) for u in heads]
            for u in heads:
                dk_ref[ks, lanes[u]] += lax.dot_general(dzb[u], qs[u], TN, preferred_element_type=F32)
            for u in heads:
                dv_ref[ks, lanes[u]] += lax.dot_general(w[u].astype(BF16), dys[u], TN, preferred_element_type=F32)
            return tuple((dq[u], carry[u][1] + jnp.sum(log_keep[u], axis=1, keepdims=True),
                          carry[u][2] + jnp.sum(g[u], axis=1, keepdims=True)) for u in heads)

        zero = jnp.zeros((t, 1), F32)
        carry = tuple((jnp.zeros((t, HEAD_DIM), F32), zero, zero) for _ in lanes)
        carry = lax.fori_loop(0, i, lambda j, cr: block(j, cr, False), carry)
        carry = block(i, carry, True)
        for u, ln in enumerate(lanes):
            dq_ref[:, ln] = carry[u][0].astype(dq_ref.dtype)

        @pl.when(i == pl.num_programs(1) - 1)
        def _():
            dk_out[...] = dk_ref[...].astype(dk_out.dtype)
            dv_out[...] = dv_ref[...].astype(dv_out.dtype)

    g = nh // hp
    tile = lambda off: pl.BlockSpec((t, wide), lambda h, i: (i, off + h))
    head = lambda off, **mode: pl.BlockSpec((s, wide), lambda h, i: (0, off + h), **mode)
    once = dict(pipeline_mode=pl.Buffered(1))
    return pl.pallas_call(
        body, name="attn_bwd", grid=(g, s // t),
        in_specs=[tile(0), head(g, **once), head(2 * g, **once), tile(0), tile(0)],
        out_specs=[tile(0), head(0), head(0)],
        out_shape=[jax.ShapeDtypeStruct((s, nh * HEAD_DIM), BF16)] * 3,
        scratch_shapes=[pltpu.VMEM((s, wide), F32), pltpu.VMEM((s, wide), F32)],
        compiler_params=_params(("parallel", "arbitrary")),
    )(qkv, qkv, qkv, dy, tot)


def _lru_gates(xc, w_r, b_r, w_i, b_i, lam):
    xb = xc.astype(BF16)
    r = _sigmoid(jnp.dot(xb, w_r.astype(BF16), preferred_element_type=F32) + b_r)
    i = _sigmoid(jnp.dot(xb, w_i.astype(BF16), preferred_element_type=F32) + b_i)
    neg_lam = -lam
    sp_lam = jnp.maximum(neg_lam, 0.0) + _log1p(jnp.exp(-jnp.abs(neg_lam)))
    log_a = -LRU_C * r * sp_lam
    a = jnp.exp(log_a)
    mult = jnp.sqrt(-_expm1(2.0 * log_a))
    return r, i, sp_lam, a, mult


def _conv_taps(xpad_chunk, conv_w, t):
    shifted = [xpad_chunk[CONV_HALO:, :]]
    for d in range(1, CONV_WIDTH):
        shifted.append(pltpu.roll(xpad_chunk, d, 0)[CONV_HALO:, :])
    weights = [conv_w[CONV_WIDTH - 1 - d:CONV_WIDTH - d, :] for d in range(CONV_WIDTH)]
    return shifted, weights


def _lru_fwd(xr_pad, proj, gr_block0, conv_w, conv_b, w_r, b_r, w_i, b_i, lam, t):
    s, w = xr_pad.shape[0] - CONV_HALO, xr_pad.shape[1]
    nblk = w // LANE
    nchunk = s // t
    steps = [1 << p for p in range(t.bit_length() - 1)]
    assert (1 << (t.bit_length() - 1)) == t and w_r.shape[1:] == (LANE, LANE)

    def body(x_ref, gr_ref, cw_ref, cb_ref, wr_ref, br_ref, wi_ref, bi_ref, lam_ref, h_ref, hp_ref, xc_ref, y_ref):
        row = lax.broadcasted_iota(jnp.int32, (t, LANE), 0)

        def chunk(ci, h_in):
            t0 = pl.multiple_of(ci * t, t)
            shifted, weights = _conv_taps(x_ref[pl.ds(t0, t + CONV_HALO), :], cw_ref[...], t)
            xc = cb_ref[...] + sum(wd * xs for wd, xs in zip(weights, shifted))
            r, i, _, a, mult = _lru_gates(xc, wr_ref[...], br_ref[...], wi_ref[...], bi_ref[...], lam_ref[...])
            coef, val = a, mult * (i * xc)
            for d in steps:
                ok = row >= d
                val = jnp.where(ok, coef * pltpu.roll(val, d, 0) + val, val)
                coef = jnp.where(ok, coef * pltpu.roll(coef, d, 0), coef)
            h = val + coef * h_in
            rows = pl.ds(t0, t)
            h_ref[rows, :] = h
            hp_ref[rows, :] = jnp.where(row == 0, h_in, pltpu.roll(h, 1, 0))
            xc_ref[rows, :] = xc
            y_ref[rows, :] = (h * _gelu_and_grad(gr_ref[rows, :])[0]).astype(y_ref.dtype)
            return h[t - 1:t, :]

        lax.fori_loop(0, nchunk, chunk, jnp.zeros((1, LANE), F32))

    col = lambda rows: pl.BlockSpec((rows, LANE), lambda n: (0, n))
    return pl.pallas_call(
        body, name="lru_fwd", grid=(nblk,),
        in_specs=[col(s + CONV_HALO), pl.BlockSpec((s, LANE), lambda n: (0, gr_block0 + n)), col(CONV_WIDTH), col(1),
                  pl.BlockSpec((None, LANE, LANE), lambda n: (n, 0, 0)), col(1),
                  pl.BlockSpec((None, LANE, LANE), lambda n: (n, 0, 0)), col(1), col(1)],
        out_specs=[col(s)] * 4,
        out_shape=[jax.ShapeDtypeStruct((s, w), F32)] * 3 + [jax.ShapeDtypeStruct((s, w), BF16)],
        compiler_params=_params(("parallel",)),
    )(xr_pad, proj, conv_w, conv_b, w_r, b_r, w_i, b_i, lam)


def _lru_bwd(dy, proj, gr_block0, h, h_prev, xc, w_r, b_r, w_i, b_i, lam, t):
    s, w = dy.shape
    nblk = w // LANE
    nchunk = s // t
    steps = [1 << p for p in range(t.bit_length() - 1)]

    def body(dy_ref, gr_ref, h_ref, hp_ref, xc_ref, wr_ref, br_ref, wi_ref, bi_ref, lam_ref,
             dgr_ref, dxc_ref, dwr_ref, dwi_ref, dbr_ref, dbi_ref, dlam_ref):
        row = lax.broadcasted_iota(jnp.int32, (t, LANE), 0)
        for ref in (dwr_ref, dwi_ref, dbr_ref, dbi_ref, dlam_ref):
            ref[...] = jnp.zeros_like(ref)

        def chunk(cc, carry):
            lam_next, a_next = carry
            rows = pl.ds(pl.multiple_of((nchunk - 1 - cc) * t, t), t)
            dyb, hb, xcb = dy_ref[rows, :], h_ref[rows, :], xc_ref[rows, :]
            gel, dgel = _gelu_and_grad(gr_ref[rows, :])
            dgr_ref[rows, :] = dyb * hb * dgel
            w_r, w_i = wr_ref[...], wi_ref[...]
            r, i, sp_lam, a, mult = _lru_gates(xcb, w_r, br_ref[...], w_i, bi_ref[...], lam_ref[...])
            coef = jnp.where(row == t - 1, a_next, pltpu.roll(a, t - 1, 0))
            val = dyb * gel
            for d in steps:
                ok = row < t - d
                val = jnp.where(ok, coef * pltpu.roll(val, t - d, 0) + val, val)
                coef = jnp.where(ok, coef * pltpu.roll(coef, t - d, 0), coef)
            adj = val + coef * lam_next
            da = adj * hp_ref[rows, :]
            v = i * xcb
            dmult, dv = adj * v, adj * mult
            dlog_a = da * a - (a * a) * dmult / mult
            dr_pre = (-LRU_C * sp_lam) * dlog_a * r * (1.0 - r)
            di_pre = dv * xcb * i * (1.0 - i)
            dlam_ref[...] += _colsum(-LRU_C * r * dlog_a)
            dbr_ref[...] += _colsum(dr_pre)
            dbi_ref[...] += _colsum(di_pre)
            xb, drb, dib = xcb.astype(BF16), dr_pre.astype(BF16), di_pre.astype(BF16)
            dwr_ref[...] += lax.dot_general(xb, drb, TN, preferred_element_type=F32)
            dwi_ref[...] += lax.dot_general(xb, dib, TN, preferred_element_type=F32)
            dxc_ref[rows, :] = (dv * i + lax.dot_general(drb, w_r.astype(BF16), NT, preferred_element_type=F32)
                                + lax.dot_general(dib, w_i.astype(BF16), NT, preferred_element_type=F32))
            return adj[0:1, :], a[0:1, :]

        lax.fori_loop(0, nchunk, chunk, (jnp.zeros((1, LANE), F32), jnp.zeros((1, LANE), F32)))
        dlam_ref[...] = dlam_ref[...] * (-_sigmoid(-lam_ref[...]))

    col = lambda rows: pl.BlockSpec((rows, LANE), lambda n: (0, n))
    mat = pl.BlockSpec((None, LANE, LANE), lambda n: (n, 0, 0))
    return pl.pallas_call(
        body, name="lru_bwd", grid=(nblk,),
        in_specs=[col(s), pl.BlockSpec((s, LANE), lambda n: (0, gr_block0 + n)), col(s), col(s), col(s),
                  mat, col(1), mat, col(1), col(1)],
        out_specs=[col(s), col(s), mat, mat, col(1), col(1), col(1)],
        out_shape=[jax.ShapeDtypeStruct((s, w), F32)] * 2 + [jax.ShapeDtypeStruct((nblk, LANE, LANE), F32)] * 2
        + [jax.ShapeDtypeStruct((1, w), F32)] * 3,
        compiler_params=_params(("parallel",)),
    )(dy, proj, h, h_prev, xc, w_r, b_r, w_i, b_i, lam)


def _conv_bwd(xr_pad, dxc_pad, conv_w, t):
    s, w = xr_pad.shape[0] - CONV_HALO, xr_pad.shape[1]
    nchunk = s // t

    def body(x_ref, g_ref, cw_ref, dx_ref, dcw_ref, dcb_ref):
        dcw_ref[...] = jnp.zeros_like(dcw_ref)
        dcb_ref[...] = jnp.zeros_like(dcb_ref)

        def chunk(ci, _):
            t0 = pl.multiple_of(ci * t, t)
            shifted, weights = _conv_taps(x_ref[pl.ds(t0, t + CONV_HALO), :], cw_ref[...], t)
            gpad = g_ref[pl.ds(t0, t + CONV_HALO), :]
            g = gpad[:t, :]
            dx = weights[0] * g
            for d in range(1, CONV_WIDTH):
                dx = dx + weights[d] * pltpu.roll(gpad, t + CONV_HALO - d, 0)[:t, :]
            dx_ref[pl.ds(t0, t), :] = dx
            for d in range(CONV_WIDTH):
                dcw_ref[CONV_WIDTH - 1 - d:CONV_WIDTH - d, :] += _colsum(g * shifted[d])
            dcb_ref[...] += _colsum(g)
            return 0

        lax.fori_loop(0, nchunk, chunk, 0)

    col = lambda rows: pl.BlockSpec((rows, LANE), lambda n: (0, n))
    return pl.pallas_call(
        body, name="conv_bwd", grid=(w // LANE,),
        in_specs=[col(s + CONV_HALO), col(s + CONV_HALO), col(CONV_WIDTH)],
        out_specs=[col(s), col(CONV_WIDTH), col(1)],
        out_shape=[jax.ShapeDtypeStruct((s, w), F32), jax.ShapeDtypeStruct((CONV_WIDTH, w), F32),
                   jax.ShapeDtypeStruct((1, w), F32)],
        compiler_params=_params(("parallel",)),
    )(xr_pad, dxc_pad, conv_w)


def _sum_parts(parts_ref):
    g = parts_ref[0].astype(F32)
    for p in range(1, parts_ref.shape[0]):
        g = g + parts_ref[p].astype(F32)
    return g


def _reduce_parts(name, parts):
    p, r, c = parts.shape
    tr = _tile(r, max(8, (1 << 19) // c), 8)

    def body(parts_ref, g_ref):
        g_ref[...] = _sum_parts(parts_ref)

    return pl.pallas_call(
        body, name=name, grid=(r // tr,), in_specs=[pl.BlockSpec((p, tr, c), lambda i: (0, i, 0))],
        out_specs=pl.BlockSpec((tr, c), lambda i: (i, 0)), out_shape=jax.ShapeDtypeStruct((r, c), F32),
        compiler_params=_params(("parallel",)),
    )(parts)


def _adamw(name, parts, w, m, v):
    p, r, c = parts.shape
    tr = _tile(r, max(8, (1 << 18) // c), 8)

    def body(parts_ref, w_ref, m_ref, v_ref, g_ref, d_ref, nm_ref, nv_ref):
        g = _sum_parts(parts_ref)
        nm = ADAM_B1 * m_ref[...] + (1.0 - ADAM_B1) * g
        nv = ADAM_B2 * v_ref[...] + (1.0 - ADAM_B2) * (g * g)
        m_hat = nm / (1.0 - ADAM_B1 ** ADAM_STEP)
        v_hat = nv / (1.0 - ADAM_B2 ** ADAM_STEP)
        g_ref[...] = g
        d_ref[...] = -ADAM_LR * (m_hat / (jnp.sqrt(v_hat) + ADAM_EPS) + ADAM_WD * w_ref[...])
        nm_ref[...] = nm
        nv_ref[...] = nv

    blk = pl.BlockSpec((tr, c), lambda i: (i, 0))
    return pl.pallas_call(
        body, name=name, grid=(r // tr,), in_specs=[pl.BlockSpec((p, tr, c), lambda i: (0, i, 0)), blk, blk, blk],
        out_specs=[blk] * 4, out_shape=[jax.ShapeDtypeStruct((r, c), F32)] * 4,
        compiler_params=_params(("parallel",)),
    )(parts, w, m, v)


def _ffn_in(tag, y, w_in_g, tm):
    s, d = y.shape
    half = N_DEV // 2
    cb = w_in_g.shape[2]
    ff = half * cb

    def swiglu(acc, extra_refs, out_refs):
        g, u = acc
        out_refs[0][0] = g.astype(BF16)
        out_refs[0][1] = u.astype(BF16)
        out_refs[1][...] = (g * _sigmoid(g) * u).astype(BF16)

    gu_shape = (2, 1, s, ff)
    gu, act = _matmul(
        tag + "_in", y[None], w_in_g.reshape(2, half, d, cb), "nn", tm, cb, d,
        outs=[(gu_shape, BF16, _bspec(gu_shape, tm, cb, _ij)), ((1, s, ff), BF16, _bspec((1, s, ff), tm, cb, _ij))],
        epilogue=swiglu, b_buffers=1)
    return gu, act


def _ffn_out(tag, act, w_out_g, res, gate, tm):
    _, s, ff = act.shape
    d = res.shape[1]
    tn = _tile(d, 1024)

    def residual(acc, extra_refs, out_refs):
        out_refs[0][...] = acc[0]
        out_refs[1][...] = extra_refs[0][...] + 0.5 * extra_refs[1][...] * acc[0]

    plain = _bspec((1, s, d), tm, tn, _ij)
    o, h_new = _matmul(
        tag + "_out", act, w_out_g.reshape(1, ff, d), "nn", tm, tn, ff,
        outs=[((1, s, d), F32, plain), ((1, s, d), F32, plain)], epilogue=residual,
        extras=[(res[None], plain), (gate, _row_spec(tn))], b_buffers=1)
    return o[0], h_new[0]


def _after_token(token):
    return token, pl.BlockSpec(token.shape, lambda j, i, k: (0, 0))


def _ffn_bwd_weights(tag, do, y, gu, act, w_in_g, w_out_g, tm):
    s, d = do.shape
    half = N_DEV // 2
    cb = w_in_g.shape[2]
    ff = half * cb

    tn_d, tm_f = _tile(d, 1024), _tile(ff, 512)
    dw_out = _matmul(tag + "_dw_out", act, do[None], "tn", tm_f, tn_d, s,
                     outs=[((1, ff, d), BF16, _bspec((1, ff, d), tm_f, tn_d, _ij))], b_buffers=1)[0]
    dw_out = dw_out.reshape(N_DEV, ff // N_DEV, d)
    out_handles, token = _exchange_begin(tag + "_scatter_out", [dw_out], True)

    def dswiglu(acc, extra_refs, out_refs):
        dact = acc[0]
        g, u = extra_refs[0][0].astype(F32), extra_refs[0][1].astype(F32)
        sg = _sigmoid(g)
        out_refs[0][0] = (dact * u * sg * (1.0 + g * (1.0 - sg))).astype(BF16)
        out_refs[0][1] = (dact * g * sg).astype(BF16)

    gu_shape = (2, 1, s, ff)
    gu_spec = _bspec(gu_shape, tm, cb, _ij)
    dgu = _matmul(tag + "_dact", do[None], w_out_g.reshape(1, ff, d), "nt", tm, cb, d,
                  outs=[(gu_shape, BF16, gu_spec)], epilogue=dswiglu, extras=[(gu, gu_spec), _after_token(token)],
                  b_buffers=1)[0]
    dgu = dgu.reshape(2, s, ff)

    tm_d = _tile(d, 512)
    dw_in = _matmul(tag + "_dw_in", y[None], dgu, "tn", tm_d, cb, s,
                    outs=[((N_DEV, d, cb), BF16, _bspec((N_DEV, d, cb), tm_d, cb, _ij))], b_buffers=1)[0]
    in_handles, token = _exchange_begin(tag + "_scatter_in", [dw_in], True)
    return dgu, in_handles + out_handles, token


def _ffn_bwd_input(tag, dgu, w_in_g, token):
    s, d = dgu.shape[1], w_in_g.shape[1]
    tm_big = _tile(s, 1024, 8)
    return _matmul(tag + "_dy", dgu, w_in_g, "nt", tm_big, d, w_in_g.shape[2],
                   outs=[((1, s, d), F32, _bspec((1, s, d), tm_big, d, _ij))], extras=[_after_token(token)])[0][0]


def kernel(x, c, w_ada, b_ada, norm_ffn1, w_ffn1_in, w_ffn1_out, norm_mix, w_in, conv_w, conv_b, w_rg_gate, b_rg_gate, w_in_gate, b_in_gate, lru_lambda, w_branch_attn, w_branch_lru, w_out, norm_ffn2, w_ffn2_in, w_ffn2_out, norm_final, loss_target, m_w_ada, m_b_ada, m_norm_ffn1, m_w_ffn1_in, m_w_ffn1_out, m_norm_mix, m_w_in, m_conv_w, m_conv_b, m_w_rg_gate, m_b_rg_gate, m_w_in_gate, m_b_in_gate, m_lru_lambda, m_w_branch_attn, m_w_branch_lru, m_w_out, m_norm_ffn2, m_w_ffn2_in, m_w_ffn2_out, m_norm_final, v_w_ada, v_b_ada, v_norm_ffn1, v_w_ffn1_in, v_w_ffn1_out, v_norm_mix, v_w_in, v_conv_w, v_conv_b, v_w_rg_gate, v_b_rg_gate, v_w_in_gate, v_b_in_gate, v_lru_lambda, v_w_branch_attn, v_w_branch_lru, v_w_out, v_norm_ffn2, v_w_ffn2_in, v_w_ffn2_out, v_norm_final):
    xs, target = x[0], loss_target[0]
    s, d = xs.shape
    aw, lw = w_branch_attn.shape[1], w_branch_lru.shape[1]
    nh, nlb = aw // HEAD_DIM, w_rg_gate.shape[1]
    cba, cbi, cbb, cwb = w_ada.shape[2], w_in.shape[2], w_branch_attn.shape[2], conv_w.shape[2]
    assert lw == nlb * LANE and cwb * N_DEV == lw and 3 * aw + 2 * lw + 2 * d == cbi * N_DEV
    me = 4 * lax.axis_index("x") + 2 * lax.axis_index("y") + lax.axis_index("c")
    tm = _tile(s, 512, 8)
    tr = _tile(s, 256, 8)
    t_attn = _tile(s, 256, 8)
    t_lru = _tile(s, 256, 8)

    small = _exchange("gather_c", [jnp.concatenate([c, conv_w.reshape(1, CONV_WIDTH * cwb)], axis=1)], False)[0][:, 0, :]
    c_all = small[:, :d]
    conv_w_full = small[:, d:].reshape(N_DEV, CONV_WIDTH, cwb).transpose(1, 0, 2).reshape(CONV_WIDTH, lw)
    c_act = _rowwise("silu_c", lambda v: v * _sigmoid(v), [(c_all, d, 0)], [], [(d, F32)], [], N_DEV)[0]

    def add_bias(acc_ref, extra_refs, out_refs):
        out_refs[0][...] = acc_ref[0] + extra_refs[0][...]

    b_ada_mine = lax.dynamic_slice(b_ada, (0, me * cba), (1, cba))
    mod_part = _matmul("mod", c_act[None], w_ada, "nn", N_DEV, cba, _tile(d, 512),
                       outs=[((1, N_DEV, cba), F32, _bspec((1, N_DEV, cba), N_DEV, cba, _ij))], epilogue=add_bias,
                       extras=[(b_ada_mine, _row_spec(cba))])[0][0]
    mod_all = _exchange("gather_mod", [mod_part], False)[0]
    mod = lax.dynamic_index_in_dim(mod_all, me, axis=1, keepdims=False).reshape(1, 9 * d)
    sh1, sc1, g1, sh2, sc2, g2, sh3, sc3, g3 = [mod[:, n * d:(n + 1) * d] for n in range(9)]

    shards = [w_ffn1_in[0], w_ffn1_out[0], w_in[0], w_branch_attn[0], w_branch_lru[0], w_out[0], w_ffn2_in[0], w_ffn2_out[0]]
    early = (0, 1, 2)
    gathers, token = _exchange_begin("gather_w", [w.astype(BF16) for w in shards], False, mod, early)

    def gathered(n, after):
        full = _exchange_end("gathered_w%d" % n, [gathers[n]], after, False)
        return (_sibling_forward("forwarded_w%d" % n, full) if n in early else full)[0]

    y1 = _norm_mod("norm1", xs, norm_ffn1 + token[:1, :1], sc1, sh1, tr)
    wf1i = gathered(0, y1)
    gu1, act1 = _ffn_in("ffn1", y1, wf1i, tm)
    wf1o = gathered(1, act1)
    o1, h1 = _ffn_out("ffn1", act1, wf1o, xs, g1, tm)

    y2 = _norm_mod("norm2", h1, norm_mix, sc2, sh2, tr)
    wi_g = gathered(2, y2)
    tn_i = _tile(cbi, 1152)
    proj = _matmul("mix_in", y2[None], wi_g, "nn", tm, tn_i, d,
                   outs=[((1, s, N_DEV * cbi), F32, _bspec((1, s, N_DEV * cbi), tm, tn_i, _ij))], b_buffers=1)[0][0]
    off_xr, off_gr, off_ga, off_gl = 3 * aw, 3 * aw + lw, 3 * aw + 2 * lw, 3 * aw + 2 * lw + d
    qkv = proj[:, :3 * aw].astype(BF16)
    y_attn, attn_tot = _attn_fwd(qkv, nh, t_attn)
    xr_pad = jnp.pad(proj[:, off_xr:off_xr + lw], ((CONV_HALO, 0), (0, 0)))
    w_r, w_i = w_rg_gate[0], w_in_gate[0]
    h_lru, h_prev, xc, y_lru = _lru_fwd(xr_pad, proj, off_gr // LANE, conv_w_full, conv_b, w_r, b_rg_gate, w_i,
                                        b_in_gate, lru_lambda, t_lru)
    wba_p = gathered(3, y_attn).transpose(1, 0, 2).reshape(1, aw, d)
    wbl_p = gathered(4, y_lru).transpose(1, 0, 2).reshape(1, lw, d)
    proj3 = proj[None]
    tm_b = _tile(s, 1024, 8)
    tn_m = _tile(math.gcd(d, off_ga, off_gl), 1024)
    plain_m = _bspec((1, s, d), tm_b, tn_m, _ij)
    gate_specs = [_bspec(proj3.shape, tm_b, tn_m, functools.partial(lambda i, j, k, o: (i, j + o), o=o // tn_m))
                  for o in (off_ga, off_gl)]
    ya = _matmul("branch_attn", y_attn[None], wba_p, "nn", tm_b, tn_m, aw, outs=[((1, s, d), F32, plain_m)], b_buffers=1)[0]

    def merge(acc_ref, extra_refs, out_refs):
        yl = acc_ref[0]
        ya_t, ga, gl = extra_refs[0][...], extra_refs[1][...], extra_refs[2][...]
        out_refs[0][...] = yl
        out_refs[1][...] = (_sigmoid(ga) * ya_t + _sigmoid(gl) * yl).astype(BF16)

    yl, merged = _matmul("branch_lru", y_lru[None], wbl_p, "nn", tm_b, tn_m, lw,
                         outs=[((1, s, d), F32, plain_m), ((1, s, d), BF16, plain_m)], epilogue=merge,
                         extras=[(ya, plain_m), (proj3, gate_specs[0]), (proj3, gate_specs[1])], b_buffers=1)
    tn_d = _tile(d, 1024)
    plain = _bspec((1, s, d), tm, tn_d, _ij)

    def residual(acc_ref, extra_refs, out_refs):
        o = acc_ref[0]
        out_refs[0][...] = o
        out_refs[1][...] = extra_refs[0][...] + extra_refs[1][...] * o

    wo_g = gathered(5, merged)
    mo, h2 = _matmul("mix_out", merged, wo_g.reshape(1, d, d), "nn", tm, tn_d, d,
                     outs=[((1, s, d), F32, plain), ((1, s, d), F32, plain)], epilogue=residual,
                     extras=[(h1[None], plain), (g2, _row_spec(tn_d))], b_buffers=1)
    mo, h2 = mo[0], h2[0]

    y3 = _norm_mod("norm3", h2, norm_ffn2, sc3, sh3, tr)
    wf2i = gathered(6, y3)
    gu3, act3 = _ffn_in("ffn2", y3, wf2i, tm)
    wf2o = gathered(7, act3)
    o3, h3 = _ffn_out("ffn2", act3, wf2o, h2, g3, tm)

    nf = norm_final.reshape(1, d)
    dh3, do3, loss_part, d_nf, dg3 = _loss_bwd("loss", h3, target, nf, o3, g3, tr)
    dgu3, scatter_ffn2, token = _ffn_bwd_weights("ffn2", do3, y3, gu3, act3, wf2i, wf2o, tm)
    dy3 = _ffn_bwd_input("ffn2", dgu3, wf2i, token)
    dh2, dmo, dsh3, dsc3, dn3, dg2 = _norm_mod_bwd("norm3_bwd", dy3, h2, dh3, norm_ffn2, sc3, tr, below=(mo, g2, 1.0))

    dwo = _matmul("mix_dw_out", merged, dmo[None], "tn", _tile(d, 512), tn_d, s,
                  outs=[((1, d, d), BF16, _bspec((1, d, d), _tile(d, 512), tn_d, _ij))], b_buffers=1)[0]

    def dmerge(acc_ref, extra_refs, out_refs):
        dm = acc_ref[0]
        ya_t, yl_t = extra_refs[0][...], extra_refs[1][...]
        sa, sl = _sigmoid(extra_refs[2][...]), _sigmoid(extra_refs[3][...])
        out_refs[0][...] = (dm * sa).astype(BF16)
        out_refs[1][...] = (dm * sl).astype(BF16)
        out_refs[2][...] = (dm * ya_t * sa * (1.0 - sa)).astype(BF16)
        out_refs[3][...] = (dm * yl_t * sl * (1.0 - sl)).astype(BF16)

    tn_m = _tile(math.gcd(d, off_ga, off_gl), 1024)
    plain_m = _bspec((1, s, d), tm, tn_m, _ij)
    gate_specs = [_bspec(proj3.shape, tm, tn_m, functools.partial(lambda i, j, k, o: (i, j + o), o=o // tn_m))
                  for o in (off_ga, off_gl)]
    dya, dyl, dga, dgl = _matmul("mix_dmerged", dmo[None], wo_g.reshape(1, d, d), "nt", tm, tn_m, d,
                                 outs=[((1, s, d), BF16, plain_m)] * 4, epilogue=dmerge,
                                 extras=[(ya, plain_m), (yl, plain_m), (proj3, gate_specs[0]), (proj3, gate_specs[1])],
                                 b_buffers=1)
    tm_a, tm_l = _tile(aw, 1024), _tile(lw, 1024)
    dwba = _matmul("dw_branch_attn", y_attn[None], dya, "tn", tm_a, cbb, s,
                   outs=[((N_DEV, aw, cbb), BF16, _bspec((N_DEV, aw, cbb), tm_a, cbb, _ij))])[0]
    dwbl = _matmul("dw_branch_lru", y_lru[None], dyl, "tn", tm_l, cbb, s,
                   outs=[((N_DEV, lw, cbb), BF16, _bspec((N_DEV, lw, cbb), tm_l, cbb, _ij))])[0]
    scatter_branch, token = _exchange_begin("scatter_branch", [dwba, dwbl, dwo.reshape(N_DEV, d // N_DEV, d)], True)
    tn_a, tn_l = _tile(aw, 1024), _tile(lw, 1024)
    dy_attn = _matmul("d_attn_out", dya, wba_p, "nt", tm_b, tn_a, d,
                      outs=[((1, s, aw), BF16, _bspec((1, s, aw), tm_b, tn_a, _ij))], extras=[_after_token(token)],
                      b_buffers=1)[0][0]
    dy_lru = _matmul("d_lru_out", dyl, wbl_p, "nt", tm_b, tn_l, d,
                     outs=[((1, s, lw), F32, _bspec((1, s, lw), tm_b, tn_l, _ij))], b_buffers=1)[0][0]
    dq, dk, dv = _attn_bwd(qkv, dy_attn, attn_tot, nh, t_attn)
    dgr, dxc, d_wr, d_wi, d_br, d_bi, d_lam = _lru_bwd(dy_lru, proj, off_gr // LANE, h_lru, h_prev, xc, w_r, b_rg_gate,
                                                       w_i, b_in_gate, lru_lambda, t_lru)
    dxr, d_cw, d_cb = _conv_bwd(xr_pad, jnp.pad(dxc, ((0, CONV_HALO), (0, 0))), conv_w_full, t_lru)
    dproj = jnp.concatenate([dq.astype(BF16), dk.astype(BF16), dv.astype(BF16), dxr.astype(BF16), dgr.astype(BF16),
                             dga[0], dgl[0]], axis=1)
    tm_d = _tile(d, 512)
    dwi = _matmul("mix_dw_in", y2[None], dproj[None], "tn", tm_d, tn_i, s,
                  outs=[((N_DEV, d, cbi), BF16, _bspec((N_DEV, d, cbi), tm_d, tn_i, _ij))], b_buffers=1)[0]
    scatter_mix, token = _exchange_begin("scatter_mix", [dwi], True)
    tm_big = _tile(s, 1024, 8)
    dy2 = _matmul("mix_dy", dproj[None], wi_g, "nt", tm_big, d, tn_i,
                  outs=[((1, s, d), F32, _bspec((1, s, d), tm_big, d, _ij))], extras=[_after_token(token)])[0][0]
    dh1, do1, dsh2, dsc2, dn2, dg1 = _norm_mod_bwd("norm2_bwd", dy2, h1, dh2, norm_mix, sc2, tr, below=(o1, g1, 0.5))

    dgu1, scatter_ffn1, token = _ffn_bwd_weights("ffn1", do1, y1, gu1, act1, wf1i, wf1o, tm)
    dy1 = _ffn_bwd_input("ffn1", dgu1, wf1i, token)
    grad_x, dsh1, dsc1, dn1 = _norm_mod_bwd("norm1_bwd", dy1, xs, dh1, norm_ffn1, sc1, tr)

    results = {}

    def update_group(wait_name, handles, leaves, after):
        for (n, w, m, v), parts in zip(leaves, _exchange_end(wait_name, handles, after, True)):
            results[n] = [o[None] for o in _adamw("adamw_" + n, parts, w[0], m[0], v[0])]
        return results[leaves[-1][0]][0]

    done = update_group("scattered_ffn2", scatter_ffn2, [("w_ffn2_in", w_ffn2_in, m_w_ffn2_in, v_w_ffn2_in),
                                                         ("w_ffn2_out", w_ffn2_out, m_w_ffn2_out, v_w_ffn2_out)], grad_x)
    done = update_group("scattered_mix", scatter_branch + scatter_mix,
                        [("w_branch_attn", w_branch_attn, m_w_branch_attn, v_w_branch_attn),
                         ("w_branch_lru", w_branch_lru, m_w_branch_lru, v_w_branch_lru),
                         ("w_out", w_out, m_w_out, v_w_out), ("w_in", w_in, m_w_in, v_w_in)], done)

    lane_pad = jnp.zeros((1, 7 * LANE), F32)
    pack = jnp.concatenate(
        [loss_part, lane_pad, dsh1, dsc1, dg1, dsh2, dsc2, dg2, dsh3, dsc3, dg3, dn1, dn2, dn3, d_nf, d_cb, d_br, d_bi, d_lam,
         d_cw.reshape(1, -1)], axis=1)
    pack = jnp.pad(pack, ((0, 0), (0, -pack.shape[1] % (8 * LANE))))
    gate_pack = jnp.concatenate([d_wr.reshape(-1, LANE), d_wi.reshape(-1, LANE)], axis=0).astype(BF16)
    n_pack, n_gate = pack.shape[1], gate_pack.size
    updated = sum(results[n][0][0, 0, 0] for n in sorted(results)).reshape(1, 1)
    small_handles, _ = _exchange_begin("gather_small", [pack, gate_pack], False, updated)

    done = update_group("scattered_ffn1", scatter_ffn1, [("w_ffn1_in", w_ffn1_in, m_w_ffn1_in, v_w_ffn1_in),
                                                         ("w_ffn1_out", w_ffn1_out, m_w_ffn1_out, v_w_ffn1_out)], updated)
    packs, gate_packs = _exchange_end("gathered_small", small_handles, done, False)
    packs = packs.reshape(N_DEV, n_pack // LANE, LANE)
    g_pack = _reduce_parts("sum_small", packs).reshape(1, n_pack)
    g_gate = _reduce_parts("sum_gates", gate_packs).reshape(1, n_gate)
    loss = g_pack[0, 0]
    off = 8 * LANE
    n_vec = 9 * d + 4 * d + 4 * lw
    n_adam = n_vec + n_gate
    g_small = jnp.concatenate([g_pack[:, off:off + n_vec], g_gate], axis=1).reshape(1, n_adam // LANE, LANE)
    d_cw_sum = g_pack[:, off + n_vec:off + n_vec + CONV_WIDTH * lw].reshape(CONV_WIDTH, lw)
    d_cw_mine = lax.dynamic_slice(d_cw_sum, (0, me * cwb), (CONV_WIDTH, cwb))

    small_names = ["b_ada", "norm_ffn1", "norm_mix", "norm_ffn2", "norm_final", "conv_b", "b_rg_gate", "b_in_gate",
                   "lru_lambda", "w_rg_gate", "w_in_gate"]
    given = dict(b_ada=(b_ada, m_b_ada, v_b_ada), norm_ffn1=(norm_ffn1, m_norm_ffn1, v_norm_ffn1),
                 norm_mix=(norm_mix, m_norm_mix, v_norm_mix), norm_ffn2=(norm_ffn2, m_norm_ffn2, v_norm_ffn2),
                 norm_final=(norm_final, m_norm_final, v_norm_final), conv_b=(conv_b, m_conv_b, v_conv_b),
                 b_rg_gate=(b_rg_gate, m_b_rg_gate, v_b_rg_gate), b_in_gate=(b_in_gate, m_b_in_gate, v_b_in_gate),
                 lru_lambda=(lru_lambda, m_lru_lambda, v_lru_lambda), w_rg_gate=(w_rg_gate, m_w_rg_gate, v_w_rg_gate),
                 w_in_gate=(w_in_gate, m_w_in_gate, v_w_in_gate))
    packed = [jnp.concatenate([given[n][q].reshape(1, -1) for n in small_names], axis=1).reshape(n_adam // LANE, LANE)
              for q in range(3)]
    small_out = _adamw("adamw_small", g_small, *packed)
    pos = 0
    for n in small_names:
        shape = given[n][0].shape
        size = math.prod(shape)
        results[n] = [o.reshape(1, n_adam)[:, pos:pos + size].reshape(shape) for o in small_out]
        pos += size
    results["conv_w"] = [o.reshape(conv_w.shape) for o in
                         _adamw("adamw_conv_w", d_cw_mine[None], conv_w[0], m_conv_w[0], v_conv_w[0])]

    dmod_all = packs.reshape(N_DEV, n_pack)[:, off:off + 9 * d]
    dmod_mine = lax.dynamic_slice(dmod_all, (0, me * cba), (N_DEV, cba))
    dmod_rows = jnp.pad(dmod_mine, ((0, LANE - N_DEV), (0, 0)))
    c_act_t = jnp.pad(c_act.T, ((0, 0), (0, LANE - N_DEV)))
    tm_d2 = _tile(d, 256)
    d_wada = _matmul("dw_ada", c_act_t[None], dmod_rows[None], "nn", tm_d2, cba, LANE,
                     outs=[((1, d, cba), F32, _bspec((1, d, cba), tm_d2, cba, _ij))])[0]
    results["w_ada"] = [o[None] for o in _adamw("adamw_w_ada", d_wada, w_ada[0], m_w_ada[0], v_w_ada[0])]


    order = ["w_ada", "b_ada", "norm_ffn1", "w_ffn1_in", "w_ffn1_out", "norm_mix", "w_in", "conv_w", "conv_b", "w_rg_gate",
             "b_rg_gate", "w_in_gate", "b_in_gate", "lru_lambda", "w_branch_attn", "w_branch_lru", "w_out", "norm_ffn2",
             "w_ffn2_in", "w_ffn2_out", "norm_final"]
    return (loss, grad_x[None], *[results[n][0] for n in order], *[results[n][1] for n in order],
            *[results[n][2] for n in order], *[results[n][3] for n in order])
```

```python
import functools
import math

import jax
import jax.numpy as jnp
from jax import lax
from jax.experimental import pallas as pl
from jax.experimental.pallas import tpu as pltpu

F32 = jnp.float32
BF16 = jnp.bfloat16
N_DEV = 8
HEAD_DIM = 128
CONV_WIDTH = 4
CONV_HALO = 8
LRU_C = 8.0
EPS = 1e-6
ADAM_LR, ADAM_B1, ADAM_B2, ADAM_EPS, ADAM_WD, ADAM_STEP = 0.001, 0.9, 0.999, 1e-08, 0.01, 10
LANE = 128
VMEM_LIMIT = 56 * 1024 * 1024
MESH = pl.DeviceIdType.MESH

NT = (((1,), (1,)), ((), ()))
NN = (((1,), (0,)), ((), ()))
TN = (((0,), (0,)), ((), ()))


def _tile(dim, target, align=LANE):
    t = (min(target, dim) // align) * align
    while t >= align:
        if dim % t == 0:
            return t
        t -= align
    return dim


def _params(sem):
    return pltpu.CompilerParams(dimension_semantics=sem, vmem_limit_bytes=VMEM_LIMIT)


def _sigmoid(x):
    return 1.0 / (1.0 + jnp.exp(-x))


def _softplus(x):
    return jnp.maximum(x, 0.0) + jnp.log(1.0 + jnp.exp(-jnp.abs(x)))


def _log1p(z):
    w = 1.0 + z
    return jnp.where(w == 1.0, z, jnp.log(w) * z / jnp.where(w == 1.0, 1.0, w - 1.0))


def _expm1(x):
    poly = x * (1.0 + x * (0.5 + x * (1.0 / 6 + x * (1.0 / 24 + x * (1.0 / 120 + x * (1.0 / 720))))))
    return jnp.where(jnp.abs(x) < 0.25, poly, jnp.exp(x) - 1.0)


_GELU_C = math.sqrt(2.0 / math.pi)


def _gelu_and_grad(x):
    inner = _GELU_C * (x + 0.044715 * x * x * x)
    th = jnp.tanh(inner)
    val = 0.5 * x * (1.0 + th)
    grad = 0.5 * (1.0 + th) + 0.5 * x * (1.0 - th * th) * _GELU_C * (1.0 + 3 * 0.044715 * x * x)
    return val, grad


def _dot_split(x, u):
    hi = x.astype(BF16)
    lo = (x - hi.astype(F32)).astype(BF16)
    return jnp.dot(hi, u, preferred_element_type=F32) + jnp.dot(lo, u, preferred_element_type=F32)


def _mesh_position():
    x, y, c = lax.axis_index("x"), lax.axis_index("y"), lax.axis_index("c")
    return x, y, c, 4 * x + 2 * y + c


def _peers(x, y, c):
    out = []
    for mask in range(1, N_DEV):
        px = 1 - x if mask & 4 else x
        py = 1 - y if mask & 2 else y
        pc = 1 - c if mask & 1 else c
        out.append((mask, (px, py, pc), 4 * px + 2 * py + pc))
    return out


def _exchange(name, arrs, scatter, after=None):
    n = len(arrs)
    behind = [] if after is None else [after]

    def body(*refs):
        ins, outs = refs[:n], refs[n + len(behind):2 * n + len(behind)]
        send_sems, recv_sems, local_sems = refs[2 * n + len(behind):]
        x, y, c, me = _mesh_position()
        peers = _peers(x, y, c)
        waits = []
        for a in range(n):
            mine = ins[a].at[me] if scatter else ins[a]
            local = pltpu.make_async_copy(mine, outs[a].at[me], local_sems.at[a])
            local.start()
            waits.append(local.wait)
            for mask, dev, idx in peers:
                k = a * (N_DEV - 1) + mask - 1
                src = ins[a].at[idx] if scatter else ins[a]
                send = pltpu.make_async_remote_copy(src_ref=src, dst_ref=outs[a].at[me], send_sem=send_sems.at[k],
                                                    recv_sem=recv_sems.at[k], device_id=dev, device_id_type=MESH)
                send.start()
                arrival = pltpu.make_async_remote_copy(src_ref=src, dst_ref=outs[a].at[idx], send_sem=send_sems.at[k],
                                                       recv_sem=recv_sems.at[k], device_id=dev, device_id_type=MESH)
                waits.append(send.wait_send)
                waits.append(arrival.wait_recv)
        for w in waits:
            w()

    any_spec = pl.BlockSpec(memory_space=pl.ANY)
    out_shape = [jax.ShapeDtypeStruct(a.shape if scatter else (N_DEV,) + a.shape, a.dtype) for a in arrs]
    return pl.pallas_call(
        body, name=name, out_shape=out_shape, in_specs=[any_spec] * (n + len(behind)), out_specs=[any_spec] * n,
        scratch_shapes=[pltpu.SemaphoreType.DMA((n * (N_DEV - 1),)), pltpu.SemaphoreType.DMA((n * (N_DEV - 1),)),
                        pltpu.SemaphoreType.DMA((n,))],
    )(*arrs, *behind)


HBM_SPEC = pl.BlockSpec(memory_space=pltpu.HBM)
SEM_SPEC = pl.BlockSpec(memory_space=pltpu.SEMAPHORE)
DATAFLOW = pltpu.SideEffectType.DATAFLOW_SIDE_EFFECTING


ALL_MASKS = tuple(range(1, N_DEV))
SAME_CORE_MASKS = (1, 2, 4, 6)


def _exchange_begin(name, arrs, scatter, after=None, once_per_chip=()):
    n = len(arrs)
    lands = [lax.empty(a.shape if scatter else (N_DEV,) + a.shape, a.dtype) for a in arrs]
    behind = [] if after is None else [after]
    masks = [SAME_CORE_MASKS if a in once_per_chip else ALL_MASKS for a in range(n)]

    def body(*refs):
        srcs, zones, outs = refs[:n], refs[n:2 * n], refs[2 * n + len(behind):]
        x, y, c, me = _mesh_position()
        for a in range(n):
            send_sems, recv_sems = outs[4 * a], outs[4 * a + 1]
            for mask, dev, idx in _peers(x, y, c):
                if mask not in masks[a]:
                    continue
                pltpu.make_async_remote_copy(
                    src_ref=srcs[a].at[idx] if scatter else srcs[a], dst_ref=zones[a].at[me], send_sem=send_sems.at[mask - 1],
                    recv_sem=recv_sems.at[mask - 1], device_id=dev, device_id_type=MESH).start()
        outs[-1][...] = jnp.zeros_like(outs[-1])

    out_shape, out_specs, aliases = [], [], {}
    for a in range(n):
        out_shape += [pltpu.SemaphoreType.DMA((N_DEV - 1,)), pltpu.SemaphoreType.DMA((N_DEV - 1,)),
                      pltpu.HBM(arrs[a].shape, arrs[a].dtype), pltpu.HBM(lands[a].shape, lands[a].dtype)]
        out_specs += [SEM_SPEC, SEM_SPEC, HBM_SPEC, HBM_SPEC]
        aliases[a] = 4 * a + 2
        aliases[n + a] = 4 * a + 3
    out_shape.append(jax.ShapeDtypeStruct((8, LANE), F32))
    out_specs.append(pl.BlockSpec(memory_space=pltpu.VMEM))
    res = pl.pallas_call(
        body, name=name, out_shape=out_shape,
        in_specs=[HBM_SPEC] * (2 * n) + [pl.BlockSpec(memory_space=pl.ANY)] * len(behind),
        out_specs=out_specs, input_output_aliases=aliases, compiler_params=pltpu.CompilerParams(has_side_effects=DATAFLOW),
    )(*[pltpu.with_memory_space_constraint(v, pltpu.HBM) for v in list(arrs) + lands], *behind)
    return [tuple(res[4 * a:4 * a + 4]) + (masks[a],) for a in range(n)], res[-1]


def _exchange_end(name, handles, after, scatter):
    n = len(handles)
    me = 4 * lax.axis_index("x") + 2 * lax.axis_index("y") + lax.axis_index("c")

    def body(*refs):
        x, y, c, me = _mesh_position()
        for a in range(n):
            src, zone, send_sems, recv_sems = refs[4 * a:4 * a + 4]
            for mask, dev, idx in _peers(x, y, c):
                if mask not in handles[a][4]:
                    continue
                cp = pltpu.make_async_remote_copy(
                    src_ref=src.at[idx] if scatter else src, dst_ref=zone.at[idx], send_sem=send_sems.at[mask - 1],
                    recv_sem=recv_sems.at[mask - 1], device_id=dev, device_id_type=MESH)
                cp.wait_send()
                cp.wait_recv()

    operands, in_specs, out_shape, aliases = [], [], [], {}
    for a, (send_sems, recv_sems, src, zone, _) in enumerate(handles):
        operands += [src, zone, send_sems, recv_sems]
        in_specs += [HBM_SPEC, HBM_SPEC, SEM_SPEC, SEM_SPEC]
        out_shape += [pltpu.HBM(src.shape, src.dtype), pltpu.HBM(zone.shape, zone.dtype)]
        aliases[4 * a] = 2 * a
        aliases[4 * a + 1] = 2 * a + 1
    res = pl.pallas_call(
        body, name=name, out_shape=out_shape, in_specs=in_specs + [pl.BlockSpec(memory_space=pl.ANY)],
        out_specs=[HBM_SPEC] * (2 * n), input_output_aliases=aliases,
        compiler_params=pltpu.CompilerParams(has_side_effects=DATAFLOW),
    )(*operands, after)
    full = []
    for a in range(n):
        src, zone = res[2 * a], res[2 * a + 1]
        own = lax.dynamic_index_in_dim(src, me, 0, keepdims=False) if scatter else src
        full.append(lax.dynamic_update_index_in_dim(zone, own, me, 0))
    return full


def _sibling_forward(name, zones):
    n = len(zones)
    hops = (2, 4, 6)

    def body(*refs):
        outs, send_sems, recv_sems = refs[n:2 * n], refs[2 * n], refs[2 * n + 1]
        x, y, c, me = _mesh_position()
        sibling = (x, y, 1 - c)
        waits = []
        for a in range(n):
            for q, mask in enumerate(hops):
                chip = 4 * (1 - x if mask & 4 else x) + 2 * (1 - y if mask & 2 else y)
                k = a * len(hops) + q
                held, missing = outs[a].at[chip + c], outs[a].at[chip + 1 - c]
                send = pltpu.make_async_remote_copy(src_ref=held, dst_ref=held, send_sem=send_sems.at[k],
                                                    recv_sem=recv_sems.at[k], device_id=sibling, device_id_type=MESH)
                send.start()
                arrival = pltpu.make_async_remote_copy(src_ref=missing, dst_ref=missing, send_sem=send_sems.at[k],
                                                       recv_sem=recv_sems.at[k], device_id=sibling, device_id_type=MESH)
                waits += [send.wait_send, arrival.wait_recv]
        for w in waits:
            w()

    any_spec = pl.BlockSpec(memory_space=pl.ANY)
    return pl.pallas_call(
        body, name=name, out_shape=[jax.ShapeDtypeStruct(z.shape, z.dtype) for z in zones], in_specs=[any_spec] * n,
        out_specs=[any_spec] * n, input_output_aliases={a: a for a in range(n)},
        scratch_shapes=[pltpu.SemaphoreType.DMA((n * len(hops),)), pltpu.SemaphoreType.DMA((n * len(hops),))],
    )(*zones)


def _bspec(shape, tr, tc, rc, buffers=None):
    per = shape[-1] // tc
    mode = {} if buffers is None else dict(pipeline_mode=pl.Buffered(buffers))
    if len(shape) == 3:
        return pl.BlockSpec((None, tr, tc), lambda j, i, k: (rc(i, j, k)[1] // per, rc(i, j, k)[0], rc(i, j, k)[1] % per), **mode)
    return pl.BlockSpec((shape[0], None, tr, tc),
                        lambda j, i, k: (0, rc(i, j, k)[1] // per, rc(i, j, k)[0], rc(i, j, k)[1] % per), **mode)


def _ij(i, j, k):
    return i, j


def _row_spec(tn, col_tile_offset=0):
    return pl.BlockSpec((1, tn), lambda j, i, k: (0, j + col_tile_offset))


def _matmul(name, a, b, mode, tm, tn, tk, outs, epilogue=None, extras=(), b_buffers=None):
    groups = b.shape[0] if b.ndim == 4 else 1
    if mode == "nn":
        m, k_dim, n = a.shape[1], a.shape[0] * a.shape[2], b.shape[-3] * b.shape[-1]
        a_spec = _bspec(a.shape, tm, tk, lambda i, j, k: (i, k))
        b_spec = _bspec(b.shape, tk, tn, lambda i, j, k: (k, j), b_buffers)
        dims = NN
    elif mode == "nt":
        m, k_dim, n = a.shape[1], a.shape[0] * a.shape[2], b.shape[-2]
        a_spec = _bspec(a.shape, tm, tk, lambda i, j, k: (i, k))
        b_spec = _bspec(b.shape, tn, tk, lambda i, j, k: (j, k), b_buffers)
        dims = NT
    else:
        m, k_dim, n = a.shape[0] * a.shape[2], a.shape[1], b.shape[-3] * b.shape[-1]
        a_spec = _bspec(a.shape, tk, tm, lambda i, j, k: (k, i))
        b_spec = _bspec(b.shape, tk, tn, lambda i, j, k: (k, j), b_buffers)
        dims = TN
    assert m % tm == 0 and n % tn == 0 and k_dim % tk == 0, (name, m, n, k_dim, tm, tn, tk)
    nk = k_dim // tk
    n_extra, n_out = len(extras), len(outs)

    def finish(acc, extra_refs, out_refs):
        if epilogue is None:
            out_refs[0][...] = acc[0].astype(out_refs[0].dtype)
        else:
            epilogue(acc, extra_refs, out_refs)

    def products(a_ref, b_ref):
        a_tile = a_ref[...].astype(BF16)
        return [lax.dot_general(a_tile, (b_ref[g] if b.ndim == 4 else b_ref[...]).astype(BF16), dims,
                                preferred_element_type=F32) for g in range(groups)]

    def body_whole_k(*refs):
        finish(products(refs[0], refs[1]), refs[2:2 + n_extra], refs[2 + n_extra:])

    def body_k_steps(*refs):
        acc_ref = refs[-1]
        k = pl.program_id(2)

        @pl.when(k == 0)
        def _():
            acc_ref[...] = jnp.zeros_like(acc_ref)

        for g, p in enumerate(products(refs[0], refs[1])):
            acc_ref[g] += p

        @pl.when(k == nk - 1)
        def _():
            finish([acc_ref[g] for g in range(groups)], refs[2:2 + n_extra], refs[2 + n_extra:2 + n_extra + n_out])

    return pl.pallas_call(
        body_whole_k if nk == 1 else body_k_steps, name=name, grid=(n // tn, m // tm, nk),
        in_specs=[a_spec, b_spec] + [s for _, s in extras],
        out_specs=[s for _, _, s in outs],
        out_shape=[jax.ShapeDtypeStruct(shape, dtype) for shape, dtype, _ in outs],
        scratch_shapes=[] if nk == 1 else [pltpu.VMEM((groups, tm, tn), F32)],
        compiler_params=_params(("parallel", "parallel", "arbitrary")),
    )(a, b, *[arr for arr, _ in extras])


def _rowwise(name, fn, rows, vecs, outs, accs, tm):
    s = rows[0][0].shape[0]
    n_in, n_out = len(rows) + len(vecs), len(outs)

    def body(*refs):
        i = pl.program_id(0)
        res = fn(*[r[...] for r in refs[:n_in]])
        res = res if isinstance(res, tuple) else (res,)
        out_refs, acc_refs = refs[n_in:n_in + n_out], refs[n_in + n_out:]
        for ref, val in zip(out_refs, res[:n_out]):
            ref[...] = val.astype(ref.dtype)

        @pl.when(i == 0)
        def _():
            for ref in acc_refs:
                ref[...] = jnp.zeros_like(ref)

        for ref, val in zip(acc_refs, res[n_out:]):
            ref[...] += val

    in_specs = [pl.BlockSpec((tm, w), functools.partial(lambda i, cb: (i, cb), cb=cb)) for _, w, cb in rows]
    in_specs += [pl.BlockSpec(v.shape, lambda i: (0,) * v.ndim) for v in vecs]
    out_specs = [pl.BlockSpec((tm, w), lambda i: (i, 0)) for w, _ in outs] + [pl.BlockSpec((1, w), lambda i: (0, 0)) for w in accs]
    out_shape = [jax.ShapeDtypeStruct((s, w), dt) for w, dt in outs] + [jax.ShapeDtypeStruct((1, w), F32) for w in accs]
    return pl.pallas_call(
        body, name=name, grid=(s // tm,), in_specs=in_specs, out_specs=out_specs, out_shape=out_shape,
        compiler_params=_params(("arbitrary",)),
    )(*[r for r, _, _ in rows], *vecs)


def _colsum(v):
    return jnp.sum(v, axis=0, keepdims=True)


def _norm_mod(name, h, nw, sc, sh, tm):
    d = h.shape[1]

    def fn(hb, nwb, scb, shb):
        r = lax.rsqrt(jnp.mean(hb * hb, axis=-1, keepdims=True) + EPS)
        return (hb * r) * nwb * (1.0 + scb) + shb

    return _rowwise(name, fn, [(h, d, 0)], [nw, sc, sh], [(d, BF16)], [], tm)[0]


def _norm_mod_bwd(name, dy, h, dh_next, nw, sc, tm, below=None):
    d = h.shape[1]

    def fn(dyb, hb, dhb, *rest):
        nwb, scb = rest[-2:] if below is None else rest[1:3]
        r = lax.rsqrt(jnp.mean(hb * hb, axis=-1, keepdims=True) + EPS)
        xh = hb * r
        dxh = dyb * (nwb * (1.0 + scb))
        dx = r * (dxh - xh * jnp.mean(dxh * xh, axis=-1, keepdims=True))
        dh = dhb + dx
        sums = (_colsum(dyb), _colsum(dyb * xh * nwb), _colsum(dyb * xh * (1.0 + scb)))
        if below is None:
            return (dh,) + sums
        ob, gb = rest[0], rest[3]
        return (dh, dh * (below[2] * gb)) + sums + (_colsum(dh * ob * below[2]),)

    rows = [(dy, d, 0), (h, d, 0), (dh_next, d, 0)]
    if below is None:
        return _rowwise(name, fn, rows, [nw, sc], [(d, F32)], [d, d, d], tm)
    return _rowwise(name, fn, rows + [(below[0], d, 0)], [nw, sc, below[1]], [(d, F32), (d, BF16)], [d, d, d, d], tm)


def _loss_bwd(name, h, target, nw, o, g, tm):
    d = h.shape[1]

    def fn(hb, tb, ob, nwb, gb):
        r = lax.rsqrt(jnp.mean(hb * hb, axis=-1, keepdims=True) + EPS)
        xh = hb * r
        err = xh * nwb - tb
        dy = err * (1.0 / d)
        dxh = dy * nwb
        dx = r * (dxh - xh * jnp.mean(dxh * xh, axis=-1, keepdims=True))
        loss = 0.5 * jnp.sum(jnp.mean(err * err, axis=-1, keepdims=True), axis=0, keepdims=True)
        return dx, dx * (0.5 * gb), jnp.broadcast_to(loss, (1, LANE)), _colsum(dy * xh), _colsum(dx * ob * 0.5)

    return _rowwise(name, fn, [(h, d, 0), (target, d, 0), (o, d, 0)], [nw, g], [(d, F32), (d, BF16)], [LANE, d, d], tm)


def _head_group(nh, most):
    return max(g for g in (1, 2, 4) if g <= most and nh % g == 0)


def _attn_fwd(qkv, nh, t):
    s = qkv.shape[0]
    scale = HEAD_DIM ** -0.5
    hp = _head_group(nh, 4)
    wide = hp * HEAD_DIM
    lanes = [slice(u * HEAD_DIM, (u + 1) * HEAD_DIM) for u in range(hp)]

    def body(q_ref, k_ref, v_ref, y_ref, tot_ref):
        i = pl.program_id(1)
        row = lax.broadcasted_iota(jnp.int32, (t, t), 0)
        col = lax.broadcasted_iota(jnp.int32, (t, t), 1)
        later = (row > col).astype(BF16)
        causal = col < row
        qs = [q_ref[:, ln] for ln in lanes]

        def block(j, carry, diagonal):
            ks = pl.ds(pl.multiple_of(j * t, t), t)
            heads = range(hp)
            z = [lax.dot_general(qs[u], k_ref[ks, lanes[u]], NT, preferred_element_type=F32) * scale for u in heads]
            sp = [_softplus(z[u]) for u in heads]
            log_keep = [jnp.where(causal, -sp[u], 0.0) if diagonal else -sp[u] for u in heads]
            between = [_dot_split(log_keep[u], later) for u in heads]
            w = [jnp.exp(z[u] - sp[u] + between[u] + carry[u][1]) for u in heads]
            if diagonal:
                w = [jnp.where(causal, w[u], 0.0) for u in heads]
            o = [carry[u][0] + jnp.dot(w[u].astype(BF16), v_ref[ks, lanes[u]], preferred_element_type=F32) for u in heads]
            return tuple((o[u], carry[u][1] + jnp.sum(log_keep[u], axis=1, keepdims=True)) for u in heads)

        carry = tuple((jnp.zeros((t, HEAD_DIM), F32), jnp.zeros((t, 1), F32)) for _ in lanes)
        carry = block(i, carry, True)
        carry = lax.fori_loop(0, i, lambda jj, cr: block(i - 1 - jj, cr, False), carry)
        for u, ln in enumerate(lanes):
            y_ref[:, ln] = carry[u][0].astype(y_ref.dtype)
            tot_ref[:, ln] = jnp.broadcast_to(carry[u][1], (t, HEAD_DIM))

    g = nh // hp
    return pl.pallas_call(
        body, name="attn_fwd", grid=(g, s // t),
        in_specs=[pl.BlockSpec((t, wide), lambda h, i: (i, h)),
                  pl.BlockSpec((s, wide), lambda h, i: (0, g + h)),
                  pl.BlockSpec((s, wide), lambda h, i: (0, 2 * g + h))],
        out_specs=[pl.BlockSpec((t, wide), lambda h, i: (i, h)), pl.BlockSpec((t, wide), lambda h, i: (i, h))],
        out_shape=[jax.ShapeDtypeStruct((s, nh * HEAD_DIM), BF16), jax.ShapeDtypeStruct((s, nh * HEAD_DIM), F32)],
        compiler_params=_params(("parallel", "arbitrary")),
    )(qkv, qkv, qkv)


def _attn_bwd(qkv, dy, tot, nh, t):
    s = qkv.shape[0]
    scale = HEAD_DIM ** -0.5
    hp = _head_group(nh, 4)
    wide = hp * HEAD_DIM
    lanes = [slice(u * HEAD_DIM, (u + 1) * HEAD_DIM) for u in range(hp)]

    def body(q_ref, k_ref, v_ref, dy_ref, tot_ref, dq_ref, dk_out, dv_out, dk_ref, dv_ref):
        i = pl.program_id(1)

        @pl.when(i == 0)
        def _():
            dk_ref[...] = jnp.zeros_like(dk_ref)
            dv_ref[...] = jnp.zeros_like(dv_ref)

        row = lax.broadcasted_iota(jnp.int32, (t, t), 0)
        col = lax.broadcasted_iota(jnp.int32, (t, t), 1)
        upto = (row <= col).astype(BF16)
        before = (row < col).astype(BF16)
        causal = col < row
        qs = [q_ref[:, ln] for ln in lanes]
        dys = [dy_ref[:, ln] for ln in lanes]
        totals = [tot_ref[:, u * HEAD_DIM:u * HEAD_DIM + 1] for u in range(hp)]

        def block(j, carry, diagonal):
            ks = pl.ds(pl.multiple_of(j * t, t), t)
            heads = range(hp)
            kb = [k_ref[ks, ln] for ln in lanes]
            vb = [v_ref[ks, ln] for ln in lanes]
            z = [lax.dot_general(qs[u], kb[u], NT, preferred_element_type=F32) * scale for u in heads]
            dw = [lax.dot_general(dys[u], vb[u], NT, preferred_element_type=F32) for u in heads]
            sp = [_softplus(z[u]) for u in heads]
            log_keep = [jnp.where(causal, -sp[u], 0.0) if diagonal else -sp[u] for u in heads]
            upto_sum = [_dot_split(log_keep[u], upto) for u in heads]
            w = [jnp.exp(z[u] - sp[u] + (totals[u] - carry[u][1] - upto_sum[u])) for u in heads]
            if diagonal:
                w = [jnp.where(causal, w[u], 0.0) for u in heads]
            g = [dw[u] * w[u] for u in heads]
            g_before = [_dot_split(g[u], before) for u in heads]
            dz = [(g[u] * jnp.exp(-sp[u]) - jnp.exp(z[u] - sp[u]) * (carry[u][2] + g_before[u])) * scale for u in heads]
            if diagonal:
                dz = [jnp.where(causal, dz[u], 0.0) for u in heads]
            dzb = [dz[u].astype(BF16) for u in heads]
            dq = [carry[u][0] + jnp.dot(dzb[u], kb[u], preferred_element_type=F32) for u in heads]
            for u in heads:
                dk_ref[ks, lanes[u]] += lax.dot_general(dzb[u], qs[u], TN, preferred_element_type=F32)
            for u in heads:
                dv_ref[ks, lanes[u]] += lax.dot_general(w[u].astype(BF16), dys[u], TN, preferred_element_type=F32)
            return tuple((dq[u], carry[u][1] + jnp.sum(log_keep[u], axis=1, keepdims=True),
                          carry[u][2] + jnp.sum(g[u], axis=1, keepdims=True)) for u in heads)

        zero = jnp.zeros((t, 1), F32)
        carry = tuple((jnp.zeros((t, HEAD_DIM), F32), zero, zero) for _ in lanes)
        carry = lax.fori_loop(0, i, lambda j, cr: block(j, cr, False), carry)
        carry = block(i, carry, True)
        for u, ln in enumerate(lanes):
            dq_ref[:, ln] = carry[u][0].astype(dq_ref.dtype)

        @pl.when(i == pl.num_programs(1) - 1)
        def _():
            dk_out[...] = dk_ref[...].astype(dk_out.dtype)
            dv_out[...] = dv_ref[...].astype(dv_out.dtype)

    g = nh // hp
    tile = lambda off: pl.BlockSpec((t, wide), lambda h, i: (i, off + h))
    head = lambda off, **mode: pl.BlockSpec((s, wide), lambda h, i: (0, off + h), **mode)
    once = dict(pipeline_mode=pl.Buffered(1))
    return pl.pallas_call(
        body, name="attn_bwd", grid=(g, s // t),
        in_specs=[tile(0), head(g, **once), head(2 * g, **once), tile(0), tile(0)],
        out_specs=[tile(0), head(0), head(0)],
        out_shape=[jax.ShapeDtypeStruct((s, nh * HEAD_DIM), BF16)] * 3,
        scratch_shapes=[pltpu.VMEM((s, wide), F32), pltpu.VMEM((s, wide), F32)],
        compiler_params=_params(("parallel", "arbitrary")),
    )(qkv, qkv, qkv, dy, tot)


def _lru_gates(xc, w_r, b_r, w_i, b_i, lam):
    xb = xc.astype(BF16)
    r = _sigmoid(jnp.dot(xb, w_r.astype(BF16), preferred_element_type=F32) + b_r)
    i = _sigmoid(jnp.dot(xb, w_i.astype(BF16), preferred_element_type=F32) + b_i)
    neg_lam = -lam
    sp_lam = jnp.maximum(neg_lam, 0.0) + _log1p(jnp.exp(-jnp.abs(neg_lam)))
    log_a = -LRU_C * r * sp_lam
    a = jnp.exp(log_a)
    mult = jnp.sqrt(-_expm1(2.0 * log_a))
    return r, i, sp_lam, a, mult


def _conv_taps(xpad_chunk, conv_w, t):
    shifted = [xpad_chunk[CONV_HALO:, :]]
    for d in range(1, CONV_WIDTH):
        shifted.append(pltpu.roll(xpad_chunk, d, 0)[CONV_HALO:, :])
    weights = [conv_w[CONV_WIDTH - 1 - d:CONV_WIDTH - d, :] for d in range(CONV_WIDTH)]
    return shifted, weights


def _lru_fwd(xr_pad, proj, gr_block0, conv_w, conv_b, w_r, b_r, w_i, b_i, lam, t):
    s, w = xr_pad.shape[0] - CONV_HALO, xr_pad.shape[1]
    nblk = w // LANE
    nchunk = s // t
    steps = [1 << p for p in range(t.bit_length() - 1)]
    assert (1 << (t.bit_length() - 1)) == t and w_r.shape[1:] == (LANE, LANE)

    def body(x_ref, gr_ref, cw_ref, cb_ref, wr_ref, br_ref, wi_ref, bi_ref, lam_ref, h_ref, hp_ref, xc_ref, y_ref):
        row = lax.broadcasted_iota(jnp.int32, (t, LANE), 0)

        def chunk(ci, h_in):
            t0 = pl.multiple_of(ci * t, t)
            shifted, weights = _conv_taps(x_ref[pl.ds(t0, t + CONV_HALO), :], cw_ref[...], t)
            xc = cb_ref[...] + sum(wd * xs for wd, xs in zip(weights, shifted))
            r, i, _, a, mult = _lru_gates(xc, wr_ref[...], br_ref[...], wi_ref[...], bi_ref[...], lam_ref[...])
            coef, val = a, mult * (i * xc)
            for d in steps:
                ok = row >= d
                val = jnp.where(ok, coef * pltpu.roll(val, d, 0) + val, val)
                coef = jnp.where(ok, coef * pltpu.roll(coef, d, 0), coef)
            h = val + coef * h_in
            rows = pl.ds(t0, t)
            h_ref[rows, :] = h
            hp_ref[rows, :] = jnp.where(row == 0, h_in, pltpu.roll(h, 1, 0))
            xc_ref[rows, :] = xc
            y_ref[rows, :] = (h * _gelu_and_grad(gr_ref[rows, :])[0]).astype(y_ref.dtype)
            return h[t - 1:t, :]

        lax.fori_loop(0, nchunk, chunk, jnp.zeros((1, LANE), F32))

    col = lambda rows: pl.BlockSpec((rows, LANE), lambda n: (0, n))
    return pl.pallas_call(
        body, name="lru_fwd", grid=(nblk,),
        in_specs=[col(s + CONV_HALO), pl.BlockSpec((s, LANE), lambda n: (0, gr_block0 + n)), col(CONV_WIDTH), col(1),
                  pl.BlockSpec((None, LANE, LANE), lambda n: (n, 0, 0)), col(1),
                  pl.BlockSpec((None, LANE, LANE), lambda n: (n, 0, 0)), col(1), col(1)],
        out_specs=[col(s)] * 4,
        out_shape=[jax.ShapeDtypeStruct((s, w), F32)] * 3 + [jax.ShapeDtypeStruct((s, w), BF16)],
        compiler_params=_params(("parallel",)),
    )(xr_pad, proj, conv_w, conv_b, w_r, b_r, w_i, b_i, lam)


def _lru_bwd(dy, proj, gr_block0, h, h_prev, xc, w_r, b_r, w_i, b_i, lam, t):
    s, w = dy.shape
    nblk = w // LANE
    nchunk = s // t
    steps = [1 << p for p in range(t.bit_length() - 1)]

    def body(dy_ref, gr_ref, h_ref, hp_ref, xc_ref, wr_ref, br_ref, wi_ref, bi_ref, lam_ref,
             dgr_ref, dxc_ref, dwr_ref, dwi_ref, dbr_ref, dbi_ref, dlam_ref):
        row = lax.broadcasted_iota(jnp.int32, (t, LANE), 0)
        for ref in (dwr_ref, dwi_ref, dbr_ref, dbi_ref, dlam_ref):
            ref[...] = jnp.zeros_like(ref)

        def chunk(cc, carry):
            lam_next, a_next = carry
            rows = pl.ds(pl.multiple_of((nchunk - 1 - cc) * t, t), t)
            dyb, hb, xcb = dy_ref[rows, :], h_ref[rows, :], xc_ref[rows, :]
            gel, dgel = _gelu_and_grad(gr_ref[rows, :])
            dgr_ref[rows, :] = dyb * hb * dgel
            w_r, w_i = wr_ref[...], wi_ref[...]
            r, i, sp_lam, a, mult = _lru_gates(xcb, w_r, br_ref[...], w_i, bi_ref[...], lam_ref[...])
            coef = jnp.where(row == t - 1, a_next, pltpu.roll(a, t - 1, 0))
            val = dyb * gel
            for d in steps:
                ok = row < t - d
                val = jnp.where(ok, coef * pltpu.roll(val, t - d, 0) + val, val)
                coef = jnp.where(ok, coef * pltpu.roll(coef, t - d, 0), coef)
            adj = val + coef * lam_next
            da = adj * hp_ref[rows, :]
            v = i * xcb
            dmult, dv = adj * v, adj * mult
            dlog_a = da * a - (a * a) * dmult / mult
            dr_pre = (-LRU_C * sp_lam) * dlog_a * r * (1.0 - r)
            di_pre = dv * xcb * i * (1.0 - i)
            dlam_ref[...] += _colsum(-LRU_C * r * dlog_a)
            dbr_ref[...] += _colsum(dr_pre)
            dbi_ref[...] += _colsum(di_pre)
            xb, drb, dib = xcb.astype(BF16), dr_pre.astype(BF16), di_pre.astype(BF16)
            dwr_ref[...] += lax.dot_general(xb, drb, TN, preferred_element_type=F32)
            dwi_ref[...] += lax.dot_general(xb, dib, TN, preferred_element_type=F32)
            dxc_ref[rows, :] = (dv * i + lax.dot_general(drb, w_r.astype(BF16), NT, preferred_element_type=F32)
                                + lax.dot_general(dib, w_i.astype(BF16), NT, preferred_element_type=F32))
            return adj[0:1, :], a[0:1, :]

        lax.fori_loop(0, nchunk, chunk, (jnp.zeros((1, LANE), F32), jnp.zeros((1, LANE), F32)))
        dlam_ref[...] = dlam_ref[...] * (-_sigmoid(-lam_ref[...]))

    col = lambda rows: pl.BlockSpec((rows, LANE), lambda n: (0, n))
    mat = pl.BlockSpec((None, LANE, LANE), lambda n: (n, 0, 0))
    return pl.pallas_call(
        body, name="lru_bwd", grid=(nblk,),
        in_specs=[col(s), pl.BlockSpec((s, LANE), lambda n: (0, gr_block0 + n)), col(s), col(s), col(s),
                  mat, col(1), mat, col(1), col(1)],
        out_specs=[col(s), col(s), mat, mat, col(1), col(1), col(1)],
        out_shape=[jax.ShapeDtypeStruct((s, w), F32)] * 2 + [jax.ShapeDtypeStruct((nblk, LANE, LANE), F32)] * 2
        + [jax.ShapeDtypeStruct((1, w), F32)] * 3,
        compiler_params=_params(("parallel",)),
    )(dy, proj, h, h_prev, xc, w_r, b_r, w_i, b_i, lam)


def _conv_bwd(xr_pad, dxc_pad, conv_w, t):
    s, w = xr_pad.shape[0] - CONV_HALO, xr_pad.shape[1]
    nchunk = s // t

    def body(x_ref, g_ref, cw_ref, dx_ref, dcw_ref, dcb_ref):
        dcw_ref[...] = jnp.zeros_like(dcw_ref)
        dcb_ref[...] = jnp.zeros_like(dcb_ref)

        def chunk(ci, _):
            t0 = pl.multiple_of(ci * t, t)
            shifted, weights = _conv_taps(x_ref[pl.ds(t0, t + CONV_HALO), :], cw_ref[...], t)
            gpad = g_ref[pl.ds(t0, t + CONV_HALO), :]
            g = gpad[:t, :]
            dx = weights[0] * g
            for d in range(1, CONV_WIDTH):
                dx = dx + weights[d] * pltpu.roll(gpad, t + CONV_HALO - d, 0)[:t, :]
            dx_ref[pl.ds(t0, t), :] = dx
            for d in range(CONV_WIDTH):
                dcw_ref[CONV_WIDTH - 1 - d:CONV_WIDTH - d, :] += _colsum(g * shifted[d])
            dcb_ref[...] += _colsum(g)
            return 0

        lax.fori_loop(0, nchunk, chunk, 0)

    col = lambda rows: pl.BlockSpec((rows, LANE), lambda n: (0, n))
    return pl.pallas_call(
        body, name="conv_bwd", grid=(w // LANE,),
        in_specs=[col(s + CONV_HALO), col(s + CONV_HALO), col(CONV_WIDTH)],
        out_specs=[col(s), col(CONV_WIDTH), col(1)],
        out_shape=[jax.ShapeDtypeStruct((s, w), F32), jax.ShapeDtypeStruct((CONV_WIDTH, w), F32),
                   jax.ShapeDtypeStruct((1, w), F32)],
        compiler_params=_params(("parallel",)),
    )(xr_pad, dxc_pad, conv_w)


def _sum_parts(parts_ref):
    g = parts_ref[0].astype(F32)
    for p in range(1, parts_ref.shape[0]):
        g = g + parts_ref[p].astype(F32)
    return g


def _reduce_parts(name, parts):
    p, r, c = parts.shape
    tr = _tile(r, max(8, (1 << 19) // c), 8)

    def body(parts_ref, g_ref):
        g_ref[...] = _sum_parts(parts_ref)

    return pl.pallas_call(
        body, name=name, grid=(r // tr,), in_specs=[pl.BlockSpec((p, tr, c), lambda i: (0, i, 0))],
        out_specs=pl.BlockSpec((tr, c), lambda i: (i, 0)), out_shape=jax.ShapeDtypeStruct((r, c), F32),
        compiler_params=_params(("parallel",)),
    )(parts)


def _adamw(name, parts, w, m, v):
    p, r, c = parts.shape
    tr = _tile(r, max(8, (1 << 18) // c), 8)

    def body(parts_ref, w_ref, m_ref, v_ref, g_ref, d_ref, nm_ref, nv_ref):
        g = _sum_parts(parts_ref)
        nm = ADAM_B1 * m_ref[...] + (1.0 - ADAM_B1) * g
        nv = ADAM_B2 * v_ref[...] + (1.0 - ADAM_B2) * (g * g)
        m_hat = nm / (1.0 - ADAM_B1 ** ADAM_STEP)
        v_hat = nv / (1.0 - ADAM_B2 ** ADAM_STEP)
        g_ref[...] = g
        d_ref[...] = -ADAM_LR * (m_hat / (jnp.sqrt(v_hat) + ADAM_EPS) + ADAM_WD * w_ref[...])
        nm_ref[...] = nm
        nv_ref[...] = nv

    blk = pl.BlockSpec((tr, c), lambda i: (i, 0))
    return pl.pallas_call(
        body, name=name, grid=(r // tr,), in_specs=[pl.BlockSpec((p, tr, c), lambda i: (0, i, 0)), blk, blk, blk],
        out_specs=[blk] * 4, out_shape=[jax.ShapeDtypeStruct((r, c), F32)] * 4,
        compiler_params=_params(("parallel",)),
    )(parts, w, m, v)


def _ffn_in(tag, y, w_in_g, tm):
    s, d = y.shape
    half = N_DEV // 2
    cb = w_in_g.shape[2]
    ff = half * cb

    def swiglu(acc, extra_refs, out_refs):
        g, u = acc
        out_refs[0][0] = g.astype(BF16)
        out_refs[0][1] = u.astype(BF16)
        out_refs[1][...] = (g * _sigmoid(g) * u).astype(BF16)

    gu_shape = (2, 1, s, ff)
    gu, act = _matmul(
        tag + "_in", y[None], w_in_g.reshape(2, half, d, cb), "nn", tm, cb, d,
        outs=[(gu_shape, BF16, _bspec(gu_shape, tm, cb, _ij)), ((1, s, ff), BF16, _bspec((1, s, ff), tm, cb, _ij))],
        epilogue=swiglu, b_buffers=1)
    return gu, act


def _ffn_out(tag, act, w_out_g, res, gate, tm):
    _, s, ff = act.shape
    d = res.shape[1]
    tn = _tile(d, 1024)

    def residual(acc, extra_refs, out_refs):
        out_refs[0][...] = acc[0]
        out_refs[1][...] = extra_refs[0][...] + 0.5 * extra_refs[1][...] * acc[0]

    plain = _bspec((1, s, d), tm, tn, _ij)
    o, h_new = _matmul(
        tag + "_out", act, w_out_g.reshape(1, ff, d), "nn", tm, tn, ff,
        outs=[((1, s, d), F32, plain), ((1, s, d), F32, plain)], epilogue=residual,
        extras=[(res[None], plain), (gate, _row_spec(tn))], b_buffers=1)
    return o[0], h_new[0]


def _after_token(token):
    return token, pl.BlockSpec(token.shape, lambda j, i, k: (0, 0))


def _ffn_bwd_weights(tag, do, y, gu, act, w_in_g, w_out_g, tm):
    s, d = do.shape
    half = N_DEV // 2
    cb = w_in_g.shape[2]
    ff = half * cb

    tn_d, tm_f = _tile(d, 1024), _tile(ff, 512)
    dw_out = _matmul(tag + "_dw_out", act, do[None], "tn", tm_f, tn_d, s,
                     outs=[((1, ff, d), BF16, _bspec((1, ff, d), tm_f, tn_d, _ij))], b_buffers=1)[0]
    dw_out = dw_out.reshape(N_DEV, ff // N_DEV, d)
    out_handles, token = _exchange_begin(tag + "_scatter_out", [dw_out], True)

    def dswiglu(acc, extra_refs, out_refs):
        dact = acc[0]
        g, u = extra_refs[0][0].astype(F32), extra_refs[0][1].astype(F32)
        sg = _sigmoid(g)
        out_refs[0][0] = (dact * u * sg * (1.0 + g * (1.0 - sg))).astype(BF16)
        out_refs[0][1] = (dact * g * sg).astype(BF16)

    gu_shape = (2, 1, s, ff)
    gu_spec = _bspec(gu_shape, tm, cb, _ij)
    dgu = _matmul(tag + "_dact", do[None], w_out_g.reshape(1, ff, d), "nt", tm, cb, d,
                  outs=[(gu_shape, BF16, gu_spec)], epilogue=dswiglu, extras=[(gu, gu_spec), _after_token(token)],
                  b_buffers=1)[0]
    dgu = dgu.reshape(2, s, ff)

    tm_d = _tile(d, 512)
    dw_in = _matmul(tag + "_dw_in", y[None], dgu, "tn", tm_d, cb, s,
                    outs=[((N_DEV, d, cb), BF16, _bspec((N_DEV, d, cb), tm_d, cb, _ij))], b_buffers=1)[0]
    in_handles, token = _exchange_begin(tag + "_scatter_in", [dw_in], True)
    return dgu, in_handles + out_handles, token


def _ffn_bwd_input(tag, dgu, w_in_g, token):
    s, d = dgu.shape[1], w_in_g.shape[1]
    tm_big = _tile(s, 1024, 8)
    return _matmul(tag + "_dy", dgu, w_in_g, "nt", tm_big, d, w_in_g.shape[2],
                   outs=[((1, s, d), F32, _bspec((1, s, d), tm_big, d, _ij))], extras=[_after_token(token)])[0][0]


def kernel(x, c, w_ada, b_ada, norm_ffn1, w_ffn1_in, w_ffn1_out, norm_mix, w_in, conv_w, conv_b, w_rg_gate, b_rg_gate, w_in_gate, b_in_gate, lru_lambda, w_branch_attn, w_branch_lru, w_out, norm_ffn2, w_ffn2_in, w_ffn2_out, norm_final, loss_target, m_w_ada, m_b_ada, m_norm_ffn1, m_w_ffn1_in, m_w_ffn1_out, m_norm_mix, m_w_in, m_conv_w, m_conv_b, m_w_rg_gate, m_b_rg_gate, m_w_in_gate, m_b_in_gate, m_lru_lambda, m_w_branch_attn, m_w_branch_lru, m_w_out, m_norm_ffn2, m_w_ffn2_in, m_w_ffn2_out, m_norm_final, v_w_ada, v_b_ada, v_norm_ffn1, v_w_ffn1_in, v_w_ffn1_out, v_norm_mix, v_w_in, v_conv_w, v_conv_b, v_w_rg_gate, v_b_rg_gate, v_w_in_gate, v_b_in_gate, v_lru_lambda, v_w_branch_attn, v_w_branch_lru, v_w_out, v_norm_ffn2, v_w_ffn2_in, v_w_ffn2_out, v_norm_final):
    xs, target = x[0], loss_target[0]
    s, d = xs.shape
    aw, lw = w_branch_attn.shape[1], w_branch_lru.shape[1]
    nh, nlb = aw // HEAD_DIM, w_rg_gate.shape[1]
    cba, cbi, cbb, cwb = w_ada.shape[2], w_in.shape[2], w_branch_attn.shape[2], conv_w.shape[2]
    assert lw == nlb * LANE and cwb * N_DEV == lw and 3 * aw + 2 * lw + 2 * d == cbi * N_DEV
    me = 4 * lax.axis_index("x") + 2 * lax.axis_index("y") + lax.axis_index("c")
    tm = _tile(s, 512, 8)
    tr = _tile(s, 256, 8)
    t_attn = _tile(s, 256, 8)
    t_lru = _tile(s, 256, 8)

    small = _exchange("gather_c", [jnp.concatenate([c, conv_w.reshape(1, CONV_WIDTH * cwb)], axis=1)], False)[0][:, 0, :]
    c_all = small[:, :d]
    conv_w_full = small[:, d:].reshape(N_DEV, CONV_WIDTH, cwb).transpose(1, 0, 2).reshape(CONV_WIDTH, lw)
    c_act = _rowwise("silu_c", lambda v: v * _sigmoid(v), [(c_all, d, 0)], [], [(d, F32)], [], N_DEV)[0]

    def add_bias(acc_ref, extra_refs, out_refs):
        out_refs[0][...] = acc_ref[0] + extra_refs[0][...]

    b_ada_mine = lax.dynamic_slice(b_ada, (0, me * cba), (1, cba))
    mod_part = _matmul("mod", c_act[None], w_ada, "nn", N_DEV, cba, _tile(d, 512),
                       outs=[((1, N_DEV, cba), F32, _bspec((1, N_DEV, cba), N_DEV, cba, _ij))], epilogue=add_bias,
                       extras=[(b_ada_mine, _row_spec(cba))])[0][0]
    mod_all = _exchange("gather_mod", [mod_part], False)[0]
    mod = lax.dynamic_index_in_dim(mod_all, me, axis=1, keepdims=False).reshape(1, 9 * d)
    sh1, sc1, g1, sh2, sc2, g2, sh3, sc3, g3 = [mod[:, n * d:(n + 1) * d] for n in range(9)]

    shards = [w_ffn1_in[0], w_ffn1_out[0], w_in[0], w_branch_attn[0], w_branch_lru[0], w_out[0], w_ffn2_in[0], w_ffn2_out[0]]
    early = (0, 1, 2)
    gathers, token = _exchange_begin("gather_w_early", [w.astype(BF16) for w in shards[:3]], False, mod, early)
    late, token = _exchange_begin("gather_w", [(w + token[0, 0]).astype(BF16) for w in shards[3:]], False)
    gathers = gathers + late

    def gathered(n, after):
        full = _exchange_end("gathered_w%d" % n, [gathers[n]], after, False)
        return (_sibling_forward("forwarded_w%d" % n, full) if n in early else full)[0]

    y1 = _norm_mod("norm1", xs, norm_ffn1 + token[:1, :1], sc1, sh1, tr)
    wf1i = gathered(0, y1)
    gu1, act1 = _ffn_in("ffn1", y1, wf1i, tm)
    wf1o = gathered(1, act1)
    o1, h1 = _ffn_out("ffn1", act1, wf1o, xs, g1, tm)

    y2 = _norm_mod("norm2", h1, norm_mix, sc2, sh2, tr)
    wi_g = gathered(2, y2)
    tn_i = _tile(cbi, 1152)
    proj = _matmul("mix_in", y2[None], wi_g, "nn", tm, tn_i, d,
                   outs=[((1, s, N_DEV * cbi), F32, _bspec((1, s, N_DEV * cbi), tm, tn_i, _ij))], b_buffers=1)[0][0]
    off_xr, off_gr, off_ga, off_gl = 3 * aw, 3 * aw + lw, 3 * aw + 2 * lw, 3 * aw + 2 * lw + d
    qkv = proj[:, :3 * aw].astype(BF16)
    y_attn, attn_tot = _attn_fwd(qkv, nh, t_attn)
    xr_pad = jnp.pad(proj[:, off_xr:off_xr + lw], ((CONV_HALO, 0), (0, 0)))
    w_r, w_i = w_rg_gate[0], w_in_gate[0]
    h_lru, h_prev, xc, y_lru = _lru_fwd(xr_pad, proj, off_gr // LANE, conv_w_full, conv_b, w_r, b_rg_gate, w_i,
                                        b_in_gate, lru_lambda, t_lru)
    wba_p = gathered(3, y_attn).transpose(1, 0, 2).reshape(1, aw, d)
    wbl_p = gathered(4, y_lru).transpose(1, 0, 2).reshape(1, lw, d)
    proj3 = proj[None]
    tm_b = _tile(s, 1024, 8)
    tn_m = _tile(math.gcd(d, off_ga, off_gl), 1024)
    plain_m = _bspec((1, s, d), tm_b, tn_m, _ij)
    gate_specs = [_bspec(proj3.shape, tm_b, tn_m, functools.partial(lambda i, j, k, o: (i, j + o), o=o // tn_m))
                  for o in (off_ga, off_gl)]
    ya = _matmul("branch_attn", y_attn[None], wba_p, "nn", tm_b, tn_m, aw, outs=[((1, s, d), F32, plain_m)], b_buffers=1)[0]

    def merge(acc_ref, extra_refs, out_refs):
        yl = acc_ref[0]
        ya_t, ga, gl = extra_refs[0][...], extra_refs[1][...], extra_refs[2][...]
        out_refs[0][...] = yl
        out_refs[1][...] = (_sigmoid(ga) * ya_t + _sigmoid(gl) * yl).astype(BF16)

    yl, merged = _matmul("branch_lru", y_lru[None], wbl_p, "nn", tm_b, tn_m, lw,
                         outs=[((1, s, d), F32, plain_m), ((1, s, d), BF16, plain_m)], epilogue=merge,
                         extras=[(ya, plain_m), (proj3, gate_specs[0]), (proj3, gate_specs[1])], b_buffers=1)
    tn_d = _tile(d, 1024)
    plain = _bspec((1, s, d), tm, tn_d, _ij)

    def residual(acc_ref, extra_refs, out_refs):
        o = acc_ref[0]
        out_refs[0][...] = o
        out_refs[1][...] = extra_refs[0][...] + extra_refs[1][...] * o

    wo_g = gathered(5, merged)
    mo, h2 = _matmul("mix_out", merged, wo_g.reshape(1, d, d), "nn", tm, tn_d, d,
                     outs=[((1, s, d), F32, plain), ((1, s, d), F32, plain)], epilogue=residual,
                     extras=[(h1[None], plain), (g2, _row_spec(tn_d))], b_buffers=1)
    mo, h2 = mo[0], h2[0]

    y3 = _norm_mod("norm3", h2, norm_ffn2, sc3, sh3, tr)
    wf2i = gathered(6, y3)
    gu3, act3 = _ffn_in("ffn2", y3, wf2i, tm)
    wf2o = gathered(7, act3)
    o3, h3 = _ffn_out("ffn2", act3, wf2o, h2, g3, tm)

    nf = norm_final.reshape(1, d)
    dh3, do3, loss_part, d_nf, dg3 = _loss_bwd("loss", h3, target, nf, o3, g3, tr)
    dgu3, scatter_ffn2, token = _ffn_bwd_weights("ffn2", do3, y3, gu3, act3, wf2i, wf2o, tm)
    dy3 = _ffn_bwd_input("ffn2", dgu3, wf2i, token)
    dh2, dmo, dsh3, dsc3, dn3, dg2 = _norm_mod_bwd("norm3_bwd", dy3, h2, dh3, norm_ffn2, sc3, tr, below=(mo, g2, 1.0))

    dwo = _matmul("mix_dw_out", merged, dmo[None], "tn", _tile(d, 512), tn_d, s,
                  outs=[((1, d, d), BF16, _bspec((1, d, d), _tile(d, 512), tn_d, _ij))], b_buffers=1)[0]

    def dmerge(acc_ref, extra_refs, out_refs):
        dm = acc_ref[0]
        ya_t, yl_t = extra_refs[0][...], extra_refs[1][...]
        sa, sl = _sigmoid(extra_refs[2][...]), _sigmoid(extra_refs[3][...])
        out_refs[0][...] = (dm * sa).astype(BF16)
        out_refs[1][...] = (dm * sl).astype(BF16)
        out_refs[2][...] = (dm * ya_t * sa * (1.0 - sa)).astype(BF16)
        out_refs[3][...] = (dm * yl_t * sl * (1.0 - sl)).astype(BF16)

    tn_m = _tile(math.gcd(d, off_ga, off_gl), 1024)
    plain_m = _bspec((1, s, d), tm, tn_m, _ij)
    gate_specs = [_bspec(proj3.shape, tm, tn_m, functools.partial(lambda i, j, k, o: (i, j + o), o=o // tn_m))
                  for o in (off_ga, off_gl)]
    dya, dyl, dga, dgl = _matmul("mix_dmerged", dmo[None], wo_g.reshape(1, d, d), "nt", tm, tn_m, d,
                                 outs=[((1, s, d), BF16, plain_m)] * 4, epilogue=dmerge,
                                 extras=[(ya, plain_m), (yl, plain_m), (proj3, gate_specs[0]), (proj3, gate_specs[1])],
                                 b_buffers=1)
    tm_a, tm_l = _tile(aw, 1024), _tile(lw, 1024)
    dwba = _matmul("dw_branch_attn", y_attn[None], dya, "tn", tm_a, cbb, s,
                   outs=[((N_DEV, aw, cbb), BF16, _bspec((N_DEV, aw, cbb), tm_a, cbb, _ij))])[0]
    dwbl = _matmul("dw_branch_lru", y_lru[None], dyl, "tn", tm_l, cbb, s,
                   outs=[((N_DEV, lw, cbb), BF16, _bspec((N_DEV, lw, cbb), tm_l, cbb, _ij))])[0]
    scatter_branch, token = _exchange_begin("scatter_branch", [dwba, dwbl, dwo.reshape(N_DEV, d // N_DEV, d)], True)
    tn_a, tn_l = _tile(aw, 1024), _tile(lw, 1024)
    dy_attn = _matmul("d_attn_out", dya, wba_p, "nt", tm_b, tn_a, d,
                      outs=[((1, s, aw), BF16, _bspec((1, s, aw), tm_b, tn_a, _ij))], extras=[_after_token(token)],
                      b_buffers=1)[0][0]
    dy_lru = _matmul("d_lru_out", dyl, wbl_p, "nt", tm_b, tn_l, d,
                     outs=[((1, s, lw), F32, _bspec((1, s, lw), tm_b, tn_l, _ij))], b_buffers=1)[0][0]
    dq, dk, dv = _attn_bwd(qkv, dy_attn, attn_tot, nh, t_attn)
    dgr, dxc, d_wr, d_wi, d_br, d_bi, d_lam = _lru_bwd(dy_lru, proj, off_gr // LANE, h_lru, h_prev, xc, w_r, b_rg_gate,
                                                       w_i, b_in_gate, lru_lambda, t_lru)
    dxr, d_cw, d_cb = _conv_bwd(xr_pad, jnp.pad(dxc, ((0, CONV_HALO), (0, 0))), conv_w_full, t_lru)
    dproj = jnp.concatenate([dq.astype(BF16), dk.astype(BF16), dv.astype(BF16), dxr.astype(BF16), dgr.astype(BF16),
                             dga[0], dgl[0]], axis=1)
    tm_d = _tile(d, 512)
    dwi = _matmul("mix_dw_in", y2[None], dproj[None], "tn", tm_d, tn_i, s,
                  outs=[((N_DEV, d, cbi), BF16, _bspec((N_DEV, d, cbi), tm_d, tn_i, _ij))], b_buffers=1)[0]
    scatter_mix, token = _exchange_begin("scatter_mix", [dwi], True)
    tm_big = _tile(s, 1024, 8)
    dy2 = _matmul("mix_dy", dproj[None], wi_g, "nt", tm_big, d, tn_i,
                  outs=[((1, s, d), F32, _bspec((1, s, d), tm_big, d, _ij))], extras=[_after_token(token)])[0][0]
    dh1, do1, dsh2, dsc2, dn2, dg1 = _norm_mod_bwd("norm2_bwd", dy2, h1, dh2, norm_mix, sc2, tr, below=(o1, g1, 0.5))

    dgu1, scatter_ffn1, token = _ffn_bwd_weights("ffn1", do1, y1, gu1, act1, wf1i, wf1o, tm)
    dy1 = _ffn_bwd_input("ffn1", dgu1, wf1i, token)
    grad_x, dsh1, dsc1, dn1 = _norm_mod_bwd("norm1_bwd", dy1, xs, dh1, norm_ffn1, sc1, tr)

    results = {}

    def update_group(wait_name, handles, leaves, after):
        for (n, w, m, v), parts in zip(leaves, _exchange_end(wait_name, handles, after, True)):
            results[n] = [o[None] for o in _adamw("adamw_" + n, parts, w[0], m[0], v[0])]
        return results[leaves[-1][0]][0]

    done = update_group("scattered_ffn2", scatter_ffn2, [("w_ffn2_in", w_ffn2_in, m_w_ffn2_in, v_w_ffn2_in),
                                                         ("w_ffn2_out", w_ffn2_out, m_w_ffn2_out, v_w_ffn2_out)], grad_x)
    done = update_group("scattered_mix", scatter_branch + scatter_mix,
                        [("w_branch_attn", w_branch_attn, m_w_branch_attn, v_w_branch_attn),
                         ("w_branch_lru", w_branch_lru, m_w_branch_lru, v_w_branch_lru),
                         ("w_out", w_out, m_w_out, v_w_out), ("w_in", w_in, m_w_in, v_w_in)], done)

    lane_pad = jnp.zeros((1, 7 * LANE), F32)
    pack = jnp.concatenate(
        [loss_part, lane_pad, dsh1, dsc1, dg1, dsh2, dsc2, dg2, dsh3, dsc3, dg3, dn1, dn2, dn3, d_nf, d_cb, d_br, d_bi, d_lam,
         d_cw.reshape(1, -1)], axis=1)
    pack = jnp.pad(pack, ((0, 0), (0, -pack.shape[1] % (8 * LANE))))
    gate_pack = jnp.concatenate([d_wr.reshape(-1, LANE), d_wi.reshape(-1, LANE)], axis=0).astype(BF16)
    n_pack, n_gate = pack.shape[1], gate_pack.size
    updated = sum(results[n][0][0, 0, 0] for n in sorted(results)).reshape(1, 1)
    small_handles, token = _exchange_begin("gather_small", [pack, gate_pack], False, updated)

    done = update_group("scattered_ffn1", scatter_ffn1, [("w_ffn1_in", w_ffn1_in, m_w_ffn1_in, v_w_ffn1_in),
                                                         ("w_ffn1_out", w_ffn1_out, m_w_ffn1_out, v_w_ffn1_out)], token)
    packs, gate_packs = _exchange_end("gathered_small", small_handles, done, False)
    packs = packs.reshape(N_DEV, n_pack // LANE, LANE)
    g_pack = _reduce_parts("sum_small", packs).reshape(1, n_pack)
    g_gate = _reduce_parts("sum_gates", gate_packs).reshape(1, n_gate)
    loss = g_pack[0, 0]
    off = 8 * LANE
    n_vec = 9 * d + 4 * d + 4 * lw
    n_adam = n_vec + n_gate
    g_small = jnp.concatenate([g_pack[:, off:off + n_vec], g_gate], axis=1).reshape(1, n_adam // LANE, LANE)
    d_cw_sum = g_pack[:, off + n_vec:off + n_vec + CONV_WIDTH * lw].reshape(CONV_WIDTH, lw)
    d_cw_mine = lax.dynamic_slice(d_cw_sum, (0, me * cwb), (CONV_WIDTH, cwb))

    small_names = ["b_ada", "norm_ffn1", "norm_mix", "norm_ffn2", "norm_final", "conv_b", "b_rg_gate", "b_in_gate",
                   "lru_lambda", "w_rg_gate", "w_in_gate"]
    given = dict(b_ada=(b_ada, m_b_ada, v_b_ada), norm_ffn1=(norm_ffn1, m_norm_ffn1, v_norm_ffn1),
                 norm_mix=(norm_mix, m_norm_mix, v_norm_mix), norm_ffn2=(norm_ffn2, m_norm_ffn2, v_norm_ffn2),
                 norm_final=(norm_final, m_norm_final, v_norm_final), conv_b=(conv_b, m_conv_b, v_conv_b),
                 b_rg_gate=(b_rg_gate, m_b_rg_gate, v_b_rg_gate), b_in_gate=(b_in_gate, m_b_in_gate, v_b_in_gate),
                 lru_lambda=(lru_lambda, m_lru_lambda, v_lru_lambda), w_rg_gate=(w_rg_gate, m_w_rg_gate, v_w_rg_gate),
                 w_in_gate=(w_in_gate, m_w_in_gate, v_w_in_gate))
    packed = [jnp.concatenate([given[n][q].reshape(1, -1) for n in small_names], axis=1).reshape(n_adam // LANE, LANE)
              for q in range(3)]
    small_out = _adamw("adamw_small", g_small, *packed)
    pos = 0
    for n in small_names:
        shape = given[n][0].shape
        size = math.prod(shape)
        results[n] = [o.reshape(1, n_adam)[:, pos:pos + size].reshape(shape) for o in small_out]
        pos += size
    results["conv_w"] = [o.reshape(conv_w.shape) for o in
                         _adamw("adamw_conv_w", d_cw_mine[None], conv_w[0], m_conv_w[0], v_conv_w[0])]

    dmod_all = packs.reshape(N_DEV, n_pack)[:, off:off + 9 * d]
    dmod_mine = lax.dynamic_slice(dmod_all, (0, me * cba), (N_DEV, cba))
    dmod_rows = jnp.pad(dmod_mine, ((0, LANE - N_DEV), (0, 0)))
    c_act_t = jnp.pad(c_act.T, ((0, 0), (0, LANE - N_DEV)))
    tm_d2 = _tile(d, 256)
    d_wada = _matmul("dw_ada", c_act_t[None], dmod_rows[None], "nn", tm_d2, cba, LANE,
                     outs=[((1, d, cba), F32, _bspec((1, d, cba), tm_d2, cba, _ij))])[0]
    results["w_ada"] = [o[None] for o in _adamw("adamw_w_ada", d_wada, w_ada[0], m_w_ada[0], v_w_ada[0])]


    order = ["w_ada", "b_ada", "norm_ffn1", "w_ffn1_in", "w_ffn1_out", "norm_mix", "w_in", "conv_w", "conv_b", "w_rg_gate",
             "b_rg_gate", "w_in_gate", "b_in_gate", "lru_lambda", "w_branch_attn", "w_branch_lru", "w_out", "norm_ffn2",
             "w_ffn2_in", "w_ffn2_out", "norm_final"]
    return (loss, grad_x[None], *[results[n][0] for n in order], *[results[n][1] for n in order],
            *[results[n][2] for n in order], *[results[n][3] for n in order])
```

```python
import functools
import math

import jax
import jax.numpy as jnp
from jax import lax
from jax.experimental import pallas as pl
from jax.experimental.pallas import tpu as pltpu

F32 = jnp.float32
BF16 = jnp.bfloat16
N_DEV = 8
HEAD_DIM = 128
CONV_WIDTH = 4
CONV_HALO = 8
LRU_C = 8.0
EPS = 1e-6
ADAM_LR, ADAM_B1, ADAM_B2, ADAM_EPS, ADAM_WD, ADAM_STEP = 0.001, 0.9, 0.999, 1e-08, 0.01, 10
LANE = 128
VMEM_LIMIT = 56 * 1024 * 1024
MESH = pl.DeviceIdType.MESH

NT = (((1,), (1,)), ((), ()))
NN = (((1,), (0,)), ((), ()))
TN = (((0,), (0,)), ((), ()))


def _tile(dim, target, align=LANE):
    t = (min(target, dim) // align) * align
    while t >= align:
        if dim % t == 0:
            return t
        t -= align
    return dim


def _params(sem):
    return pltpu.CompilerParams(dimension_semantics=sem, vmem_limit_bytes=VMEM_LIMIT)


def _sigmoid(x):
    return 1.0 / (1.0 + jnp.exp(-x))


def _softplus(x):
    return jnp.maximum(x, 0.0) + jnp.log(1.0 + jnp.exp(-jnp.abs(x)))


def _log1p(z):
    w = 1.0 + z
    return jnp.where(w == 1.0, z, jnp.log(w) * z / jnp.where(w == 1.0, 1.0, w - 1.0))


def _expm1(x):
    poly = x * (1.0 + x * (0.5 + x * (1.0 / 6 + x * (1.0 / 24 + x * (1.0 / 120 + x * (1.0 / 720))))))
    return jnp.where(jnp.abs(x) < 0.25, poly, jnp.exp(x) - 1.0)


_GELU_C = math.sqrt(2.0 / math.pi)


def _gelu_and_grad(x):
    inner = _GELU_C * (x + 0.044715 * x * x * x)
    th = jnp.tanh(inner)
    val = 0.5 * x * (1.0 + th)
    grad = 0.5 * (1.0 + th) + 0.5 * x * (1.0 - th * th) * _GELU_C * (1.0 + 3 * 0.044715 * x * x)
    return val, grad


def _dot_split(x, u):
    hi = x.astype(BF16)
    lo = (x - hi.astype(F32)).astype(BF16)
    return jnp.dot(hi, u, preferred_element_type=F32) + jnp.dot(lo, u, preferred_element_type=F32)


def _mesh_position():
    x, y, c = lax.axis_index("x"), lax.axis_index("y"), lax.axis_index("c")
    return x, y, c, 4 * x + 2 * y + c


def _peers(x, y, c):
    out = []
    for mask in range(1, N_DEV):
        px = 1 - x if mask & 4 else x
        py = 1 - y if mask & 2 else y
        pc = 1 - c if mask & 1 else c
        out.append((mask, (px, py, pc), 4 * px + 2 * py + pc))
    return out


def _exchange(name, arrs, scatter, after=None):
    n = len(arrs)
    behind = [] if after is None else [after]

    def body(*refs):
        ins, outs = refs[:n], refs[n + len(behind):2 * n + len(behind)]
        send_sems, recv_sems, local_sems = refs[2 * n + len(behind):]
        x, y, c, me = _mesh_position()
        peers = _peers(x, y, c)
        waits = []
        for a in range(n):
            mine = ins[a].at[me] if scatter else ins[a]
            local = pltpu.make_async_copy(mine, outs[a].at[me], local_sems.at[a])
            local.start()
            waits.append(local.wait)
            for mask, dev, idx in peers:
                k = a * (N_DEV - 1) + mask - 1
                src = ins[a].at[idx] if scatter else ins[a]
                send = pltpu.make_async_remote_copy(src_ref=src, dst_ref=outs[a].at[me], send_sem=send_sems.at[k],
                                                    recv_sem=recv_sems.at[k], device_id=dev, device_id_type=MESH)
                send.start()
                arrival = pltpu.make_async_remote_copy(src_ref=src, dst_ref=outs[a].at[idx], send_sem=send_sems.at[k],
                                                       recv_sem=recv_sems.at[k], device_id=dev, device_id_type=MESH)
                waits.append(send.wait_send)
                waits.append(arrival.wait_recv)
        for w in waits:
            w()

    any_spec = pl.BlockSpec(memory_space=pl.ANY)
    out_shape = [jax.ShapeDtypeStruct(a.shape if scatter else (N_DEV,) + a.shape, a.dtype) for a in arrs]
    return pl.pallas_call(
        body, name=name, out_shape=out_shape, in_specs=[any_spec] * (n + len(behind)), out_specs=[any_spec] * n,
        scratch_shapes=[pltpu.SemaphoreType.DMA((n * (N_DEV - 1),)), pltpu.SemaphoreType.DMA((n * (N_DEV - 1),)),
                        pltpu.SemaphoreType.DMA((n,))],
    )(*arrs, *behind)


HBM_SPEC = pl.BlockSpec(memory_space=pltpu.HBM)
SEM_SPEC = pl.BlockSpec(memory_space=pltpu.SEMAPHORE)
DATAFLOW = pltpu.SideEffectType.DATAFLOW_SIDE_EFFECTING


ALL_MASKS = tuple(range(1, N_DEV))
SAME_CORE_MASKS = (1, 2, 4, 6)


def _exchange_begin(name, arrs, scatter, after=None, once_per_chip=()):
    n = len(arrs)
    lands = [lax.empty(a.shape if scatter else (N_DEV,) + a.shape, a.dtype) for a in arrs]
    behind = [] if after is None else [after]
    masks = [SAME_CORE_MASKS if a in once_per_chip else ALL_MASKS for a in range(n)]

    def body(*refs):
        srcs, zones, outs = refs[:n], refs[n:2 * n], refs[2 * n + len(behind):]
        x, y, c, me = _mesh_position()
        for a in range(n):
            send_sems, recv_sems = outs[4 * a], outs[4 * a + 1]
            for mask, dev, idx in _peers(x, y, c):
                if mask not in masks[a]:
                    continue
                pltpu.make_async_remote_copy(
                    src_ref=srcs[a].at[idx] if scatter else srcs[a], dst_ref=zones[a].at[me], send_sem=send_sems.at[mask - 1],
                    recv_sem=recv_sems.at[mask - 1], device_id=dev, device_id_type=MESH).start()
        outs[-1][...] = jnp.zeros_like(outs[-1])

    out_shape, out_specs, aliases = [], [], {}
    for a in range(n):
        out_shape += [pltpu.SemaphoreType.DMA((N_DEV - 1,)), pltpu.SemaphoreType.DMA((N_DEV - 1,)),
                      pltpu.HBM(arrs[a].shape, arrs[a].dtype), pltpu.HBM(lands[a].shape, lands[a].dtype)]
        out_specs += [SEM_SPEC, SEM_SPEC, HBM_SPEC, HBM_SPEC]
        aliases[a] = 4 * a + 2
        aliases[n + a] = 4 * a + 3
    out_shape.append(jax.ShapeDtypeStruct((8, LANE), F32))
    out_specs.append(pl.BlockSpec(memory_space=pltpu.VMEM))
    res = pl.pallas_call(
        body, name=name, out_shape=out_shape,
        in_specs=[HBM_SPEC] * (2 * n) + [pl.BlockSpec(memory_space=pl.ANY)] * len(behind),
        out_specs=out_specs, input_output_aliases=aliases, compiler_params=pltpu.CompilerParams(has_side_effects=DATAFLOW),
    )(*[pltpu.with_memory_space_constraint(v, pltpu.HBM) for v in list(arrs) + lands], *behind)
    return [tuple(res[4 * a:4 * a + 4]) + (masks[a],) for a in range(n)], res[-1]


def _exchange_end(name, handles, after, scatter):
    n = len(handles)
    me = 4 * lax.axis_index("x") + 2 * lax.axis_index("y") + lax.axis_index("c")

    def body(*refs):
        x, y, c, me = _mesh_position()
        for a in range(n):
            src, zone, send_sems, recv_sems = refs[4 * a:4 * a + 4]
            for mask, dev, idx in _peers(x, y, c):
                if mask not in handles[a][4]:
                    continue
                cp = pltpu.make_async_remote_copy(
                    src_ref=src.at[idx] if scatter else src, dst_ref=zone.at[idx], send_sem=send_sems.at[mask - 1],
                    recv_sem=recv_sems.at[mask - 1], device_id=dev, device_id_type=MESH)
                cp.wait_send()
                cp.wait_recv()

    operands, in_specs, out_shape, aliases = [], [], [], {}
    for a, (send_sems, recv_sems, src, zone, _) in enumerate(handles):
        operands += [src, zone, send_sems, recv_sems]
        in_specs += [HBM_SPEC, HBM_SPEC, SEM_SPEC, SEM_SPEC]
        out_shape += [pltpu.HBM(src.shape, src.dtype), pltpu.HBM(zone.shape, zone.dtype)]
        aliases[4 * a] = 2 * a
        aliases[4 * a + 1] = 2 * a + 1
    res = pl.pallas_call(
        body, name=name, out_shape=out_shape, in_specs=in_specs + [pl.BlockSpec(memory_space=pl.ANY)],
        out_specs=[HBM_SPEC] * (2 * n), input_output_aliases=aliases,
        compiler_params=pltpu.CompilerParams(has_side_effects=DATAFLOW),
    )(*operands, after)
    full = []
    for a in range(n):
        src, zone = res[2 * a], res[2 * a + 1]
        own = lax.dynamic_index_in_dim(src, me, 0, keepdims=False) if scatter else src
        full.append(lax.dynamic_update_index_in_dim(zone, own, me, 0))
    return full


def _sibling_forward(name, zones):
    n = len(zones)
    hops = (2, 4, 6)

    def body(*refs):
        outs, send_sems, recv_sems = refs[n:2 * n], refs[2 * n], refs[2 * n + 1]
        x, y, c, me = _mesh_position()
        sibling = (x, y, 1 - c)
        waits = []
        for a in range(n):
            for q, mask in enumerate(hops):
                chip = 4 * (1 - x if mask & 4 else x) + 2 * (1 - y if mask & 2 else y)
                k = a * len(hops) + q
                held, missing = outs[a].at[chip + c], outs[a].at[chip + 1 - c]
                send = pltpu.make_async_remote_copy(src_ref=held, dst_ref=held, send_sem=send_sems.at[k],
                                                    recv_sem=recv_sems.at[k], device_id=sibling, device_id_type=MESH)
                send.start()
                arrival = pltpu.make_async_remote_copy(src_ref=missing, dst_ref=missing, send_sem=send_sems.at[k],
                                                       recv_sem=recv_sems.at[k], device_id=sibling, device_id_type=MESH)
                waits += [send.wait_send, arrival.wait_recv]
        for w in waits:
            w()

    any_spec = pl.BlockSpec(memory_space=pl.ANY)
    return pl.pallas_call(
        body, name=name, out_shape=[jax.ShapeDtypeStruct(z.shape, z.dtype) for z in zones], in_specs=[any_spec] * n,
        out_specs=[any_spec] * n, input_output_aliases={a: a for a in range(n)},
        scratch_shapes=[pltpu.SemaphoreType.DMA((n * len(hops),)), pltpu.SemaphoreType.DMA((n * len(hops),))],
    )(*zones)


def _bspec(shape, tr, tc, rc, buffers=None):
    per = shape[-1] // tc
    mode = {} if buffers is None else dict(pipeline_mode=pl.Buffered(buffers))
    if len(shape) == 3:
        return pl.BlockSpec((None, tr, tc), lambda j, i, k: (rc(i, j, k)[1] // per, rc(i, j, k)[0], rc(i, j, k)[1] % per), **mode)
    return pl.BlockSpec((shape[0], None, tr, tc),
                        lambda j, i, k: (0, rc(i, j, k)[1] // per, rc(i, j, k)[0], rc(i, j, k)[1] % per), **mode)


def _ij(i, j, k):
    return i, j


def _row_spec(tn, col_tile_offset=0):
    return pl.BlockSpec((1, tn), lambda j, i, k: (0, j + col_tile_offset))


def _matmul(name, a, b, mode, tm, tn, tk, outs, epilogue=None, extras=(), b_buffers=None, side_by_side=False,
            pairs=False):
    groups = 1 if b.ndim == 3 else 2 if pairs else b.shape[0]
    if mode == "nn" and pairs:
        m, k_dim, n = a.shape[1], a.shape[0] * a.shape[2], b.shape[0] * tn
        a_spec = _bspec(a.shape, tm, tk, lambda i, j, k: (i, k))
        b_spec = pl.BlockSpec((None, 2, tk, tn), lambda j, i, k: (j, 0, k, 0),
                              **({} if b_buffers is None else dict(pipeline_mode=pl.Buffered(b_buffers))))
        dims = NN
    elif mode == "nn":
        m, k_dim, n = a.shape[1], a.shape[0] * a.shape[2], b.shape[-3] * b.shape[-1]
        a_spec = _bspec(a.shape, tm, tk, lambda i, j, k: (i, k))
        b_spec = _bspec(b.shape, tk, tn, lambda i, j, k: (k, j), b_buffers)
        dims = NN
    elif mode == "nt":
        m, k_dim, n = a.shape[1], a.shape[0] * a.shape[2], b.shape[-2]
        a_spec = _bspec(a.shape, tm, tk, lambda i, j, k: (i, k))
        b_spec = _bspec(b.shape, tn, tk, lambda i, j, k: (j, k), b_buffers)
        dims = NT
    else:
        m, k_dim, n = a.shape[0] * a.shape[2], a.shape[1], b.shape[-3] * b.shape[-1]
        a_spec = _bspec(a.shape, tk, tm, lambda i, j, k: (k, i))
        b_spec = _bspec(b.shape, tk, tn, lambda i, j, k: (k, j), b_buffers)
        dims = TN
    assert m % tm == 0 and n % tn == 0 and k_dim % tk == 0, (name, m, n, k_dim, tm, tn, tk)
    nk = k_dim // tk
    n_extra, n_out = len(extras), len(outs)

    def finish(acc, extra_refs, out_refs):
        if epilogue is None:
            out_refs[0][...] = acc[0].astype(out_refs[0].dtype)
        else:
            epilogue(acc, extra_refs, out_refs)

    def products(a_ref, b_ref):
        a_tile = a_ref[...].astype(BF16)
        return [lax.dot_general(a_tile, (b_ref[g] if b.ndim == 4 else b_ref[...]).astype(BF16), dims,
                                preferred_element_type=F32) for g in range(groups)]

    def body_whole_k(*refs):
        finish(products(refs[0], refs[1]), refs[2:2 + n_extra], refs[2 + n_extra:])

    def body_side_by_side(*refs):
        wide_ref = refs[-1]

        @pl.when(pl.program_id(1) == 0)
        def _():
            for g in range(groups):
                wide_ref[:, g * tn:(g + 1) * tn] = refs[1][g].astype(BF16)

        full = lax.dot_general(refs[0][...].astype(BF16), wide_ref[...], dims, preferred_element_type=F32)
        acc = [full] if pairs else [full[:, g * tn:(g + 1) * tn] for g in range(groups)]
        finish(acc, refs[2:2 + n_extra], refs[2 + n_extra:-1])

    def body_k_steps(*refs):
        acc_ref = refs[-1]
        k = pl.program_id(2)

        @pl.when(k == 0)
        def _():
            acc_ref[...] = jnp.zeros_like(acc_ref)

        for g, p in enumerate(products(refs[0], refs[1])):
            acc_ref[g] += p

        @pl.when(k == nk - 1)
        def _():
            finish([acc_ref[g] for g in range(groups)], refs[2:2 + n_extra], refs[2 + n_extra:2 + n_extra + n_out])

    if side_by_side:
        assert groups > 1 and nk == 1 and mode == "nn", name
        body, scratch, order = body_side_by_side, [pltpu.VMEM((tk, groups * tn), BF16)], "arbitrary"
    elif nk == 1:
        body, scratch, order = body_whole_k, [], "parallel"
    else:
        body, scratch, order = body_k_steps, [pltpu.VMEM((groups, tm, tn), F32)], "parallel"
    return pl.pallas_call(
        body, name=name, grid=(n // tn, m // tm, nk),
        in_specs=[a_spec, b_spec] + [s for _, s in extras],
        out_specs=[s for _, _, s in outs],
        out_shape=[jax.ShapeDtypeStruct(shape, dtype) for shape, dtype, _ in outs],
        scratch_shapes=scratch,
        compiler_params=_params(("parallel", order, "arbitrary")),
    )(a, b, *[arr for arr, _ in extras])


def _rowwise(name, fn, rows, vecs, outs, accs, tm):
    s = rows[0][0].shape[0]
    n_in, n_out = len(rows) + len(vecs), len(outs)

    def body(*refs):
        i = pl.program_id(0)
        res = fn(*[r[...] for r in refs[:n_in]])
        res = res if isinstance(res, tuple) else (res,)
        out_refs, acc_refs = refs[n_in:n_in + n_out], refs[n_in + n_out:]
        for ref, val in zip(out_refs, res[:n_out]):
            ref[...] = val.astype(ref.dtype)

        @pl.when(i == 0)
        def _():
            for ref in acc_refs:
                ref[...] = jnp.zeros_like(ref)

        for ref, val in zip(acc_refs, res[n_out:]):
            ref[...] += val

    in_specs = [pl.BlockSpec((tm, w), functools.partial(lambda i, cb: (i, cb), cb=cb)) for _, w, cb in rows]
    in_specs += [pl.BlockSpec(v.shape, lambda i: (0,) * v.ndim) for v in vecs]
    out_specs = [pl.BlockSpec((tm, w), lambda i: (i, 0)) for w, _ in outs] + [pl.BlockSpec((1, w), lambda i: (0, 0)) for w in accs]
    out_shape = [jax.ShapeDtypeStruct((s, w), dt) for w, dt in outs] + [jax.ShapeDtypeStruct((1, w), F32) for w in accs]
    return pl.pallas_call(
        body, name=name, grid=(s // tm,), in_specs=in_specs, out_specs=out_specs, out_shape=out_shape,
        compiler_params=_params(("arbitrary",)),
    )(*[r for r, _, _ in rows], *vecs)


def _colsum(v):
    return jnp.sum(v, axis=0, keepdims=True)


def _norm_mod(name, h, nw, sc, sh, tm):
    d = h.shape[1]

    def fn(hb, nwb, scb, shb):
        r = lax.rsqrt(jnp.mean(hb * hb, axis=-1, keepdims=True) + EPS)
        return (hb * r) * nwb * (1.0 + scb) + shb

    return _rowwise(name, fn, [(h, d, 0)], [nw, sc, sh], [(d, BF16)], [], tm)[0]


def _norm_mod_bwd(name, dy, h, dh_next, nw, sc, tm, below=None):
    d = h.shape[1]

    def fn(dyb, hb, dhb, *rest):
        nwb, scb = rest[-2:] if below is None else rest[1:3]
        r = lax.rsqrt(jnp.mean(hb * hb, axis=-1, keepdims=True) + EPS)
        xh = hb * r
        dxh = dyb * (nwb * (1.0 + scb))
        dx = r * (dxh - xh * jnp.mean(dxh * xh, axis=-1, keepdims=True))
        dh = dhb + dx
        sums = (_colsum(dyb), _colsum(dyb * xh * nwb), _colsum(dyb * xh * (1.0 + scb)))
        if below is None:
            return (dh,) + sums
        ob, gb = rest[0], rest[3]
        return (dh, dh * (below[2] * gb)) + sums + (_colsum(dh * ob * below[2]),)

    rows = [(dy, d, 0), (h, d, 0), (dh_next, d, 0)]
    if below is None:
        return _rowwise(name, fn, rows, [nw, sc], [(d, F32)], [d, d, d], tm)
    return _rowwise(name, fn, rows + [(below[0], d, 0)], [nw, sc, below[1]], [(d, F32), (d, BF16)], [d, d, d, d], tm)


def _loss_bwd(name, h, target, nw, o, g, tm):
    d = h.shape[1]

    def fn(hb, tb, ob, nwb, gb):
        r = lax.rsqrt(jnp.mean(hb * hb, axis=-1, keepdims=True) + EPS)
        xh = hb * r
        err = xh * nwb - tb
        dy = err * (1.0 / d)
        dxh = dy * nwb
        dx = r * (dxh - xh * jnp.mean(dxh * xh, axis=-1, keepdims=True))
        loss = 0.5 * jnp.sum(jnp.mean(err * err, axis=-1, keepdims=True), axis=0, keepdims=True)
        return dx, dx * (0.5 * gb), jnp.broadcast_to(loss, (1, LANE)), _colsum(dy * xh), _colsum(dx * ob * 0.5)

    return _rowwise(name, fn, [(h, d, 0), (target, d, 0), (o, d, 0)], [nw, g], [(d, F32), (d, BF16)], [LANE, d, d], tm)


def _head_group(nh, most):
    return max(g for g in (1, 2, 4) if g <= most and nh % g == 0)


def _attn_fwd(qkv, nh, t):
    s = qkv.shape[0]
    scale = HEAD_DIM ** -0.5
    hp = _head_group(nh, 4)
    wide = hp * HEAD_DIM
    lanes = [slice(u * HEAD_DIM, (u + 1) * HEAD_DIM) for u in range(hp)]

    def body(q_ref, k_ref, v_ref, y_ref, tot_ref):
        i = pl.program_id(1)
        row = lax.broadcasted_iota(jnp.int32, (t, t), 0)
        col = lax.broadcasted_iota(jnp.int32, (t, t), 1)
        later = (row > col).astype(BF16)
        causal = col < row
        qs = [q_ref[:, ln] for ln in lanes]

        def block(j, carry, diagonal):
            ks = pl.ds(pl.multiple_of(j * t, t), t)
            heads = range(hp)
            z = [lax.dot_general(qs[u], k_ref[ks, lanes[u]], NT, preferred_element_type=F32) * scale for u in heads]
            sp = [_softplus(z[u]) for u in heads]
            log_keep = [jnp.where(causal, -sp[u], 0.0) if diagonal else -sp[u] for u in heads]
            between = [_dot_split(log_keep[u], later) for u in heads]
            w = [jnp.exp(z[u] - sp[u] + between[u] + carry[u][1]) for u in heads]
            if diagonal:
                w = [jnp.where(causal, w[u], 0.0) for u in heads]
            o = [carry[u][0] + jnp.dot(w[u].astype(BF16), v_ref[ks, lanes[u]], preferred_element_type=F32) for u in heads]
            return tuple((o[u], carry[u][1] + jnp.sum(log_keep[u], axis=1, keepdims=True)) for u in heads)

        carry = tuple((jnp.zeros((t, HEAD_DIM), F32), jnp.zeros((t, 1), F32)) for _ in lanes)
        carry = block(i, carry, True)
        carry = lax.fori_loop(0, i, lambda jj, cr: block(i - 1 - jj, cr, False), carry)
        for u, ln in enumerate(lanes):
            y_ref[:, ln] = carry[u][0].astype(y_ref.dtype)
            tot_ref[:, ln] = jnp.broadcast_to(carry[u][1], (t, HEAD_DIM))

    g = nh // hp
    return pl.pallas_call(
        body, name="attn_fwd", grid=(g, s // t),
        in_specs=[pl.BlockSpec((t, wide), lambda h, i: (i, h)),
                  pl.BlockSpec((s, wide), lambda h, i: (0, g + h)),
                  pl.BlockSpec((s, wide), lambda h, i: (0, 2 * g + h))],
        out_specs=[pl.BlockSpec((t, wide), lambda h, i: (i, h)), pl.BlockSpec((t, wide), lambda h, i: (i, h))],
        out_shape=[jax.ShapeDtypeStruct((s, nh * HEAD_DIM), BF16), jax.ShapeDtypeStruct((s, nh * HEAD_DIM), F32)],
        compiler_params=_params(("parallel", "arbitrary")),
    )(qkv, qkv, qkv)


def _attn_bwd(qkv, dy, tot, nh, t):
    s = qkv.shape[0]
    scale = HEAD_DIM ** -0.5
    hp = _head_group(nh, 4)
    wide = hp * HEAD_DIM
    lanes = [slice(u * HEAD_DIM, (u + 1) * HEAD_DIM) for u in range(hp)]

    def body(q_ref, k_ref, v_ref, dy_ref, tot_ref, dq_ref, dk_out, dv_out, dk_ref, dv_ref):
        i = pl.program_id(1)

        @pl.when(i == 0)
        def _():
            dk_ref[...] = jnp.zeros_like(dk_ref)
            dv_ref[...] = jnp.zeros_like(dv_ref)

        row = lax.broadcasted_iota(jnp.int32, (t, t), 0)
        col = lax.broadcasted_iota(jnp.int32, (t, t), 1)
        upto = (row <= col).astype(BF16)
        before = (row < col).astype(BF16)
        causal = col < row
        qs = [q_ref[:, ln] for ln in lanes]
        dys = [dy_ref[:, ln] for ln in lanes]
        totals = [tot_ref[:, u * HEAD_DIM:u * HEAD_DIM + 1] for u in range(hp)]

        def block(j, carry, diagonal):
            ks = pl.ds(pl.multiple_of(j * t, t), t)
            heads = range(hp)
            kb = [k_ref[ks, ln] for ln in lanes]
            vb = [v_ref[ks, ln] for ln in lanes]
            z = [lax.dot_general(qs[u], kb[u], NT, preferred_element_type=F32) * scale for u in heads]
            dw = [lax.dot_general(dys[u], vb[u], NT, preferred_element_type=F32) for u in heads]
            sp = [_softplus(z[u]) for u in heads]
            log_keep = [jnp.where(causal, -sp[u], 0.0) if diagonal else -sp[u] for u in heads]
            upto_sum = [_dot_split(log_keep[u], upto) for u in heads]
            w = [jnp.exp(z[u] - sp[u] + (totals[u] - carry[u][1] - upto_sum[u])) for u in heads]
            if diagonal:
                w = [jnp.where(causal, w[u], 0.0) for u in heads]
            g = [dw[u] * w[u] for u in heads]
            g_before = [_dot_split(g[u], before) for u in heads]
            dz = [(g[u] * jnp.exp(-sp[u]) - jnp.exp(z[u] - sp[u]) * (carry[u][2] + g_before[u])) * scale for u in heads]
            if diagonal:
                dz = [jnp.where(causal, dz[u], 0.0) for u in heads]
            dzb = [dz[u].astype(BF16) for u in heads]
            dq = [carry[u][0] + jnp.dot(dzb[u], kb[u], preferred_element_type=F32) for u in heads]
            for u in heads:
                dk_ref[ks, lanes[u]] += lax.dot_general(dzb[u], qs[u], TN, preferred_element_type=F32)
            for u in heads:
                dv_ref[ks, lanes[u]] += lax.dot_general(w[u].astype(BF16), dys[u], TN, preferred_element_type=F32)
            return tuple((dq[u], carry[u][1] + jnp.sum(log_keep[u], axis=1, keepdims=True),
                          carry[u][2] + jnp.sum(g[u], axis=1, keepdims=True)) for u in heads)

        zero = jnp.zeros((t, 1), F32)
        carry = tuple((jnp.zeros((t, HEAD_DIM), F32), zero, zero) for _ in lanes)
        carry = lax.fori_loop(0, i, lambda j, cr: block(j, cr, False), carry)
        carry = block(i, carry, True)
        for u, ln in enumerate(lanes):
            dq_ref[:, ln] = carry[u][0].astype(dq_ref.dtype)

        @pl.when(i == pl.num_programs(1) - 1)
        def _():
            dk_out[...] = dk_ref[...].astype(dk_out.dtype)
            dv_out[...] = dv_ref[...].astype(dv_out.dtype)

    g = nh // hp
    tile = lambda off: pl.BlockSpec((t, wide), lambda h, i: (i, off + h))
    head = lambda off, **mode: pl.BlockSpec((s, wide), lambda h, i: (0, off + h), **mode)
    once = dict(pipeline_mode=pl.Buffered(1))
    return pl.pallas_call(
        body, name="attn_bwd", grid=(g, s // t),
        in_specs=[tile(0), head(g, **once), head(2 * g, **once), tile(0), tile(0)],
        out_specs=[tile(0), head(0), head(0)],
        out_shape=[jax.ShapeDtypeStruct((s, nh * HEAD_DIM), BF16)] * 3,
        scratch_shapes=[pltpu.VMEM((s, wide), F32), pltpu.VMEM((s, wide), F32)],
        compiler_params=_params(("parallel", "arbitrary")),
    )(qkv, qkv, qkv, dy, tot)


def _lru_gates(xc, w_r, b_r, w_i, b_i, lam):
    xb = xc.astype(BF16)
    r = _sigmoid(jnp.dot(xb, w_r.astype(BF16), preferred_element_type=F32) + b_r)
    i = _sigmoid(jnp.dot(xb, w_i.astype(BF16), preferred_element_type=F32) + b_i)
    neg_lam = -lam
    sp_lam = jnp.maximum(neg_lam, 0.0) + _log1p(jnp.exp(-jnp.abs(neg_lam)))
    log_a = -LRU_C * r * sp_lam
    a = jnp.exp(log_a)
    mult = jnp.sqrt(-_expm1(2.0 * log_a))
    return r, i, sp_lam, a, mult


def _conv_taps(xpad_chunk, conv_w, t):
    shifted = [xpad_chunk[CONV_HALO:, :]]
    for d in range(1, CONV_WIDTH):
        shifted.append(pltpu.roll(xpad_chunk, d, 0)[CONV_HALO:, :])
    weights = [conv_w[CONV_WIDTH - 1 - d:CONV_WIDTH - d, :] for d in range(CONV_WIDTH)]
    return shifted, weights


def _lru_fwd(xr_pad, proj, gr_block0, conv_w, conv_b, w_r, b_r, w_i, b_i, lam, t):
    s, w = xr_pad.shape[0] - CONV_HALO, xr_pad.shape[1]
    nblk = w // LANE
    nchunk = s // t
    steps = [1 << p for p in range(t.bit_length() - 1)]
    assert (1 << (t.bit_length() - 1)) == t and w_r.shape[1:] == (LANE, LANE)

    def body(x_ref, gr_ref, cw_ref, cb_ref, wr_ref, br_ref, wi_ref, bi_ref, lam_ref, h_ref, hp_ref, xc_ref, y_ref):
        row = lax.broadcasted_iota(jnp.int32, (t, LANE), 0)

        def chunk(ci, h_in):
            t0 = pl.multiple_of(ci * t, t)
            shifted, weights = _conv_taps(x_ref[pl.ds(t0, t + CONV_HALO), :], cw_ref[...], t)
            xc = cb_ref[...] + sum(wd * xs for wd, xs in zip(weights, shifted))
            r, i, _, a, mult = _lru_gates(xc, wr_ref[...], br_ref[...], wi_ref[...], bi_ref[...], lam_ref[...])
            coef, val = a, mult * (i * xc)
            for d in steps:
                ok = row >= d
                val = jnp.where(ok, coef * pltpu.roll(val, d, 0) + val, val)
                coef = jnp.where(ok, coef * pltpu.roll(coef, d, 0), coef)
            h = val + coef * h_in
            rows = pl.ds(t0, t)
            h_ref[rows, :] = h
            hp_ref[rows, :] = jnp.where(row == 0, h_in, pltpu.roll(h, 1, 0))
            xc_ref[rows, :] = xc
            y_ref[rows, :] = (h * _gelu_and_grad(gr_ref[rows, :])[0]).astype(y_ref.dtype)
            return h[t - 1:t, :]

        lax.fori_loop(0, nchunk, chunk, jnp.zeros((1, LANE), F32))

    col = lambda rows: pl.BlockSpec((rows, LANE), lambda n: (0, n))
    return pl.pallas_call(
        body, name="lru_fwd", grid=(nblk,),
        in_specs=[col(s + CONV_HALO), pl.BlockSpec((s, LANE), lambda n: (0, gr_block0 + n)), col(CONV_WIDTH), col(1),
                  pl.BlockSpec((None, LANE, LANE), lambda n: (n, 0, 0)), col(1),
                  pl.BlockSpec((None, LANE, LANE), lambda n: (n, 0, 0)), col(1), col(1)],
        out_specs=[col(s)] * 4,
        out_shape=[jax.ShapeDtypeStruct((s, w), F32)] * 3 + [jax.ShapeDtypeStruct((s, w), BF16)],
        compiler_params=_params(("parallel",)),
    )(xr_pad, proj, conv_w, conv_b, w_r, b_r, w_i, b_i, lam)


def _lru_bwd(dy, proj, gr_block0, h, h_prev, xc, w_r, b_r, w_i, b_i, lam, t):
    s, w = dy.shape
    nblk = w // LANE
    nchunk = s // t
    steps = [1 << p for p in range(t.bit_length() - 1)]

    def body(dy_ref, gr_ref, h_ref, hp_ref, xc_ref, wr_ref, br_ref, wi_ref, bi_ref, lam_ref,
             dgr_ref, dxc_ref, dwr_ref, dwi_ref, dbr_ref, dbi_ref, dlam_ref):
        row = lax.broadcasted_iota(jnp.int32, (t, LANE), 0)
        for ref in (dwr_ref, dwi_ref, dbr_ref, dbi_ref, dlam_ref):
            ref[...] = jnp.zeros_like(ref)

        def chunk(cc, carry):
            lam_next, a_next = carry
            rows = pl.ds(pl.multiple_of((nchunk - 1 - cc) * t, t), t)
            dyb, hb, xcb = dy_ref[rows, :], h_ref[rows, :], xc_ref[rows, :]
            gel, dgel = _gelu_and_grad(gr_ref[rows, :])
            dgr_ref[rows, :] = dyb * hb * dgel
            w_r, w_i = wr_ref[...], wi_ref[...]
            r, i, sp_lam, a, mult = _lru_gates(xcb, w_r, br_ref[...], w_i, bi_ref[...], lam_ref[...])
            coef = jnp.where(row == t - 1, a_next, pltpu.roll(a, t - 1, 0))
            val = dyb * gel
            for d in steps:
                ok = row < t - d
                val = jnp.where(ok, coef * pltpu.roll(val, t - d, 0) + val, val)
                coef = jnp.where(ok, coef * pltpu.roll(coef, t - d, 0), coef)
            adj = val + coef * lam_next
            da = adj * hp_ref[rows, :]
            v = i * xcb
            dmult, dv = adj * v, adj * mult
            dlog_a = da * a - (a * a) * dmult / mult
            dr_pre = (-LRU_C * sp_lam) * dlog_a * r * (1.0 - r)
            di_pre = dv * xcb * i * (1.0 - i)
            dlam_ref[...] += _colsum(-LRU_C * r * dlog_a)
            dbr_ref[...] += _colsum(dr_pre)
            dbi_ref[...] += _colsum(di_pre)
            xb, drb, dib = xcb.astype(BF16), dr_pre.astype(BF16), di_pre.astype(BF16)
            dwr_ref[...] += lax.dot_general(xb, drb, TN, preferred_element_type=F32)
            dwi_ref[...] += lax.dot_general(xb, dib, TN, preferred_element_type=F32)
            dxc_ref[rows, :] = (dv * i + lax.dot_general(drb, w_r.astype(BF16), NT, preferred_element_type=F32)
                                + lax.dot_general(dib, w_i.astype(BF16), NT, preferred_element_type=F32))
            return adj[0:1, :], a[0:1, :]

        lax.fori_loop(0, nchunk, chunk, (jnp.zeros((1, LANE), F32), jnp.zeros((1, LANE), F32)))
        dlam_ref[...] = dlam_ref[...] * (-_sigmoid(-lam_ref[...]))

    col = lambda rows: pl.BlockSpec((rows, LANE), lambda n: (0, n))
    mat = pl.BlockSpec((None, LANE, LANE), lambda n: (n, 0, 0))
    return pl.pallas_call(
        body, name="lru_bwd", grid=(nblk,),
        in_specs=[col(s), pl.BlockSpec((s, LANE), lambda n: (0, gr_block0 + n)), col(s), col(s), col(s),
                  mat, col(1), mat, col(1), col(1)],
        out_specs=[col(s), col(s), mat, mat, col(1), col(1), col(1)],
        out_shape=[jax.ShapeDtypeStruct((s, w), F32)] * 2 + [jax.ShapeDtypeStruct((nblk, LANE, LANE), F32)] * 2
        + [jax.ShapeDtypeStruct((1, w), F32)] * 3,
        compiler_params=_params(("parallel",)),
    )(dy, proj, h, h_prev, xc, w_r, b_r, w_i, b_i, lam)


def _conv_bwd(xr_pad, dxc_pad, conv_w, t):
    s, w = xr_pad.shape[0] - CONV_HALO, xr_pad.shape[1]
    nchunk = s // t

    def body(x_ref, g_ref, cw_ref, dx_ref, dcw_ref, dcb_ref):
        dcw_ref[...] = jnp.zeros_like(dcw_ref)
        dcb_ref[...] = jnp.zeros_like(dcb_ref)

        def chunk(ci, _):
            t0 = pl.multiple_of(ci * t, t)
            shifted, weights = _conv_taps(x_ref[pl.ds(t0, t + CONV_HALO), :], cw_ref[...], t)
            gpad = g_ref[pl.ds(t0, t + CONV_HALO), :]
            g = gpad[:t, :]
            dx = weights[0] * g
            for d in range(1, CONV_WIDTH):
                dx = dx + weights[d] * pltpu.roll(gpad, t + CONV_HALO - d, 0)[:t, :]
            dx_ref[pl.ds(t0, t), :] = dx
            for d in range(CONV_WIDTH):
                dcw_ref[CONV_WIDTH - 1 - d:CONV_WIDTH - d, :] += _colsum(g * shifted[d])
            dcb_ref[...] += _colsum(g)
            return 0

        lax.fori_loop(0, nchunk, chunk, 0)

    col = lambda rows: pl.BlockSpec((rows, LANE), lambda n: (0, n))
    return pl.pallas_call(
        body, name="conv_bwd", grid=(w // LANE,),
        in_specs=[col(s + CONV_HALO), col(s + CONV_HALO), col(CONV_WIDTH)],
        out_specs=[col(s), col(CONV_WIDTH), col(1)],
        out_shape=[jax.ShapeDtypeStruct((s, w), F32), jax.ShapeDtypeStruct((CONV_WIDTH, w), F32),
                   jax.ShapeDtypeStruct((1, w), F32)],
        compiler_params=_params(("parallel",)),
    )(xr_pad, dxc_pad, conv_w)


def _sum_parts(parts_ref):
    g = parts_ref[0].astype(F32)
    for p in range(1, parts_ref.shape[0]):
        g = g + parts_ref[p].astype(F32)
    return g


def _reduce_parts(name, parts):
    p, r, c = parts.shape
    tr = _tile(r, max(8, (1 << 19) // c), 8)

    def body(parts_ref, g_ref):
        g_ref[...] = _sum_parts(parts_ref)

    return pl.pallas_call(
        body, name=name, grid=(r // tr,), in_specs=[pl.BlockSpec((p, tr, c), lambda i: (0, i, 0))],
        out_specs=pl.BlockSpec((tr, c), lambda i: (i, 0)), out_shape=jax.ShapeDtypeStruct((r, c), F32),
        compiler_params=_params(("parallel",)),
    )(parts)


def _adamw(name, parts, w, m, v):
    p, r, c = parts.shape
    tr = _tile(r, max(8, (1 << 18) // c), 8)

    def body(parts_ref, w_ref, m_ref, v_ref, g_ref, d_ref, nm_ref, nv_ref):
        g = _sum_parts(parts_ref)
        nm = ADAM_B1 * m_ref[...] + (1.0 - ADAM_B1) * g
        nv = ADAM_B2 * v_ref[...] + (1.0 - ADAM_B2) * (g * g)
        m_hat = nm / (1.0 - ADAM_B1 ** ADAM_STEP)
        v_hat = nv / (1.0 - ADAM_B2 ** ADAM_STEP)
        g_ref[...] = g
        d_ref[...] = -ADAM_LR * (m_hat / (jnp.sqrt(v_hat) + ADAM_EPS) + ADAM_WD * w_ref[...])
        nm_ref[...] = nm
        nv_ref[...] = nv

    blk = pl.BlockSpec((tr, c), lambda i: (i, 0))
    return pl.pallas_call(
        body, name=name, grid=(r // tr,), in_specs=[pl.BlockSpec((p, tr, c), lambda i: (0, i, 0)), blk, blk, blk],
        out_specs=[blk] * 4, out_shape=[jax.ShapeDtypeStruct((r, c), F32)] * 4,
        compiler_params=_params(("parallel",)),
    )(parts, w, m, v)


def _ffn_in(tag, y, w_in_g, tm):
    s, d = y.shape
    half = N_DEV // 2
    cb = w_in_g.shape[2]
    ff = half * cb

    def swiglu(acc, extra_refs, out_refs):
        g, u = acc
        out_refs[0][0] = g.astype(BF16)
        out_refs[0][1] = u.astype(BF16)
        out_refs[1][...] = (g * _sigmoid(g) * u).astype(BF16)

    gu_shape = (2, 1, s, ff)
    gu, act = _matmul(
        tag + "_in", y[None], w_in_g.reshape(2, half, d, cb), "nn", tm, cb, d,
        outs=[(gu_shape, BF16, _bspec(gu_shape, tm, cb, _ij)), ((1, s, ff), BF16, _bspec((1, s, ff), tm, cb, _ij))],
        epilogue=swiglu, b_buffers=1, side_by_side=True)
    return gu, act


def _ffn_out(tag, act, w_out_g, res, gate, tm):
    _, s, ff = act.shape
    d = res.shape[1]
    tn = _tile(d, 1024)

    def residual(acc, extra_refs, out_refs):
        out_refs[0][...] = acc[0]
        out_refs[1][...] = extra_refs[0][...] + 0.5 * extra_refs[1][...] * acc[0]

    plain = _bspec((1, s, d), tm, tn, _ij)
    o, h_new = _matmul(
        tag + "_out", act, w_out_g.reshape(1, ff, d), "nn", tm, tn, ff,
        outs=[((1, s, d), F32, plain), ((1, s, d), F32, plain)], epilogue=residual,
        extras=[(res[None], plain), (gate, _row_spec(tn))], b_buffers=1)
    return o[0], h_new[0]


def _after_token(token):
    return token, pl.BlockSpec(token.shape, lambda j, i, k: (0, 0))


def _ffn_bwd_weights(tag, do, y, gu, act, w_in_g, w_out_g, tm):
    s, d = do.shape
    half = N_DEV // 2
    cb = w_in_g.shape[2]
    ff = half * cb

    tn_d, tm_f = _tile(d, 1024), _tile(ff, 512)
    dw_out = _matmul(tag + "_dw_out", act, do[None], "tn", tm_f, tn_d, s,
                     outs=[((1, ff, d), BF16, _bspec((1, ff, d), tm_f, tn_d, _ij))], b_buffers=1)[0]
    dw_out = dw_out.reshape(N_DEV, ff // N_DEV, d)
    out_handles, token = _exchange_begin(tag + "_scatter_out", [dw_out], True)

    def dswiglu(acc, extra_refs, out_refs):
        dact = acc[0]
        g, u = extra_refs[0][0].astype(F32), extra_refs[0][1].astype(F32)
        sg = _sigmoid(g)
        out_refs[0][0] = (dact * u * sg * (1.0 + g * (1.0 - sg))).astype(BF16)
        out_refs[0][1] = (dact * g * sg).astype(BF16)

    gu_shape = (2, 1, s, ff)
    tn_e = 2 * cb
    gu_spec = _bspec(gu_shape, tm, tn_e, _ij)
    dgu = _matmul(tag + "_dact", do[None], w_out_g.reshape(1, ff, d), "nt", tm, tn_e, d,
                  outs=[(gu_shape, BF16, gu_spec)], epilogue=dswiglu, extras=[(gu, gu_spec), _after_token(token)],
                  b_buffers=1)[0]
    dgu = dgu.reshape(2, s, ff)

    tm_d = _tile(d, 512)

    def two_blocks(acc, extra_refs, out_refs):
        out_refs[0][0] = acc[0][:, :cb].astype(BF16)
        out_refs[0][1] = acc[0][:, cb:].astype(BF16)

    pair_spec = pl.BlockSpec((None, 2, tm_d, cb), lambda j, i, k: (j, 0, i, 0))
    dw_in = _matmul(tag + "_dw_in", y[None], dgu, "tn", tm_d, tn_e, s,
                    outs=[((half, 2, d, cb), BF16, pair_spec)], epilogue=two_blocks, b_buffers=1)[0]
    dw_in = dw_in.reshape(N_DEV, d, cb)
    in_handles, token = _exchange_begin(tag + "_scatter_in", [dw_in], True)
    return dgu, in_handles + out_handles, token


def _ffn_bwd_input(tag, dgu, w_in_g, token):
    s, d = dgu.shape[1], w_in_g.shape[1]
    tm_big = _tile(s, 1024, 8)
    return _matmul(tag + "_dy", dgu, w_in_g, "nt", tm_big, d, w_in_g.shape[2],
                   outs=[((1, s, d), F32, _bspec((1, s, d), tm_big, d, _ij))], extras=[_after_token(token)])[0][0]


def kernel(x, c, w_ada, b_ada, norm_ffn1, w_ffn1_in, w_ffn1_out, norm_mix, w_in, conv_w, conv_b, w_rg_gate, b_rg_gate, w_in_gate, b_in_gate, lru_lambda, w_branch_attn, w_branch_lru, w_out, norm_ffn2, w_ffn2_in, w_ffn2_out, norm_final, loss_target, m_w_ada, m_b_ada, m_norm_ffn1, m_w_ffn1_in, m_w_ffn1_out, m_norm_mix, m_w_in, m_conv_w, m_conv_b, m_w_rg_gate, m_b_rg_gate, m_w_in_gate, m_b_in_gate, m_lru_lambda, m_w_branch_attn, m_w_branch_lru, m_w_out, m_norm_ffn2, m_w_ffn2_in, m_w_ffn2_out, m_norm_final, v_w_ada, v_b_ada, v_norm_ffn1, v_w_ffn1_in, v_w_ffn1_out, v_norm_mix, v_w_in, v_conv_w, v_conv_b, v_w_rg_gate, v_b_rg_gate, v_w_in_gate, v_b_in_gate, v_lru_lambda, v_w_branch_attn, v_w_branch_lru, v_w_out, v_norm_ffn2, v_w_ffn2_in, v_w_ffn2_out, v_norm_final):
    xs, target = x[0], loss_target[0]
    s, d = xs.shape
    aw, lw = w_branch_attn.shape[1], w_branch_lru.shape[1]
    nh, nlb = aw // HEAD_DIM, w_rg_gate.shape[1]
    cba, cbi, cbb, cwb = w_ada.shape[2], w_in.shape[2], w_branch_attn.shape[2], conv_w.shape[2]
    assert lw == nlb * LANE and cwb * N_DEV == lw and 3 * aw + 2 * lw + 2 * d == cbi * N_DEV
    me = 4 * lax.axis_index("x") + 2 * lax.axis_index("y") + lax.axis_index("c")
    tm = _tile(s, 512, 8)
    tr = _tile(s, 256, 8)
    t_attn = _tile(s, 256, 8)
    t_lru = _tile(s, 256, 8)

    small = _exchange("gather_c", [jnp.concatenate([c, conv_w.reshape(1, CONV_WIDTH * cwb)], axis=1)], False)[0][:, 0, :]
    c_all = small[:, :d]
    conv_w_full = small[:, d:].reshape(N_DEV, CONV_WIDTH, cwb).transpose(1, 0, 2).reshape(CONV_WIDTH, lw)
    c_act = _rowwise("silu_c", lambda v: v * _sigmoid(v), [(c_all, d, 0)], [], [(d, F32)], [], N_DEV)[0]

    def add_bias(acc_ref, extra_refs, out_refs):
        out_refs[0][...] = acc_ref[0] + extra_refs[0][...]

    b_ada_mine = lax.dynamic_slice(b_ada, (0, me * cba), (1, cba))
    mod_part = _matmul("mod", c_act[None], w_ada, "nn", N_DEV, cba, _tile(d, 512),
                       outs=[((1, N_DEV, cba), F32, _bspec((1, N_DEV, cba), N_DEV, cba, _ij))], epilogue=add_bias,
                       extras=[(b_ada_mine, _row_spec(cba))])[0][0]
    mod_all = _exchange("gather_mod", [mod_part], False)[0]
    mod = lax.dynamic_index_in_dim(mod_all, me, axis=1, keepdims=False).reshape(1, 9 * d)
    sh1, sc1, g1, sh2, sc2, g2, sh3, sc3, g3 = [mod[:, n * d:(n + 1) * d] for n in range(9)]

    shards = [w_ffn1_in[0], w_ffn1_out[0], w_in[0], w_branch_attn[0], w_branch_lru[0], w_out[0], w_ffn2_in[0], w_ffn2_out[0]]
    early = (0, 1, 2)
    gathers, token = _exchange_begin("gather_w_early", [w.astype(BF16) for w in shards[:3]], False, mod, early)
    late, token = _exchange_begin("gather_w", [(w + token[0, 0]).astype(BF16) for w in shards[3:]], False)
    gathers = gathers + late

    def gathered(n, after):
        full = _exchange_end("gathered_w%d" % n, [gathers[n]], after, False)
        return (_sibling_forward("forwarded_w%d" % n, full) if n in early else full)[0]

    y1 = _norm_mod("norm1", xs, norm_ffn1 + token[:1, :1], sc1, sh1, tr)
    wf1i = gathered(0, y1)
    gu1, act1 = _ffn_in("ffn1", y1, wf1i, tm)
    wf1o = gathered(1, act1)
    o1, h1 = _ffn_out("ffn1", act1, wf1o, xs, g1, tm)

    y2 = _norm_mod("norm2", h1, norm_mix, sc2, sh2, tr)
    wi_g = gathered(2, y2)
    tn_i = _tile(cbi, 1152)
    tm_big = _tile(s, 1024, 8)
    proj = _matmul("mix_in", y2[None], wi_g.reshape(N_DEV // 2, 2, d, cbi), "nn", tm, cbi, d,
                   outs=[((1, s, N_DEV * cbi), F32, _bspec((1, s, N_DEV * cbi), tm, 2 * cbi, _ij))], b_buffers=1,
                   side_by_side=True, pairs=True)[0][0]
    off_xr, off_gr, off_ga, off_gl = 3 * aw, 3 * aw + lw, 3 * aw + 2 * lw, 3 * aw + 2 * lw + d
    qkv = proj[:, :3 * aw].astype(BF16)
    y_attn, attn_tot = _attn_fwd(qkv, nh, t_attn)
    xr_pad = jnp.pad(proj[:, off_xr:off_xr + lw], ((CONV_HALO, 0), (0, 0)))
    w_r, w_i = w_rg_gate[0], w_in_gate[0]
    h_lru, h_prev, xc, y_lru = _lru_fwd(xr_pad, proj, off_gr // LANE, conv_w_full, conv_b, w_r, b_rg_gate, w_i,
                                        b_in_gate, lru_lambda, t_lru)
    wba_p = gathered(3, y_attn).transpose(1, 0, 2).reshape(1, aw, d)
    wbl_p = gathered(4, y_lru).transpose(1, 0, 2).reshape(1, lw, d)
    proj3 = proj[None]
    tm_b = _tile(s, 1024, 8)
    tn_m = _tile(math.gcd(d, off_ga, off_gl), 1024)
    plain_m = _bspec((1, s, d), tm_b, tn_m, _ij)
    gate_specs = [_bspec(proj3.shape, tm_b, tn_m, functools.partial(lambda i, j, k, o: (i, j + o), o=o // tn_m))
                  for o in (off_ga, off_gl)]
    ya = _matmul("branch_attn", y_attn[None], wba_p, "nn", tm_b, tn_m, aw, outs=[((1, s, d), F32, plain_m)], b_buffers=1)[0]

    def merge(acc_ref, extra_refs, out_refs):
        yl = acc_ref[0]
        ya_t, ga, gl = extra_refs[0][...], extra_refs[1][...], extra_refs[2][...]
        out_refs[0][...] = yl
        out_refs[1][...] = (_sigmoid(ga) * ya_t + _sigmoid(gl) * yl).astype(BF16)

    yl, merged = _matmul("branch_lru", y_lru[None], wbl_p, "nn", tm_b, tn_m, lw,
                         outs=[((1, s, d), F32, plain_m), ((1, s, d), BF16, plain_m)], epilogue=merge,
                         extras=[(ya, plain_m), (proj3, gate_specs[0]), (proj3, gate_specs[1])], b_buffers=1)
    tn_d = _tile(d, 1024)
    plain = _bspec((1, s, d), tm, tn_d, _ij)

    def residual(acc_ref, extra_refs, out_refs):
        o = acc_ref[0]
        out_refs[0][...] = o
        out_refs[1][...] = extra_refs[0][...] + extra_refs[1][...] * o

    wo_g = gathered(5, merged)
    mo, h2 = _matmul("mix_out", merged, wo_g.reshape(1, d, d), "nn", tm, tn_d, d,
                     outs=[((1, s, d), F32, plain), ((1, s, d), F32, plain)], epilogue=residual,
                     extras=[(h1[None], plain), (g2, _row_spec(tn_d))], b_buffers=1)
    mo, h2 = mo[0], h2[0]

    y3 = _norm_mod("norm3", h2, norm_ffn2, sc3, sh3, tr)
    wf2i = gathered(6, y3)
    gu3, act3 = _ffn_in("ffn2", y3, wf2i, tm)
    wf2o = gathered(7, act3)
    o3, h3 = _ffn_out("ffn2", act3, wf2o, h2, g3, tm)

    nf = norm_final.reshape(1, d)
    dh3, do3, loss_part, d_nf, dg3 = _loss_bwd("loss", h3, target, nf, o3, g3, tr)
    dgu3, scatter_ffn2, token = _ffn_bwd_weights("ffn2", do3, y3, gu3, act3, wf2i, wf2o, tm)
    dy3 = _ffn_bwd_input("ffn2", dgu3, wf2i, token)
    dh2, dmo, dsh3, dsc3, dn3, dg2 = _norm_mod_bwd("norm3_bwd", dy3, h2, dh3, norm_ffn2, sc3, tr, below=(mo, g2, 1.0))

    dwo = _matmul("mix_dw_out", merged, dmo[None], "tn", _tile(d, 512), tn_d, s,
                  outs=[((1, d, d), BF16, _bspec((1, d, d), _tile(d, 512), tn_d, _ij))], b_buffers=1)[0]

    def dmerge(acc_ref, extra_refs, out_refs):
        dm = acc_ref[0]
        ya_t, yl_t = extra_refs[0][...], extra_refs[1][...]
        sa, sl = _sigmoid(extra_refs[2][...]), _sigmoid(extra_refs[3][...])
        out_refs[0][...] = (dm * sa).astype(BF16)
        out_refs[1][...] = (dm * sl).astype(BF16)
        out_refs[2][...] = (dm * ya_t * sa * (1.0 - sa)).astype(BF16)
        out_refs[3][...] = (dm * yl_t * sl * (1.0 - sl)).astype(BF16)

    tn_m = _tile(math.gcd(d, off_ga, off_gl), 1024)
    plain_m = _bspec((1, s, d), tm, tn_m, _ij)
    gate_specs = [_bspec(proj3.shape, tm, tn_m, functools.partial(lambda i, j, k, o: (i, j + o), o=o // tn_m))
                  for o in (off_ga, off_gl)]
    dya, dyl, dga, dgl = _matmul("mix_dmerged", dmo[None], wo_g.reshape(1, d, d), "nt", tm, tn_m, d,
                                 outs=[((1, s, d), BF16, plain_m)] * 4, epilogue=dmerge,
                                 extras=[(ya, plain_m), (yl, plain_m), (proj3, gate_specs[0]), (proj3, gate_specs[1])],
                                 b_buffers=1)
    tm_a, tm_l = _tile(aw, 1024), _tile(lw, 1024)
    dwba = _matmul("dw_branch_attn", y_attn[None], dya, "tn", tm_a, cbb, s,
                   outs=[((N_DEV, aw, cbb), BF16, _bspec((N_DEV, aw, cbb), tm_a, cbb, _ij))])[0]
    dwbl = _matmul("dw_branch_lru", y_lru[None], dyl, "tn", tm_l, cbb, s,
                   outs=[((N_DEV, lw, cbb), BF16, _bspec((N_DEV, lw, cbb), tm_l, cbb, _ij))])[0]
    scatter_branch, token = _exchange_begin("scatter_branch", [dwba, dwbl, dwo.reshape(N_DEV, d // N_DEV, d)], True)
    tn_a, tn_l = _tile(aw, 1024), _tile(lw, 1024)
    dy_attn = _matmul("d_attn_out", dya, wba_p, "nt", tm_b, tn_a, d,
                      outs=[((1, s, aw), BF16, _bspec((1, s, aw), tm_b, tn_a, _ij))], extras=[_after_token(token)],
                      b_buffers=1)[0][0]
    dy_lru = _matmul("d_lru_out", dyl, wbl_p, "nt", tm_b, tn_l, d,
                     outs=[((1, s, lw), F32, _bspec((1, s, lw), tm_b, tn_l, _ij))], b_buffers=1)[0][0]
    dq, dk, dv = _attn_bwd(qkv, dy_attn, attn_tot, nh, t_attn)
    dgr, dxc, d_wr, d_wi, d_br, d_bi, d_lam = _lru_bwd(dy_lru, proj, off_gr // LANE, h_lru, h_prev, xc, w_r, b_rg_gate,
                                                       w_i, b_in_gate, lru_lambda, t_lru)
    dxr, d_cw, d_cb = _conv_bwd(xr_pad, jnp.pad(dxc, ((0, CONV_HALO), (0, 0))), conv_w_full, t_lru)
    dproj = jnp.concatenate([dq.astype(BF16), dk.astype(BF16), dv.astype(BF16), dxr.astype(BF16), dgr.astype(BF16),
                             dga[0], dgl[0]], axis=1)
    tm_d = _tile(d, 1024)
    dwi = _matmul("mix_dw_in", y2[None], dproj[None], "tn", tm_d, tn_i, s,
                  outs=[((N_DEV, d, cbi), BF16, _bspec((N_DEV, d, cbi), tm_d, tn_i, _ij))], b_buffers=1)[0]
    scatter_mix, token = _exchange_begin("scatter_mix", [dwi], True)
    dy2 = _matmul("mix_dy", dproj[None], wi_g, "nt", tm_big, d, tn_i,
                  outs=[((1, s, d), F32, _bspec((1, s, d), tm_big, d, _ij))], extras=[_after_token(token)])[0][0]
    dh1, do1, dsh2, dsc2, dn2, dg1 = _norm_mod_bwd("norm2_bwd", dy2, h1, dh2, norm_mix, sc2, tr, below=(o1, g1, 0.5))

    dgu1, scatter_ffn1, token = _ffn_bwd_weights("ffn1", do1, y1, gu1, act1, wf1i, wf1o, tm)
    dy1 = _ffn_bwd_input("ffn1", dgu1, wf1i, token)
    grad_x, dsh1, dsc1, dn1 = _norm_mod_bwd("norm1_bwd", dy1, xs, dh1, norm_ffn1, sc1, tr)

    results = {}

    def update_group(wait_name, handles, leaves, after):
        for (n, w, m, v), parts in zip(leaves, _exchange_end(wait_name, handles, after, True)):
            results[n] = [o[None] for o in _adamw("adamw_" + n, parts, w[0], m[0], v[0])]
        return results[leaves[-1][0]][0]

    done = update_group("scattered_ffn2", scatter_ffn2, [("w_ffn2_in", w_ffn2_in, m_w_ffn2_in, v_w_ffn2_in),
                                                         ("w_ffn2_out", w_ffn2_out, m_w_ffn2_out, v_w_ffn2_out)], grad_x)
    done = update_group("scattered_mix", scatter_branch + scatter_mix,
                        [("w_branch_attn", w_branch_attn, m_w_branch_attn, v_w_branch_attn),
                         ("w_branch_lru", w_branch_lru, m_w_branch_lru, v_w_branch_lru),
                         ("w_out", w_out, m_w_out, v_w_out), ("w_in", w_in, m_w_in, v_w_in)], done)

    lane_pad = jnp.zeros((1, 7 * LANE), F32)
    pack = jnp.concatenate(
        [loss_part, lane_pad, dsh1, dsc1, dg1, dsh2, dsc2, dg2, dsh3, dsc3, dg3, dn1, dn2, dn3, d_nf, d_cb, d_br, d_bi, d_lam,
         d_cw.reshape(1, -1)], axis=1)
    pack = jnp.pad(pack, ((0, 0), (0, -pack.shape[1] % (8 * LANE))))
    gate_pack = jnp.concatenate([d_wr.reshape(-1, LANE), d_wi.reshape(-1, LANE)], axis=0).astype(BF16)
    n_pack, n_gate = pack.shape[1], gate_pack.size
    updated = sum(results[n][0][0, 0, 0] for n in sorted(results)).reshape(1, 1)
    small_handles, token = _exchange_begin("gather_small", [pack, gate_pack], False, updated)

    done = update_group("scattered_ffn1", scatter_ffn1, [("w_ffn1_in", w_ffn1_in, m_w_ffn1_in, v_w_ffn1_in),
                                                         ("w_ffn1_out", w_ffn1_out, m_w_ffn1_out, v_w_ffn1_out)], token)
    packs, gate_packs = _exchange_end("gathered_small", small_handles, done, False)
    packs = packs.reshape(N_DEV, n_pack // LANE, LANE)
    g_pack = _reduce_parts("sum_small", packs).reshape(1, n_pack)
    g_gate = _reduce_parts("sum_gates", gate_packs).reshape(1, n_gate)
    loss = g_pack[0, 0]
    off = 8 * LANE
    n_vec = 9 * d + 4 * d + 4 * lw
    n_adam = n_vec + n_gate
    g_small = jnp.concatenate([g_pack[:, off:off + n_vec], g_gate], axis=1).reshape(1, n_adam // LANE, LANE)
    d_cw_sum = g_pack[:, off + n_vec:off + n_vec + CONV_WIDTH * lw].reshape(CONV_WIDTH, lw)
    d_cw_mine = lax.dynamic_slice(d_cw_sum, (0, me * cwb), (CONV_WIDTH, cwb))

    small_names = ["b_ada", "norm_ffn1", "norm_mix", "norm_ffn2", "norm_final", "conv_b", "b_rg_gate", "b_in_gate",
                   "lru_lambda", "w_rg_gate", "w_in_gate"]
    given = dict(b_ada=(b_ada, m_b_ada, v_b_ada), norm_ffn1=(norm_ffn1, m_norm_ffn1, v_norm_ffn1),
                 norm_mix=(norm_mix, m_norm_mix, v_norm_mix), norm_ffn2=(norm_ffn2, m_norm_ffn2, v_norm_ffn2),
                 norm_final=(norm_final, m_norm_final, v_norm_final), conv_b=(conv_b, m_conv_b, v_conv_b),
                 b_rg_gate=(b_rg_gate, m_b_rg_gate, v_b_rg_gate), b_in_gate=(b_in_gate, m_b_in_gate, v_b_in_gate),
                 lru_lambda=(lru_lambda, m_lru_lambda, v_lru_lambda), w_rg_gate=(w_rg_gate, m_w_rg_gate, v_w_rg_gate),
                 w_in_gate=(w_in_gate, m_w_in_gate, v_w_in_gate))
    packed = [jnp.concatenate([given[n][q].reshape(1, -1) for n in small_names], axis=1).reshape(n_adam // LANE, LANE)
              for q in range(3)]
    small_out = _adamw("adamw_small", g_small, *packed)
    pos = 0
    for n in small_names:
        shape = given[n][0].shape
        size = math.prod(shape)
        results[n] = [o.reshape(1, n_adam)[:, pos:pos + size].reshape(shape) for o in small_out]
        pos += size
    results["conv_w"] = [o.reshape(conv_w.shape) for o in
                         _adamw("adamw_conv_w", d_cw_mine[None], conv_w[0], m_conv_w[0], v_conv_w[0])]

    dmod_all = packs.reshape(N_DEV, n_pack)[:, off:off + 9 * d]
    dmod_mine = lax.dynamic_slice(dmod_all, (0, me * cba), (N_DEV, cba))
    dmod_rows = jnp.pad(dmod_mine, ((0, LANE - N_DEV), (0, 0)))
    c_act_t = jnp.pad(c_act.T, ((0, 0), (0, LANE - N_DEV)))
    tm_d2 = _tile(d, 256)
    d_wada = _matmul("dw_ada", c_act_t[None], dmod_rows[None], "nn", tm_d2, cba, LANE,
                     outs=[((1, d, cba), F32, _bspec((1, d, cba), tm_d2, cba, _ij))])[0]
    results["w_ada"] = [o[None] for o in _adamw("adamw_w_ada", d_wada, w_ada[0], m_w_ada[0], v_w_ada[0])]


    order = ["w_ada", "b_ada", "norm_ffn1", "w_ffn1_in", "w_ffn1_out", "norm_mix", "w_in", "conv_w", "conv_b", "w_rg_gate",
             "b_rg_gate", "w_in_gate", "b_in_gate", "lru_lambda", "w_branch_attn", "w_branch_lru", "w_out", "norm_ffn2",
             "w_ffn2_in", "w_ffn2_out", "norm_final"]
    return (loss, grad_x[None], *[results[n][0] for n in order], *[results[n][1] for n in order],
            *[results[n][2] for n in order], *[results[n][3] for n in order])
```

```python
import functools
import math

import jax
import jax.numpy as jnp
from jax import lax
from jax.experimental import pallas as pl
from jax.experimental.pallas import tpu as pltpu

F32 = jnp.float32
BF16 = jnp.bfloat16
N_DEV = 8
HEAD_DIM = 128
CONV_WIDTH = 4
CONV_HALO = 8
LRU_C = 8.0
EPS = 1e-6
ADAM_LR, ADAM_B1, ADAM_B2, ADAM_EPS, ADAM_WD, ADAM_STEP = 0.001, 0.9, 0.999, 1e-08, 0.01, 10
LANE = 128
VMEM_LIMIT = 56 * 1024 * 1024
MESH = pl.DeviceIdType.MESH

NT = (((1,), (1,)), ((), ()))
NN = (((1,), (0,)), ((), ()))
TN = (((0,), (0,)), ((), ()))


def _tile(dim, target, align=LANE):
    t = (min(target, dim) // align) * align
    while t >= align:
        if dim % t == 0:
            return t
        t -= align
    return dim


def _params(sem):
    return pltpu.CompilerParams(dimension_semantics=sem, vmem_limit_bytes=VMEM_LIMIT)


def _sigmoid(x):
    return 1.0 / (1.0 + jnp.exp(-x))


def _softplus(x):
    return jnp.maximum(x, 0.0) + jnp.log(1.0 + jnp.exp(-jnp.abs(x)))


def _log1p(z):
    w = 1.0 + z
    return jnp.where(w == 1.0, z, jnp.log(w) * z / jnp.where(w == 1.0, 1.0, w - 1.0))


def _expm1(x):
    poly = x * (1.0 + x * (0.5 + x * (1.0 / 6 + x * (1.0 / 24 + x * (1.0 / 120 + x * (1.0 / 720))))))
    return jnp.where(jnp.abs(x) < 0.25, poly, jnp.exp(x) - 1.0)


_GELU_C = math.sqrt(2.0 / math.pi)


def _gelu_and_grad(x):
    inner = _GELU_C * (x + 0.044715 * x * x * x)
    th = jnp.tanh(inner)
    val = 0.5 * x * (1.0 + th)
    grad = 0.5 * (1.0 + th) + 0.5 * x * (1.0 - th * th) * _GELU_C * (1.0 + 3 * 0.044715 * x * x)
    return val, grad


def _dot_split(x, u):
    hi = x.astype(BF16)
    lo = (x - hi.astype(F32)).astype(BF16)
    return jnp.dot(hi, u, preferred_element_type=F32) + jnp.dot(lo, u, preferred_element_type=F32)


def _mesh_position():
    x, y, c = lax.axis_index("x"), lax.axis_index("y"), lax.axis_index("c")
    return x, y, c, 4 * x + 2 * y + c


def _peers(x, y, c):
    out = []
    for mask in range(1, N_DEV):
        px = 1 - x if mask & 4 else x
        py = 1 - y if mask & 2 else y
        pc = 1 - c if mask & 1 else c
        out.append((mask, (px, py, pc), 4 * px + 2 * py + pc))
    return out


def _exchange(name, arrs, scatter, after=None):
    n = len(arrs)
    behind = [] if after is None else [after]

    def body(*refs):
        ins, outs = refs[:n], refs[n + len(behind):2 * n + len(behind)]
        send_sems, recv_sems, local_sems = refs[2 * n + len(behind):]
        x, y, c, me = _mesh_position()
        peers = _peers(x, y, c)
        waits = []
        for a in range(n):
            mine = ins[a].at[me] if scatter else ins[a]
            local = pltpu.make_async_copy(mine, outs[a].at[me], local_sems.at[a])
            local.start()
            waits.append(local.wait)
            for mask, dev, idx in peers:
                k = a * (N_DEV - 1) + mask - 1
                src = ins[a].at[idx] if scatter else ins[a]
                send = pltpu.make_async_remote_copy(src_ref=src, dst_ref=outs[a].at[me], send_sem=send_sems.at[k],
                                                    recv_sem=recv_sems.at[k], device_id=dev, device_id_type=MESH)
                send.start()
                arrival = pltpu.make_async_remote_copy(src_ref=src, dst_ref=outs[a].at[idx], send_sem=send_sems.at[k],
                                                       recv_sem=recv_sems.at[k], device_id=dev, device_id_type=MESH)
                waits.append(send.wait_send)
                waits.append(arrival.wait_recv)
        for w in waits:
            w()

    any_spec = pl.BlockSpec(memory_space=pl.ANY)
    out_shape = [jax.ShapeDtypeStruct(a.shape if scatter else (N_DEV,) + a.shape, a.dtype) for a in arrs]
    return pl.pallas_call(
        body, name=name, out_shape=out_shape, in_specs=[any_spec] * (n + len(behind)), out_specs=[any_spec] * n,
        scratch_shapes=[pltpu.SemaphoreType.DMA((n * (N_DEV - 1),)), pltpu.SemaphoreType.DMA((n * (N_DEV - 1),)),
                        pltpu.SemaphoreType.DMA((n,))],
    )(*arrs, *behind)


HBM_SPEC = pl.BlockSpec(memory_space=pltpu.HBM)
SEM_SPEC = pl.BlockSpec(memory_space=pltpu.SEMAPHORE)
DATAFLOW = pltpu.SideEffectType.DATAFLOW_SIDE_EFFECTING


ALL_MASKS = tuple(range(1, N_DEV))
SAME_CORE_MASKS = (1, 2, 4, 6)


def _exchange_begin(name, arrs, scatter, after=None, once_per_chip=()):
    n = len(arrs)
    lands = [lax.empty(a.shape if scatter else (N_DEV,) + a.shape, a.dtype) for a in arrs]
    behind = [] if after is None else [after]
    masks = [SAME_CORE_MASKS if a in once_per_chip else ALL_MASKS for a in range(n)]

    def body(*refs):
        srcs, zones, outs = refs[:n], refs[n:2 * n], refs[2 * n + len(behind):]
        x, y, c, me = _mesh_position()
        for a in range(n):
            send_sems, recv_sems = outs[4 * a], outs[4 * a + 1]
            for mask, dev, idx in _peers(x, y, c):
                if mask not in masks[a]:
                    continue
                pltpu.make_async_remote_copy(
                    src_ref=srcs[a].at[idx] if scatter else srcs[a], dst_ref=zones[a].at[me], send_sem=send_sems.at[mask - 1],
                    recv_sem=recv_sems.at[mask - 1], device_id=dev, device_id_type=MESH).start()
        outs[-1][...] = jnp.zeros_like(outs[-1])

    out_shape, out_specs, aliases = [], [], {}
    for a in range(n):
        out_shape += [pltpu.SemaphoreType.DMA((N_DEV - 1,)), pltpu.SemaphoreType.DMA((N_DEV - 1,)),
                      pltpu.HBM(arrs[a].shape, arrs[a].dtype), pltpu.HBM(lands[a].shape, lands[a].dtype)]
        out_specs += [SEM_SPEC, SEM_SPEC, HBM_SPEC, HBM_SPEC]
        aliases[a] = 4 * a + 2
        aliases[n + a] = 4 * a + 3
    out_shape.append(jax.ShapeDtypeStruct((8, LANE), F32))
    out_specs.append(pl.BlockSpec(memory_space=pltpu.VMEM))
    res = pl.pallas_call(
        body, name=name, out_shape=out_shape,
        in_specs=[HBM_SPEC] * (2 * n) + [pl.BlockSpec(memory_space=pl.ANY)] * len(behind),
        out_specs=out_specs, input_output_aliases=aliases, compiler_params=pltpu.CompilerParams(has_side_effects=DATAFLOW),
    )(*[pltpu.with_memory_space_constraint(v, pltpu.HBM) for v in list(arrs) + lands], *behind)
    return [tuple(res[4 * a:4 * a + 4]) + (masks[a],) for a in range(n)], res[-1]


def _exchange_end(name, handles, after, scatter):
    n = len(handles)
    me = 4 * lax.axis_index("x") + 2 * lax.axis_index("y") + lax.axis_index("c")

    def body(*refs):
        x, y, c, me = _mesh_position()
        for a in range(n):
            src, zone, send_sems, recv_sems = refs[4 * a:4 * a + 4]
            for mask, dev, idx in _peers(x, y, c):
                if mask not in handles[a][4]:
                    continue
                cp = pltpu.make_async_remote_copy(
                    src_ref=src.at[idx] if scatter else src, dst_ref=zone.at[idx], send_sem=send_sems.at[mask - 1],
                    recv_sem=recv_sems.at[mask - 1], device_id=dev, device_id_type=MESH)
                cp.wait_send()
                cp.wait_recv()

    operands, in_specs, out_shape, aliases = [], [], [], {}
    for a, (send_sems, recv_sems, src, zone, _) in enumerate(handles):
        operands += [src, zone, send_sems, recv_sems]
        in_specs += [HBM_SPEC, HBM_SPEC, SEM_SPEC, SEM_SPEC]
        out_shape += [pltpu.HBM(src.shape, src.dtype), pltpu.HBM(zone.shape, zone.dtype)]
        aliases[4 * a] = 2 * a
        aliases[4 * a + 1] = 2 * a + 1
    res = pl.pallas_call(
        body, name=name, out_shape=out_shape, in_specs=in_specs + [pl.BlockSpec(memory_space=pl.ANY)],
        out_specs=[HBM_SPEC] * (2 * n), input_output_aliases=aliases,
        compiler_params=pltpu.CompilerParams(has_side_effects=DATAFLOW),
    )(*operands, after)
    full = []
    for a in range(n):
        src, zone = res[2 * a], res[2 * a + 1]
        own = lax.dynamic_index_in_dim(src, me, 0, keepdims=False) if scatter else src
        full.append(lax.dynamic_update_index_in_dim(zone, own, me, 0))
    return full


def _sibling_forward(name, zones):
    n = len(zones)
    hops = (2, 4, 6)

    def body(*refs):
        outs, send_sems, recv_sems = refs[n:2 * n], refs[2 * n], refs[2 * n + 1]
        x, y, c, me = _mesh_position()
        sibling = (x, y, 1 - c)
        waits = []
        for a in range(n):
            for q, mask in enumerate(hops):
                chip = 4 * (1 - x if mask & 4 else x) + 2 * (1 - y if mask & 2 else y)
                k = a * len(hops) + q
                held, missing = outs[a].at[chip + c], outs[a].at[chip + 1 - c]
                send = pltpu.make_async_remote_copy(src_ref=held, dst_ref=held, send_sem=send_sems.at[k],
                                                    recv_sem=recv_sems.at[k], device_id=sibling, device_id_type=MESH)
                send.start()
                arrival = pltpu.make_async_remote_copy(src_ref=missing, dst_ref=missing, send_sem=send_sems.at[k],
                                                       recv_sem=recv_sems.at[k], device_id=sibling, device_id_type=MESH)
                waits += [send.wait_send, arrival.wait_recv]
        for w in waits:
            w()

    any_spec = pl.BlockSpec(memory_space=pl.ANY)
    return pl.pallas_call(
        body, name=name, out_shape=[jax.ShapeDtypeStruct(z.shape, z.dtype) for z in zones], in_specs=[any_spec] * n,
        out_specs=[any_spec] * n, input_output_aliases={a: a for a in range(n)},
        scratch_shapes=[pltpu.SemaphoreType.DMA((n * len(hops),)), pltpu.SemaphoreType.DMA((n * len(hops),))],
    )(*zones)


def _bspec(shape, tr, tc, rc, buffers=None):
    per = shape[-1] // tc
    mode = {} if buffers is None else dict(pipeline_mode=pl.Buffered(buffers))
    if len(shape) == 3:
        return pl.BlockSpec((None, tr, tc), lambda j, i, k: (rc(i, j, k)[1] // per, rc(i, j, k)[0], rc(i, j, k)[1] % per), **mode)
    return pl.BlockSpec((shape[0], None, tr, tc),
                        lambda j, i, k: (0, rc(i, j, k)[1] // per, rc(i, j, k)[0], rc(i, j, k)[1] % per), **mode)


def _ij(i, j, k):
    return i, j


def _row_spec(tn, col_tile_offset=0):
    return pl.BlockSpec((1, tn), lambda j, i, k: (0, j + col_tile_offset))


def _matmul(name, a, b, mode, tm, tn, tk, outs, epilogue=None, extras=(), b_buffers=None, side_by_side=False,
            pairs=False):
    groups = 1 if b.ndim == 3 else 2 if pairs else b.shape[0]
    if mode == "nn" and pairs:
        m, k_dim, n = a.shape[1], a.shape[0] * a.shape[2], b.shape[0] * tn
        a_spec = _bspec(a.shape, tm, tk, lambda i, j, k: (i, k))
        b_spec = pl.BlockSpec((None, 2, tk, tn), lambda j, i, k: (j, 0, k, 0),
                              **({} if b_buffers is None else dict(pipeline_mode=pl.Buffered(b_buffers))))
        dims = NN
    elif mode == "nn":
        m, k_dim, n = a.shape[1], a.shape[0] * a.shape[2], b.shape[-3] * b.shape[-1]
        a_spec = _bspec(a.shape, tm, tk, lambda i, j, k: (i, k))
        b_spec = _bspec(b.shape, tk, tn, lambda i, j, k: (k, j), b_buffers)
        dims = NN
    elif mode == "nt":
        m, k_dim, n = a.shape[1], a.shape[0] * a.shape[2], b.shape[-2]
        a_spec = _bspec(a.shape, tm, tk, lambda i, j, k: (i, k))
        b_spec = _bspec(b.shape, tn, tk, lambda i, j, k: (j, k), b_buffers)
        dims = NT
    else:
        m, k_dim, n = a.shape[0] * a.shape[2], a.shape[1], b.shape[-3] * b.shape[-1]
        a_spec = _bspec(a.shape, tk, tm, lambda i, j, k: (k, i))
        b_spec = _bspec(b.shape, tk, tn, lambda i, j, k: (k, j), b_buffers)
        dims = TN
    assert m % tm == 0 and n % tn == 0 and k_dim % tk == 0, (name, m, n, k_dim, tm, tn, tk)
    nk = k_dim // tk
    n_extra, n_out = len(extras), len(outs)

    def finish(acc, extra_refs, out_refs):
        if epilogue is None:
            out_refs[0][...] = acc[0].astype(out_refs[0].dtype)
        else:
            epilogue(acc, extra_refs, out_refs)

    def products(a_ref, b_ref):
        a_tile = a_ref[...].astype(BF16)
        return [lax.dot_general(a_tile, (b_ref[g] if b.ndim == 4 else b_ref[...]).astype(BF16), dims,
                                preferred_element_type=F32) for g in range(groups)]

    def body_whole_k(*refs):
        finish(products(refs[0], refs[1]), refs[2:2 + n_extra], refs[2 + n_extra:])

    def body_side_by_side(*refs):
        wide_ref = refs[-1]

        @pl.when(pl.program_id(1) == 0)
        def _():
            for g in range(groups):
                wide_ref[:, g * tn:(g + 1) * tn] = refs[1][g].astype(BF16)

        full = lax.dot_general(refs[0][...].astype(BF16), wide_ref[...], dims, preferred_element_type=F32)
        acc = [full] if pairs else [full[:, g * tn:(g + 1) * tn] for g in range(groups)]
        finish(acc, refs[2:2 + n_extra], refs[2 + n_extra:-1])

    def body_k_steps(*refs):
        acc_ref = refs[-1]
        k = pl.program_id(2)

        @pl.when(k == 0)
        def _():
            acc_ref[...] = jnp.zeros_like(acc_ref)

        for g, p in enumerate(products(refs[0], refs[1])):
            acc_ref[g] += p

        @pl.when(k == nk - 1)
        def _():
            finish([acc_ref[g] for g in range(groups)], refs[2:2 + n_extra], refs[2 + n_extra:2 + n_extra + n_out])

    if side_by_side:
        assert groups > 1 and nk == 1 and mode == "nn", name
        body, scratch, order = body_side_by_side, [pltpu.VMEM((tk, groups * tn), BF16)], "arbitrary"
    elif nk == 1:
        body, scratch, order = body_whole_k, [], "parallel"
    else:
        body, scratch, order = body_k_steps, [pltpu.VMEM((groups, tm, tn), F32)], "parallel"
    return pl.pallas_call(
        body, name=name, grid=(n // tn, m // tm, nk),
        in_specs=[a_spec, b_spec] + [s for _, s in extras],
        out_specs=[s for _, _, s in outs],
        out_shape=[jax.ShapeDtypeStruct(shape, dtype) for shape, dtype, _ in outs],
        scratch_shapes=scratch,
        compiler_params=_params(("parallel", order, "arbitrary")),
    )(a, b, *[arr for arr, _ in extras])


def _rowwise(name, fn, rows, vecs, outs, accs, tm):
    s = rows[0][0].shape[0]
    n_in, n_out = len(rows) + len(vecs), len(outs)

    def body(*refs):
        i = pl.program_id(0)
        res = fn(*[r[...] for r in refs[:n_in]])
        res = res if isinstance(res, tuple) else (res,)
        out_refs, acc_refs = refs[n_in:n_in + n_out], refs[n_in + n_out:]
        for ref, val in zip(out_refs, res[:n_out]):
            ref[...] = val.astype(ref.dtype)

        @pl.when(i == 0)
        def _():
            for ref in acc_refs:
                ref[...] = jnp.zeros_like(ref)

        for ref, val in zip(acc_refs, res[n_out:]):
            ref[...] += val

    in_specs = [pl.BlockSpec((tm, w), functools.partial(lambda i, cb: (i, cb), cb=cb)) for _, w, cb in rows]
    in_specs += [pl.BlockSpec(v.shape, lambda i: (0,) * v.ndim) for v in vecs]
    out_specs = [pl.BlockSpec((tm, w), lambda i: (i, 0)) for w, _ in outs] + [pl.BlockSpec((1, w), lambda i: (0, 0)) for w in accs]
    out_shape = [jax.ShapeDtypeStruct((s, w), dt) for w, dt in outs] + [jax.ShapeDtypeStruct((1, w), F32) for w in accs]
    return pl.pallas_call(
        body, name=name, grid=(s // tm,), in_specs=in_specs, out_specs=out_specs, out_shape=out_shape,
        compiler_params=_params(("arbitrary",)),
    )(*[r for r, _, _ in rows], *vecs)


def _colsum(v):
    return jnp.sum(v, axis=0, keepdims=True)


def _norm_mod(name, h, nw, sc, sh, tm):
    d = h.shape[1]

    def fn(hb, nwb, scb, shb):
        r = lax.rsqrt(jnp.mean(hb * hb, axis=-1, keepdims=True) + EPS)
        return (hb * r) * nwb * (1.0 + scb) + shb

    return _rowwise(name, fn, [(h, d, 0)], [nw, sc, sh], [(d, BF16)], [], tm)[0]


def _norm_mod_bwd(name, dy, h, dh_next, nw, sc, tm, below=None):
    d = h.shape[1]

    def fn(dyb, hb, dhb, *rest):
        nwb, scb = rest[-2:] if below is None else rest[1:3]
        r = lax.rsqrt(jnp.mean(hb * hb, axis=-1, keepdims=True) + EPS)
        xh = hb * r
        dxh = dyb * (nwb * (1.0 + scb))
        dx = r * (dxh - xh * jnp.mean(dxh * xh, axis=-1, keepdims=True))
        dh = dhb + dx
        sums = (_colsum(dyb), _colsum(dyb * xh * nwb), _colsum(dyb * xh * (1.0 + scb)))
        if below is None:
            return (dh,) + sums
        ob, gb = rest[0], rest[3]
        return (dh, dh * (below[2] * gb)) + sums + (_colsum(dh * ob * below[2]),)

    rows = [(dy, d, 0), (h, d, 0), (dh_next, d, 0)]
    if below is None:
        return _rowwise(name, fn, rows, [nw, sc], [(d, F32)], [d, d, d], tm)
    return _rowwise(name, fn, rows + [(below[0], d, 0)], [nw, sc, below[1]], [(d, F32), (d, BF16)], [d, d, d, d], tm)


def _loss_bwd(name, h, target, nw, o, g, tm):
    d = h.shape[1]

    def fn(hb, tb, ob, nwb, gb):
        r = lax.rsqrt(jnp.mean(hb * hb, axis=-1, keepdims=True) + EPS)
        xh = hb * r
        err = xh * nwb - tb
        dy = err * (1.0 / d)
        dxh = dy * nwb
        dx = r * (dxh - xh * jnp.mean(dxh * xh, axis=-1, keepdims=True))
        loss = 0.5 * jnp.sum(jnp.mean(err * err, axis=-1, keepdims=True), axis=0, keepdims=True)
        return dx, dx * (0.5 * gb), jnp.broadcast_to(loss, (1, LANE)), _colsum(dy * xh), _colsum(dx * ob * 0.5)

    return _rowwise(name, fn, [(h, d, 0), (target, d, 0), (o, d, 0)], [nw, g], [(d, F32), (d, BF16)], [LANE, d, d], tm)


def _head_group(nh, most):
    return max(g for g in (1, 2, 4) if g <= most and nh % g == 0)


def _attn_fwd(qkv, nh, t):
    s = qkv.shape[0]
    scale = HEAD_DIM ** -0.5
    hp = _head_group(nh, 4)
    wide = hp * HEAD_DIM
    lanes = [slice(u * HEAD_DIM, (u + 1) * HEAD_DIM) for u in range(hp)]

    def body(q_ref, k_ref, v_ref, y_ref, tot_ref):
        i = pl.program_id(1)
        row = lax.broadcasted_iota(jnp.int32, (t, t), 0)
        col = lax.broadcasted_iota(jnp.int32, (t, t), 1)
        later = (row > col).astype(BF16)
        causal = col < row
        qs = [q_ref[:, ln] for ln in lanes]

        def block(j, carry, diagonal):
            ks = pl.ds(pl.multiple_of(j * t, t), t)
            heads = range(hp)
            z = [lax.dot_general(qs[u], k_ref[ks, lanes[u]], NT, preferred_element_type=F32) * scale for u in heads]
            sp = [_softplus(z[u]) for u in heads]
            log_keep = [jnp.where(causal, -sp[u], 0.0) if diagonal else -sp[u] for u in heads]
            between = [_dot_split(log_keep[u], later) for u in heads]
            w = [jnp.exp(z[u] - sp[u] + between[u] + carry[u][1]) for u in heads]
            if diagonal:
                w = [jnp.where(causal, w[u], 0.0) for u in heads]
            o = [carry[u][0] + jnp.dot(w[u].astype(BF16), v_ref[ks, lanes[u]], preferred_element_type=F32) for u in heads]
            return tuple((o[u], carry[u][1] + jnp.sum(log_keep[u], axis=1, keepdims=True)) for u in heads)

        carry = tuple((jnp.zeros((t, HEAD_DIM), F32), jnp.zeros((t, 1), F32)) for _ in lanes)
        carry = block(i, carry, True)
        carry = lax.fori_loop(0, i, lambda jj, cr: block(i - 1 - jj, cr, False), carry)
        for u, ln in enumerate(lanes):
            y_ref[:, ln] = carry[u][0].astype(y_ref.dtype)
            tot_ref[:, ln] = jnp.broadcast_to(carry[u][1], (t, HEAD_DIM))

    g = nh // hp
    return pl.pallas_call(
        body, name="attn_fwd", grid=(g, s // t),
        in_specs=[pl.BlockSpec((t, wide), lambda h, i: (i, h)),
                  pl.BlockSpec((s, wide), lambda h, i: (0, g + h)),
                  pl.BlockSpec((s, wide), lambda h, i: (0, 2 * g + h))],
        out_specs=[pl.BlockSpec((t, wide), lambda h, i: (i, h)), pl.BlockSpec((t, wide), lambda h, i: (i, h))],
        out_shape=[jax.ShapeDtypeStruct((s, nh * HEAD_DIM), BF16), jax.ShapeDtypeStruct((s, nh * HEAD_DIM), F32)],
        compiler_params=_params(("parallel", "arbitrary")),
    )(qkv, qkv, qkv)


def _attn_bwd(qkv, dy, tot, nh, t):
    s = qkv.shape[0]
    scale = HEAD_DIM ** -0.5
    hp = _head_group(nh, 4)
    wide = hp * HEAD_DIM
    lanes = [slice(u * HEAD_DIM, (u + 1) * HEAD_DIM) for u in range(hp)]

    def body(q_ref, k_ref, v_ref, dy_ref, tot_ref, dq_ref, dk_out, dv_out, dk_ref, dv_ref):
        i = pl.program_id(1)

        @pl.when(i == 0)
        def _():
            dk_ref[...] = jnp.zeros_like(dk_ref)
            dv_ref[...] = jnp.zeros_like(dv_ref)

        row = lax.broadcasted_iota(jnp.int32, (t, t), 0)
        col = lax.broadcasted_iota(jnp.int32, (t, t), 1)
        upto = (row <= col).astype(BF16)
        before = (row < col).astype(BF16)
        causal = col < row
        qs = [q_ref[:, ln] for ln in lanes]
        dys = [dy_ref[:, ln] for ln in lanes]
        totals = [tot_ref[:, u * HEAD_DIM:u * HEAD_DIM + 1] for u in range(hp)]

        def block(j, carry, diagonal):
            ks = pl.ds(pl.multiple_of(j * t, t), t)
            heads = range(hp)
            kb = [k_ref[ks, ln] for ln in lanes]
            vb = [v_ref[ks, ln] for ln in lanes]
            z = [lax.dot_general(qs[u], kb[u], NT, preferred_element_type=F32) * scale for u in heads]
            dw = [lax.dot_general(dys[u], vb[u], NT, preferred_element_type=F32) for u in heads]
            sp = [_softplus(z[u]) for u in heads]
            log_keep = [jnp.where(causal, -sp[u], 0.0) if diagonal else -sp[u] for u in heads]
            upto_sum = [_dot_split(log_keep[u], upto) for u in heads]
            w = [jnp.exp(z[u] - sp[u] + (totals[u] - carry[u][1] - upto_sum[u])) for u in heads]
            if diagonal:
                w = [jnp.where(causal, w[u], 0.0) for u in heads]
            g = [dw[u] * w[u] for u in heads]
            g_before = [_dot_split(g[u], before) for u in heads]
            dz = [(g[u] * jnp.exp(-sp[u]) - jnp.exp(z[u] - sp[u]) * (carry[u][2] + g_before[u])) * scale for u in heads]
            if diagonal:
                dz = [jnp.where(causal, dz[u], 0.0) for u in heads]
            dzb = [dz[u].astype(BF16) for u in heads]
            dq = [carry[u][0] + jnp.dot(dzb[u], kb[u], preferred_element_type=F32) for u in heads]
            for u in heads:
                dk_ref[ks, lanes[u]] += lax.dot_general(dzb[u], qs[u], TN, preferred_element_type=F32)
            for u in heads:
                dv_ref[ks, lanes[u]] += lax.dot_general(w[u].astype(BF16), dys[u], TN, preferred_element_type=F32)
            return tuple((dq[u], carry[u][1] + jnp.sum(log_keep[u], axis=1, keepdims=True),
                          carry[u][2] + jnp.sum(g[u], axis=1, keepdims=True)) for u in heads)

        zero = jnp.zeros((t, 1), F32)
        carry = tuple((jnp.zeros((t, HEAD_DIM), F32), zero, zero) for _ in lanes)
        carry = lax.fori_loop(0, i, lambda j, cr: block(j, cr, False), carry)
        carry = block(i, carry, True)
        for u, ln in enumerate(lanes):
            dq_ref[:, ln] = carry[u][0].astype(dq_ref.dtype)

        @pl.when(i == pl.num_programs(1) - 1)
        def _():
            dk_out[...] = dk_ref[...].astype(dk_out.dtype)
            dv_out[...] = dv_ref[...].astype(dv_out.dtype)

    g = nh // hp
    tile = lambda off: pl.BlockSpec((t, wide), lambda h, i: (i, off + h))
    head = lambda off, **mode: pl.BlockSpec((s, wide), lambda h, i: (0, off + h), **mode)
    once = dict(pipeline_mode=pl.Buffered(1))
    return pl.pallas_call(
        body, name="attn_bwd", grid=(g, s // t),
        in_specs=[tile(0), head(g, **once), head(2 * g, **once), tile(0), tile(0)],
        out_specs=[tile(0), head(0), head(0)],
        out_shape=[jax.ShapeDtypeStruct((s, nh * HEAD_DIM), BF16)] * 3,
        scratch_shapes=[pltpu.VMEM((s, wide), F32), pltpu.VMEM((s, wide), F32)],
        compiler_params=_params(("parallel", "arbitrary")),
    )(qkv, qkv, qkv, dy, tot)


def _lru_gates(xc, w_r, b_r, w_i, b_i, lam):
    xb = xc.astype(BF16)
    r = _sigmoid(jnp.dot(xb, w_r.astype(BF16), preferred_element_type=F32) + b_r)
    i = _sigmoid(jnp.dot(xb, w_i.astype(BF16), preferred_element_type=F32) + b_i)
    neg_lam = -lam
    sp_lam = jnp.maximum(neg_lam, 0.0) + _log1p(jnp.exp(-jnp.abs(neg_lam)))
    log_a = -LRU_C * r * sp_lam
    a = jnp.exp(log_a)
    mult = jnp.sqrt(-_expm1(2.0 * log_a))
    return r, i, sp_lam, a, mult


def _conv_taps(xpad_chunk, conv_w, t):
    shifted = [xpad_chunk[CONV_HALO:, :]]
    for d in range(1, CONV_WIDTH):
        shifted.append(pltpu.roll(xpad_chunk, d, 0)[CONV_HALO:, :])
    weights = [conv_w[CONV_WIDTH - 1 - d:CONV_WIDTH - d, :] for d in range(CONV_WIDTH)]
    return shifted, weights


def _lru_fwd(xr_pad, proj, gr_block0, conv_w, conv_b, w_r, b_r, w_i, b_i, lam, t):
    s, w = xr_pad.shape[0] - CONV_HALO, xr_pad.shape[1]
    nblk = w // LANE
    nchunk = s // t
    steps = [1 << p for p in range(t.bit_length() - 1)]
    assert (1 << (t.bit_length() - 1)) == t and w_r.shape[1:] == (LANE, LANE)

    def body(x_ref, gr_ref, cw_ref, cb_ref, wr_ref, br_ref, wi_ref, bi_ref, lam_ref, h_ref, hp_ref, xc_ref, y_ref):
        row = lax.broadcasted_iota(jnp.int32, (t, LANE), 0)

        def chunk(ci, h_in):
            t0 = pl.multiple_of(ci * t, t)
            shifted, weights = _conv_taps(x_ref[pl.ds(t0, t + CONV_HALO), :], cw_ref[...], t)
            xc = cb_ref[...] + sum(wd * xs for wd, xs in zip(weights, shifted))
            r, i, _, a, mult = _lru_gates(xc, wr_ref[...], br_ref[...], wi_ref[...], bi_ref[...], lam_ref[...])
            coef, val = a, mult * (i * xc)
            for d in steps:
                ok = row >= d
                val = jnp.where(ok, coef * pltpu.roll(val, d, 0) + val, val)
                coef = jnp.where(ok, coef * pltpu.roll(coef, d, 0), coef)
            h = val + coef * h_in
            rows = pl.ds(t0, t)
            h_ref[rows, :] = h
            hp_ref[rows, :] = jnp.where(row == 0, h_in, pltpu.roll(h, 1, 0))
            xc_ref[rows, :] = xc
            y_ref[rows, :] = (h * _gelu_and_grad(gr_ref[rows, :])[0]).astype(y_ref.dtype)
            return h[t - 1:t, :]

        lax.fori_loop(0, nchunk, chunk, jnp.zeros((1, LANE), F32))

    col = lambda rows: pl.BlockSpec((rows, LANE), lambda n: (0, n))
    return pl.pallas_call(
        body, name="lru_fwd", grid=(nblk,),
        in_specs=[col(s + CONV_HALO), pl.BlockSpec((s, LANE), lambda n: (0, gr_block0 + n)), col(CONV_WIDTH), col(1),
                  pl.BlockSpec((None, LANE, LANE), lambda n: (n, 0, 0)), col(1),
                  pl.BlockSpec((None, LANE, LANE), lambda n: (n, 0, 0)), col(1), col(1)],
        out_specs=[col(s)] * 4,
        out_shape=[jax.ShapeDtypeStruct((s, w), F32)] * 3 + [jax.ShapeDtypeStruct((s, w), BF16)],
        compiler_params=_params(("parallel",)),
    )(xr_pad, proj, conv_w, conv_b, w_r, b_r, w_i, b_i, lam)


def _lru_bwd(dy, proj, gr_block0, h, h_prev, xc, w_r, b_r, w_i, b_i, lam, t):
    s, w = dy.shape
    nblk = w // LANE
    nchunk = s // t
    steps = [1 << p for p in range(t.bit_length() - 1)]

    def body(dy_ref, gr_ref, h_ref, hp_ref, xc_ref, wr_ref, br_ref, wi_ref, bi_ref, lam_ref,
             dgr_ref, dxc_ref, dwr_ref, dwi_ref, dbr_ref, dbi_ref, dlam_ref):
        row = lax.broadcasted_iota(jnp.int32, (t, LANE), 0)
        for ref in (dwr_ref, dwi_ref, dbr_ref, dbi_ref, dlam_ref):
            ref[...] = jnp.zeros_like(ref)

        def chunk(cc, carry):
            lam_next, a_next = carry
            rows = pl.ds(pl.multiple_of((nchunk - 1 - cc) * t, t), t)
            dyb, hb, xcb = dy_ref[rows, :], h_ref[rows, :], xc_ref[rows, :]
            gel, dgel = _gelu_and_grad(gr_ref[rows, :])
            dgr_ref[rows, :] = dyb * hb * dgel
            w_r, w_i = wr_ref[...], wi_ref[...]
            r, i, sp_lam, a, mult = _lru_gates(xcb, w_r, br_ref[...], w_i, bi_ref[...], lam_ref[...])
            coef = jnp.where(row == t - 1, a_next, pltpu.roll(a, t - 1, 0))
            val = dyb * gel
            for d in steps:
                ok = row < t - d
                val = jnp.where(ok, coef * pltpu.roll(val, t - d, 0) + val, val)
                coef = jnp.where(ok, coef * pltpu.roll(coef, t - d, 0), coef)
            adj = val + coef * lam_next
            da = adj * hp_ref[rows, :]
            v = i * xcb
            dmult, dv = adj * v, adj * mult
            dlog_a = da * a - (a * a) * dmult / mult
            dr_pre = (-LRU_C * sp_lam) * dlog_a * r * (1.0 - r)
            di_pre = dv * xcb * i * (1.0 - i)
            dlam_ref[...] += _colsum(-LRU_C * r * dlog_a)
            dbr_ref[...] += _colsum(dr_pre)
            dbi_ref[...] += _colsum(di_pre)
            xb, drb, dib = xcb.astype(BF16), dr_pre.astype(BF16), di_pre.astype(BF16)
            dwr_ref[...] += lax.dot_general(xb, drb, TN, preferred_element_type=F32)
            dwi_ref[...] += lax.dot_general(xb, dib, TN, preferred_element_type=F32)
            dxc_ref[rows, :] = (dv * i + lax.dot_general(drb, w_r.astype(BF16), NT, preferred_element_type=F32)
                                + lax.dot_general(dib, w_i.astype(BF16), NT, preferred_element_type=F32))
            return adj[0:1, :], a[0:1, :]

        lax.fori_loop(0, nchunk, chunk, (jnp.zeros((1, LANE), F32), jnp.zeros((1, LANE), F32)))
        dlam_ref[...] = dlam_ref[...] * (-_sigmoid(-lam_ref[...]))

    col = lambda rows: pl.BlockSpec((rows, LANE), lambda n: (0, n))
    mat = pl.BlockSpec((None, LANE, LANE), lambda n: (n, 0, 0))
    return pl.pallas_call(
        body, name="lru_bwd", grid=(nblk,),
        in_specs=[col(s), pl.BlockSpec((s, LANE), lambda n: (0, gr_block0 + n)), col(s), col(s), col(s),
                  mat, col(1), mat, col(1), col(1)],
        out_specs=[col(s), col(s), mat, mat, col(1), col(1), col(1)],
        out_shape=[jax.ShapeDtypeStruct((s, w), F32)] * 2 + [jax.ShapeDtypeStruct((nblk, LANE, LANE), F32)] * 2
        + [jax.ShapeDtypeStruct((1, w), F32)] * 3,
        compiler_params=_params(("parallel",)),
    )(dy, proj, h, h_prev, xc, w_r, b_r, w_i, b_i, lam)


def _conv_bwd(xr_pad, dxc_pad, conv_w, t):
    s, w = xr_pad.shape[0] - CONV_HALO, xr_pad.shape[1]
    nchunk = s // t

    def body(x_ref, g_ref, cw_ref, dx_ref, dcw_ref, dcb_ref):
        dcw_ref[...] = jnp.zeros_like(dcw_ref)
        dcb_ref[...] = jnp.zeros_like(dcb_ref)

        def chunk(ci, _):
            t0 = pl.multiple_of(ci * t, t)
            shifted, weights = _conv_taps(x_ref[pl.ds(t0, t + CONV_HALO), :], cw_ref[...], t)
            gpad = g_ref[pl.ds(t0, t + CONV_HALO), :]
            g = gpad[:t, :]
            dx = weights[0] * g
            for d in range(1, CONV_WIDTH):
                dx = dx + weights[d] * pltpu.roll(gpad, t + CONV_HALO - d, 0)[:t, :]
            dx_ref[pl.ds(t0, t), :] = dx
            for d in range(CONV_WIDTH):
                dcw_ref[CONV_WIDTH - 1 - d:CONV_WIDTH - d, :] += _colsum(g * shifted[d])
            dcb_ref[...] += _colsum(g)
            return 0

        lax.fori_loop(0, nchunk, chunk, 0)

    col = lambda rows: pl.BlockSpec((rows, LANE), lambda n: (0, n))
    return pl.pallas_call(
        body, name="conv_bwd", grid=(w // LANE,),
        in_specs=[col(s + CONV_HALO), col(s + CONV_HALO), col(CONV_WIDTH)],
        out_specs=[col(s), col(CONV_WIDTH), col(1)],
        out_shape=[jax.ShapeDtypeStruct((s, w), F32), jax.ShapeDtypeStruct((CONV_WIDTH, w), F32),
                   jax.ShapeDtypeStruct((1, w), F32)],
        compiler_params=_params(("parallel",)),
    )(xr_pad, dxc_pad, conv_w)


def _sum_parts(parts_ref):
    g = parts_ref[0].astype(F32)
    for p in range(1, parts_ref.shape[0]):
        g = g + parts_ref[p].astype(F32)
    return g


def _reduce_parts(name, parts):
    p, r, c = parts.shape
    tr = _tile(r, max(8, (1 << 19) // c), 8)

    def body(parts_ref, g_ref):
        g_ref[...] = _sum_parts(parts_ref)

    return pl.pallas_call(
        body, name=name, grid=(r // tr,), in_specs=[pl.BlockSpec((p, tr, c), lambda i: (0, i, 0))],
        out_specs=pl.BlockSpec((tr, c), lambda i: (i, 0)), out_shape=jax.ShapeDtypeStruct((r, c), F32),
        compiler_params=_params(("parallel",)),
    )(parts)


def _adamw(name, parts, w, m, v):
    p, r, c = parts.shape
    tr = _tile(r, max(8, (1 << 18) // c), 8)

    def body(parts_ref, w_ref, m_ref, v_ref, g_ref, d_ref, nm_ref, nv_ref):
        g = _sum_parts(parts_ref)
        nm = ADAM_B1 * m_ref[...] + (1.0 - ADAM_B1) * g
        nv = ADAM_B2 * v_ref[...] + (1.0 - ADAM_B2) * (g * g)
        m_hat = nm / (1.0 - ADAM_B1 ** ADAM_STEP)
        v_hat = nv / (1.0 - ADAM_B2 ** ADAM_STEP)
        g_ref[...] = g
        d_ref[...] = -ADAM_LR * (m_hat / (jnp.sqrt(v_hat) + ADAM_EPS) + ADAM_WD * w_ref[...])
        nm_ref[...] = nm
        nv_ref[...] = nv

    blk = pl.BlockSpec((tr, c), lambda i: (i, 0))
    return pl.pallas_call(
        body, name=name, grid=(r // tr,), in_specs=[pl.BlockSpec((p, tr, c), lambda i: (0, i, 0)), blk, blk, blk],
        out_specs=[blk] * 4, out_shape=[jax.ShapeDtypeStruct((r, c), F32)] * 4,
        compiler_params=_params(("parallel",)),
    )(parts, w, m, v)


def _ffn_in(tag, y, w_in_g, tm):
    s, d = y.shape
    half = N_DEV // 2
    cb = w_in_g.shape[2]
    ff = half * cb

    def swiglu(acc, extra_refs, out_refs):
        g, u = acc
        out_refs[0][0] = g.astype(BF16)
        out_refs[0][1] = u.astype(BF16)
        out_refs[1][...] = (g * _sigmoid(g) * u).astype(BF16)

    gu_shape = (2, 1, s, ff)
    gu, act = _matmul(
        tag + "_in", y[None], w_in_g.reshape(2, half, d, cb), "nn", tm, cb, d,
        outs=[(gu_shape, BF16, _bspec(gu_shape, tm, cb, _ij)), ((1, s, ff), BF16, _bspec((1, s, ff), tm, cb, _ij))],
        epilogue=swiglu, b_buffers=1, side_by_side=True)
    return gu, act


def _ffn_out(tag, act, w_out_g, res, gate, tm):
    _, s, ff = act.shape
    d = res.shape[1]
    tn = _tile(d, 1024)

    def residual(acc, extra_refs, out_refs):
        out_refs[0][...] = acc[0]
        out_refs[1][...] = extra_refs[0][...] + 0.5 * extra_refs[1][...] * acc[0]

    plain = _bspec((1, s, d), tm, tn, _ij)
    o, h_new = _matmul(
        tag + "_out", act, w_out_g.reshape(1, ff, d), "nn", tm, tn, ff,
        outs=[((1, s, d), F32, plain), ((1, s, d), F32, plain)], epilogue=residual,
        extras=[(res[None], plain), (gate, _row_spec(tn))], b_buffers=1)
    return o[0], h_new[0]


def _after_token(token):
    return token, pl.BlockSpec(token.shape, lambda j, i, k: (0, 0))


def _ffn_bwd_weights(tag, do, y, gu, act, w_in_g, w_out_g, tm):
    s, d = do.shape
    half = N_DEV // 2
    cb = w_in_g.shape[2]
    ff = half * cb

    tn_d, tm_f = _tile(d, 1024), _tile(ff, 512)
    dw_out = _matmul(tag + "_dw_out", act, do[None], "tn", tm_f, tn_d, s,
                     outs=[((1, ff, d), BF16, _bspec((1, ff, d), tm_f, tn_d, _ij))], b_buffers=1)[0]
    dw_out = dw_out.reshape(N_DEV, ff // N_DEV, d)
    out_handles, token = _exchange_begin(tag + "_scatter_out", [dw_out], True)

    def dswiglu(acc, extra_refs, out_refs):
        dact = acc[0]
        g, u = extra_refs[0][0].astype(F32), extra_refs[0][1].astype(F32)
        sg = _sigmoid(g)
        out_refs[0][0] = (dact * u * sg * (1.0 + g * (1.0 - sg))).astype(BF16)
        out_refs[0][1] = (dact * g * sg).astype(BF16)

    gu_shape = (2, 1, s, ff)
    tn_e = 2 * cb
    gu_spec = _bspec(gu_shape, tm, tn_e, _ij)
    dgu = _matmul(tag + "_dact", do[None], w_out_g.reshape(1, ff, d), "nt", tm, tn_e, d,
                  outs=[(gu_shape, BF16, gu_spec)], epilogue=dswiglu, extras=[(gu, gu_spec), _after_token(token)],
                  b_buffers=1)[0]
    dgu = dgu.reshape(2, s, ff)

    tm_d = _tile(d, 512)

    def two_blocks(acc, extra_refs, out_refs):
        out_refs[0][0] = acc[0][:, :cb].astype(BF16)
        out_refs[0][1] = acc[0][:, cb:].astype(BF16)

    pair_spec = pl.BlockSpec((None, 2, tm_d, cb), lambda j, i, k: (j, 0, i, 0))
    dw_in = _matmul(tag + "_dw_in", y[None], dgu, "tn", tm_d, tn_e, s,
                    outs=[((half, 2, d, cb), BF16, pair_spec)], epilogue=two_blocks, b_buffers=1)[0]
    dw_in = dw_in.reshape(N_DEV, d, cb)
    in_handles, token = _exchange_begin(tag + "_scatter_in", [dw_in], True)
    return dgu, in_handles + out_handles, token


def _ffn_bwd_input(tag, dgu, w_in_g, token):
    s, d = dgu.shape[1], w_in_g.shape[1]
    tm_big = _tile(s, 1024, 8)
    return _matmul(tag + "_dy", dgu, w_in_g, "nt", tm_big, d, w_in_g.shape[2],
                   outs=[((1, s, d), F32, _bspec((1, s, d), tm_big, d, _ij))], extras=[_after_token(token)])[0][0]


def kernel(x, c, w_ada, b_ada, norm_ffn1, w_ffn1_in, w_ffn1_out, norm_mix, w_in, conv_w, conv_b, w_rg_gate, b_rg_gate, w_in_gate, b_in_gate, lru_lambda, w_branch_attn, w_branch_lru, w_out, norm_ffn2, w_ffn2_in, w_ffn2_out, norm_final, loss_target, m_w_ada, m_b_ada, m_norm_ffn1, m_w_ffn1_in, m_w_ffn1_out, m_norm_mix, m_w_in, m_conv_w, m_conv_b, m_w_rg_gate, m_b_rg_gate, m_w_in_gate, m_b_in_gate, m_lru_lambda, m_w_branch_attn, m_w_branch_lru, m_w_out, m_norm_ffn2, m_w_ffn2_in, m_w_ffn2_out, m_norm_final, v_w_ada, v_b_ada, v_norm_ffn1, v_w_ffn1_in, v_w_ffn1_out, v_norm_mix, v_w_in, v_conv_w, v_conv_b, v_w_rg_gate, v_b_rg_gate, v_w_in_gate, v_b_in_gate, v_lru_lambda, v_w_branch_attn, v_w_branch_lru, v_w_out, v_norm_ffn2, v_w_ffn2_in, v_w_ffn2_out, v_norm_final):
    xs, target = x[0], loss_target[0]
    s, d = xs.shape
    aw, lw = w_branch_attn.shape[1], w_branch_lru.shape[1]
    nh, nlb = aw // HEAD_DIM, w_rg_gate.shape[1]
    cba, cbi, cbb, cwb = w_ada.shape[2], w_in.shape[2], w_branch_attn.shape[2], conv_w.shape[2]
    assert lw == nlb * LANE and cwb * N_DEV == lw and 3 * aw + 2 * lw + 2 * d == cbi * N_DEV
    me = 4 * lax.axis_index("x") + 2 * lax.axis_index("y") + lax.axis_index("c")
    tm = _tile(s, 512, 8)
    tr = _tile(s, 256, 8)
    t_attn = _tile(s, 256, 8)
    t_lru = _tile(s, 256, 8)

    small = _exchange("gather_c", [jnp.concatenate([c, conv_w.reshape(1, CONV_WIDTH * cwb)], axis=1)], False)[0][:, 0, :]
    c_all = small[:, :d]
    conv_w_full = small[:, d:].reshape(N_DEV, CONV_WIDTH, cwb).transpose(1, 0, 2).reshape(CONV_WIDTH, lw)
    c_act = _rowwise("silu_c", lambda v: v * _sigmoid(v), [(c_all, d, 0)], [], [(d, F32)], [], N_DEV)[0]

    def add_bias(acc_ref, extra_refs, out_refs):
        out_refs[0][...] = acc_ref[0] + extra_refs[0][...]

    b_ada_mine = lax.dynamic_slice(b_ada, (0, me * cba), (1, cba))
    mod_part = _matmul("mod", c_act[None], w_ada, "nn", N_DEV, cba, _tile(d, 512),
                       outs=[((1, N_DEV, cba), F32, _bspec((1, N_DEV, cba), N_DEV, cba, _ij))], epilogue=add_bias,
                       extras=[(b_ada_mine, _row_spec(cba))])[0][0]
    mod_all = _exchange("gather_mod", [mod_part], False)[0]
    mod = lax.dynamic_index_in_dim(mod_all, me, axis=1, keepdims=False).reshape(1, 9 * d)
    sh1, sc1, g1, sh2, sc2, g2, sh3, sc3, g3 = [mod[:, n * d:(n + 1) * d] for n in range(9)]

    shards = [w_ffn1_in[0], w_ffn1_out[0], w_in[0], w_branch_attn[0], w_branch_lru[0], w_out[0], w_ffn2_in[0], w_ffn2_out[0]]
    early = (0, 1, 2)
    gathers, token = _exchange_begin("gather_w_early", [w.astype(BF16) for w in shards[:3]], False, mod, early)
    late, token = _exchange_begin("gather_w", [(w + token[0, 0]).astype(BF16) for w in shards[3:]], False)
    gathers = gathers + late

    def gathered(n, after):
        full = _exchange_end("gathered_w%d" % n, [gathers[n]], after, False)
        return (_sibling_forward("forwarded_w%d" % n, full) if n in early else full)[0]

    y1 = _norm_mod("norm1", xs, norm_ffn1 + token[:1, :1], sc1, sh1, tr)
    wf1i = gathered(0, y1)
    gu1, act1 = _ffn_in("ffn1", y1, wf1i, tm)
    wf1o = gathered(1, act1)
    o1, h1 = _ffn_out("ffn1", act1, wf1o, xs, g1, tm)

    y2 = _norm_mod("norm2", h1, norm_mix, sc2, sh2, tr)
    wi_g = gathered(2, y2)
    tn_i = _tile(cbi, 1152)
    tm_big = _tile(s, 1024, 8)
    proj = _matmul("mix_in", y2[None], wi_g.reshape(N_DEV // 2, 2, d, cbi), "nn", tm, cbi, d,
                   outs=[((1, s, N_DEV * cbi), F32, _bspec((1, s, N_DEV * cbi), tm, 2 * cbi, _ij))], b_buffers=1,
                   side_by_side=True, pairs=True)[0][0]
    off_xr, off_gr, off_ga, off_gl = 3 * aw, 3 * aw + lw, 3 * aw + 2 * lw, 3 * aw + 2 * lw + d
    qkv = proj[:, :3 * aw].astype(BF16)
    y_attn, attn_tot = _attn_fwd(qkv, nh, t_attn)
    xr_pad = jnp.pad(proj[:, off_xr:off_xr + lw], ((CONV_HALO, 0), (0, 0)))
    w_r, w_i = w_rg_gate[0], w_in_gate[0]
    h_lru, h_prev, xc, y_lru = _lru_fwd(xr_pad, proj, off_gr // LANE, conv_w_full, conv_b, w_r, b_rg_gate, w_i,
                                        b_in_gate, lru_lambda, t_lru)
    wba_p = gathered(3, y_attn).transpose(1, 0, 2).reshape(1, aw, d)
    wbl_p = gathered(4, y_lru).transpose(1, 0, 2).reshape(1, lw, d)
    proj3 = proj[None]
    tm_b = _tile(s, 1024, 8)
    tn_m = _tile(math.gcd(d, off_ga, off_gl), 1024)
    plain_m = _bspec((1, s, d), tm_b, tn_m, _ij)
    gate_specs = [_bspec(proj3.shape, tm_b, tn_m, functools.partial(lambda i, j, k, o: (i, j + o), o=o // tn_m))
                  for o in (off_ga, off_gl)]
    ya = _matmul("branch_attn", y_attn[None], wba_p, "nn", tm_b, tn_m, aw, outs=[((1, s, d), F32, plain_m)], b_buffers=1)[0]

    def merge(acc_ref, extra_refs, out_refs):
        yl = acc_ref[0]
        ya_t, ga, gl = extra_refs[0][...], extra_refs[1][...], extra_refs[2][...]
        out_refs[0][...] = yl
        out_refs[1][...] = (_sigmoid(ga) * ya_t + _sigmoid(gl) * yl).astype(BF16)

    yl, merged = _matmul("branch_lru", y_lru[None], wbl_p, "nn", tm_b, tn_m, lw,
                         outs=[((1, s, d), F32, plain_m), ((1, s, d), BF16, plain_m)], epilogue=merge,
                         extras=[(ya, plain_m), (proj3, gate_specs[0]), (proj3, gate_specs[1])], b_buffers=1)
    tn_d = _tile(d, 1024)
    plain = _bspec((1, s, d), tm, tn_d, _ij)

    def residual(acc_ref, extra_refs, out_refs):
        o = acc_ref[0]
        out_refs[0][...] = o
        out_refs[1][...] = extra_refs[0][...] + extra_refs[1][...] * o

    wo_g = gathered(5, merged)
    mo, h2 = _matmul("mix_out", merged, wo_g.reshape(1, d, d), "nn", tm, tn_d, d,
                     outs=[((1, s, d), F32, plain), ((1, s, d), F32, plain)], epilogue=residual,
                     extras=[(h1[None], plain), (g2, _row_spec(tn_d))], b_buffers=1)
    mo, h2 = mo[0], h2[0]

    y3 = _norm_mod("norm3", h2, norm_ffn2, sc3, sh3, tr)
    wf2i = gathered(6, y3)
    gu3, act3 = _ffn_in("ffn2", y3, wf2i, tm)
    wf2o = gathered(7, act3)
    o3, h3 = _ffn_out("ffn2", act3, wf2o, h2, g3, tm)

    nf = norm_final.reshape(1, d)
    dh3, do3, loss_part, d_nf, dg3 = _loss_bwd("loss", h3, target, nf, o3, g3, tr)
    dgu3, scatter_ffn2, token = _ffn_bwd_weights("ffn2", do3, y3, gu3, act3, wf2i, wf2o, tm)
    dy3 = _ffn_bwd_input("ffn2", dgu3, wf2i, token)
    dh2, dmo, dsh3, dsc3, dn3, dg2 = _norm_mod_bwd("norm3_bwd", dy3, h2, dh3, norm_ffn2, sc3, tr, below=(mo, g2, 1.0))

    dwo = _matmul("mix_dw_out", merged, dmo[None], "tn", _tile(d, 512), tn_d, s,
                  outs=[((1, d, d), BF16, _bspec((1, d, d), _tile(d, 512), tn_d, _ij))], b_buffers=1)[0]

    def dmerge(acc_ref, extra_refs, out_refs):
        dm = acc_ref[0]
        ya_t, yl_t = extra_refs[0][...], extra_refs[1][...]
        sa, sl = _sigmoid(extra_refs[2][...]), _sigmoid(extra_refs[3][...])
        out_refs[0][...] = (dm * sa).astype(BF16)
        out_refs[1][...] = (dm * sl).astype(BF16)
        out_refs[2][...] = (dm * ya_t * sa * (1.0 - sa)).astype(BF16)
        out_refs[3][...] = (dm * yl_t * sl * (1.0 - sl)).astype(BF16)

    tn_m = _tile(math.gcd(d, off_ga, off_gl), 1024)
    plain_m = _bspec((1, s, d), tm, tn_m, _ij)
    gate_specs = [_bspec(proj3.shape, tm, tn_m, functools.partial(lambda i, j, k, o: (i, j + o), o=o // tn_m))
                  for o in (off_ga, off_gl)]
    dya, dyl, dga, dgl = _matmul("mix_dmerged", dmo[None], wo_g.reshape(1, d, d), "nt", tm, tn_m, d,
                                 outs=[((1, s, d), BF16, plain_m)] * 4, epilogue=dmerge,
                                 extras=[(ya, plain_m), (yl, plain_m), (proj3, gate_specs[0]), (proj3, gate_specs[1])],
                                 b_buffers=1)
    tm_a, tm_l = _tile(aw, 1024), _tile(lw, 1024)
    dwba = _matmul("dw_branch_attn", y_attn[None], dya, "tn", tm_a, cbb, s,
                   outs=[((N_DEV, aw, cbb), BF16, _bspec((N_DEV, aw, cbb), tm_a, cbb, _ij))])[0]
    dwbl = _matmul("dw_branch_lru", y_lru[None], dyl, "tn", tm_l, cbb, s,
                   outs=[((N_DEV, lw, cbb), BF16, _bspec((N_DEV, lw, cbb), tm_l, cbb, _ij))])[0]
    scatter_branch, token = _exchange_begin("scatter_branch", [dwba, dwbl, dwo.reshape(N_DEV, d // N_DEV, d)], True)
    tn_a, tn_l = _tile(aw, 1024), _tile(lw, 1024)
    dy_attn = _matmul("d_attn_out", dya, wba_p, "nt", tm_b, tn_a, d,
                      outs=[((1, s, aw), BF16, _bspec((1, s, aw), tm_b, tn_a, _ij))], extras=[_after_token(token)],
                      b_buffers=1)[0][0]
    dy_lru = _matmul("d_lru_out", dyl, wbl_p, "nt", tm_b, tn_l, d,
                     outs=[((1, s, lw), F32, _bspec((1, s, lw), tm_b, tn_l, _ij))], b_buffers=1)[0][0]
    dq, dk, dv = _attn_bwd(qkv, dy_attn, attn_tot, nh, t_attn)
    dgr, dxc, d_wr, d_wi, d_br, d_bi, d_lam = _lru_bwd(dy_lru, proj, off_gr // LANE, h_lru, h_prev, xc, w_r, b_rg_gate,
                                                       w_i, b_in_gate, lru_lambda, t_lru)
    dxr, d_cw, d_cb = _conv_bwd(xr_pad, jnp.pad(dxc, ((0, CONV_HALO), (0, 0))), conv_w_full, t_lru)
    dproj = jnp.concatenate([dq.astype(BF16), dk.astype(BF16), dv.astype(BF16), dxr.astype(BF16), dgr.astype(BF16),
                             dga[0], dgl[0]], axis=1)
    tm_d = _tile(d, 512)

    def two_blocks(acc_ref, extra_refs, out_refs):
        out_refs[0][0] = acc_ref[0][:, :cbi].astype(BF16)
        out_refs[0][1] = acc_ref[0][:, cbi:].astype(BF16)

    dwi = _matmul("mix_dw_in", y2[None], dproj[None], "tn", tm_d, 2 * cbi, s,
                  outs=[((N_DEV // 2, 2, d, cbi), BF16, pl.BlockSpec((None, 2, tm_d, cbi), lambda j, i, k: (j, 0, i, 0)))],
                  epilogue=two_blocks, b_buffers=1)[0].reshape(N_DEV, d, cbi)
    scatter_mix, token = _exchange_begin("scatter_mix", [dwi], True)
    dy2 = _matmul("mix_dy", dproj[None], wi_g, "nt", tm_big, d, tn_i,
                  outs=[((1, s, d), F32, _bspec((1, s, d), tm_big, d, _ij))], extras=[_after_token(token)])[0][0]
    dh1, do1, dsh2, dsc2, dn2, dg1 = _norm_mod_bwd("norm2_bwd", dy2, h1, dh2, norm_mix, sc2, tr, below=(o1, g1, 0.5))

    dgu1, scatter_ffn1, token = _ffn_bwd_weights("ffn1", do1, y1, gu1, act1, wf1i, wf1o, tm)
    dy1 = _ffn_bwd_input("ffn1", dgu1, wf1i, token)
    grad_x, dsh1, dsc1, dn1 = _norm_mod_bwd("norm1_bwd", dy1, xs, dh1, norm_ffn1, sc1, tr)

    results = {}

    def update_group(wait_name, handles, leaves, after):
        for (n, w, m, v), parts in zip(leaves, _exchange_end(wait_name, handles, after, True)):
            results[n] = [o[None] for o in _adamw("adamw_" + n, parts, w[0], m[0], v[0])]
        return results[leaves[-1][0]][0]

    done = update_group("scattered_ffn2", scatter_ffn2, [("w_ffn2_in", w_ffn2_in, m_w_ffn2_in, v_w_ffn2_in),
                                                         ("w_ffn2_out", w_ffn2_out, m_w_ffn2_out, v_w_ffn2_out)], grad_x)
    done = update_group("scattered_mix", scatter_branch + scatter_mix,
                        [("w_branch_attn", w_branch_attn, m_w_branch_attn, v_w_branch_attn),
                         ("w_branch_lru", w_branch_lru, m_w_branch_lru, v_w_branch_lru),
                         ("w_out", w_out, m_w_out, v_w_out), ("w_in", w_in, m_w_in, v_w_in)], done)

    lane_pad = jnp.zeros((1, 7 * LANE), F32)
    pack = jnp.concatenate(
        [loss_part, lane_pad, dsh1, dsc1, dg1, dsh2, dsc2, dg2, dsh3, dsc3, dg3, dn1, dn2, dn3, d_nf, d_cb, d_br, d_bi, d_lam,
         d_cw.reshape(1, -1)], axis=1)
    pack = jnp.pad(pack, ((0, 0), (0, -pack.shape[1] % (8 * LANE))))
    gate_pack = jnp.concatenate([d_wr.reshape(-1, LANE), d_wi.reshape(-1, LANE)], axis=0).astype(BF16)
    n_pack, n_gate = pack.shape[1], gate_pack.size
    updated = sum(results[n][0][0, 0, 0] for n in sorted(results)).reshape(1, 1)
    small_handles, token = _exchange_begin("gather_small", [pack, gate_pack], False, updated)

    done = update_group("scattered_ffn1", scatter_ffn1, [("w_ffn1_in", w_ffn1_in, m_w_ffn1_in, v_w_ffn1_in),
                                                         ("w_ffn1_out", w_ffn1_out, m_w_ffn1_out, v_w_ffn1_out)], token)
    packs, gate_packs = _exchange_end("gathered_small", small_handles, done, False)
    packs = packs.reshape(N_DEV, n_pack // LANE, LANE)
    g_pack = _reduce_parts("sum_small", packs).reshape(1, n_pack)
    g_gate = _reduce_parts("sum_gates", gate_packs).reshape(1, n_gate)
    loss = g_pack[0, 0]
    off = 8 * LANE
    n_vec = 9 * d + 4 * d + 4 * lw
    n_adam = n_vec + n_gate
    g_small = jnp.concatenate([g_pack[:, off:off + n_vec], g_gate], axis=1).reshape(1, n_adam // LANE, LANE)
    d_cw_sum = g_pack[:, off + n_vec:off + n_vec + CONV_WIDTH * lw].reshape(CONV_WIDTH, lw)
    d_cw_mine = lax.dynamic_slice(d_cw_sum, (0, me * cwb), (CONV_WIDTH, cwb))

    small_names = ["b_ada", "norm_ffn1", "norm_mix", "norm_ffn2", "norm_final", "conv_b", "b_rg_gate", "b_in_gate",
                   "lru_lambda", "w_rg_gate", "w_in_gate"]
    given = dict(b_ada=(b_ada, m_b_ada, v_b_ada), norm_ffn1=(norm_ffn1, m_norm_ffn1, v_norm_ffn1),
                 norm_mix=(norm_mix, m_norm_mix, v_norm_mix), norm_ffn2=(norm_ffn2, m_norm_ffn2, v_norm_ffn2),
                 norm_final=(norm_final, m_norm_final, v_norm_final), conv_b=(conv_b, m_conv_b, v_conv_b),
                 b_rg_gate=(b_rg_gate, m_b_rg_gate, v_b_rg_gate), b_in_gate=(b_in_gate, m_b_in_gate, v_b_in_gate),
                 lru_lambda=(lru_lambda, m_lru_lambda, v_lru_lambda), w_rg_gate=(w_rg_gate, m_w_rg_gate, v_w_rg_gate),
                 w_in_gate=(w_in_gate, m_w_in_gate, v_w_in_gate))
    packed = [jnp.concatenate([given[n][q].reshape(1, -1) for n in small_names], axis=1).reshape(n_adam // LANE, LANE)
              for q in range(3)]
    small_out = _adamw("adamw_small", g_small, *packed)
    pos = 0
    for n in small_names:
        shape = given[n][0].shape
        size = math.prod(shape)
        results[n] = [o.reshape(1, n_adam)[:, pos:pos + size].reshape(shape) for o in small_out]
        pos += size
    results["conv_w"] = [o.reshape(conv_w.shape) for o in
                         _adamw("adamw_conv_w", d_cw_mine[None], conv_w[0], m_conv_w[0], v_conv_w[0])]

    dmod_all = packs.reshape(N_DEV, n_pack)[:, off:off + 9 * d]
    dmod_mine = lax.dynamic_slice(dmod_all, (0, me * cba), (N_DEV, cba))
    dmod_rows = jnp.pad(dmod_mine, ((0, LANE - N_DEV), (0, 0)))
    c_act_t = jnp.pad(c_act.T, ((0, 0), (0, LANE - N_DEV)))
    tm_d2 = _tile(d, 256)
    d_wada = _matmul("dw_ada", c_act_t[None], dmod_rows[None], "nn", tm_d2, cba, LANE,
                     outs=[((1, d, cba), F32, _bspec((1, d, cba), tm_d2, cba, _ij))])[0]
    results["w_ada"] = [o[None] for o in _adamw("adamw_w_ada", d_wada, w_ada[0], m_w_ada[0], v_w_ada[0])]


    order = ["w_ada", "b_ada", "norm_ffn1", "w_ffn1_in", "w_ffn1_out", "norm_mix", "w_in", "conv_w", "conv_b", "w_rg_gate",
             "b_rg_gate", "w_in_gate", "b_in_gate", "lru_lambda", "w_branch_attn", "w_branch_lru", "w_out", "norm_ffn2",
             "w_ffn2_in", "w_ffn2_out", "norm_final"]
    return (loss, grad_x[None], *[results[n][0] for n in order], *[results[n][1] for n in order],
            *[results[n][2] for n in order], *[results[n][3] for n in order])
```

```python
import functools
import math

import jax
import jax.numpy as jnp
from jax import lax
from jax.experimental import pallas as pl
from jax.experimental.pallas import tpu as pltpu

F32 = jnp.float32
BF16 = jnp.bfloat16
N_DEV = 8
HEAD_DIM = 128
CONV_WIDTH = 4
CONV_HALO = 8
LRU_C = 8.0
EPS = 1e-6
ADAM_LR, ADAM_B1, ADAM_B2, ADAM_EPS, ADAM_WD, ADAM_STEP = 0.001, 0.9, 0.999, 1e-08, 0.01, 10
LANE = 128
VMEM_LIMIT = 56 * 1024 * 1024
MESH = pl.DeviceIdType.MESH

NT = (((1,), (1,)), ((), ()))
NN = (((1,), (0,)), ((), ()))
TN = (((0,), (0,)), ((), ()))


def _tile(dim, target, align=LANE):
    t = (min(target, dim) // align) * align
    while t >= align:
        if dim % t == 0:
            return t
        t -= align
    return dim


def _params(sem):
    return pltpu.CompilerParams(dimension_semantics=sem, vmem_limit_bytes=VMEM_LIMIT)


def _sigmoid(x):
    return 1.0 / (1.0 + jnp.exp(-x))


def _softplus(x):
    return jnp.maximum(x, 0.0) + jnp.log(1.0 + jnp.exp(-jnp.abs(x)))


def _log1p(z):
    w = 1.0 + z
    return jnp.where(w == 1.0, z, jnp.log(w) * z / jnp.where(w == 1.0, 1.0, w - 1.0))


def _expm1(x):
    poly = x * (1.0 + x * (0.5 + x * (1.0 / 6 + x * (1.0 / 24 + x * (1.0 / 120 + x * (1.0 / 720))))))
    return jnp.where(jnp.abs(x) < 0.25, poly, jnp.exp(x) - 1.0)


_GELU_C = math.sqrt(2.0 / math.pi)


def _gelu_and_grad(x):
    inner = _GELU_C * (x + 0.044715 * x * x * x)
    th = jnp.tanh(inner)
    val = 0.5 * x * (1.0 + th)
    grad = 0.5 * (1.0 + th) + 0.5 * x * (1.0 - th * th) * _GELU_C * (1.0 + 3 * 0.044715 * x * x)
    return val, grad


def _dot_split(x, u):
    hi = x.astype(BF16)
    lo = (x - hi.astype(F32)).astype(BF16)
    return jnp.dot(hi, u, preferred_element_type=F32) + jnp.dot(lo, u, preferred_element_type=F32)


def _mesh_position():
    x, y, c = lax.axis_index("x"), lax.axis_index("y"), lax.axis_index("c")
    return x, y, c, 4 * x + 2 * y + c


def _peers(x, y, c):
    out = []
    for mask in range(1, N_DEV):
        px = 1 - x if mask & 4 else x
        py = 1 - y if mask & 2 else y
        pc = 1 - c if mask & 1 else c
        out.append((mask, (px, py, pc), 4 * px + 2 * py + pc))
    return out


def _exchange(name, arrs, scatter, after=None):
    n = len(arrs)
    behind = [] if after is None else [after]

    def body(*refs):
        ins, outs = refs[:n], refs[n + len(behind):2 * n + len(behind)]
        send_sems, recv_sems, local_sems = refs[2 * n + len(behind):]
        x, y, c, me = _mesh_position()
        peers = _peers(x, y, c)
        waits = []
        for a in range(n):
            mine = ins[a].at[me] if scatter else ins[a]
            local = pltpu.make_async_copy(mine, outs[a].at[me], local_sems.at[a])
            local.start()
            waits.append(local.wait)
            for mask, dev, idx in peers:
                k = a * (N_DEV - 1) + mask - 1
                src = ins[a].at[idx] if scatter else ins[a]
                send = pltpu.make_async_remote_copy(src_ref=src, dst_ref=outs[a].at[me], send_sem=send_sems.at[k],
                                                    recv_sem=recv_sems.at[k], device_id=dev, device_id_type=MESH)
                send.start()
                arrival = pltpu.make_async_remote_copy(src_ref=src, dst_ref=outs[a].at[idx], send_sem=send_sems.at[k],
                                                       recv_sem=recv_sems.at[k], device_id=dev, device_id_type=MESH)
                waits.append(send.wait_send)
                waits.append(arrival.wait_recv)
        for w in waits:
            w()

    any_spec = pl.BlockSpec(memory_space=pl.ANY)
    out_shape = [jax.ShapeDtypeStruct(a.shape if scatter else (N_DEV,) + a.shape, a.dtype) for a in arrs]
    return pl.pallas_call(
        body, name=name, out_shape=out_shape, in_specs=[any_spec] * (n + len(behind)), out_specs=[any_spec] * n,
        scratch_shapes=[pltpu.SemaphoreType.DMA((n * (N_DEV - 1),)), pltpu.SemaphoreType.DMA((n * (N_DEV - 1),)),
                        pltpu.SemaphoreType.DMA((n,))],
    )(*arrs, *behind)


HBM_SPEC = pl.BlockSpec(memory_space=pltpu.HBM)
SEM_SPEC = pl.BlockSpec(memory_space=pltpu.SEMAPHORE)
DATAFLOW = pltpu.SideEffectType.DATAFLOW_SIDE_EFFECTING


ALL_MASKS = tuple(range(1, N_DEV))
SAME_CORE_MASKS = (1, 2, 4, 6)


def _exchange_begin(name, arrs, scatter, after=None, once_per_chip=()):
    n = len(arrs)
    lands = [lax.empty(a.shape if scatter else (N_DEV,) + a.shape, a.dtype) for a in arrs]
    behind = [] if after is None else [after]
    masks = [SAME_CORE_MASKS if a in once_per_chip else ALL_MASKS for a in range(n)]

    def body(*refs):
        srcs, zones, outs = refs[:n], refs[n:2 * n], refs[2 * n + len(behind):]
        x, y, c, me = _mesh_position()
        for a in range(n):
            send_sems, recv_sems = outs[4 * a], outs[4 * a + 1]
            for mask, dev, idx in _peers(x, y, c):
                if mask not in masks[a]:
                    continue
                pltpu.make_async_remote_copy(
                    src_ref=srcs[a].at[idx] if scatter else srcs[a], dst_ref=zones[a].at[me], send_sem=send_sems.at[mask - 1],
                    recv_sem=recv_sems.at[mask - 1], device_id=dev, device_id_type=MESH).start()
        outs[-1][...] = jnp.zeros_like(outs[-1])

    out_shape, out_specs, aliases = [], [], {}
    for a in range(n):
        out_shape += [pltpu.SemaphoreType.DMA((N_DEV - 1,)), pltpu.SemaphoreType.DMA((N_DEV - 1,)),
                      pltpu.HBM(arrs[a].shape, arrs[a].dtype), pltpu.HBM(lands[a].shape, lands[a].dtype)]
        out_specs += [SEM_SPEC, SEM_SPEC, HBM_SPEC, HBM_SPEC]
        aliases[a] = 4 * a + 2
        aliases[n + a] = 4 * a + 3
    out_shape.append(jax.ShapeDtypeStruct((8, LANE), F32))
    out_specs.append(pl.BlockSpec(memory_space=pltpu.VMEM))
    res = pl.pallas_call(
        body, name=name, out_shape=out_shape,
        in_specs=[HBM_SPEC] * (2 * n) + [pl.BlockSpec(memory_space=pl.ANY)] * len(behind),
        out_specs=out_specs, input_output_aliases=aliases, compiler_params=pltpu.CompilerParams(has_side_effects=DATAFLOW),
    )(*[pltpu.with_memory_space_constraint(v, pltpu.HBM) for v in list(arrs) + lands], *behind)
    return [tuple(res[4 * a:4 * a + 4]) + (masks[a],) for a in range(n)], res[-1]


def _exchange_end(name, handles, after, scatter):
    n = len(handles)
    me = 4 * lax.axis_index("x") + 2 * lax.axis_index("y") + lax.axis_index("c")

    def body(*refs):
        x, y, c, me = _mesh_position()
        for a in range(n):
            src, zone, send_sems, recv_sems = refs[4 * a:4 * a + 4]
            for mask, dev, idx in _peers(x, y, c):
                if mask not in handles[a][4]:
                    continue
                cp = pltpu.make_async_remote_copy(
                    src_ref=src.at[idx] if scatter else src, dst_ref=zone.at[idx], send_sem=send_sems.at[mask - 1],
                    recv_sem=recv_sems.at[mask - 1], device_id=dev, device_id_type=MESH)
                cp.wait_send()
                cp.wait_recv()

    operands, in_specs, out_shape, aliases = [], [], [], {}
    for a, (send_sems, recv_sems, src, zone, _) in enumerate(handles):
        operands += [src, zone, send_sems, recv_sems]
        in_specs += [HBM_SPEC, HBM_SPEC, SEM_SPEC, SEM_SPEC]
        out_shape += [pltpu.HBM(src.shape, src.dtype), pltpu.HBM(zone.shape, zone.dtype)]
        aliases[4 * a] = 2 * a
        aliases[4 * a + 1] = 2 * a + 1
    res = pl.pallas_call(
        body, name=name, out_shape=out_shape, in_specs=in_specs + [pl.BlockSpec(memory_space=pl.ANY)],
        out_specs=[HBM_SPEC] * (2 * n), input_output_aliases=aliases,
        compiler_params=pltpu.CompilerParams(has_side_effects=DATAFLOW),
    )(*operands, after)
    full = []
    for a in range(n):
        src, zone = res[2 * a], res[2 * a + 1]
        own = lax.dynamic_index_in_dim(src, me, 0, keepdims=False) if scatter else src
        full.append(lax.dynamic_update_index_in_dim(zone, own, me, 0))
    return full


def _sibling_forward(name, zones):
    n = len(zones)
    hops = (2, 4, 6)

    def body(*refs):
        outs, send_sems, recv_sems = refs[n:2 * n], refs[2 * n], refs[2 * n + 1]
        x, y, c, me = _mesh_position()
        sibling = (x, y, 1 - c)
        waits = []
        for a in range(n):
            for q, mask in enumerate(hops):
                chip = 4 * (1 - x if mask & 4 else x) + 2 * (1 - y if mask & 2 else y)
                k = a * len(hops) + q
                held, missing = outs[a].at[chip + c], outs[a].at[chip + 1 - c]
                send = pltpu.make_async_remote_copy(src_ref=held, dst_ref=held, send_sem=send_sems.at[k],
                                                    recv_sem=recv_sems.at[k], device_id=sibling, device_id_type=MESH)
                send.start()
                arrival = pltpu.make_async_remote_copy(src_ref=missing, dst_ref=missing, send_sem=send_sems.at[k],
                                                       recv_sem=recv_sems.at[k], device_id=sibling, device_id_type=MESH)
                waits += [send.wait_send, arrival.wait_recv]
        for w in waits:
            w()

    any_spec = pl.BlockSpec(memory_space=pl.ANY)
    return pl.pallas_call(
        body, name=name, out_shape=[jax.ShapeDtypeStruct(z.shape, z.dtype) for z in zones], in_specs=[any_spec] * n,
        out_specs=[any_spec] * n, input_output_aliases={a: a for a in range(n)},
        scratch_shapes=[pltpu.SemaphoreType.DMA((n * len(hops),)), pltpu.SemaphoreType.DMA((n * len(hops),))],
    )(*zones)


def _bspec(shape, tr, tc, rc, buffers=None):
    per = shape[-1] // tc
    mode = {} if buffers is None else dict(pipeline_mode=pl.Buffered(buffers))
    if len(shape) == 3:
        return pl.BlockSpec((None, tr, tc), lambda j, i, k: (rc(i, j, k)[1] // per, rc(i, j, k)[0], rc(i, j, k)[1] % per), **mode)
    return pl.BlockSpec((shape[0], None, tr, tc),
                        lambda j, i, k: (0, rc(i, j, k)[1] // per, rc(i, j, k)[0], rc(i, j, k)[1] % per), **mode)


def _ij(i, j, k):
    return i, j


def _row_spec(tn, col_tile_offset=0):
    return pl.BlockSpec((1, tn), lambda j, i, k: (0, j + col_tile_offset))


def _matmul(name, a, b, mode, tm, tn, tk, outs, epilogue=None, extras=(), b_buffers=None, side_by_side=False,
            pairs=False):
    groups = 1 if b.ndim == 3 else 2 if pairs else b.shape[0]
    if mode == "nn" and pairs:
        m, k_dim, n = a.shape[1], a.shape[0] * a.shape[2], b.shape[0] * tn
        a_spec = _bspec(a.shape, tm, tk, lambda i, j, k: (i, k))
        b_spec = pl.BlockSpec((None, 2, tk, tn), lambda j, i, k: (j, 0, k, 0),
                              **({} if b_buffers is None else dict(pipeline_mode=pl.Buffered(b_buffers))))
        dims = NN
    elif mode == "nn":
        m, k_dim, n = a.shape[1], a.shape[0] * a.shape[2], b.shape[-3] * b.shape[-1]
        a_spec = _bspec(a.shape, tm, tk, lambda i, j, k: (i, k))
        b_spec = _bspec(b.shape, tk, tn, lambda i, j, k: (k, j), b_buffers)
        dims = NN
    elif mode == "nt":
        m, k_dim, n = a.shape[1], a.shape[0] * a.shape[2], b.shape[-2]
        a_spec = _bspec(a.shape, tm, tk, lambda i, j, k: (i, k))
        b_spec = _bspec(b.shape, tn, tk, lambda i, j, k: (j, k), b_buffers)
        dims = NT
    else:
        m, k_dim, n = a.shape[0] * a.shape[2], a.shape[1], b.shape[-3] * b.shape[-1]
        a_spec = _bspec(a.shape, tk, tm, lambda i, j, k: (k, i))
        b_spec = _bspec(b.shape, tk, tn, lambda i, j, k: (k, j), b_buffers)
        dims = TN
    assert m % tm == 0 and n % tn == 0 and k_dim % tk == 0, (name, m, n, k_dim, tm, tn, tk)
    nk = k_dim // tk
    n_extra, n_out = len(extras), len(outs)

    def finish(acc, extra_refs, out_refs):
        if epilogue is None:
            out_refs[0][...] = acc[0].astype(out_refs[0].dtype)
        else:
            epilogue(acc, extra_refs, out_refs)

    def products(a_ref, b_ref):
        a_tile = a_ref[...].astype(BF16)
        return [lax.dot_general(a_tile, (b_ref[g] if b.ndim == 4 else b_ref[...]).astype(BF16), dims,
                                preferred_element_type=F32) for g in range(groups)]

    def body_whole_k(*refs):
        finish(products(refs[0], refs[1]), refs[2:2 + n_extra], refs[2 + n_extra:])

    def body_side_by_side(*refs):
        wide_ref = refs[-1]

        @pl.when(pl.program_id(1) == 0)
        def _():
            for g in range(groups):
                wide_ref[:, g * tn:(g + 1) * tn] = refs[1][g].astype(BF16)

        full = lax.dot_general(refs[0][...].astype(BF16), wide_ref[...], dims, preferred_element_type=F32)
        acc = [full] if pairs else [full[:, g * tn:(g + 1) * tn] for g in range(groups)]
        finish(acc, refs[2:2 + n_extra], refs[2 + n_extra:-1])

    def body_k_steps(*refs):
        acc_ref = refs[-1]
        k = pl.program_id(2)

        @pl.when(k == 0)
        def _():
            acc_ref[...] = jnp.zeros_like(acc_ref)

        for g, p in enumerate(products(refs[0], refs[1])):
            acc_ref[g] += p

        @pl.when(k == nk - 1)
        def _():
            finish([acc_ref[g] for g in range(groups)], refs[2:2 + n_extra], refs[2 + n_extra:2 + n_extra + n_out])

    if side_by_side:
        assert groups > 1 and nk == 1 and mode == "nn", name
        body, scratch, order = body_side_by_side, [pltpu.VMEM((tk, groups * tn), BF16)], "arbitrary"
    elif nk == 1:
        body, scratch, order = body_whole_k, [], "parallel"
    else:
        body, scratch, order = body_k_steps, [pltpu.VMEM((groups, tm, tn), F32)], "parallel"
    return pl.pallas_call(
        body, name=name, grid=(n // tn, m // tm, nk),
        in_specs=[a_spec, b_spec] + [s for _, s in extras],
        out_specs=[s for _, _, s in outs],
        out_shape=[jax.ShapeDtypeStruct(shape, dtype) for shape, dtype, _ in outs],
        scratch_shapes=scratch,
        compiler_params=_params(("parallel", order, "arbitrary")),
    )(a, b, *[arr for arr, _ in extras])


def _rowwise(name, fn, rows, vecs, outs, accs, tm):
    s = rows[0][0].shape[0]
    n_in, n_out = len(rows) + len(vecs), len(outs)

    def body(*refs):
        i = pl.program_id(0)
        res = fn(*[r[...] for r in refs[:n_in]])
        res = res if isinstance(res, tuple) else (res,)
        out_refs, acc_refs = refs[n_in:n_in + n_out], refs[n_in + n_out:]
        for ref, val in zip(out_refs, res[:n_out]):
            ref[...] = val.astype(ref.dtype)

        @pl.when(i == 0)
        def _():
            for ref in acc_refs:
                ref[...] = jnp.zeros_like(ref)

        for ref, val in zip(acc_refs, res[n_out:]):
            ref[...] += val

    in_specs = [pl.BlockSpec((tm, w), functools.partial(lambda i, cb: (i, cb), cb=cb)) for _, w, cb in rows]
    in_specs += [pl.BlockSpec(v.shape, lambda i: (0,) * v.ndim) for v in vecs]
    out_specs = [pl.BlockSpec((tm, w), lambda i: (i, 0)) for w, _ in outs] + [pl.BlockSpec((1, w), lambda i: (0, 0)) for w in accs]
    out_shape = [jax.ShapeDtypeStruct((s, w), dt) for w, dt in outs] + [jax.ShapeDtypeStruct((1, w), F32) for w in accs]
    return pl.pallas_call(
        body, name=name, grid=(s // tm,), in_specs=in_specs, out_specs=out_specs, out_shape=out_shape,
        compiler_params=_params(("arbitrary",)),
    )(*[r for r, _, _ in rows], *vecs)


def _colsum(v):
    return jnp.sum(v, axis=0, keepdims=True)


def _norm_mod(name, h, nw, sc, sh, tm):
    d = h.shape[1]

    def fn(hb, nwb, scb, shb):
        r = lax.rsqrt(jnp.mean(hb * hb, axis=-1, keepdims=True) + EPS)
        return (hb * r) * nwb * (1.0 + scb) + shb

    return _rowwise(name, fn, [(h, d, 0)], [nw, sc, sh], [(d, BF16)], [], tm)[0]


def _norm_mod_bwd(name, dy, h, dh_next, nw, sc, tm, below=None):
    d = h.shape[1]

    def fn(dyb, hb, dhb, *rest):
        nwb, scb = rest[-2:] if below is None else rest[1:3]
        r = lax.rsqrt(jnp.mean(hb * hb, axis=-1, keepdims=True) + EPS)
        xh = hb * r
        dxh = dyb * (nwb * (1.0 + scb))
        dx = r * (dxh - xh * jnp.mean(dxh * xh, axis=-1, keepdims=True))
        dh = dhb + dx
        sums = (_colsum(dyb), _colsum(dyb * xh * nwb), _colsum(dyb * xh * (1.0 + scb)))
        if below is None:
            return (dh,) + sums
        ob, gb = rest[0], rest[3]
        return (dh, dh * (below[2] * gb)) + sums + (_colsum(dh * ob * below[2]),)

    rows = [(dy, d, 0), (h, d, 0), (dh_next, d, 0)]
    if below is None:
        return _rowwise(name, fn, rows, [nw, sc], [(d, F32)], [d, d, d], tm)
    return _rowwise(name, fn, rows + [(below[0], d, 0)], [nw, sc, below[1]], [(d, F32), (d, BF16)], [d, d, d, d], tm)


def _loss_bwd(name, h, target, nw, o, g, tm):
    d = h.shape[1]

    def fn(hb, tb, ob, nwb, gb):
        r = lax.rsqrt(jnp.mean(hb * hb, axis=-1, keepdims=True) + EPS)
        xh = hb * r
        err = xh * nwb - tb
        dy = err * (1.0 / d)
        dxh = dy * nwb
        dx = r * (dxh - xh * jnp.mean(dxh * xh, axis=-1, keepdims=True))
        loss = 0.5 * jnp.sum(jnp.mean(err * err, axis=-1, keepdims=True), axis=0, keepdims=True)
        return dx, dx * (0.5 * gb), jnp.broadcast_to(loss, (1, LANE)), _colsum(dy * xh), _colsum(dx * ob * 0.5)

    return _rowwise(name, fn, [(h, d, 0), (target, d, 0), (o, d, 0)], [nw, g], [(d, F32), (d, BF16)], [LANE, d, d], tm)


def _head_group(nh, most):
    return max(g for g in (1, 2, 4) if g <= most and nh % g == 0)


def _attn_fwd(qkv, nh, t):
    s = qkv.shape[0]
    scale = HEAD_DIM ** -0.5
    hp = _head_group(nh, 4)
    wide = hp * HEAD_DIM
    lanes = [slice(u * HEAD_DIM, (u + 1) * HEAD_DIM) for u in range(hp)]

    def body(q_ref, k_ref, v_ref, y_ref, tot_ref):
        i = pl.program_id(1)
        row = lax.broadcasted_iota(jnp.int32, (t, t), 0)
        col = lax.broadcasted_iota(jnp.int32, (t, t), 1)
        later = (row > col).astype(BF16)
        causal = col < row
        qs = [q_ref[:, ln] for ln in lanes]

        def block(j, carry, diagonal):
            ks = pl.ds(pl.multiple_of(j * t, t), t)
            heads = range(hp)
            z = [lax.dot_general(qs[u], k_ref[ks, lanes[u]], NT, preferred_element_type=F32) * scale for u in heads]
            sp = [_softplus(z[u]) for u in heads]
            log_keep = [jnp.where(causal, -sp[u], 0.0) if diagonal else -sp[u] for u in heads]
            between = [_dot_split(log_keep[u], later) for u in heads]
            w = [jnp.exp(z[u] - sp[u] + between[u] + carry[u][1]) for u in heads]
            if diagonal:
                w = [jnp.where(causal, w[u], 0.0) for u in heads]
            o = [carry[u][0] + jnp.dot(w[u].astype(BF16), v_ref[ks, lanes[u]], preferred_element_type=F32) for u in heads]
            return tuple((o[u], carry[u][1] + jnp.sum(log_keep[u], axis=1, keepdims=True)) for u in heads)

        carry = tuple((jnp.zeros((t, HEAD_DIM), F32), jnp.zeros((t, 1), F32)) for _ in lanes)
        carry = block(i, carry, True)
        carry = lax.fori_loop(0, i, lambda jj, cr: block(i - 1 - jj, cr, False), carry)
        for u, ln in enumerate(lanes):
            y_ref[:, ln] = carry[u][0].astype(y_ref.dtype)
            tot_ref[:, ln] = jnp.broadcast_to(carry[u][1], (t, HEAD_DIM))

    g = nh // hp
    return pl.pallas_call(
        body, name="attn_fwd", grid=(g, s // t),
        in_specs=[pl.BlockSpec((t, wide), lambda h, i: (i, h)),
                  pl.BlockSpec((s, wide), lambda h, i: (0, g + h)),
                  pl.BlockSpec((s, wide), lambda h, i: (0, 2 * g + h))],
        out_specs=[pl.BlockSpec((t, wide), lambda h, i: (i, h)), pl.BlockSpec((t, wide), lambda h, i: (i, h))],
        out_shape=[jax.ShapeDtypeStruct((s, nh * HEAD_DIM), BF16), jax.ShapeDtypeStruct((s, nh * HEAD_DIM), F32)],
        compiler_params=_params(("parallel", "arbitrary")),
    )(qkv, qkv, qkv)


def _attn_bwd(qkv, dy, tot, nh, t):
    s = qkv.shape[0]
    scale = HEAD_DIM ** -0.5
    hp = _head_group(nh, 4)
    wide = hp * HEAD_DIM
    lanes = [slice(u * HEAD_DIM, (u + 1) * HEAD_DIM) for u in range(hp)]

    def body(q_ref, k_ref, v_ref, dy_ref, tot_ref, dq_ref, dk_out, dv_out, dk_ref, dv_ref):
        i = pl.program_id(1)

        @pl.when(i == 0)
        def _():
            dk_ref[...] = jnp.zeros_like(dk_ref)
            dv_ref[...] = jnp.zeros_like(dv_ref)

        row = lax.broadcasted_iota(jnp.int32, (t, t), 0)
        col = lax.broadcasted_iota(jnp.int32, (t, t), 1)
        upto = (row <= col).astype(BF16)
        before = (row < col).astype(BF16)
        causal = col < row
        qs = [q_ref[:, ln] for ln in lanes]
        dys = [dy_ref[:, ln] for ln in lanes]
        totals = [tot_ref[:, u * HEAD_DIM:u * HEAD_DIM + 1] for u in range(hp)]

        def block(j, carry, diagonal):
            ks = pl.ds(pl.multiple_of(j * t, t), t)
            heads = range(hp)
            kb = [k_ref[ks, ln] for ln in lanes]
            vb = [v_ref[ks, ln] for ln in lanes]
            z = [lax.dot_general(qs[u], kb[u], NT, preferred_element_type=F32) * scale for u in heads]
            dw = [lax.dot_general(dys[u], vb[u], NT, preferred_element_type=F32) for u in heads]
            sp = [_softplus(z[u]) for u in heads]
            log_keep = [jnp.where(causal, -sp[u], 0.0) if diagonal else -sp[u] for u in heads]
            upto_sum = [_dot_split(log_keep[u], upto) for u in heads]
            w = [jnp.exp(z[u] - sp[u] + (totals[u] - carry[u][1] - upto_sum[u])) for u in heads]
            if diagonal:
                w = [jnp.where(causal, w[u], 0.0) for u in heads]
            g = [dw[u] * w[u] for u in heads]
            g_before = [_dot_split(g[u], before) for u in heads]
            dz = [(g[u] * jnp.exp(-sp[u]) - jnp.exp(z[u] - sp[u]) * (carry[u][2] + g_before[u])) * scale for u in heads]
            if diagonal:
                dz = [jnp.where(causal, dz[u], 0.0) for u in heads]
            dzb = [dz[u].astype(BF16) for u in heads]
            dq = [carry[u][0] + jnp.dot(dzb[u], kb[u], preferred_element_type=F32) for u in heads]
            for u in heads:
                dk_ref[ks, lanes[u]] += lax.dot_general(dzb[u], qs[u], TN, preferred_element_type=F32)
            for u in heads:
                dv_ref[ks, lanes[u]] += lax.dot_general(w[u].astype(BF16), dys[u], TN, preferred_element_type=F32)
            return tuple((dq[u], carry[u][1] + jnp.sum(log_keep[u], axis=1, keepdims=True),
                          carry[u][2] + jnp.sum(g[u], axis=1, keepdims=True)) for u in heads)

        zero = jnp.zeros((t, 1), F32)
        carry = tuple((jnp.zeros((t, HEAD_DIM), F32), zero, zero) for _ in lanes)
        carry = lax.fori_loop(0, i, lambda j, cr: block(j, cr, False), carry)
        carry = block(i, carry, True)
        for u, ln in enumerate(lanes):
            dq_ref[:, ln] = carry[u][0].astype(dq_ref.dtype)

        @pl.when(i == pl.num_programs(1) - 1)
        def _():
            dk_out[...] = dk_ref[...].astype(dk_out.dtype)
            dv_out[...] = dv_ref[...].astype(dv_out.dtype)

    g = nh // hp
    tile = lambda off: pl.BlockSpec((t, wide), lambda h, i: (i, off + h))
    head = lambda off, **mode: pl.BlockSpec((s, wide), lambda h, i: (0, off + h), **mode)
    once = dict(pipeline_mode=pl.Buffered(1))
    return pl.pallas_call(
        body, name="attn_bwd", grid=(g, s // t),
        in_specs=[tile(0), head(g, **once), head(2 * g, **once), tile(0), tile(0)],
        out_specs=[tile(0), head(0), head(0)],
        out_shape=[jax.ShapeDtypeStruct((s, nh * HEAD_DIM), BF16)] * 3,
        scratch_shapes=[pltpu.VMEM((s, wide), F32), pltpu.VMEM((s, wide), F32)],
        compiler_params=_params(("parallel", "arbitrary")),
    )(qkv, qkv, qkv, dy, tot)


def _lru_gates(xc, w_r, b_r, w_i, b_i, lam):
    xb = xc.astype(BF16)
    r = _sigmoid(jnp.dot(xb, w_r.astype(BF16), preferred_element_type=F32) + b_r)
    i = _sigmoid(jnp.dot(xb, w_i.astype(BF16), preferred_element_type=F32) + b_i)
    neg_lam = -lam
    sp_lam = jnp.maximum(neg_lam, 0.0) + _log1p(jnp.exp(-jnp.abs(neg_lam)))
    log_a = -LRU_C * r * sp_lam
    a = jnp.exp(log_a)
    mult = jnp.sqrt(-_expm1(2.0 * log_a))
    return r, i, sp_lam, a, mult


def _conv_taps(xpad_chunk, conv_w, t):
    shifted = [xpad_chunk[CONV_HALO:, :]]
    for d in range(1, CONV_WIDTH):
        shifted.append(pltpu.roll(xpad_chunk, d, 0)[CONV_HALO:, :])
    weights = [conv_w[CONV_WIDTH - 1 - d:CONV_WIDTH - d, :] for d in range(CONV_WIDTH)]
    return shifted, weights


def _lru_fwd(xr_pad, proj, gr_block0, conv_w, conv_b, w_r, b_r, w_i, b_i, lam, t):
    s, w = xr_pad.shape[0] - CONV_HALO, xr_pad.shape[1]
    nblk = w // LANE
    nchunk = s // t
    steps = [1 << p for p in range(t.bit_length() - 1)]
    assert (1 << (t.bit_length() - 1)) == t and w_r.shape[1:] == (LANE, LANE)

    def body(x_ref, gr_ref, cw_ref, cb_ref, wr_ref, br_ref, wi_ref, bi_ref, lam_ref, h_ref, hp_ref, xc_ref, y_ref):
        row = lax.broadcasted_iota(jnp.int32, (t, LANE), 0)

        def chunk(ci, h_in):
            t0 = pl.multiple_of(ci * t, t)
            shifted, weights = _conv_taps(x_ref[pl.ds(t0, t + CONV_HALO), :], cw_ref[...], t)
            xc = cb_ref[...] + sum(wd * xs for wd, xs in zip(weights, shifted))
            r, i, _, a, mult = _lru_gates(xc, wr_ref[...], br_ref[...], wi_ref[...], bi_ref[...], lam_ref[...])
            coef, val = a, mult * (i * xc)
            for d in steps:
                ok = row >= d
                val = jnp.where(ok, coef * pltpu.roll(val, d, 0) + val, val)
                coef = jnp.where(ok, coef * pltpu.roll(coef, d, 0), coef)
            h = val + coef * h_in
            rows = pl.ds(t0, t)
            h_ref[rows, :] = h
            hp_ref[rows, :] = jnp.where(row == 0, h_in, pltpu.roll(h, 1, 0))
            xc_ref[rows, :] = xc
            y_ref[rows, :] = (h * _gelu_and_grad(gr_ref[rows, :])[0]).astype(y_ref.dtype)
            return h[t - 1:t, :]

        lax.fori_loop(0, nchunk, chunk, jnp.zeros((1, LANE), F32))

    col = lambda rows: pl.BlockSpec((rows, LANE), lambda n: (0, n))
    return pl.pallas_call(
        body, name="lru_fwd", grid=(nblk,),
        in_specs=[col(s + CONV_HALO), pl.BlockSpec((s, LANE), lambda n: (0, gr_block0 + n)), col(CONV_WIDTH), col(1),
                  pl.BlockSpec((None, LANE, LANE), lambda n: (n, 0, 0)), col(1),
                  pl.BlockSpec((None, LANE, LANE), lambda n: (n, 0, 0)), col(1), col(1)],
        out_specs=[col(s)] * 4,
        out_shape=[jax.ShapeDtypeStruct((s, w), F32)] * 3 + [jax.ShapeDtypeStruct((s, w), BF16)],
        compiler_params=_params(("parallel",)),
    )(xr_pad, proj, conv_w, conv_b, w_r, b_r, w_i, b_i, lam)


def _lru_bwd(dy, proj, gr_block0, h, h_prev, xc, w_r, b_r, w_i, b_i, lam, t):
    s, w = dy.shape
    nblk = w // LANE
    nchunk = s // t
    steps = [1 << p for p in range(t.bit_length() - 1)]

    def body(dy_ref, gr_ref, h_ref, hp_ref, xc_ref, wr_ref, br_ref, wi_ref, bi_ref, lam_ref,
             dgr_ref, dxc_ref, dwr_ref, dwi_ref, dbr_ref, dbi_ref, dlam_ref):
        row = lax.broadcasted_iota(jnp.int32, (t, LANE), 0)
        for ref in (dwr_ref, dwi_ref, dbr_ref, dbi_ref, dlam_ref):
            ref[...] = jnp.zeros_like(ref)

        def chunk(cc, carry):
            lam_next, a_next = carry
            rows = pl.ds(pl.multiple_of((nchunk - 1 - cc) * t, t), t)
            dyb, hb, xcb = dy_ref[rows, :], h_ref[rows, :], xc_ref[rows, :]
            gel, dgel = _gelu_and_grad(gr_ref[rows, :])
            dgr_ref[rows, :] = dyb * hb * dgel
            w_r, w_i = wr_ref[...], wi_ref[...]
            r, i, sp_lam, a, mult = _lru_gates(xcb, w_r, br_ref[...], w_i, bi_ref[...], lam_ref[...])
            coef = jnp.where(row == t - 1, a_next, pltpu.roll(a, t - 1, 0))
            val = dyb * gel
            for d in steps:
                ok = row < t - d
                val = jnp.where(ok, coef * pltpu.roll(val, t - d, 0) + val, val)
                coef = jnp.where(ok, coef * pltpu.roll(coef, t - d, 0), coef)
            adj = val + coef * lam_next
            da = adj * hp_ref[rows, :]
            v = i * xcb
            dmult, dv = adj * v, adj * mult
            dlog_a = da * a - (a * a) * dmult / mult
            dr_pre = (-LRU_C * sp_lam) * dlog_a * r * (1.0 - r)
            di_pre = dv * xcb * i * (1.0 - i)
            dlam_ref[...] += _colsum(-LRU_C * r * dlog_a)
            dbr_ref[...] += _colsum(dr_pre)
            dbi_ref[...] += _colsum(di_pre)
            xb, drb, dib = xcb.astype(BF16), dr_pre.astype(BF16), di_pre.astype(BF16)
            dwr_ref[...] += lax.dot_general(xb, drb, TN, preferred_element_type=F32)
            dwi_ref[...] += lax.dot_general(xb, dib, TN, preferred_element_type=F32)
            dxc_ref[rows, :] = (dv * i + lax.dot_general(drb, w_r.astype(BF16), NT, preferred_element_type=F32)
                                + lax.dot_general(dib, w_i.astype(BF16), NT, preferred_element_type=F32))
            return adj[0:1, :], a[0:1, :]

        lax.fori_loop(0, nchunk, chunk, (jnp.zeros((1, LANE), F32), jnp.zeros((1, LANE), F32)))
        dlam_ref[...] = dlam_ref[...] * (-_sigmoid(-lam_ref[...]))

    col = lambda rows: pl.BlockSpec((rows, LANE), lambda n: (0, n))
    mat = pl.BlockSpec((None, LANE, LANE), lambda n: (n, 0, 0))
    return pl.pallas_call(
        body, name="lru_bwd", grid=(nblk,),
        in_specs=[col(s), pl.BlockSpec((s, LANE), lambda n: (0, gr_block0 + n)), col(s), col(s), col(s),
                  mat, col(1), mat, col(1), col(1)],
        out_specs=[col(s), col(s), mat, mat, col(1), col(1), col(1)],
        out_shape=[jax.ShapeDtypeStruct((s, w), F32)] * 2 + [jax.ShapeDtypeStruct((nblk, LANE, LANE), F32)] * 2
        + [jax.ShapeDtypeStruct((1, w), F32)] * 3,
        compiler_params=_params(("parallel",)),
    )(dy, proj, h, h_prev, xc, w_r, b_r, w_i, b_i, lam)


def _conv_bwd(xr_pad, dxc_pad, conv_w, t):
    s, w = xr_pad.shape[0] - CONV_HALO, xr_pad.shape[1]
    nchunk = s // t

    def body(x_ref, g_ref, cw_ref, dx_ref, dcw_ref, dcb_ref):
        dcw_ref[...] = jnp.zeros_like(dcw_ref)
        dcb_ref[...] = jnp.zeros_like(dcb_ref)

        def chunk(ci, _):
            t0 = pl.multiple_of(ci * t, t)
            shifted, weights = _conv_taps(x_ref[pl.ds(t0, t + CONV_HALO), :], cw_ref[...], t)
            gpad = g_ref[pl.ds(t0, t + CONV_HALO), :]
            g = gpad[:t, :]
            dx = weights[0] * g
            for d in range(1, CONV_WIDTH):
                dx = dx + weights[d] * pltpu.roll(gpad, t + CONV_HALO - d, 0)[:t, :]
            dx_ref[pl.ds(t0, t), :] = dx
            for d in range(CONV_WIDTH):
                dcw_ref[CONV_WIDTH - 1 - d:CONV_WIDTH - d, :] += _colsum(g * shifted[d])
            dcb_ref[...] += _colsum(g)
            return 0

        lax.fori_loop(0, nchunk, chunk, 0)

    col = lambda rows: pl.BlockSpec((rows, LANE), lambda n: (0, n))
    return pl.pallas_call(
        body, name="conv_bwd", grid=(w // LANE,),
        in_specs=[col(s + CONV_HALO), col(s + CONV_HALO), col(CONV_WIDTH)],
        out_specs=[col(s), col(CONV_WIDTH), col(1)],
        out_shape=[jax.ShapeDtypeStruct((s, w), F32), jax.ShapeDtypeStruct((CONV_WIDTH, w), F32),
                   jax.ShapeDtypeStruct((1, w), F32)],
        compiler_params=_params(("parallel",)),
    )(xr_pad, dxc_pad, conv_w)


def _sum_parts(parts_ref):
    g = parts_ref[0].astype(F32)
    for p in range(1, parts_ref.shape[0]):
        g = g + parts_ref[p].astype(F32)
    return g


def _reduce_parts(name, parts):
    p, r, c = parts.shape
    tr = _tile(r, max(8, (1 << 19) // c), 8)

    def body(parts_ref, g_ref):
        g_ref[...] = _sum_parts(parts_ref)

    return pl.pallas_call(
        body, name=name, grid=(r // tr,), in_specs=[pl.BlockSpec((p, tr, c), lambda i: (0, i, 0))],
        out_specs=pl.BlockSpec((tr, c), lambda i: (i, 0)), out_shape=jax.ShapeDtypeStruct((r, c), F32),
        compiler_params=_params(("parallel",)),
    )(parts)


def _adamw(name, parts, w, m, v):
    p, r, c = parts.shape
    tr = _tile(r, max(8, (1 << 18) // c), 8)

    def body(parts_ref, w_ref, m_ref, v_ref, g_ref, d_ref, nm_ref, nv_ref):
        g = _sum_parts(parts_ref)
        nm = ADAM_B1 * m_ref[...] + (1.0 - ADAM_B1) * g
        nv = ADAM_B2 * v_ref[...] + (1.0 - ADAM_B2) * (g * g)
        m_hat = nm / (1.0 - ADAM_B1 ** ADAM_STEP)
        v_hat = nv / (1.0 - ADAM_B2 ** ADAM_STEP)
        g_ref[...] = g
        d_ref[...] = -ADAM_LR * (m_hat / (jnp.sqrt(v_hat) + ADAM_EPS) + ADAM_WD * w_ref[...])
        nm_ref[...] = nm
        nv_ref[...] = nv

    blk = pl.BlockSpec((tr, c), lambda i: (i, 0))
    return pl.pallas_call(
        body, name=name, grid=(r // tr,), in_specs=[pl.BlockSpec((p, tr, c), lambda i: (0, i, 0)), blk, blk, blk],
        out_specs=[blk] * 4, out_shape=[jax.ShapeDtypeStruct((r, c), F32)] * 4,
        compiler_params=_params(("parallel",)),
    )(parts, w, m, v)


def _ffn_in(tag, y, w_in_g, tm):
    s, d = y.shape
    half = N_DEV // 2
    cb = w_in_g.shape[2]
    ff = half * cb

    def swiglu(acc, extra_refs, out_refs):
        g, u = acc
        out_refs[0][0] = g.astype(BF16)
        out_refs[0][1] = u.astype(BF16)
        out_refs[1][...] = (g * _sigmoid(g) * u).astype(BF16)

    gu_shape = (2, 1, s, ff)
    gu, act = _matmul(
        tag + "_in", y[None], w_in_g.reshape(2, half, d, cb), "nn", tm, cb, d,
        outs=[(gu_shape, BF16, _bspec(gu_shape, tm, cb, _ij)), ((1, s, ff), BF16, _bspec((1, s, ff), tm, cb, _ij))],
        epilogue=swiglu, b_buffers=1, side_by_side=True)
    return gu, act


def _ffn_out(tag, act, w_out_g, res, gate, tm):
    _, s, ff = act.shape
    d = res.shape[1]
    tn = _tile(d, 1024)

    def residual(acc, extra_refs, out_refs):
        out_refs[0][...] = acc[0].astype(BF16)
        out_refs[1][...] = extra_refs[0][...] + 0.5 * extra_refs[1][...] * acc[0]

    plain = _bspec((1, s, d), tm, tn, _ij)
    o, h_new = _matmul(
        tag + "_out", act, w_out_g.reshape(1, ff, d), "nn", tm, tn, ff,
        outs=[((1, s, d), BF16, plain), ((1, s, d), F32, plain)], epilogue=residual,
        extras=[(res[None], plain), (gate, _row_spec(tn))], b_buffers=1)
    return o[0], h_new[0]


def _after_token(token):
    return token, pl.BlockSpec(token.shape, lambda j, i, k: (0, 0))


def _ffn_bwd_weights(tag, do, y, gu, act, w_in_g, w_out_g, tm):
    s, d = do.shape
    half = N_DEV // 2
    cb = w_in_g.shape[2]
    ff = half * cb

    tn_d, tm_f = _tile(d, 1024), _tile(ff, 512)
    dw_out = _matmul(tag + "_dw_out", act, do[None], "tn", tm_f, tn_d, s,
                     outs=[((1, ff, d), BF16, _bspec((1, ff, d), tm_f, tn_d, _ij))], b_buffers=1)[0]
    dw_out = dw_out.reshape(N_DEV, ff // N_DEV, d)
    out_handles, token = _exchange_begin(tag + "_scatter_out", [dw_out], True)

    def dswiglu(acc, extra_refs, out_refs):
        dact = acc[0]
        g, u = extra_refs[0][0].astype(F32), extra_refs[0][1].astype(F32)
        sg = _sigmoid(g)
        out_refs[0][0] = (dact * u * sg * (1.0 + g * (1.0 - sg))).astype(BF16)
        out_refs[0][1] = (dact * g * sg).astype(BF16)

    gu_shape = (2, 1, s, ff)
    tn_e = 2 * cb
    gu_spec = _bspec(gu_shape, tm, tn_e, _ij)
    dgu = _matmul(tag + "_dact", do[None], w_out_g.reshape(1, ff, d), "nt", tm, tn_e, d,
                  outs=[(gu_shape, BF16, gu_spec)], epilogue=dswiglu, extras=[(gu, gu_spec), _after_token(token)],
                  b_buffers=1)[0]
    dgu = dgu.reshape(2, s, ff)

    tm_d = _tile(d, 512)

    def two_blocks(acc, extra_refs, out_refs):
        out_refs[0][0] = acc[0][:, :cb].astype(BF16)
        out_refs[0][1] = acc[0][:, cb:].astype(BF16)

    pair_spec = pl.BlockSpec((None, 2, tm_d, cb), lambda j, i, k: (j, 0, i, 0))
    dw_in = _matmul(tag + "_dw_in", y[None], dgu, "tn", tm_d, tn_e, s,
                    outs=[((half, 2, d, cb), BF16, pair_spec)], epilogue=two_blocks, b_buffers=1)[0]
    dw_in = dw_in.reshape(N_DEV, d, cb)
    in_handles, token = _exchange_begin(tag + "_scatter_in", [dw_in], True)
    return dgu, in_handles + out_handles, token


def _ffn_bwd_input(tag, dgu, w_in_g, token):
    s, d = dgu.shape[1], w_in_g.shape[1]
    tm_big = _tile(s, 1024, 8)
    return _matmul(tag + "_dy", dgu, w_in_g, "nt", tm_big, d, w_in_g.shape[2],
                   outs=[((1, s, d), F32, _bspec((1, s, d), tm_big, d, _ij))], extras=[_after_token(token)])[0][0]


def kernel(x, c, w_ada, b_ada, norm_ffn1, w_ffn1_in, w_ffn1_out, norm_mix, w_in, conv_w, conv_b, w_rg_gate, b_rg_gate, w_in_gate, b_in_gate, lru_lambda, w_branch_attn, w_branch_lru, w_out, norm_ffn2, w_ffn2_in, w_ffn2_out, norm_final, loss_target, m_w_ada, m_b_ada, m_norm_ffn1, m_w_ffn1_in, m_w_ffn1_out, m_norm_mix, m_w_in, m_conv_w, m_conv_b, m_w_rg_gate, m_b_rg_gate, m_w_in_gate, m_b_in_gate, m_lru_lambda, m_w_branch_attn, m_w_branch_lru, m_w_out, m_norm_ffn2, m_w_ffn2_in, m_w_ffn2_out, m_norm_final, v_w_ada, v_b_ada, v_norm_ffn1, v_w_ffn1_in, v_w_ffn1_out, v_norm_mix, v_w_in, v_conv_w, v_conv_b, v_w_rg_gate, v_b_rg_gate, v_w_in_gate, v_b_in_gate, v_lru_lambda, v_w_branch_attn, v_w_branch_lru, v_w_out, v_norm_ffn2, v_w_ffn2_in, v_w_ffn2_out, v_norm_final):
    xs, target = x[0], loss_target[0]
    s, d = xs.shape
    aw, lw = w_branch_attn.shape[1], w_branch_lru.shape[1]
    nh, nlb = aw // HEAD_DIM, w_rg_gate.shape[1]
    cba, cbi, cbb, cwb = w_ada.shape[2], w_in.shape[2], w_branch_attn.shape[2], conv_w.shape[2]
    assert lw == nlb * LANE and cwb * N_DEV == lw and 3 * aw + 2 * lw + 2 * d == cbi * N_DEV
    me = 4 * lax.axis_index("x") + 2 * lax.axis_index("y") + lax.axis_index("c")
    tm = _tile(s, 512, 8)
    tr = _tile(s, 256, 8)
    t_attn = _tile(s, 256, 8)
    t_lru = _tile(s, 256, 8)

    small = _exchange("gather_c", [jnp.concatenate([c, conv_w.reshape(1, CONV_WIDTH * cwb)], axis=1)], False)[0][:, 0, :]
    c_all = small[:, :d]
    conv_w_full = small[:, d:].reshape(N_DEV, CONV_WIDTH, cwb).transpose(1, 0, 2).reshape(CONV_WIDTH, lw)
    c_act = _rowwise("silu_c", lambda v: v * _sigmoid(v), [(c_all, d, 0)], [], [(d, F32)], [], N_DEV)[0]

    def add_bias(acc_ref, extra_refs, out_refs):
        out_refs[0][...] = acc_ref[0] + extra_refs[0][...]

    b_ada_mine = lax.dynamic_slice(b_ada, (0, me * cba), (1, cba))
    mod_part = _matmul("mod", c_act[None], w_ada, "nn", N_DEV, cba, _tile(d, 512),
                       outs=[((1, N_DEV, cba), F32, _bspec((1, N_DEV, cba), N_DEV, cba, _ij))], epilogue=add_bias,
                       extras=[(b_ada_mine, _row_spec(cba))])[0][0]
    mod_all = _exchange("gather_mod", [mod_part], False)[0]
    mod = lax.dynamic_index_in_dim(mod_all, me, axis=1, keepdims=False).reshape(1, 9 * d)
    sh1, sc1, g1, sh2, sc2, g2, sh3, sc3, g3 = [mod[:, n * d:(n + 1) * d] for n in range(9)]

    shards = [w_ffn1_in[0], w_ffn1_out[0], w_in[0], w_branch_attn[0], w_branch_lru[0], w_out[0], w_ffn2_in[0], w_ffn2_out[0]]
    early = (0, 1, 2)
    gathers, token = _exchange_begin("gather_w_early", [w.astype(BF16) for w in shards[:3]], False, mod, early)
    late, token = _exchange_begin("gather_w", [(w + token[0, 0]).astype(BF16) for w in shards[3:]], False)
    gathers = gathers + late

    def gathered(n, after):
        full = _exchange_end("gathered_w%d" % n, [gathers[n]], after, False)
        return (_sibling_forward("forwarded_w%d" % n, full) if n in early else full)[0]

    y1 = _norm_mod("norm1", xs, norm_ffn1 + token[:1, :1], sc1, sh1, tr)
    wf1i = gathered(0, y1)
    gu1, act1 = _ffn_in("ffn1", y1, wf1i, tm)
    wf1o = gathered(1, act1)
    o1, h1 = _ffn_out("ffn1", act1, wf1o, xs, g1, tm)

    y2 = _norm_mod("norm2", h1, norm_mix, sc2, sh2, tr)
    wi_g = gathered(2, y2)
    tn_i = _tile(cbi, 1152)
    tm_big = _tile(s, 1024, 8)
    proj = _matmul("mix_in", y2[None], wi_g.reshape(N_DEV // 2, 2, d, cbi), "nn", tm, cbi, d,
                   outs=[((1, s, N_DEV * cbi), F32, _bspec((1, s, N_DEV * cbi), tm, 2 * cbi, _ij))], b_buffers=1,
                   side_by_side=True, pairs=True)[0][0]
    off_xr, off_gr, off_ga, off_gl = 3 * aw, 3 * aw + lw, 3 * aw + 2 * lw, 3 * aw + 2 * lw + d
    qkv = proj[:, :3 * aw].astype(BF16)
    y_attn, attn_tot = _attn_fwd(qkv, nh, t_attn)
    xr_pad = jnp.pad(proj[:, off_xr:off_xr + lw], ((CONV_HALO, 0), (0, 0)))
    w_r, w_i = w_rg_gate[0], w_in_gate[0]
    h_lru, h_prev, xc, y_lru = _lru_fwd(xr_pad, proj, off_gr // LANE, conv_w_full, conv_b, w_r, b_rg_gate, w_i,
                                        b_in_gate, lru_lambda, t_lru)
    wba_p = gathered(3, y_attn).transpose(1, 0, 2).reshape(1, aw, d)
    wbl_p = gathered(4, y_lru).transpose(1, 0, 2).reshape(1, lw, d)
    proj3 = proj[None]
    tm_b = _tile(s, 1024, 8)
    tn_m = _tile(math.gcd(d, off_ga, off_gl), 1024)
    plain_m = _bspec((1, s, d), tm_b, tn_m, _ij)
    gate_specs = [_bspec(proj3.shape, tm_b, tn_m, functools.partial(lambda i, j, k, o: (i, j + o), o=o // tn_m))
                  for o in (off_ga, off_gl)]
    ya = _matmul("branch_attn", y_attn[None], wba_p, "nn", tm_b, tn_m, aw, outs=[((1, s, d), F32, plain_m)], b_buffers=1)[0]

    def merge(acc_ref, extra_refs, out_refs):
        yl = acc_ref[0]
        ya_t, ga, gl = extra_refs[0][...], extra_refs[1][...], extra_refs[2][...]
        out_refs[0][...] = yl
        out_refs[1][...] = (_sigmoid(ga) * ya_t + _sigmoid(gl) * yl).astype(BF16)

    yl, merged = _matmul("branch_lru", y_lru[None], wbl_p, "nn", tm_b, tn_m, lw,
                         outs=[((1, s, d), F32, plain_m), ((1, s, d), BF16, plain_m)], epilogue=merge,
                         extras=[(ya, plain_m), (proj3, gate_specs[0]), (proj3, gate_specs[1])], b_buffers=1)
    tn_d = _tile(d, 1024)
    plain = _bspec((1, s, d), tm, tn_d, _ij)

    def residual(acc_ref, extra_refs, out_refs):
        o = acc_ref[0]
        out_refs[0][...] = o.astype(BF16)
        out_refs[1][...] = extra_refs[0][...] + extra_refs[1][...] * o

    wo_g = gathered(5, merged)
    mo, h2 = _matmul("mix_out", merged, wo_g.reshape(1, d, d), "nn", tm, tn_d, d,
                     outs=[((1, s, d), BF16, plain), ((1, s, d), F32, plain)], epilogue=residual,
                     extras=[(h1[None], plain), (g2, _row_spec(tn_d))], b_buffers=1)
    mo, h2 = mo[0], h2[0]

    y3 = _norm_mod("norm3", h2, norm_ffn2, sc3, sh3, tr)
    wf2i = gathered(6, y3)
    gu3, act3 = _ffn_in("ffn2", y3, wf2i, tm)
    wf2o = gathered(7, act3)
    o3, h3 = _ffn_out("ffn2", act3, wf2o, h2, g3, tm)

    nf = norm_final.reshape(1, d)
    dh3, do3, loss_part, d_nf, dg3 = _loss_bwd("loss", h3, target, nf, o3, g3, tr)
    dgu3, scatter_ffn2, token = _ffn_bwd_weights("ffn2", do3, y3, gu3, act3, wf2i, wf2o, tm)
    dy3 = _ffn_bwd_input("ffn2", dgu3, wf2i, token)
    dh2, dmo, dsh3, dsc3, dn3, dg2 = _norm_mod_bwd("norm3_bwd", dy3, h2, dh3, norm_ffn2, sc3, tr, below=(mo, g2, 1.0))

    dwo = _matmul("mix_dw_out", merged, dmo[None], "tn", _tile(d, 512), tn_d, s,
                  outs=[((1, d, d), BF16, _bspec((1, d, d), _tile(d, 512), tn_d, _ij))], b_buffers=1)[0]

    def dmerge(acc_ref, extra_refs, out_refs):
        dm = acc_ref[0]
        ya_t, yl_t = extra_refs[0][...], extra_refs[1][...]
        sa, sl = _sigmoid(extra_refs[2][...]), _sigmoid(extra_refs[3][...])
        out_refs[0][...] = (dm * sa).astype(BF16)
        out_refs[1][...] = (dm * sl).astype(BF16)
        out_refs[2][...] = (dm * ya_t * sa * (1.0 - sa)).astype(BF16)
        out_refs[3][...] = (dm * yl_t * sl * (1.0 - sl)).astype(BF16)

    tn_m = _tile(math.gcd(d, off_ga, off_gl), 1024)
    plain_m = _bspec((1, s, d), tm, tn_m, _ij)
    gate_specs = [_bspec(proj3.shape, tm, tn_m, functools.partial(lambda i, j, k, o: (i, j + o), o=o // tn_m))
                  for o in (off_ga, off_gl)]
    dya, dyl, dga, dgl = _matmul("mix_dmerged", dmo[None], wo_g.reshape(1, d, d), "nt", tm, tn_m, d,
                                 outs=[((1, s, d), BF16, plain_m)] * 4, epilogue=dmerge,
                                 extras=[(ya, plain_m), (yl, plain_m), (proj3, gate_specs[0]), (proj3, gate_specs[1])],
                                 b_buffers=1)
    tm_a, tm_l = _tile(aw, 1024), _tile(lw, 1024)
    dwba = _matmul("dw_branch_attn", y_attn[None], dya, "tn", tm_a, cbb, s,
                   outs=[((N_DEV, aw, cbb), BF16, _bspec((N_DEV, aw, cbb), tm_a, cbb, _ij))])[0]
    dwbl = _matmul("dw_branch_lru", y_lru[None], dyl, "tn", tm_l, cbb, s,
                   outs=[((N_DEV, lw, cbb), BF16, _bspec((N_DEV, lw, cbb), tm_l, cbb, _ij))])[0]
    scatter_branch, token = _exchange_begin("scatter_branch", [dwba, dwbl, dwo.reshape(N_DEV, d // N_DEV, d)], True)
    tn_a, tn_l = _tile(aw, 1024), _tile(lw, 1024)
    dy_attn = _matmul("d_attn_out", dya, wba_p, "nt", tm_b, tn_a, d,
                      outs=[((1, s, aw), BF16, _bspec((1, s, aw), tm_b, tn_a, _ij))], extras=[_after_token(token)],
                      b_buffers=1)[0][0]
    dy_lru = _matmul("d_lru_out", dyl, wbl_p, "nt", tm_b, tn_l, d,
                     outs=[((1, s, lw), F32, _bspec((1, s, lw), tm_b, tn_l, _ij))], b_buffers=1)[0][0]
    dq, dk, dv = _attn_bwd(qkv, dy_attn, attn_tot, nh, t_attn)
    dgr, dxc, d_wr, d_wi, d_br, d_bi, d_lam = _lru_bwd(dy_lru, proj, off_gr // LANE, h_lru, h_prev, xc, w_r, b_rg_gate,
                                                       w_i, b_in_gate, lru_lambda, t_lru)
    dxr, d_cw, d_cb = _conv_bwd(xr_pad, jnp.pad(dxc, ((0, CONV_HALO), (0, 0))), conv_w_full, t_lru)
    dproj = jnp.concatenate([dq.astype(BF16), dk.astype(BF16), dv.astype(BF16), dxr.astype(BF16), dgr.astype(BF16),
                             dga[0], dgl[0]], axis=1)
    tm_d = _tile(d, 512)

    def two_blocks(acc_ref, extra_refs, out_refs):
        out_refs[0][0] = acc_ref[0][:, :cbi].astype(BF16)
        out_refs[0][1] = acc_ref[0][:, cbi:].astype(BF16)

    dwi = _matmul("mix_dw_in", y2[None], dproj[None], "tn", tm_d, 2 * cbi, s,
                  outs=[((N_DEV // 2, 2, d, cbi), BF16, pl.BlockSpec((None, 2, tm_d, cbi), lambda j, i, k: (j, 0, i, 0)))],
                  epilogue=two_blocks, b_buffers=1)[0].reshape(N_DEV, d, cbi)
    scatter_mix, token = _exchange_begin("scatter_mix", [dwi], True)
    dy2 = _matmul("mix_dy", dproj[None], wi_g, "nt", tm_big, d, tn_i,
                  outs=[((1, s, d), F32, _bspec((1, s, d), tm_big, d, _ij))], extras=[_after_token(token)])[0][0]
    dh1, do1, dsh2, dsc2, dn2, dg1 = _norm_mod_bwd("norm2_bwd", dy2, h1, dh2, norm_mix, sc2, tr, below=(o1, g1, 0.5))

    dgu1, scatter_ffn1, token = _ffn_bwd_weights("ffn1", do1, y1, gu1, act1, wf1i, wf1o, tm)
    dy1 = _ffn_bwd_input("ffn1", dgu1, wf1i, token)
    grad_x, dsh1, dsc1, dn1 = _norm_mod_bwd("norm1_bwd", dy1, xs, dh1, norm_ffn1, sc1, tr)

    results = {}

    def update_group(wait_name, handles, leaves, after):
        for (n, w, m, v), parts in zip(leaves, _exchange_end(wait_name, handles, after, True)):
            results[n] = [o[None] for o in _adamw("adamw_" + n, parts, w[0], m[0], v[0])]
        return results[leaves[-1][0]][0]

    done = update_group("scattered_ffn2", scatter_ffn2, [("w_ffn2_in", w_ffn2_in, m_w_ffn2_in, v_w_ffn2_in),
                                                         ("w_ffn2_out", w_ffn2_out, m_w_ffn2_out, v_w_ffn2_out)], grad_x)
    done = update_group("scattered_mix", scatter_branch + scatter_mix,
                        [("w_branch_attn", w_branch_attn, m_w_branch_attn, v_w_branch_attn),
                         ("w_branch_lru", w_branch_lru, m_w_branch_lru, v_w_branch_lru),
                         ("w_out", w_out, m_w_out, v_w_out), ("w_in", w_in, m_w_in, v_w_in)], done)

    lane_pad = jnp.zeros((1, 7 * LANE), F32)
    pack = jnp.concatenate(
        [loss_part, lane_pad, dsh1, dsc1, dg1, dsh2, dsc2, dg2, dsh3, dsc3, dg3, dn1, dn2, dn3, d_nf, d_cb, d_br, d_bi, d_lam,
         d_cw.reshape(1, -1)], axis=1)
    pack = jnp.pad(pack, ((0, 0), (0, -pack.shape[1] % (8 * LANE))))
    gate_pack = jnp.concatenate([d_wr.reshape(-1, LANE), d_wi.reshape(-1, LANE)], axis=0).astype(BF16)
    n_pack, n_gate = pack.shape[1], gate_pack.size
    updated = sum(results[n][0][0, 0, 0] for n in sorted(results)).reshape(1, 1)
    small_handles, token = _exchange_begin("gather_small", [pack, gate_pack], False, updated)

    done = update_group("scattered_ffn1", scatter_ffn1, [("w_ffn1_in", w_ffn1_in, m_w_ffn1_in, v_w_ffn1_in),
                                                         ("w_ffn1_out", w_ffn1_out, m_w_ffn1_out, v_w_ffn1_out)], token)
    packs, gate_packs = _exchange_end("gathered_small", small_handles, done, False)
    packs = packs.reshape(N_DEV, n_pack // LANE, LANE)
    g_pack = _reduce_parts("sum_small", packs).reshape(1, n_pack)
    g_gate = _reduce_parts("sum_gates", gate_packs).reshape(1, n_gate)
    loss = g_pack[0, 0]
    off = 8 * LANE
    n_vec = 9 * d + 4 * d + 4 * lw
    n_adam = n_vec + n_gate
    g_small = jnp.concatenate([g_pack[:, off:off + n_vec], g_gate], axis=1).reshape(1, n_adam // LANE, LANE)
    d_cw_sum = g_pack[:, off + n_vec:off + n_vec + CONV_WIDTH * lw].reshape(CONV_WIDTH, lw)
    d_cw_mine = lax.dynamic_slice(d_cw_sum, (0, me * cwb), (CONV_WIDTH, cwb))

    small_names = ["b_ada", "norm_ffn1", "norm_mix", "norm_ffn2", "norm_final", "conv_b", "b_rg_gate", "b_in_gate",
                   "lru_lambda", "w_rg_gate", "w_in_gate"]
    given = dict(b_ada=(b_ada, m_b_ada, v_b_ada), norm_ffn1=(norm_ffn1, m_norm_ffn1, v_norm_ffn1),
                 norm_mix=(norm_mix, m_norm_mix, v_norm_mix), norm_ffn2=(norm_ffn2, m_norm_ffn2, v_norm_ffn2),
                 norm_final=(norm_final, m_norm_final, v_norm_final), conv_b=(conv_b, m_conv_b, v_conv_b),
                 b_rg_gate=(b_rg_gate, m_b_rg_gate, v_b_rg_gate), b_in_gate=(b_in_gate, m_b_in_gate, v_b_in_gate),
                 lru_lambda=(lru_lambda, m_lru_lambda, v_lru_lambda), w_rg_gate=(w_rg_gate, m_w_rg_gate, v_w_rg_gate),
                 w_in_gate=(w_in_gate, m_w_in_gate, v_w_in_gate))
    packed = [jnp.concatenate([given[n][q].reshape(1, -1) for n in small_names], axis=1).reshape(n_adam // LANE, LANE)
              for q in range(3)]
    small_out = _adamw("adamw_small", g_small, *packed)
    pos = 0
    for n in small_names:
        shape = given[n][0].shape
        size = math.prod(shape)
        results[n] = [o.reshape(1, n_adam)[:, pos:pos + size].reshape(shape) for o in small_out]
        pos += size
    results["conv_w"] = [o.reshape(conv_w.shape) for o in
                         _adamw("adamw_conv_w", d_cw_mine[None], conv_w[0], m_conv_w[0], v_conv_w[0])]

    dmod_all = packs.reshape(N_DEV, n_pack)[:, off:off + 9 * d]
    dmod_mine = lax.dynamic_slice(dmod_all, (0, me * cba), (N_DEV, cba))
    dmod_rows = jnp.pad(dmod_mine, ((0, LANE - N_DEV), (0, 0)))
    c_act_t = jnp.pad(c_act.T, ((0, 0), (0, LANE - N_DEV)))
    tm_d2 = _tile(d, 256)
    d_wada = _matmul("dw_ada", c_act_t[None], dmod_rows[None], "nn", tm_d2, cba, LANE,
                     outs=[((1, d, cba), F32, _bspec((1, d, cba), tm_d2, cba, _ij))])[0]
    results["w_ada"] = [o[None] for o in _adamw("adamw_w_ada", d_wada, w_ada[0], m_w_ada[0], v_w_ada[0])]


    order = ["w_ada", "b_ada", "norm_ffn1", "w_ffn1_in", "w_ffn1_out", "norm_mix", "w_in", "conv_w", "conv_b", "w_rg_gate",
             "b_rg_gate", "w_in_gate", "b_in_gate", "lru_lambda", "w_branch_attn", "w_branch_lru", "w_out", "norm_ffn2",
             "w_ffn2_in", "w_ffn2_out", "norm_final"]
    return (loss, grad_x[None], *[results[n][0] for n in order], *[results[n][1] for n in order],
            *[results[n][2] for n in order], *[results[n][3] for n in order])
```

```python
import functools
import math

import jax
import jax.numpy as jnp
from jax import lax
from jax.experimental import pallas as pl
from jax.experimental.pallas import tpu as pltpu

F32 = jnp.float32
BF16 = jnp.bfloat16
N_DEV = 8
HEAD_DIM = 128
CONV_WIDTH = 4
CONV_HALO = 8
LRU_C = 8.0
EPS = 1e-6
ADAM_LR, ADAM_B1, ADAM_B2, ADAM_EPS, ADAM_WD, ADAM_STEP = 0.001, 0.9, 0.999, 1e-08, 0.01, 10
LANE = 128
VMEM_LIMIT = 56 * 1024 * 1024
MESH = pl.DeviceIdType.MESH

NT = (((1,), (1,)), ((), ()))
NN = (((1,), (0,)), ((), ()))
TN = (((0,), (0,)), ((), ()))


def _tile(dim, target, align=LANE):
    t = (min(target, dim) // align) * align
    while t >= align:
        if dim % t == 0:
            return t
        t -= align
    return dim


def _params(sem):
    return pltpu.CompilerParams(dimension_semantics=sem, vmem_limit_bytes=VMEM_LIMIT)


def _sigmoid(x):
    return 1.0 / (1.0 + jnp.exp(-x))


def _softplus(x):
    return jnp.maximum(x, 0.0) + jnp.log(1.0 + jnp.exp(-jnp.abs(x)))


def _log1p(z):
    w = 1.0 + z
    return jnp.where(w == 1.0, z, jnp.log(w) * z / jnp.where(w == 1.0, 1.0, w - 1.0))


def _expm1(x):
    poly = x * (1.0 + x * (0.5 + x * (1.0 / 6 + x * (1.0 / 24 + x * (1.0 / 120 + x * (1.0 / 720))))))
    return jnp.where(jnp.abs(x) < 0.25, poly, jnp.exp(x) - 1.0)


_GELU_C = math.sqrt(2.0 / math.pi)


def _gelu_and_grad(x):
    inner = _GELU_C * (x + 0.044715 * x * x * x)
    th = jnp.tanh(inner)
    val = 0.5 * x * (1.0 + th)
    grad = 0.5 * (1.0 + th) + 0.5 * x * (1.0 - th * th) * _GELU_C * (1.0 + 3 * 0.044715 * x * x)
    return val, grad


def _dot_split(x, u):
    hi = x.astype(BF16)
    lo = (x - hi.astype(F32)).astype(BF16)
    return jnp.dot(hi, u, preferred_element_type=F32) + jnp.dot(lo, u, preferred_element_type=F32)


def _mesh_position():
    x, y, c = lax.axis_index("x"), lax.axis_index("y"), lax.axis_index("c")
    return x, y, c, 4 * x + 2 * y + c


def _peers(x, y, c):
    out = []
    for mask in range(1, N_DEV):
        px = 1 - x if mask & 4 else x
        py = 1 - y if mask & 2 else y
        pc = 1 - c if mask & 1 else c
        out.append((mask, (px, py, pc), 4 * px + 2 * py + pc))
    return out


def _exchange(name, arrs, scatter, after=None):
    n = len(arrs)
    behind = [] if after is None else [after]

    def body(*refs):
        ins, outs = refs[:n], refs[n + len(behind):2 * n + len(behind)]
        send_sems, recv_sems, local_sems = refs[2 * n + len(behind):]
        x, y, c, me = _mesh_position()
        peers = _peers(x, y, c)
        waits = []
        for a in range(n):
            mine = ins[a].at[me] if scatter else ins[a]
            local = pltpu.make_async_copy(mine, outs[a].at[me], local_sems.at[a])
            local.start()
            waits.append(local.wait)
            for mask, dev, idx in peers:
                k = a * (N_DEV - 1) + mask - 1
                src = ins[a].at[idx] if scatter else ins[a]
                send = pltpu.make_async_remote_copy(src_ref=src, dst_ref=outs[a].at[me], send_sem=send_sems.at[k],
                                                    recv_sem=recv_sems.at[k], device_id=dev, device_id_type=MESH)
                send.start()
                arrival = pltpu.make_async_remote_copy(src_ref=src, dst_ref=outs[a].at[idx], send_sem=send_sems.at[k],
                                                       recv_sem=recv_sems.at[k], device_id=dev, device_id_type=MESH)
                waits.append(send.wait_send)
                waits.append(arrival.wait_recv)
        for w in waits:
            w()

    any_spec = pl.BlockSpec(memory_space=pl.ANY)
    out_shape = [jax.ShapeDtypeStruct(a.shape if scatter else (N_DEV,) + a.shape, a.dtype) for a in arrs]
    return pl.pallas_call(
        body, name=name, out_shape=out_shape, in_specs=[any_spec] * (n + len(behind)), out_specs=[any_spec] * n,
        scratch_shapes=[pltpu.SemaphoreType.DMA((n * (N_DEV - 1),)), pltpu.SemaphoreType.DMA((n * (N_DEV - 1),)),
                        pltpu.SemaphoreType.DMA((n,))],
    )(*arrs, *behind)


HBM_SPEC = pl.BlockSpec(memory_space=pltpu.HBM)
SEM_SPEC = pl.BlockSpec(memory_space=pltpu.SEMAPHORE)
DATAFLOW = pltpu.SideEffectType.DATAFLOW_SIDE_EFFECTING


ALL_MASKS = tuple(range(1, N_DEV))
SAME_CORE_MASKS = (1, 2, 4, 6)


def _exchange_begin(name, arrs, scatter, after=None, once_per_chip=()):
    n = len(arrs)
    lands = [lax.empty(a.shape if scatter else (N_DEV,) + a.shape, a.dtype) for a in arrs]
    behind = [] if after is None else [after]
    masks = [SAME_CORE_MASKS if a in once_per_chip else ALL_MASKS for a in range(n)]

    def body(*refs):
        srcs, zones, outs = refs[:n], refs[n:2 * n], refs[2 * n + len(behind):]
        x, y, c, me = _mesh_position()
        for a in range(n):
            send_sems, recv_sems = outs[4 * a], outs[4 * a + 1]
            for mask, dev, idx in _peers(x, y, c):
                if mask not in masks[a]:
                    continue
                pltpu.make_async_remote_copy(
                    src_ref=srcs[a].at[idx] if scatter else srcs[a], dst_ref=zones[a].at[me], send_sem=send_sems.at[mask - 1],
                    recv_sem=recv_sems.at[mask - 1], device_id=dev, device_id_type=MESH).start()
        outs[-1][...] = jnp.zeros_like(outs[-1])

    out_shape, out_specs, aliases = [], [], {}
    for a in range(n):
        out_shape += [pltpu.SemaphoreType.DMA((N_DEV - 1,)), pltpu.SemaphoreType.DMA((N_DEV - 1,)),
                      pltpu.HBM(arrs[a].shape, arrs[a].dtype), pltpu.HBM(lands[a].shape, lands[a].dtype)]
        out_specs += [SEM_SPEC, SEM_SPEC, HBM_SPEC, HBM_SPEC]
        aliases[a] = 4 * a + 2
        aliases[n + a] = 4 * a + 3
    out_shape.append(jax.ShapeDtypeStruct((8, LANE), F32))
    out_specs.append(pl.BlockSpec(memory_space=pltpu.VMEM))
    res = pl.pallas_call(
        body, name=name, out_shape=out_shape,
        in_specs=[HBM_SPEC] * (2 * n) + [pl.BlockSpec(memory_space=pl.ANY)] * len(behind),
        out_specs=out_specs, input_output_aliases=aliases, compiler_params=pltpu.CompilerParams(has_side_effects=DATAFLOW),
    )(*[pltpu.with_memory_space_constraint(v, pltpu.HBM) for v in list(arrs) + lands], *behind)
    return [tuple(res[4 * a:4 * a + 4]) + (masks[a],) for a in range(n)], res[-1]


def _exchange_end(name, handles, after, scatter):
    n = len(handles)
    me = 4 * lax.axis_index("x") + 2 * lax.axis_index("y") + lax.axis_index("c")

    def body(*refs):
        x, y, c, me = _mesh_position()
        for a in range(n):
            src, zone, send_sems, recv_sems = refs[4 * a:4 * a + 4]
            for mask, dev, idx in _peers(x, y, c):
                if mask not in handles[a][4]:
                    continue
                cp = pltpu.make_async_remote_copy(
                    src_ref=src.at[idx] if scatter else src, dst_ref=zone.at[idx], send_sem=send_sems.at[mask - 1],
                    recv_sem=recv_sems.at[mask - 1], device_id=dev, device_id_type=MESH)
                cp.wait_send()
                cp.wait_recv()

    operands, in_specs, out_shape, aliases = [], [], [], {}
    for a, (send_sems, recv_sems, src, zone, _) in enumerate(handles):
        operands += [src, zone, send_sems, recv_sems]
        in_specs += [HBM_SPEC, HBM_SPEC, SEM_SPEC, SEM_SPEC]
        out_shape += [pltpu.HBM(src.shape, src.dtype), pltpu.HBM(zone.shape, zone.dtype)]
        aliases[4 * a] = 2 * a
        aliases[4 * a + 1] = 2 * a + 1
    res = pl.pallas_call(
        body, name=name, out_shape=out_shape, in_specs=in_specs + [pl.BlockSpec(memory_space=pl.ANY)],
        out_specs=[HBM_SPEC] * (2 * n), input_output_aliases=aliases,
        compiler_params=pltpu.CompilerParams(has_side_effects=DATAFLOW),
    )(*operands, after)
    full = []
    for a in range(n):
        src, zone = res[2 * a], res[2 * a + 1]
        own = lax.dynamic_index_in_dim(src, me, 0, keepdims=False) if scatter else src
        full.append(lax.dynamic_update_index_in_dim(zone, own, me, 0))
    return full


def _sibling_forward(name, zones):
    n = len(zones)
    hops = (2, 4, 6)

    def body(*refs):
        outs, send_sems, recv_sems = refs[n:2 * n], refs[2 * n], refs[2 * n + 1]
        x, y, c, me = _mesh_position()
        sibling = (x, y, 1 - c)
        waits = []
        for a in range(n):
            for q, mask in enumerate(hops):
                chip = 4 * (1 - x if mask & 4 else x) + 2 * (1 - y if mask & 2 else y)
                k = a * len(hops) + q
                held, missing = outs[a].at[chip + c], outs[a].at[chip + 1 - c]
                send = pltpu.make_async_remote_copy(src_ref=held, dst_ref=held, send_sem=send_sems.at[k],
                                                    recv_sem=recv_sems.at[k], device_id=sibling, device_id_type=MESH)
                send.start()
                arrival = pltpu.make_async_remote_copy(src_ref=missing, dst_ref=missing, send_sem=send_sems.at[k],
                                                       recv_sem=recv_sems.at[k], device_id=sibling, device_id_type=MESH)
                waits += [send.wait_send, arrival.wait_recv]
        for w in waits:
            w()

    any_spec = pl.BlockSpec(memory_space=pl.ANY)
    return pl.pallas_call(
        body, name=name, out_shape=[jax.ShapeDtypeStruct(z.shape, z.dtype) for z in zones], in_specs=[any_spec] * n,
        out_specs=[any_spec] * n, input_output_aliases={a: a for a in range(n)},
        scratch_shapes=[pltpu.SemaphoreType.DMA((n * len(hops),)), pltpu.SemaphoreType.DMA((n * len(hops),))],
    )(*zones)


def _bspec(shape, tr, tc, rc, buffers=None):
    per = shape[-1] // tc
    mode = {} if buffers is None else dict(pipeline_mode=pl.Buffered(buffers))
    if len(shape) == 3:
        return pl.BlockSpec((None, tr, tc), lambda j, i, k: (rc(i, j, k)[1] // per, rc(i, j, k)[0], rc(i, j, k)[1] % per), **mode)
    return pl.BlockSpec((shape[0], None, tr, tc),
                        lambda j, i, k: (0, rc(i, j, k)[1] // per, rc(i, j, k)[0], rc(i, j, k)[1] % per), **mode)


def _ij(i, j, k):
    return i, j


def _row_spec(tn, col_tile_offset=0):
    return pl.BlockSpec((1, tn), lambda j, i, k: (0, j + col_tile_offset))


def _matmul(name, a, b, mode, tm, tn, tk, outs, epilogue=None, extras=(), b_buffers=None, side_by_side=False,
            pairs=False):
    groups = 1 if b.ndim == 3 else 2 if pairs else b.shape[0]
    if mode == "nn" and pairs:
        m, k_dim, n = a.shape[1], a.shape[0] * a.shape[2], b.shape[0] * tn
        a_spec = _bspec(a.shape, tm, tk, lambda i, j, k: (i, k))
        b_spec = pl.BlockSpec((None, 2, tk, tn), lambda j, i, k: (j, 0, k, 0),
                              **({} if b_buffers is None else dict(pipeline_mode=pl.Buffered(b_buffers))))
        dims = NN
    elif mode == "nn":
        m, k_dim, n = a.shape[1], a.shape[0] * a.shape[2], b.shape[-3] * b.shape[-1]
        a_spec = _bspec(a.shape, tm, tk, lambda i, j, k: (i, k))
        b_spec = _bspec(b.shape, tk, tn, lambda i, j, k: (k, j), b_buffers)
        dims = NN
    elif mode == "nt":
        m, k_dim, n = a.shape[1], a.shape[0] * a.shape[2], b.shape[-2]
        a_spec = _bspec(a.shape, tm, tk, lambda i, j, k: (i, k))
        b_spec = _bspec(b.shape, tn, tk, lambda i, j, k: (j, k), b_buffers)
        dims = NT
    else:
        m, k_dim, n = a.shape[0] * a.shape[2], a.shape[1], b.shape[-3] * b.shape[-1]
        a_spec = _bspec(a.shape, tk, tm, lambda i, j, k: (k, i))
        b_spec = _bspec(b.shape, tk, tn, lambda i, j, k: (k, j), b_buffers)
        dims = TN
    assert m % tm == 0 and n % tn == 0 and k_dim % tk == 0, (name, m, n, k_dim, tm, tn, tk)
    nk = k_dim // tk
    n_extra, n_out = len(extras), len(outs)

    def finish(acc, extra_refs, out_refs):
        if epilogue is None:
            out_refs[0][...] = acc[0].astype(out_refs[0].dtype)
        else:
            epilogue(acc, extra_refs, out_refs)

    def products(a_ref, b_ref):
        a_tile = a_ref[...].astype(BF16)
        return [lax.dot_general(a_tile, (b_ref[g] if b.ndim == 4 else b_ref[...]).astype(BF16), dims,
                                preferred_element_type=F32) for g in range(groups)]

    def body_whole_k(*refs):
        finish(products(refs[0], refs[1]), refs[2:2 + n_extra], refs[2 + n_extra:])

    def body_side_by_side(*refs):
        wide_ref = refs[-1]

        @pl.when(pl.program_id(1) == 0)
        def _():
            for g in range(groups):
                wide_ref[:, g * tn:(g + 1) * tn] = refs[1][g].astype(BF16)

        full = lax.dot_general(refs[0][...].astype(BF16), wide_ref[...], dims, preferred_element_type=F32)
        acc = [full] if pairs else [full[:, g * tn:(g + 1) * tn] for g in range(groups)]
        finish(acc, refs[2:2 + n_extra], refs[2 + n_extra:-1])

    def body_k_steps(*refs):
        acc_ref = refs[-1]
        k = pl.program_id(2)

        @pl.when(k == 0)
        def _():
            acc_ref[...] = jnp.zeros_like(acc_ref)

        for g, p in enumerate(products(refs[0], refs[1])):
            acc_ref[g] += p

        @pl.when(k == nk - 1)
        def _():
            finish([acc_ref[g] for g in range(groups)], refs[2:2 + n_extra], refs[2 + n_extra:2 + n_extra + n_out])

    if side_by_side:
        assert groups > 1 and nk == 1 and mode == "nn", name
        body, scratch, order = body_side_by_side, [pltpu.VMEM((tk, groups * tn), BF16)], "arbitrary"
    elif nk == 1:
        body, scratch, order = body_whole_k, [], "parallel"
    else:
        body, scratch, order = body_k_steps, [pltpu.VMEM((groups, tm, tn), F32)], "parallel"
    return pl.pallas_call(
        body, name=name, grid=(n // tn, m // tm, nk),
        in_specs=[a_spec, b_spec] + [s for _, s in extras],
        out_specs=[s for _, _, s in outs],
        out_shape=[jax.ShapeDtypeStruct(shape, dtype) for shape, dtype, _ in outs],
        scratch_shapes=scratch,
        compiler_params=_params(("parallel", order, "arbitrary")),
    )(a, b, *[arr for arr, _ in extras])


def _rowwise(name, fn, rows, vecs, outs, accs, tm):
    s = rows[0][0].shape[0]
    n_in, n_out = len(rows) + len(vecs), len(outs)

    def body(*refs):
        i = pl.program_id(0)
        res = fn(*[r[...] for r in refs[:n_in]])
        res = res if isinstance(res, tuple) else (res,)
        out_refs, acc_refs = refs[n_in:n_in + n_out], refs[n_in + n_out:]
        for ref, val in zip(out_refs, res[:n_out]):
            ref[...] = val.astype(ref.dtype)

        @pl.when(i == 0)
        def _():
            for ref in acc_refs:
                ref[...] = jnp.zeros_like(ref)

        for ref, val in zip(acc_refs, res[n_out:]):
            ref[...] += val

    in_specs = [pl.BlockSpec((tm, w), functools.partial(lambda i, cb: (i, cb), cb=cb)) for _, w, cb in rows]
    in_specs += [pl.BlockSpec(v.shape, lambda i: (0,) * v.ndim) for v in vecs]
    out_specs = [pl.BlockSpec((tm, w), lambda i: (i, 0)) for w, _ in outs] + [pl.BlockSpec((1, w), lambda i: (0, 0)) for w in accs]
    out_shape = [jax.ShapeDtypeStruct((s, w), dt) for w, dt in outs] + [jax.ShapeDtypeStruct((1, w), F32) for w in accs]
    return pl.pallas_call(
        body, name=name, grid=(s // tm,), in_specs=in_specs, out_specs=out_specs, out_shape=out_shape,
        compiler_params=_params(("arbitrary",)),
    )(*[r for r, _, _ in rows], *vecs)


def _colsum(v):
    return jnp.sum(v, axis=0, keepdims=True)


def _norm_mod(name, h, nw, sc, sh, tm):
    d = h.shape[1]

    def fn(hb, nwb, scb, shb):
        r = lax.rsqrt(jnp.mean(hb * hb, axis=-1, keepdims=True) + EPS)
        return (hb * r) * nwb * (1.0 + scb) + shb

    return _rowwise(name, fn, [(h, d, 0)], [nw, sc, sh], [(d, BF16)], [], tm)[0]


def _norm_mod_bwd(name, dy, h, dh_next, nw, sc, tm, below=None):
    d = h.shape[1]

    def fn(dyb, hb, dhb, *rest):
        nwb, scb = rest[-2:] if below is None else rest[1:3]
        r = lax.rsqrt(jnp.mean(hb * hb, axis=-1, keepdims=True) + EPS)
        xh = hb * r
        dxh = dyb * (nwb * (1.0 + scb))
        dx = r * (dxh - xh * jnp.mean(dxh * xh, axis=-1, keepdims=True))
        dh = dhb + dx
        sums = (_colsum(dyb), _colsum(dyb * xh * nwb), _colsum(dyb * xh * (1.0 + scb)))
        if below is None:
            return (dh,) + sums
        ob, gb = rest[0], rest[3]
        return (dh, dh * (below[2] * gb)) + sums + (_colsum(dh * ob * below[2]),)

    rows = [(dy, d, 0), (h, d, 0), (dh_next, d, 0)]
    if below is None:
        return _rowwise(name, fn, rows, [nw, sc], [(d, F32)], [d, d, d], tm)
    return _rowwise(name, fn, rows + [(below[0], d, 0)], [nw, sc, below[1]], [(d, F32), (d, BF16)], [d, d, d, d], tm)


def _loss_bwd(name, h, target, nw, o, g, tm):
    d = h.shape[1]

    def fn(hb, tb, ob, nwb, gb):
        r = lax.rsqrt(jnp.mean(hb * hb, axis=-1, keepdims=True) + EPS)
        xh = hb * r
        err = xh * nwb - tb
        dy = err * (1.0 / d)
        dxh = dy * nwb
        dx = r * (dxh - xh * jnp.mean(dxh * xh, axis=-1, keepdims=True))
        loss = 0.5 * jnp.sum(jnp.mean(err * err, axis=-1, keepdims=True), axis=0, keepdims=True)
        return dx, dx * (0.5 * gb), jnp.broadcast_to(loss, (1, LANE)), _colsum(dy * xh), _colsum(dx * ob * 0.5)

    return _rowwise(name, fn, [(h, d, 0), (target, d, 0), (o, d, 0)], [nw, g], [(d, F32), (d, BF16)], [LANE, d, d], tm)


def _head_group(nh, most):
    return max(g for g in (1, 2, 4) if g <= most and nh % g == 0)


def _attn_fwd(qkv, nh, t):
    s = qkv.shape[0]
    scale = HEAD_DIM ** -0.5
    hp = _head_group(nh, 4)
    wide = hp * HEAD_DIM
    lanes = [slice(u * HEAD_DIM, (u + 1) * HEAD_DIM) for u in range(hp)]

    def body(q_ref, k_ref, v_ref, y_ref, tot_ref):
        i = pl.program_id(1)
        row = lax.broadcasted_iota(jnp.int32, (t, t), 0)
        col = lax.broadcasted_iota(jnp.int32, (t, t), 1)
        later = (row > col).astype(BF16)
        causal = col < row
        qs = [q_ref[:, ln] for ln in lanes]

        def block(j, carry, diagonal):
            ks = pl.ds(pl.multiple_of(j * t, t), t)
            heads = range(hp)
            z = [lax.dot_general(qs[u], k_ref[ks, lanes[u]], NT, preferred_element_type=F32) * scale for u in heads]
            sp = [_softplus(z[u]) for u in heads]
            log_keep = [jnp.where(causal, -sp[u], 0.0) if diagonal else -sp[u] for u in heads]
            between = [_dot_split(log_keep[u], later) for u in heads]
            w = [jnp.exp(z[u] - sp[u] + between[u] + carry[u][1]) for u in heads]
            if diagonal:
                w = [jnp.where(causal, w[u], 0.0) for u in heads]
            o = [carry[u][0] + jnp.dot(w[u].astype(BF16), v_ref[ks, lanes[u]], preferred_element_type=F32) for u in heads]
            return tuple((o[u], carry[u][1] + jnp.sum(log_keep[u], axis=1, keepdims=True)) for u in heads)

        carry = tuple((jnp.zeros((t, HEAD_DIM), F32), jnp.zeros((t, 1), F32)) for _ in lanes)
        carry = block(i, carry, True)
        carry = lax.fori_loop(0, i, lambda jj, cr: block(i - 1 - jj, cr, False), carry)
        for u, ln in enumerate(lanes):
            y_ref[:, ln] = carry[u][0].astype(y_ref.dtype)
            tot_ref[:, ln] = jnp.broadcast_to(carry[u][1], (t, HEAD_DIM))

    g = nh // hp
    return pl.pallas_call(
        body, name="attn_fwd", grid=(g, s // t),
        in_specs=[pl.BlockSpec((t, wide), lambda h, i: (i, h)),
                  pl.BlockSpec((s, wide), lambda h, i: (0, g + h)),
                  pl.BlockSpec((s, wide), lambda h, i: (0, 2 * g + h))],
        out_specs=[pl.BlockSpec((t, wide), lambda h, i: (i, h)), pl.BlockSpec((t, wide), lambda h, i: (i, h))],
        out_shape=[jax.ShapeDtypeStruct((s, nh * HEAD_DIM), BF16), jax.ShapeDtypeStruct((s, nh * HEAD_DIM), F32)],
        compiler_params=_params(("parallel", "arbitrary")),
    )(qkv, qkv, qkv)


def _attn_bwd(qkv, dy, tot, nh, t):
    s = qkv.shape[0]
    scale = HEAD_DIM ** -0.5
    hp = _head_group(nh, 4)
    wide = hp * HEAD_DIM
    lanes = [slice(u * HEAD_DIM, (u + 1) * HEAD_DIM) for u in range(hp)]

    def body(q_ref, k_ref, v_ref, dy_ref, tot_ref, dq_ref, dk_out, dv_out, dk_ref, dv_ref):
        i = pl.program_id(1)

        @pl.when(i == 0)
        def _():
            dk_ref[...] = jnp.zeros_like(dk_ref)
            dv_ref[...] = jnp.zeros_like(dv_ref)

        row = lax.broadcasted_iota(jnp.int32, (t, t), 0)
        col = lax.broadcasted_iota(jnp.int32, (t, t), 1)
        upto = (row <= col).astype(BF16)
        before = (row < col).astype(BF16)
        causal = col < row
        qs = [q_ref[:, ln] for ln in lanes]
        dys = [dy_ref[:, ln] for ln in lanes]
        totals = [tot_ref[:, u * HEAD_DIM:u * HEAD_DIM + 1] for u in range(hp)]

        def block(j, carry, diagonal):
            ks = pl.ds(pl.multiple_of(j * t, t), t)
            heads = range(hp)
            kb = [k_ref[ks, ln] for ln in lanes]
            vb = [v_ref[ks, ln] for ln in lanes]
            z = [lax.dot_general(qs[u], kb[u], NT, preferred_element_type=F32) * scale for u in heads]
            dw = [lax.dot_general(dys[u], vb[u], NT, preferred_element_type=F32) for u in heads]
            sp = [_softplus(z[u]) for u in heads]
            log_keep = [jnp.where(causal, -sp[u], 0.0) if diagonal else -sp[u] for u in heads]
            upto_sum = [_dot_split(log_keep[u], upto) for u in heads]
            w = [jnp.exp(z[u] - sp[u] + (totals[u] - carry[u][1] - upto_sum[u])) for u in heads]
            if diagonal:
                w = [jnp.where(causal, w[u], 0.0) for u in heads]
            g = [dw[u] * w[u] for u in heads]
            g_before = [_dot_split(g[u], before) for u in heads]
            dz = [(g[u] * jnp.exp(-sp[u]) - jnp.exp(z[u] - sp[u]) * (carry[u][2] + g_before[u])) * scale for u in heads]
            if diagonal:
                dz = [jnp.where(causal, dz[u], 0.0) for u in heads]
            dzb = [dz[u].astype(BF16) for u in heads]
            dq = [carry[u][0] + jnp.dot(dzb[u], kb[u], preferred_element_type=F32) for u in heads]
            for u in heads:
                dk_ref[ks, lanes[u]] += lax.dot_general(dzb[u], qs[u], TN, preferred_element_type=F32)
            for u in heads:
                dv_ref[ks, lanes[u]] += lax.dot_general(w[u].astype(BF16), dys[u], TN, preferred_element_type=F32)
            return tuple((dq[u], carry[u][1] + jnp.sum(log_keep[u], axis=1, keepdims=True),
                          carry[u][2] + jnp.sum(g[u], axis=1, keepdims=True)) for u in heads)

        zero = jnp.zeros((t, 1), F32)
        carry = tuple((jnp.zeros((t, HEAD_DIM), F32), zero, zero) for _ in lanes)
        carry = lax.fori_loop(0, i, lambda j, cr: block(j, cr, False), carry)
        carry = block(i, carry, True)
        for u, ln in enumerate(lanes):
            dq_ref[:, ln] = carry[u][0].astype(dq_ref.dtype)

        @pl.when(i == pl.num_programs(1) - 1)
        def _():
            dk_out[...] = dk_ref[...].astype(dk_out.dtype)
            dv_out[...] = dv_ref[...].astype(dv_out.dtype)

    g = nh // hp
    tile = lambda off: pl.BlockSpec((t, wide), lambda h, i: (i, off + h))
    head = lambda off, **mode: pl.BlockSpec((s, wide), lambda h, i: (0, off + h), **mode)
    once = dict(pipeline_mode=pl.Buffered(1))
    return pl.pallas_call(
        body, name="attn_bwd", grid=(g, s // t),
        in_specs=[tile(0), head(g, **once), head(2 * g, **once), tile(0), tile(0)],
        out_specs=[tile(0), head(0), head(0)],
        out_shape=[jax.ShapeDtypeStruct((s, nh * HEAD_DIM), BF16)] * 3,
        scratch_shapes=[pltpu.VMEM((s, wide), F32), pltpu.VMEM((s, wide), F32)],
        compiler_params=_params(("parallel", "arbitrary")),
    )(qkv, qkv, qkv, dy, tot)


def _lru_gates(xc, w_r, b_r, w_i, b_i, lam):
    xb = xc.astype(BF16)
    r = _sigmoid(jnp.dot(xb, w_r.astype(BF16), preferred_element_type=F32) + b_r)
    i = _sigmoid(jnp.dot(xb, w_i.astype(BF16), preferred_element_type=F32) + b_i)
    neg_lam = -lam
    sp_lam = jnp.maximum(neg_lam, 0.0) + _log1p(jnp.exp(-jnp.abs(neg_lam)))
    log_a = -LRU_C * r * sp_lam
    a = jnp.exp(log_a)
    mult = jnp.sqrt(-_expm1(2.0 * log_a))
    return r, i, sp_lam, a, mult


def _conv_taps(xpad_chunk, conv_w, t):
    shifted = [xpad_chunk[CONV_HALO:, :]]
    for d in range(1, CONV_WIDTH):
        shifted.append(pltpu.roll(xpad_chunk, d, 0)[CONV_HALO:, :])
    weights = [conv_w[CONV_WIDTH - 1 - d:CONV_WIDTH - d, :] for d in range(CONV_WIDTH)]
    return shifted, weights


def _lru_fwd(xr_pad, proj, gr_block0, conv_w, conv_b, w_r, b_r, w_i, b_i, lam, t):
    s, w = xr_pad.shape[0] - CONV_HALO, xr_pad.shape[1]
    nblk = w // LANE
    nchunk = s // t
    steps = [1 << p for p in range(t.bit_length() - 1)]
    assert (1 << (t.bit_length() - 1)) == t and w_r.shape[1:] == (LANE, LANE)

    def body(x_ref, gr_ref, cw_ref, cb_ref, wr_ref, br_ref, wi_ref, bi_ref, lam_ref, h_ref, hp_ref, xc_ref, y_ref):
        row = lax.broadcasted_iota(jnp.int32, (t, LANE), 0)

        def chunk(ci, h_in):
            t0 = pl.multiple_of(ci * t, t)
            shifted, weights = _conv_taps(x_ref[pl.ds(t0, t + CONV_HALO), :], cw_ref[...], t)
            xc = cb_ref[...] + sum(wd * xs for wd, xs in zip(weights, shifted))
            r, i, _, a, mult = _lru_gates(xc, wr_ref[...], br_ref[...], wi_ref[...], bi_ref[...], lam_ref[...])
            coef, val = a, mult * (i * xc)
            for d in steps:
                ok = row >= d
                val = jnp.where(ok, coef * pltpu.roll(val, d, 0) + val, val)
                coef = jnp.where(ok, coef * pltpu.roll(coef, d, 0), coef)
            h = val + coef * h_in
            rows = pl.ds(t0, t)
            h_ref[rows, :] = h
            hp_ref[rows, :] = jnp.where(row == 0, h_in, pltpu.roll(h, 1, 0))
            xc_ref[rows, :] = xc
            y_ref[rows, :] = (h * _gelu_and_grad(gr_ref[rows, :])[0]).astype(y_ref.dtype)
            return h[t - 1:t, :]

        lax.fori_loop(0, nchunk, chunk, jnp.zeros((1, LANE), F32))

    col = lambda rows: pl.BlockSpec((rows, LANE), lambda n: (0, n))
    return pl.pallas_call(
        body, name="lru_fwd", grid=(nblk,),
        in_specs=[col(s + CONV_HALO), pl.BlockSpec((s, LANE), lambda n: (0, gr_block0 + n)), col(CONV_WIDTH), col(1),
                  pl.BlockSpec((None, LANE, LANE), lambda n: (n, 0, 0)), col(1),
                  pl.BlockSpec((None, LANE, LANE), lambda n: (n, 0, 0)), col(1), col(1)],
        out_specs=[col(s)] * 4,
        out_shape=[jax.ShapeDtypeStruct((s, w), F32)] * 3 + [jax.ShapeDtypeStruct((s, w), BF16)],
        compiler_params=_params(("parallel",)),
    )(xr_pad, proj, conv_w, conv_b, w_r, b_r, w_i, b_i, lam)


def _lru_bwd(dy, proj, gr_block0, h, h_prev, xc, w_r, b_r, w_i, b_i, lam, t):
    s, w = dy.shape
    nblk = w // LANE
    nchunk = s // t
    steps = [1 << p for p in range(t.bit_length() - 1)]

    def body(dy_ref, gr_ref, h_ref, hp_ref, xc_ref, wr_ref, br_ref, wi_ref, bi_ref, lam_ref,
             dgr_ref, dxc_ref, dwr_ref, dwi_ref, dbr_ref, dbi_ref, dlam_ref):
        row = lax.broadcasted_iota(jnp.int32, (t, LANE), 0)
        for ref in (dwr_ref, dwi_ref, dbr_ref, dbi_ref, dlam_ref):
            ref[...] = jnp.zeros_like(ref)

        def chunk(cc, carry):
            lam_next, a_next = carry
            rows = pl.ds(pl.multiple_of((nchunk - 1 - cc) * t, t), t)
            dyb, hb, xcb = dy_ref[rows, :], h_ref[rows, :], xc_ref[rows, :]
            gel, dgel = _gelu_and_grad(gr_ref[rows, :])
            dgr_ref[rows, :] = dyb * hb * dgel
            w_r, w_i = wr_ref[...], wi_ref[...]
            r, i, sp_lam, a, mult = _lru_gates(xcb, w_r, br_ref[...], w_i, bi_ref[...], lam_ref[...])
            coef = jnp.where(row == t - 1, a_next, pltpu.roll(a, t - 1, 0))
            val = dyb * gel
            for d in steps:
                ok = row < t - d
                val = jnp.where(ok, coef * pltpu.roll(val, t - d, 0) + val, val)
                coef = jnp.where(ok, coef * pltpu.roll(coef, t - d, 0), coef)
            adj = val + coef * lam_next
            da = adj * hp_ref[rows, :]
            v = i * xcb
            dmult, dv = adj * v, adj * mult
            dlog_a = da * a - (a * a) * dmult / mult
            dr_pre = (-LRU_C * sp_lam) * dlog_a * r * (1.0 - r)
            di_pre = dv * xcb * i * (1.0 - i)
            dlam_ref[...] += _colsum(-LRU_C * r * dlog_a)
            dbr_ref[...] += _colsum(dr_pre)
            dbi_ref[...] += _colsum(di_pre)
            xb, drb, dib = xcb.astype(BF16), dr_pre.astype(BF16), di_pre.astype(BF16)
            dwr_ref[...] += lax.dot_general(xb, drb, TN, preferred_element_type=F32)
            dwi_ref[...] += lax.dot_general(xb, dib, TN, preferred_element_type=F32)
            dxc_ref[rows, :] = (dv * i + lax.dot_general(drb, w_r.astype(BF16), NT, preferred_element_type=F32)
                                + lax.dot_general(dib, w_i.astype(BF16), NT, preferred_element_type=F32))
            return adj[0:1, :], a[0:1, :]

        lax.fori_loop(0, nchunk, chunk, (jnp.zeros((1, LANE), F32), jnp.zeros((1, LANE), F32)))
        dlam_ref[...] = dlam_ref[...] * (-_sigmoid(-lam_ref[...]))

    col = lambda rows: pl.BlockSpec((rows, LANE), lambda n: (0, n))
    mat = pl.BlockSpec((None, LANE, LANE), lambda n: (n, 0, 0))
    return pl.pallas_call(
        body, name="lru_bwd", grid=(nblk,),
        in_specs=[col(s), pl.BlockSpec((s, LANE), lambda n: (0, gr_block0 + n)), col(s), col(s), col(s),
                  mat, col(1), mat, col(1), col(1)],
        out_specs=[col(s), col(s), mat, mat, col(1), col(1), col(1)],
        out_shape=[jax.ShapeDtypeStruct((s, w), F32)] * 2 + [jax.ShapeDtypeStruct((nblk, LANE, LANE), F32)] * 2
        + [jax.ShapeDtypeStruct((1, w), F32)] * 3,
        compiler_params=_params(("parallel",)),
    )(dy, proj, h, h_prev, xc, w_r, b_r, w_i, b_i, lam)


def _conv_bwd(xr_pad, dxc_pad, conv_w, t):
    s, w = xr_pad.shape[0] - CONV_HALO, xr_pad.shape[1]
    nchunk = s // t

    def body(x_ref, g_ref, cw_ref, dx_ref, dcw_ref, dcb_ref):
        dcw_ref[...] = jnp.zeros_like(dcw_ref)
        dcb_ref[...] = jnp.zeros_like(dcb_ref)

        def chunk(ci, _):
            t0 = pl.multiple_of(ci * t, t)
            shifted, weights = _conv_taps(x_ref[pl.ds(t0, t + CONV_HALO), :], cw_ref[...], t)
            gpad = g_ref[pl.ds(t0, t + CONV_HALO), :]
            g = gpad[:t, :]
            dx = weights[0] * g
            for d in range(1, CONV_WIDTH):
                dx = dx + weights[d] * pltpu.roll(gpad, t + CONV_HALO - d, 0)[:t, :]
            dx_ref[pl.ds(t0, t), :] = dx
            for d in range(CONV_WIDTH):
                dcw_ref[CONV_WIDTH - 1 - d:CONV_WIDTH - d, :] += _colsum(g * shifted[d])
            dcb_ref[...] += _colsum(g)
            return 0

        lax.fori_loop(0, nchunk, chunk, 0)

    col = lambda rows: pl.BlockSpec((rows, LANE), lambda n: (0, n))
    return pl.pallas_call(
        body, name="conv_bwd", grid=(w // LANE,),
        in_specs=[col(s + CONV_HALO), col(s + CONV_HALO), col(CONV_WIDTH)],
        out_specs=[col(s), col(CONV_WIDTH), col(1)],
        out_shape=[jax.ShapeDtypeStruct((s, w), F32), jax.ShapeDtypeStruct((CONV_WIDTH, w), F32),
                   jax.ShapeDtypeStruct((1, w), F32)],
        compiler_params=_params(("parallel",)),
    )(xr_pad, dxc_pad, conv_w)


def _sum_parts(parts_ref):
    g = parts_ref[0].astype(F32)
    for p in range(1, parts_ref.shape[0]):
        g = g + parts_ref[p].astype(F32)
    return g


def _reduce_parts(name, parts):
    p, r, c = parts.shape
    tr = _tile(r, max(8, (1 << 19) // c), 8)

    def body(parts_ref, g_ref):
        g_ref[...] = _sum_parts(parts_ref)

    return pl.pallas_call(
        body, name=name, grid=(r // tr,), in_specs=[pl.BlockSpec((p, tr, c), lambda i: (0, i, 0))],
        out_specs=pl.BlockSpec((tr, c), lambda i: (i, 0)), out_shape=jax.ShapeDtypeStruct((r, c), F32),
        compiler_params=_params(("parallel",)),
    )(parts)


def _adamw(name, parts, w, m, v):
    p, r, c = parts.shape
    tr = _tile(r, max(8, (1 << 18) // c), 8)

    def body(parts_ref, w_ref, m_ref, v_ref, g_ref, d_ref, nm_ref, nv_ref):
        g = _sum_parts(parts_ref)
        nm = ADAM_B1 * m_ref[...] + (1.0 - ADAM_B1) * g
        nv = ADAM_B2 * v_ref[...] + (1.0 - ADAM_B2) * (g * g)
        m_hat = nm / (1.0 - ADAM_B1 ** ADAM_STEP)
        v_hat = nv / (1.0 - ADAM_B2 ** ADAM_STEP)
        g_ref[...] = g
        d_ref[...] = -ADAM_LR * (m_hat / (jnp.sqrt(v_hat) + ADAM_EPS) + ADAM_WD * w_ref[...])
        nm_ref[...] = nm
        nv_ref[...] = nv

    blk = pl.BlockSpec((tr, c), lambda i: (i, 0))
    return pl.pallas_call(
        body, name=name, grid=(r // tr,), in_specs=[pl.BlockSpec((p, tr, c), lambda i: (0, i, 0)), blk, blk, blk],
        out_specs=[blk] * 4, out_shape=[jax.ShapeDtypeStruct((r, c), F32)] * 4,
        compiler_params=_params(("parallel",)),
    )(parts, w, m, v)


def _ffn_in(tag, y, w_in_g, tm):
    s, d = y.shape
    half = N_DEV // 2
    cb = w_in_g.shape[2]
    ff = half * cb

    def swiglu(acc, extra_refs, out_refs):
        g, u = acc
        out_refs[0][0] = g.astype(BF16)
        out_refs[0][1] = u.astype(BF16)
        out_refs[1][...] = (g * _sigmoid(g) * u).astype(BF16)

    gu_shape = (2, 1, s, ff)
    gu, act = _matmul(
        tag + "_in", y[None], w_in_g.reshape(2, half, d, cb), "nn", tm, cb, d,
        outs=[(gu_shape, BF16, _bspec(gu_shape, tm, cb, _ij)), ((1, s, ff), BF16, _bspec((1, s, ff), tm, cb, _ij))],
        epilogue=swiglu, b_buffers=1, side_by_side=True)
    return gu, act


def _ffn_out(tag, act, w_out_g, res, gate, tm):
    _, s, ff = act.shape
    d = res.shape[1]
    tn = _tile(d, 1024)

    def residual(acc, extra_refs, out_refs):
        out_refs[0][...] = acc[0].astype(BF16)
        out_refs[1][...] = extra_refs[0][...] + 0.5 * extra_refs[1][...] * acc[0]

    plain = _bspec((1, s, d), tm, tn, _ij)
    o, h_new = _matmul(
        tag + "_out", act, w_out_g.reshape(1, ff, d), "nn", tm, tn, ff,
        outs=[((1, s, d), BF16, plain), ((1, s, d), F32, plain)], epilogue=residual,
        extras=[(res[None], plain), (gate, _row_spec(tn))], b_buffers=1)
    return o[0], h_new[0]


def _after_token(token):
    return token, pl.BlockSpec(token.shape, lambda j, i, k: (0, 0))


def _ffn_bwd_weights(tag, do, y, gu, act, w_in_g, w_out_g, tm):
    s, d = do.shape
    half = N_DEV // 2
    cb = w_in_g.shape[2]
    ff = half * cb

    tn_d, tm_f = _tile(d, 1024), _tile(ff, 512)
    dw_out = _matmul(tag + "_dw_out", act, do[None], "tn", tm_f, tn_d, s,
                     outs=[((1, ff, d), BF16, _bspec((1, ff, d), tm_f, tn_d, _ij))], b_buffers=1)[0]
    dw_out = dw_out.reshape(N_DEV, ff // N_DEV, d)
    out_handles, token = _exchange_begin(tag + "_scatter_out", [dw_out], True)

    def dswiglu(acc, extra_refs, out_refs):
        dact = acc[0]
        g, u = extra_refs[0][0].astype(F32), extra_refs[0][1].astype(F32)
        sg = _sigmoid(g)
        out_refs[0][0] = (dact * u * sg * (1.0 + g * (1.0 - sg))).astype(BF16)
        out_refs[0][1] = (dact * g * sg).astype(BF16)

    gu_shape = (2, 1, s, ff)
    tn_e = 2 * cb
    gu_spec = _bspec(gu_shape, tm, tn_e, _ij)
    dgu = _matmul(tag + "_dact", do[None], w_out_g.reshape(1, ff, d), "nt", tm, tn_e, d,
                  outs=[(gu_shape, BF16, gu_spec)], epilogue=dswiglu, extras=[(gu, gu_spec), _after_token(token)],
                  b_buffers=1)[0]
    dgu = dgu.reshape(2, s, ff)

    tm_d = _tile(d, 512)

    def two_blocks(acc, extra_refs, out_refs):
        out_refs[0][0] = acc[0][:, :cb].astype(BF16)
        out_refs[0][1] = acc[0][:, cb:].astype(BF16)

    pair_spec = pl.BlockSpec((None, 2, tm_d, cb), lambda j, i, k: (j, 0, i, 0))
    dw_in = _matmul(tag + "_dw_in", y[None], dgu, "tn", tm_d, tn_e, s,
                    outs=[((half, 2, d, cb), BF16, pair_spec)], epilogue=two_blocks, b_buffers=1)[0]
    dw_in = dw_in.reshape(N_DEV, d, cb)
    in_handles, token = _exchange_begin(tag + "_scatter_in", [dw_in], True)
    return dgu, in_handles + out_handles, token


def _ffn_bwd_input(tag, dgu, w_in_g, token):
    s, d = dgu.shape[1], w_in_g.shape[1]
    tm_big = _tile(s, 1024, 8)
    return _matmul(tag + "_dy", dgu, w_in_g, "nt", tm_big, d, w_in_g.shape[2],
                   outs=[((1, s, d), F32, _bspec((1, s, d), tm_big, d, _ij))], extras=[_after_token(token)])[0][0]


def kernel(x, c, w_ada, b_ada, norm_ffn1, w_ffn1_in, w_ffn1_out, norm_mix, w_in, conv_w, conv_b, w_rg_gate, b_rg_gate, w_in_gate, b_in_gate, lru_lambda, w_branch_attn, w_branch_lru, w_out, norm_ffn2, w_ffn2_in, w_ffn2_out, norm_final, loss_target, m_w_ada, m_b_ada, m_norm_ffn1, m_w_ffn1_in, m_w_ffn1_out, m_norm_mix, m_w_in, m_conv_w, m_conv_b, m_w_rg_gate, m_b_rg_gate, m_w_in_gate, m_b_in_gate, m_lru_lambda, m_w_branch_attn, m_w_branch_lru, m_w_out, m_norm_ffn2, m_w_ffn2_in, m_w_ffn2_out, m_norm_final, v_w_ada, v_b_ada, v_norm_ffn1, v_w_ffn1_in, v_w_ffn1_out, v_norm_mix, v_w_in, v_conv_w, v_conv_b, v_w_rg_gate, v_b_rg_gate, v_w_in_gate, v_b_in_gate, v_lru_lambda, v_w_branch_attn, v_w_branch_lru, v_w_out, v_norm_ffn2, v_w_ffn2_in, v_w_ffn2_out, v_norm_final):
    xs, target = x[0], loss_target[0]
    s, d = xs.shape
    aw, lw = w_branch_attn.shape[1], w_branch_lru.shape[1]
    nh, nlb = aw // HEAD_DIM, w_rg_gate.shape[1]
    cba, cbi, cbb, cwb = w_ada.shape[2], w_in.shape[2], w_branch_attn.shape[2], conv_w.shape[2]
    assert lw == nlb * LANE and cwb * N_DEV == lw and 3 * aw + 2 * lw + 2 * d == cbi * N_DEV
    me = 4 * lax.axis_index("x") + 2 * lax.axis_index("y") + lax.axis_index("c")
    tm = _tile(s, 512, 8)
    tr = _tile(s, 256, 8)
    t_attn = _tile(s, 256, 8)
    t_lru = _tile(s, 256, 8)

    small = _exchange("gather_c", [jnp.concatenate([c, conv_w.reshape(1, CONV_WIDTH * cwb)], axis=1)], False)[0][:, 0, :]
    c_all = small[:, :d]
    conv_w_full = small[:, d:].reshape(N_DEV, CONV_WIDTH, cwb).transpose(1, 0, 2).reshape(CONV_WIDTH, lw)
    c_act = _rowwise("silu_c", lambda v: v * _sigmoid(v), [(c_all, d, 0)], [], [(d, F32)], [], N_DEV)[0]

    def add_bias(acc_ref, extra_refs, out_refs):
        out_refs[0][...] = acc_ref[0] + extra_refs[0][...]

    b_ada_mine = lax.dynamic_slice(b_ada, (0, me * cba), (1, cba))
    mod_part = _matmul("mod", c_act[None], w_ada, "nn", N_DEV, cba, _tile(d, 512),
                       outs=[((1, N_DEV, cba), F32, _bspec((1, N_DEV, cba), N_DEV, cba, _ij))], epilogue=add_bias,
                       extras=[(b_ada_mine, _row_spec(cba))])[0][0]
    mod_all = _exchange("gather_mod", [mod_part], False)[0]
    mod = lax.dynamic_index_in_dim(mod_all, me, axis=1, keepdims=False).reshape(1, 9 * d)
    sh1, sc1, g1, sh2, sc2, g2, sh3, sc3, g3 = [mod[:, n * d:(n + 1) * d] for n in range(9)]

    shards = [w_ffn1_in[0], w_ffn1_out[0], w_in[0], w_branch_attn[0], w_branch_lru[0], w_out[0], w_ffn2_in[0], w_ffn2_out[0]]
    early = (0, 1, 2)
    gathers, token = _exchange_begin("gather_w_early", [w.astype(BF16) for w in shards[:3]], False, mod, early)
    late, token = _exchange_begin("gather_w", [(w + token[0, 0]).astype(BF16) for w in shards[3:]], False)
    gathers = gathers + late

    def gathered(n, after):
        full = _exchange_end("gathered_w%d" % n, [gathers[n]], after, False)
        return (_sibling_forward("forwarded_w%d" % n, full) if n in early else full)[0]

    y1 = _norm_mod("norm1", xs, norm_ffn1 + token[:1, :1], sc1, sh1, tr)
    wf1i = gathered(0, y1)
    gu1, act1 = _ffn_in("ffn1", y1, wf1i, tm)
    wf1o = gathered(1, act1)
    o1, h1 = _ffn_out("ffn1", act1, wf1o, xs, g1, tm)

    y2 = _norm_mod("norm2", h1, norm_mix, sc2, sh2, tr)
    wi_g = gathered(2, y2)
    tn_i = _tile(cbi, 1152)
    tm_big = _tile(s, 1024, 8)
    proj = _matmul("mix_in", y2[None], wi_g.reshape(N_DEV // 2, 2, d, cbi), "nn", tm, cbi, d,
                   outs=[((1, s, N_DEV * cbi), F32, _bspec((1, s, N_DEV * cbi), tm, 2 * cbi, _ij))], b_buffers=1,
                   side_by_side=True, pairs=True)[0][0]
    off_xr, off_gr, off_ga, off_gl = 3 * aw, 3 * aw + lw, 3 * aw + 2 * lw, 3 * aw + 2 * lw + d
    qkv = proj[:, :3 * aw].astype(BF16)
    y_attn, attn_tot = _attn_fwd(qkv, nh, t_attn)
    xr_pad = jnp.pad(proj[:, off_xr:off_xr + lw], ((CONV_HALO, 0), (0, 0)))
    w_r, w_i = w_rg_gate[0], w_in_gate[0]
    h_lru, h_prev, xc, y_lru = _lru_fwd(xr_pad, proj, off_gr // LANE, conv_w_full, conv_b, w_r, b_rg_gate, w_i,
                                        b_in_gate, lru_lambda, t_lru)
    wba_p = gathered(3, y_attn).transpose(1, 0, 2).reshape(1, aw, d)
    wbl_p = gathered(4, y_lru).transpose(1, 0, 2).reshape(1, lw, d)
    proj3 = proj[None]
    tm_b = _tile(s, 1024, 8)
    tn_m = _tile(math.gcd(d, off_ga, off_gl), 1024)
    plain_m = _bspec((1, s, d), tm_b, tn_m, _ij)
    gate_specs = [_bspec(proj3.shape, tm_b, tn_m, functools.partial(lambda i, j, k, o: (i, j + o), o=o // tn_m))
                  for o in (off_ga, off_gl)]
    ya = _matmul("branch_attn", y_attn[None], wba_p, "nn", tm_b, tn_m, aw, outs=[((1, s, d), BF16, plain_m)], b_buffers=1)[0]

    def merge(acc_ref, extra_refs, out_refs):
        yl = acc_ref[0]
        ya_t, ga, gl = extra_refs[0][...], extra_refs[1][...], extra_refs[2][...]
        out_refs[0][...] = yl.astype(BF16)
        out_refs[1][...] = (_sigmoid(ga) * ya_t + _sigmoid(gl) * yl).astype(BF16)

    yl, merged = _matmul("branch_lru", y_lru[None], wbl_p, "nn", tm_b, tn_m, lw,
                         outs=[((1, s, d), BF16, plain_m), ((1, s, d), BF16, plain_m)], epilogue=merge,
                         extras=[(ya, plain_m), (proj3, gate_specs[0]), (proj3, gate_specs[1])], b_buffers=1)
    tn_d = _tile(d, 1024)
    plain = _bspec((1, s, d), tm, tn_d, _ij)

    def residual(acc_ref, extra_refs, out_refs):
        o = acc_ref[0]
        out_refs[0][...] = o.astype(BF16)
        out_refs[1][...] = extra_refs[0][...] + extra_refs[1][...] * o

    wo_g = gathered(5, merged)
    mo, h2 = _matmul("mix_out", merged, wo_g.reshape(1, d, d), "nn", tm, tn_d, d,
                     outs=[((1, s, d), BF16, plain), ((1, s, d), F32, plain)], epilogue=residual,
                     extras=[(h1[None], plain), (g2, _row_spec(tn_d))], b_buffers=1)
    mo, h2 = mo[0], h2[0]

    y3 = _norm_mod("norm3", h2, norm_ffn2, sc3, sh3, tr)
    wf2i = gathered(6, y3)
    gu3, act3 = _ffn_in("ffn2", y3, wf2i, tm)
    wf2o = gathered(7, act3)
    o3, h3 = _ffn_out("ffn2", act3, wf2o, h2, g3, tm)

    nf = norm_final.reshape(1, d)
    dh3, do3, loss_part, d_nf, dg3 = _loss_bwd("loss", h3, target, nf, o3, g3, tr)
    dgu3, scatter_ffn2, token = _ffn_bwd_weights("ffn2", do3, y3, gu3, act3, wf2i, wf2o, tm)
    dy3 = _ffn_bwd_input("ffn2", dgu3, wf2i, token)
    dh2, dmo, dsh3, dsc3, dn3, dg2 = _norm_mod_bwd("norm3_bwd", dy3, h2, dh3, norm_ffn2, sc3, tr, below=(mo, g2, 1.0))

    dwo = _matmul("mix_dw_out", merged, dmo[None], "tn", _tile(d, 512), tn_d, s,
                  outs=[((1, d, d), BF16, _bspec((1, d, d), _tile(d, 512), tn_d, _ij))], b_buffers=1)[0]

    def dmerge(acc_ref, extra_refs, out_refs):
        dm = acc_ref[0]
        ya_t, yl_t = extra_refs[0][...], extra_refs[1][...]
        sa, sl = _sigmoid(extra_refs[2][...]), _sigmoid(extra_refs[3][...])
        out_refs[0][...] = (dm * sa).astype(BF16)
        out_refs[1][...] = (dm * sl).astype(BF16)
        out_refs[2][...] = (dm * ya_t * sa * (1.0 - sa)).astype(BF16)
        out_refs[3][...] = (dm * yl_t * sl * (1.0 - sl)).astype(BF16)

    tn_m = _tile(math.gcd(d, off_ga, off_gl), 1024)
    plain_m = _bspec((1, s, d), tm, tn_m, _ij)
    gate_specs = [_bspec(proj3.shape, tm, tn_m, functools.partial(lambda i, j, k, o: (i, j + o), o=o // tn_m))
                  for o in (off_ga, off_gl)]
    dya, dyl, dga, dgl = _matmul("mix_dmerged", dmo[None], wo_g.reshape(1, d, d), "nt", tm, tn_m, d,
                                 outs=[((1, s, d), BF16, plain_m)] * 4, epilogue=dmerge,
                                 extras=[(ya, plain_m), (yl, plain_m), (proj3, gate_specs[0]), (proj3, gate_specs[1])],
                                 b_buffers=1)
    tm_a, tm_l = _tile(aw, 1024), _tile(lw, 1024)
    dwba = _matmul("dw_branch_attn", y_attn[None], dya, "tn", tm_a, cbb, s,
                   outs=[((N_DEV, aw, cbb), BF16, _bspec((N_DEV, aw, cbb), tm_a, cbb, _ij))])[0]
    dwbl = _matmul("dw_branch_lru", y_lru[None], dyl, "tn", tm_l, cbb, s,
                   outs=[((N_DEV, lw, cbb), BF16, _bspec((N_DEV, lw, cbb), tm_l, cbb, _ij))])[0]
    scatter_branch, token = _exchange_begin("scatter_branch", [dwba, dwbl, dwo.reshape(N_DEV, d // N_DEV, d)], True)
    tn_a, tn_l = _tile(aw, 1024), _tile(lw, 1024)
    dy_attn = _matmul("d_attn_out", dya, wba_p, "nt", tm_b, tn_a, d,
                      outs=[((1, s, aw), BF16, _bspec((1, s, aw), tm_b, tn_a, _ij))], extras=[_after_token(token)],
                      b_buffers=1)[0][0]
    dy_lru = _matmul("d_lru_out", dyl, wbl_p, "nt", tm_b, tn_l, d,
                     outs=[((1, s, lw), F32, _bspec((1, s, lw), tm_b, tn_l, _ij))], b_buffers=1)[0][0]
    dq, dk, dv = _attn_bwd(qkv, dy_attn, attn_tot, nh, t_attn)
    dgr, dxc, d_wr, d_wi, d_br, d_bi, d_lam = _lru_bwd(dy_lru, proj, off_gr // LANE, h_lru, h_prev, xc, w_r, b_rg_gate,
                                                       w_i, b_in_gate, lru_lambda, t_lru)
    dxr, d_cw, d_cb = _conv_bwd(xr_pad, jnp.pad(dxc, ((0, CONV_HALO), (0, 0))), conv_w_full, t_lru)
    dproj = jnp.concatenate([dq.astype(BF16), dk.astype(BF16), dv.astype(BF16), dxr.astype(BF16), dgr.astype(BF16),
                             dga[0], dgl[0]], axis=1)
    tm_d = _tile(d, 512)

    def two_blocks(acc_ref, extra_refs, out_refs):
        out_refs[0][0] = acc_ref[0][:, :cbi].astype(BF16)
        out_refs[0][1] = acc_ref[0][:, cbi:].astype(BF16)

    dwi = _matmul("mix_dw_in", y2[None], dproj[None], "tn", tm_d, 2 * cbi, s,
                  outs=[((N_DEV // 2, 2, d, cbi), BF16, pl.BlockSpec((None, 2, tm_d, cbi), lambda j, i, k: (j, 0, i, 0)))],
                  epilogue=two_blocks, b_buffers=1)[0].reshape(N_DEV, d, cbi)
    scatter_mix, token = _exchange_begin("scatter_mix", [dwi], True)
    dy2 = _matmul("mix_dy", dproj[None], wi_g, "nt", tm_big, d, tn_i,
                  outs=[((1, s, d), F32, _bspec((1, s, d), tm_big, d, _ij))], extras=[_after_token(token)])[0][0]
    dh1, do1, dsh2, dsc2, dn2, dg1 = _norm_mod_bwd("norm2_bwd", dy2, h1, dh2, norm_mix, sc2, tr, below=(o1, g1, 0.5))

    dgu1, scatter_ffn1, token = _ffn_bwd_weights("ffn1", do1, y1, gu1, act1, wf1i, wf1o, tm)
    dy1 = _ffn_bwd_input("ffn1", dgu1, wf1i, token)
    grad_x, dsh1, dsc1, dn1 = _norm_mod_bwd("norm1_bwd", dy1, xs, dh1, norm_ffn1, sc1, tr)

    results = {}

    def update_group(wait_name, handles, leaves, after):
        for (n, w, m, v), parts in zip(leaves, _exchange_end(wait_name, handles, after, True)):
            results[n] = [o[None] for o in _adamw("adamw_" + n, parts, w[0], m[0], v[0])]
        return results[leaves[-1][0]][0]

    done = update_group("scattered_ffn2", scatter_ffn2, [("w_ffn2_in", w_ffn2_in, m_w_ffn2_in, v_w_ffn2_in),
                                                         ("w_ffn2_out", w_ffn2_out, m_w_ffn2_out, v_w_ffn2_out)], grad_x)
    done = update_group("scattered_mix", scatter_branch + scatter_mix,
                        [("w_branch_attn", w_branch_attn, m_w_branch_attn, v_w_branch_attn),
                         ("w_branch_lru", w_branch_lru, m_w_branch_lru, v_w_branch_lru),
                         ("w_out", w_out, m_w_out, v_w_out), ("w_in", w_in, m_w_in, v_w_in)], done)

    lane_pad = jnp.zeros((1, 7 * LANE), F32)
    pack = jnp.concatenate(
        [loss_part, lane_pad, dsh1, dsc1, dg1, dsh2, dsc2, dg2, dsh3, dsc3, dg3, dn1, dn2, dn3, d_nf, d_cb, d_br, d_bi, d_lam,
         d_cw.reshape(1, -1)], axis=1)
    pack = jnp.pad(pack, ((0, 0), (0, -pack.shape[1] % (8 * LANE))))
    gate_pack = jnp.concatenate([d_wr.reshape(-1, LANE), d_wi.reshape(-1, LANE)], axis=0).astype(BF16)
    n_pack, n_gate = pack.shape[1], gate_pack.size
    updated = sum(results[n][0][0, 0, 0] for n in sorted(results)).reshape(1, 1)
    small_handles, token = _exchange_begin("gather_small", [pack, gate_pack], False, updated)

    done = update_group("scattered_ffn1", scatter_ffn1, [("w_ffn1_in", w_ffn1_in, m_w_ffn1_in, v_w_ffn1_in),
                                                         ("w_ffn1_out", w_ffn1_out, m_w_ffn1_out, v_w_ffn1_out)], token)
    packs, gate_packs = _exchange_end("gathered_small", small_handles, done, False)
    packs = packs.reshape(N_DEV, n_pack // LANE, LANE)
    g_pack = _reduce_parts("sum_small", packs).reshape(1, n_pack)
    g_gate = _reduce_parts("sum_gates", gate_packs).reshape(1, n_gate)
    loss = g_pack[0, 0]
    off = 8 * LANE
    n_vec = 9 * d + 4 * d + 4 * lw
    n_adam = n_vec + n_gate
    g_small = jnp.concatenate([g_pack[:, off:off + n_vec], g_gate], axis=1).reshape(1, n_adam // LANE, LANE)
    d_cw_sum = g_pack[:, off + n_vec:off + n_vec + CONV_WIDTH * lw].reshape(CONV_WIDTH, lw)
    d_cw_mine = lax.dynamic_slice(d_cw_sum, (0, me * cwb), (CONV_WIDTH, cwb))

    small_names = ["b_ada", "norm_ffn1", "norm_mix", "norm_ffn2", "norm_final", "conv_b", "b_rg_gate", "b_in_gate",
                   "lru_lambda", "w_rg_gate", "w_in_gate"]
    given = dict(b_ada=(b_ada, m_b_ada, v_b_ada), norm_ffn1=(norm_ffn1, m_norm_ffn1, v_norm_ffn1),
                 norm_mix=(norm_mix, m_norm_mix, v_norm_mix), norm_ffn2=(norm_ffn2, m_norm_ffn2, v_norm_ffn2),
                 norm_final=(norm_final, m_norm_final, v_norm_final), conv_b=(conv_b, m_conv_b, v_conv_b),
                 b_rg_gate=(b_rg_gate, m_b_rg_gate, v_b_rg_gate), b_in_gate=(b_in_gate, m_b_in_gate, v_b_in_gate),
                 lru_lambda=(lru_lambda, m_lru_lambda, v_lru_lambda), w_rg_gate=(w_rg_gate, m_w_rg_gate, v_w_rg_gate),
                 w_in_gate=(w_in_gate, m_w_in_gate, v_w_in_gate))
    packed = [jnp.concatenate([given[n][q].reshape(1, -1) for n in small_names], axis=1).reshape(n_adam // LANE, LANE)
              for q in range(3)]
    small_out = _adamw("adamw_small", g_small, *packed)
    pos = 0
    for n in small_names:
        shape = given[n][0].shape
        size = math.prod(shape)
        results[n] = [o.reshape(1, n_adam)[:, pos:pos + size].reshape(shape) for o in small_out]
        pos += size
    results["conv_w"] = [o.reshape(conv_w.shape) for o in
                         _adamw("adamw_conv_w", d_cw_mine[None], conv_w[0], m_conv_w[0], v_conv_w[0])]

    dmod_all = packs.reshape(N_DEV, n_pack)[:, off:off + 9 * d]
    dmod_mine = lax.dynamic_slice(dmod_all, (0, me * cba), (N_DEV, cba))
    dmod_rows = jnp.pad(dmod_mine, ((0, LANE - N_DEV), (0, 0)))
    c_act_t = jnp.pad(c_act.T, ((0, 0), (0, LANE - N_DEV)))
    tm_d2 = _tile(d, 256)
    d_wada = _matmul("dw_ada", c_act_t[None], dmod_rows[None], "nn", tm_d2, cba, LANE,
                     outs=[((1, d, cba), F32, _bspec((1, d, cba), tm_d2, cba, _ij))])[0]
    results["w_ada"] = [o[None] for o in _adamw("adamw_w_ada", d_wada, w_ada[0], m_w_ada[0], v_w_ada[0])]


    order = ["w_ada", "b_ada", "norm_ffn1", "w_ffn1_in", "w_ffn1_out", "norm_mix", "w_in", "conv_w", "conv_b", "w_rg_gate",
             "b_rg_gate", "w_in_gate", "b_in_gate", "lru_lambda", "w_branch_attn", "w_branch_lru", "w_out", "norm_ffn2",
             "w_ffn2_in", "w_ffn2_out", "norm_final"]
    return (loss, grad_x[None], *[results[n][0] for n in order], *[results[n][1] for n in order],
            *[results[n][2] for n in order], *[results[n][3] for n in order])
```
